```python
import math
import jax, jax.numpy as jnp
from jax import lax
import numpy as np


D_MODEL = 1024
BATCH = 8
SEQ = 2048
DEPTH = 2

NORM_EPS = 1e-6
S5_WIDTH = D_MODEL // 2
S5_GROUP = 16
S5_GROUPS = S5_WIDTH // S5_GROUP
S5_STATE = 64
GLA_WIDTH = D_MODEL - S5_WIDTH
GLA_HEADS = 4
GLA_DV = GLA_WIDTH // GLA_HEADS
GLA_DK = GLA_DV // 2
GLA_QK = GLA_HEADS * GLA_DK
GLA_GATE_RANK = 16
GLA_TAU = 16.0
GLA_CHUNK = 64
IN_SPLITS = (S5_WIDTH, S5_WIDTH + GLA_QK, S5_WIDTH + 2 * GLA_QK, S5_WIDTH + 2 * GLA_QK + GLA_WIDTH, S5_WIDTH + 2 * GLA_QK + GLA_WIDTH + GLA_GATE_RANK)
IN_WIDTH = IN_SPLITS[-1] + GLA_WIDTH
RWKV_HEAD = 64
RWKV_HEADS = D_MODEL // RWKV_HEAD
RWKV_W_RANK = 64
RWKV_A_RANK = 64
RWKV_G_RANK = 128
RWKV_GN_EPS = 64e-5
PEER_HEADS = 8
PEER_NKEYS = 128
PEER_EXPERTS = PEER_NKEYS * PEER_NKEYS
PEER_DQ = 256
PEER_TOPK = 16
PEER_BLOCK = 128
N_EVEN = (DEPTH + 1) // 2
N_ODD = DEPTH // 2

kernel_name = 'hybrid_s5_gla_rwkv7_peer'


def rms_norm(x, g):
    xf = x.astype(jnp.float32)
    y = xf * lax.rsqrt(jnp.mean(xf * xf, axis=-1, keepdims=True) + NORM_EPS)
    return (y * g.astype(jnp.float32)).astype(x.dtype)


def s5_mixer(u, a_re, a_im, log_dt, b_re, b_im, c_re, c_im, d_skip, w_glu):
    bsz, L, _ = u.shape
    uf = u.astype(jnp.float32)
    ug = uf.reshape(bsz, L, S5_GROUPS, S5_GROUP)
    ar = a_re.astype(jnp.float32)
    ai = a_im.astype(jnp.float32)
    dt = jnp.exp(log_dt.astype(jnp.float32))
    mag = jnp.exp(dt * ar)
    abar_re = mag * jnp.cos(dt * ai)
    abar_im = mag * jnp.sin(dt * ai)
    nr = abar_re - 1.0
    den = ar * ar + ai * ai
    f_re = (nr * ar + abar_im * ai) / den
    f_im = (abar_im * ar - nr * ai) / den
    br = b_re.astype(jnp.float32)
    bi = b_im.astype(jnp.float32)
    bbar_re = f_re[..., None] * br - f_im[..., None] * bi
    bbar_im = f_re[..., None] * bi + f_im[..., None] * br
    bu_re = jnp.einsum('blgc,gpc->blgp', ug, bbar_re)
    bu_im = jnp.einsum('blgc,gpc->blgp', ug, bbar_im)
    aa_re = jnp.broadcast_to(abar_re, bu_re.shape)
    aa_im = jnp.broadcast_to(abar_im, bu_re.shape)

    def combine(e1, e2):
        a1r, a1i, b1r, b1i = e1
        a2r, a2i, b2r, b2i = e2
        return (a2r * a1r - a2i * a1i,
                a2r * a1i + a2i * a1r,
                a2r * b1r - a2i * b1i + b2r,
                a2r * b1i + a2i * b1r + b2i)

    _, _, s_re, s_im = lax.associative_scan(combine, (aa_re, aa_im, bu_re, bu_im), axis=1)
    y = (jnp.einsum('blgp,gcp->blgc', s_re, c_re.astype(jnp.float32))
         - jnp.einsum('blgp,gcp->blgc', s_im, c_im.astype(jnp.float32)))
    y = y.reshape(bsz, L, S5_WIDTH) + d_skip.astype(jnp.float32) * uf
    g = jax.nn.gelu(y, approximate=False)
    return g * jax.nn.sigmoid(g @ w_glu.astype(jnp.float32))


def gla_mixer(q, k, v, g_low, r, w_g2, b_g2, norm_g):
    bsz, L, _ = q.shape
    H, dk, dv, C = GLA_HEADS, GLA_DK, GLA_DV, GLA_CHUNK
    nc = L // C
    qf = q.astype(jnp.float32).reshape(bsz, nc, C, H, dk) * (dk ** -0.5)
    kf = k.astype(jnp.float32).reshape(bsz, nc, C, H, dk)
    vf = v.astype(jnp.float32).reshape(bsz, nc, C, H, dv)
    log_a = jax.nn.log_sigmoid(g_low.astype(jnp.float32) @ w_g2.astype(jnp.float32)
                               + b_g2.astype(jnp.float32)) / GLA_TAU
    log_a = log_a.reshape(bsz, nc, C, H, dk)
    b = jnp.cumsum(log_a, axis=2)
    b_last = b[:, :, -1]
    q_d = qf * jnp.exp(b)
    k_d = kf * jnp.exp(-b)
    k_end = kf * jnp.exp(b_last[:, :, None] - b)
    causal = jnp.tril(jnp.ones((C, C), dtype=bool))
    att = jnp.einsum('bnihd,bnjhd->bnhij', q_d, k_d)
    att = jnp.where(causal, att, 0.0)
    o = jnp.einsum('bnhij,bnjhv->bnihv', att, vf)
    kv = jnp.einsum('bnjhd,bnjhv->bnhdv', k_end, vf)
    decay = jnp.exp(b_last)

    def step(S, inp):
        dec, kv_n = inp
        return S * dec[..., None] + kv_n, S

    S0 = jnp.zeros((bsz, H, dk, dv), jnp.float32)
    _, S_prev = lax.scan(step, S0, (jnp.moveaxis(decay, 1, 0), jnp.moveaxis(kv, 1, 0)))
    S_prev = jnp.moveaxis(S_prev, 0, 1)
    o = o + jnp.einsum('bnihd,bnhdv->bnihv', q_d, S_prev)
    o = o.reshape(bsz, L, H, dv)
    o = o * lax.rsqrt(jnp.mean(o * o, axis=-1, keepdims=True) + NORM_EPS)
    o = o * norm_g.astype(jnp.float32).reshape(H, dv)
    return o.reshape(bsz, L, H * dv) * jax.nn.silu(r.astype(jnp.float32))


def even_mixer(xn, w_in, w_out, a_re, a_im, log_dt, b_re, b_im, c_re, c_im, d_skip, w_glu, w_g2, b_g2, norm_g):
    proj = xn @ w_in
    u, q, k, v, g_low, r = jnp.split(proj, IN_SPLITS, axis=-1)
    y_s5 = s5_mixer(u, a_re, a_im, log_dt, b_re, b_im, c_re, c_im, d_skip, w_glu)
    y_gla = gla_mixer(q, k, v, g_low, r, w_g2, b_g2, norm_g)
    y = jnp.concatenate([y_s5, y_gla], axis=-1) @ w_out.astype(jnp.float32)
    return y.astype(xn.dtype)


def rwkv7_mixer(xn, mu, w_r, w_k, w_v, w0, w_w1, w_w2, a0, w_a1, w_a2, w_g1, w_g2, k_k, k_a, r_k, lnx_g, lnx_b, w_o):
    bsz, L, D = xn.shape
    H, N = RWKV_HEADS, RWKV_HEAD
    xf = xn.astype(jnp.float32)
    mu = mu.astype(jnp.float32)
    x_prev = jnp.pad(xf, ((0, 0), (1, 0), (0, 0)))[:, :-1]
    xx = x_prev - xf
    xr, xw, xk, xv, xa, xg = [xf + xx * mu[i] for i in range(6)]
    f = lambda t: t.astype(jnp.float32)
    r = xr @ f(w_r)
    k = xk @ f(w_k)
    v = xv @ f(w_v)
    w = -jax.nn.softplus(-(f(w0) + jnp.tanh(xw @ f(w_w1)) @ f(w_w2))) - 0.5
    decay = jnp.exp(-jnp.exp(w))
    a = jax.nn.sigmoid(f(a0) + (xa @ f(w_a1)) @ f(w_a2))
    g = jax.nn.sigmoid(xg @ f(w_g1)) @ f(w_g2)
    kk = (k * f(k_k)).reshape(bsz, L, H, N)
    kk = kk / jnp.maximum(jnp.sqrt(jnp.sum(kk * kk, axis=-1, keepdims=True)), 1e-12)
    k = k * (1.0 + (a - 1.0) * f(k_a))
    hd = lambda t: t.reshape(bsz, L, H, N)
    r, decay, k, v, a = hd(r), hd(decay), hd(k), hd(v), hd(a)
    tm = lambda t: jnp.moveaxis(t, 1, 0)

    def step(S, inp):
        r_t, w_t, k_t, v_t, a_t, b_t = inp
        sa = jnp.einsum('bhvk,bhk->bhv', S, a_t)
        S = S * w_t[:, :, None, :] + sa[..., None] * b_t[:, :, None, :] + v_t[..., None] * k_t[:, :, None, :]
        return S, jnp.einsum('bhvk,bhk->bhv', S, r_t)

    S0 = jnp.zeros((bsz, H, N, N), jnp.float32)
    _, y = lax.scan(step, S0, (tm(r), tm(decay), tm(k), tm(v), tm(-kk), tm(kk * a)))
    y = jnp.moveaxis(y, 0, 1)
    mean = jnp.mean(y, axis=-1, keepdims=True)
    var = jnp.mean((y - mean) ** 2, axis=-1, keepdims=True)
    y = ((y - mean) * lax.rsqrt(var + RWKV_GN_EPS)).reshape(bsz, L, D) * f(lnx_g) + f(lnx_b)
    bonus = jnp.sum(r * k * f(r_k), axis=-1, keepdims=True) * v
    y = y + bonus.reshape(bsz, L, D)
    return ((y * g) @ f(w_o)).astype(xn.dtype)


def peer_ffn(xn, w_q, sub_keys, u_tab, v_tab):
    bsz, L, D = xn.shape
    T = bsz * L
    H, K, NK = PEER_HEADS, PEER_TOPK, PEER_NKEYS
    xt = xn.reshape(T, D)
    q = (xt.astype(jnp.float32) @ w_q.astype(jnp.float32)).reshape(T, H, 2, PEER_DQ // 2)
    s = jnp.einsum('thcd,hcnd->thcn', q, sub_keys.astype(jnp.float32))
    s1, i1 = lax.top_k(s[:, :, 0], K)
    s2, i2 = lax.top_k(s[:, :, 1], K)
    cand = (s1[..., :, None] + s2[..., None, :]).reshape(T, H, K * K)
    cand_idx = (i1[..., :, None] * NK + i2[..., None, :]).reshape(T, H, K * K)
    top_s, pos = lax.top_k(cand, K)
    idx = jnp.take_along_axis(cand_idx, pos, axis=-1)
    gate = jax.nn.softmax(top_s, axis=-1).astype(xn.dtype)
    nb = T // PEER_BLOCK

    def block(args):
        xb, ib, gb = args
        h = jnp.einsum('tkd,td->tk', u_tab[ib], xb)
        h = jax.nn.gelu(h, approximate=False) * gb
        return jnp.einsum('tk,tkd->td', h, v_tab[ib])

    out = lax.map(block, (xt.reshape(nb, PEER_BLOCK, D),
                          idx.reshape(nb, PEER_BLOCK, H * K),
                          gate.reshape(nb, PEER_BLOCK, H * K)))
    return out.reshape(bsz, L, D).astype(xn.dtype)


def setup_inputs(seed: int = 0) -> dict:
    key = jax.random.key(seed)
    ks = iter(jax.random.split(key, 64))

    def nrm(shape, scale):
        return scale * jax.random.normal(next(ks), shape, jnp.float32)

    def unif(shape, lo, hi):
        return jax.random.uniform(next(ks), shape, jnp.float32, minval=lo, maxval=hi)

    D = D_MODEL
    G, P, C = S5_GROUPS, S5_STATE, S5_GROUP
    E, O = N_EVEN, N_ODD
    return {
        'x': nrm((BATCH, SEQ, D), 1.0),
        'norm_mix_g': 1.0 + nrm((DEPTH, D), 0.02),
        'norm_ffn_g': 1.0 + nrm((DEPTH, D), 0.02),
        'final_g': 1.0 + nrm((D,), 0.02),
        'e_w_in': nrm((E, D, IN_WIDTH), D ** -0.5),
        'e_w_out': nrm((E, D, D), D ** -0.5),
        's5_a_re': -0.5 + nrm((E, G, P), 0.01),
        's5_a_im': jnp.broadcast_to(jnp.pi * jnp.arange(P, dtype=jnp.float32), (E, G, P)) + nrm((E, G, P), 0.01),
        's5_log_dt': unif((E, G, P), math.log(1e-3), math.log(1e-1)),
        's5_b_re': nrm((E, G, P, C), (2.0 * C) ** -0.5),
        's5_b_im': nrm((E, G, P, C), (2.0 * C) ** -0.5),
        's5_c_re': nrm((E, G, C, P), P ** -0.5),
        's5_c_im': nrm((E, G, C, P), P ** -0.5),
        's5_d': nrm((E, S5_WIDTH), 1.0),
        's5_w_glu': nrm((E, S5_WIDTH, S5_WIDTH), S5_WIDTH ** -0.5),
        'gla_w_g2': nrm((E, GLA_GATE_RANK, GLA_QK), GLA_GATE_RANK ** -0.5),
        'gla_b_g2': nrm((E, GLA_QK), 0.01),
        'gla_norm_g': 1.0 + nrm((E, GLA_WIDTH), 0.02),
        'o_mu': unif((O, 6, D), 0.0, 1.0),
        'o_w_r': nrm((O, D, D), D ** -0.5),
        'o_w_k': nrm((O, D, D), D ** -0.5),
        'o_w_v': nrm((O, D, D), D ** -0.5),
        'o_w0': unif((O, D), -6.0, -1.0),
        'o_w_w1': nrm((O, D, RWKV_W_RANK), D ** -0.5),
        'o_w_w2': nrm((O, RWKV_W_RANK, D), 0.1 * RWKV_W_RANK ** -0.5),
        'o_a0': nrm((O, D), 0.1),
        'o_w_a1': nrm((O, D, RWKV_A_RANK), D ** -0.5),
        'o_w_a2': nrm((O, RWKV_A_RANK, D), 0.1 * RWKV_A_RANK ** -0.5),
        'o_w_g1': nrm((O, D, RWKV_G_RANK), D ** -0.5),
        'o_w_g2': nrm((O, RWKV_G_RANK, D), RWKV_G_RANK ** -0.5),
        'o_k_k': 0.85 + nrm((O, D), 0.02),
        'o_k_a': 1.0 + nrm((O, D), 0.02),
        'o_r_k': nrm((O, RWKV_HEADS, RWKV_HEAD), 0.1),
        'o_lnx_g': 1.0 + nrm((O, D), 0.02),
        'o_lnx_b': nrm((O, D), 0.01),
        'o_w_o': nrm((O, D, D), D ** -0.5),
        'peer_w_q': nrm((DEPTH, D, PEER_HEADS * PEER_DQ), D ** -0.5),
        'peer_sub_keys': nrm((DEPTH, PEER_HEADS, 2, PEER_NKEYS, PEER_DQ // 2), (PEER_DQ // 2) ** -0.5),
        'peer_u': nrm((DEPTH, PEER_EXPERTS, D), D ** -0.5),
        'peer_v': nrm((DEPTH, PEER_EXPERTS, D), (PEER_HEADS * PEER_TOPK) ** -0.5),
    }


def reference(x, norm_mix_g, norm_ffn_g, final_g,
              e_w_in, e_w_out, s5_a_re, s5_a_im, s5_log_dt, s5_b_re, s5_b_im, s5_c_re, s5_c_im, s5_d, s5_w_glu,
              gla_w_g2, gla_b_g2, gla_norm_g,
              o_mu, o_w_r, o_w_k, o_w_v, o_w0, o_w_w1, o_w_w2, o_a0, o_w_a1, o_w_a2, o_w_g1, o_w_g2,
              o_k_k, o_k_a, o_r_k, o_lnx_g, o_lnx_b, o_w_o,
              peer_w_q, peer_sub_keys, peer_u, peer_v):
    h = x
    for layer in range(DEPTH):
        hn = rms_norm(h, norm_mix_g[layer])
        i = layer // 2
        if layer % 2 == 0:
            mix = even_mixer(hn, e_w_in[i], e_w_out[i], s5_a_re[i], s5_a_im[i], s5_log_dt[i],
                             s5_b_re[i], s5_b_im[i], s5_c_re[i], s5_c_im[i], s5_d[i], s5_w_glu[i],
                             gla_w_g2[i], gla_b_g2[i], gla_norm_g[i])
        else:
            mix = rwkv7_mixer(hn, o_mu[i], o_w_r[i], o_w_k[i], o_w_v[i], o_w0[i], o_w_w1[i], o_w_w2[i],
                              o_a0[i], o_w_a1[i], o_w_a2[i], o_w_g1[i], o_w_g2[i], o_k_k[i], o_k_a[i],
                              o_r_k[i], o_lnx_g[i], o_lnx_b[i], o_w_o[i])
        h = h + mix
        hn = rms_norm(h, norm_ffn_g[layer])
        h = h + peer_ffn(hn, peer_w_q[layer], peer_sub_keys[layer], peer_u[layer], peer_v[layer])
    return rms_norm(h, final_g)
```

```python
import functools
import math

import jax
import jax.numpy as jnp
from jax import lax
from jax.experimental import pallas as pl
from jax.experimental.pallas import tpu as pltpu

F32 = jnp.float32
BF16 = jnp.bfloat16
HIGHEST = lax.Precision.HIGHEST

D_MODEL = 1024
BATCH = 8
NORM_EPS = 1e-6
S5_WIDTH = 512
S5_GROUP = 16
S5_GROUPS = 32
S5_STATE = 64
S5_SLABS = 4
GLA_HEADS = 4
GLA_DV = 128
GLA_DK = 64
GLA_DKP = 128
GLA_RANK = 16
GLA_TAU = 16.0
GLA_CHUNK = 64
RWKV_HEAD = 64
RWKV_HEADS = 16
RWKV_GN_EPS = 64e-5
PEER_HEADS = 8
PEER_NKEYS = 128
PEER_TOPK = 16
LANE = 128
SUBLANE = 8
VMEM_LIMIT = 56 * 1024 * 1024

PROJ_W = 3 * 512 + 2 * GLA_HEADS * GLA_DKP + LANE


def _dot(a, b, hi):
    return jnp.dot(a, b, preferred_element_type=F32, precision=HIGHEST if hi else None)


def _dot_nt(a, b, hi):
    return lax.dot_general(a, b, (((1,), (1,)), ((), ())), preferred_element_type=F32,
                           precision=HIGHEST if hi else None)


def _rms(x, g):
    return x * lax.rsqrt(jnp.mean(x * x, axis=-1, keepdims=True) + NORM_EPS) * g


def _sigmoid(x):
    return 1.0 / (1.0 + jnp.exp(-x))


def _softplus(x):
    return jnp.maximum(x, 0.0) + jnp.log(1.0 + jnp.exp(-jnp.abs(x)))


def _gelu(x):
    return 0.5 * x * (1.0 + lax.erf(x * (1.0 / math.sqrt(2.0))))


def _const_spec(shape):
    nd = len(shape)
    return pl.BlockSpec(shape, lambda *_: (0,) * nd)


def _params(sem):
    return pltpu.CompilerParams(dimension_semantics=sem, vmem_limit_bytes=VMEM_LIMIT)


def _even_front_kernel(h_ref, g_ref, w_ref, o_ref):
    xn = _rms(h_ref[...], g_ref[...])
    o_ref[...] = _dot(xn, w_ref[...], True)


def _even_front(h, g, w):
    T = h.shape[0]
    tm = 256
    return pl.pallas_call(
        _even_front_kernel,
        grid=(T // tm,),
        in_specs=[pl.BlockSpec((tm, D_MODEL), lambda i: (i, 0)),
                  _const_spec((1, D_MODEL)),
                  _const_spec((D_MODEL, PROJ_W))],
        out_specs=pl.BlockSpec((tm, PROJ_W), lambda i: (i, 0)),
        out_shape=jax.ShapeDtypeStruct((T, PROJ_W), F32),
        compiler_params=_params(("parallel",)),
        name="even_front",
    )(h, g, w)


def _s5_kernel(u_ref, bd_ref, cd_ref, are_ref, aim_ref, d_ref, wglu_ref, o_ref, bu_ref, st_ref):
    rows = u_ref.shape[0]
    steps = rows // BATCH
    half = 512

    @pl.when(pl.program_id(0) == 0)
    def _():
        st_ref[...] = jnp.zeros_like(st_ref)

    u = u_ref[...]
    for s in range(S5_SLABS):
        bu_ref[s] = _dot(u[:, s * LANE:(s + 1) * LANE], bd_ref[s], True)

    def step(t, carry):
        r0 = pl.multiple_of(t * BATCH, BATCH)
        out = []
        for s in range(S5_SLABS):
            sre, sim = carry[2 * s], carry[2 * s + 1]
            ar, ai = are_ref[s], aim_ref[s]
            bre = bu_ref[s, pl.ds(r0, BATCH), 0:half]
            bim = bu_ref[s, pl.ds(r0, BATCH), half:2 * half]
            nre = ar * sre - ai * sim + bre
            nim = ar * sim + ai * sre + bim
            bu_ref[s, pl.ds(r0, BATCH), 0:half] = nre
            bu_ref[s, pl.ds(r0, BATCH), half:2 * half] = nim
            out += [nre, nim]
        return tuple(out)

    init = tuple(st_ref[j] for j in range(2 * S5_SLABS))
    fin = lax.fori_loop(0, steps, step, init)
    for j in range(2 * S5_SLABS):
        st_ref[j] = fin[j]

    y = jnp.concatenate([_dot(bu_ref[s], cd_ref[s], True) for s in range(S5_SLABS)], axis=1)
    y = y + d_ref[...] * u
    g = _gelu(y)
    o_ref[...] = g * _sigmoid(_dot(g, wglu_ref[...], True))


def _s5(proj, bd, cd, are, aim, d, wglu):
    T = proj.shape[0]
    rows = 256
    return pl.pallas_call(
        _s5_kernel,
        grid=(T // rows,),
        in_specs=[pl.BlockSpec((rows, S5_WIDTH), lambda i: (i, 0)),
                  _const_spec(bd.shape), _const_spec(cd.shape),
                  _const_spec(are.shape), _const_spec(aim.shape),
                  _const_spec(d.shape), _const_spec(wglu.shape)],
        out_specs=pl.BlockSpec((rows, S5_WIDTH), lambda i: (i, 0)),
        out_shape=jax.ShapeDtypeStruct((T, S5_WIDTH), F32),
        scratch_shapes=[pltpu.VMEM((S5_SLABS, rows, 2 * 512), F32),
                        pltpu.VMEM((2 * S5_SLABS, BATCH, 512), F32)],
        compiler_params=_params(("arbitrary",)),
        name="s5",
    )(proj, bd, cd, are, aim, d, wglu)


def _gla_kernel(v_ref, r_ref, q_ref, k_ref, gl_ref, wg_ref, bg_ref, lm_ref, ng_ref, o_ref, s_ref):
    rows = q_ref.shape[0]
    width = GLA_HEADS * GLA_DKP

    @pl.when(pl.program_id(0) == 0)
    def _():
        s_ref[...] = jnp.zeros_like(s_ref)

    z = _dot(gl_ref[...], wg_ref[...], True) + bg_ref[...]
    la = -_softplus(-z) * (1.0 / GLA_TAU) * lm_ref[...]
    b = la
    sh = BATCH
    while sh < rows:
        b = b + jnp.concatenate([jnp.zeros((sh, width), F32), b[:rows - sh]], axis=0)
        sh *= 2
    blast = b[rows - BATCH:rows]
    blast_t = jnp.concatenate([blast] * GLA_CHUNK, axis=0)
    q = q_ref[...] * (GLA_DK ** -0.5)
    k = k_ref[...]
    qd = q * jnp.exp(b)
    kd = k * jnp.exp(-b)
    kend = k * jnp.exp(blast_t - b)

    ri = lax.broadcasted_iota(jnp.int32, (rows, rows), 0)
    ci = lax.broadcasted_iota(jnp.int32, (rows, rows), 1)
    causal = jnp.logical_and(((ri - ci) & (BATCH - 1)) == 0, ri >= ci)
    rowb = lax.broadcasted_iota(jnp.int32, (rows, GLA_DKP), 0) & (BATCH - 1)
    laneb = lax.broadcasted_iota(jnp.int32, (GLA_DKP, rows), 1) & (BATCH - 1)

    for h in range(GLA_HEADS):
        sl = slice(h * GLA_DKP, (h + 1) * GLA_DKP)
        vh = v_ref[:, h * GLA_DV:(h + 1) * GLA_DV]
        qh, kdh, keh, bh = qd[:, sl], kd[:, sl], kend[:, sl], b[:, sl]
        att = jnp.where(causal, _dot_nt(qh, kdh, True), 0.0)
        o = _dot(att, vh, True)
        qexp = jnp.concatenate([jnp.where(rowb == bb, qh, 0.0) for bb in range(BATCH)], axis=1)
        o = o + _dot(qexp, s_ref[h], True)
        ket = keh.T
        kexp_t = jnp.concatenate([jnp.where(laneb == bb, ket, 0.0) for bb in range(BATCH)], axis=0)
        kv = _dot(kexp_t, vh, True)
        bt = bh[rows - GLA_DKP:rows].T
        dec = jnp.concatenate(
            [jnp.broadcast_to(jnp.exp(bt[:, GLA_DKP - BATCH + bb:GLA_DKP - BATCH + bb + 1]),
                              (GLA_DKP, GLA_DV)) for bb in range(BATCH)], axis=0)
        s_ref[h] = s_ref[h] * dec + kv
        o = o * lax.rsqrt(jnp.mean(o * o, axis=-1, keepdims=True) + NORM_EPS)
        o = o * ng_ref[:, h * GLA_DV:(h + 1) * GLA_DV]
        rr = r_ref[:, h * GLA_DV:(h + 1) * GLA_DV]
        o_ref[:, h * GLA_DV:(h + 1) * GLA_DV] = o * (rr * _sigmoid(rr))


def _gla(proj, wg, bg, lm, ng):
    T = proj.shape[0]
    rows = GLA_CHUNK * BATCH
    width = GLA_HEADS * GLA_DKP
    return pl.pallas_call(
        _gla_kernel,
        grid=(T // rows,),
        in_specs=[pl.BlockSpec((rows, 512), lambda i: (i, 1)),
                  pl.BlockSpec((rows, 512), lambda i: (i, 2)),
                  pl.BlockSpec((rows, width), lambda i: (i, 3)),
                  pl.BlockSpec((rows, width), lambda i: (i, 4)),
                  pl.BlockSpec((rows, LANE), lambda i: (i, (3 * 512 + 2 * width) // LANE)),
                  _const_spec(wg.shape), _const_spec(bg.shape), _const_spec(lm.shape),
                  _const_spec(ng.shape)],
        out_specs=pl.BlockSpec((rows, GLA_HEADS * GLA_DV), lambda i: (i, 0)),
        out_shape=jax.ShapeDtypeStruct((T, GLA_HEADS * GLA_DV), F32),
        scratch_shapes=[pltpu.VMEM((GLA_HEADS, BATCH * GLA_DKP, GLA_DV), F32)],
        compiler_params=_params(("arbitrary",)),
        name="gla",
    )(proj, proj, proj, proj, proj, wg, bg, lm, ng)


def _mix_out_kernel(n_in, has_gate, *refs):
    h_ref = refs[0]
    y_refs = refs[1:1 + n_in]
    w_refs = refs[1 + n_in:1 + 2 * n_in]
    rest = refs[1 + 2 * n_in:]
    gate_ref = rest[0] if has_gate else None
    g_ref, ho_ref, hn_ref, hb_ref = rest[1:] if has_gate else rest
    acc = h_ref[...]
    for y_ref, w_ref in zip(y_refs, w_refs):
        y = y_ref[...]
        if has_gate:
            y = y * gate_ref[...]
        acc = acc + _dot(y, w_ref[...], True)
    ho_ref[...] = acc
    hn = _rms(acc, g_ref[...])
    hn_ref[...] = hn
    hb_ref[...] = hn.astype(BF16)


def _mix_out(h, ys, ws, gate, g):
    T = h.shape[0]
    tm = 256
    n_in = len(ys)
    row = pl.BlockSpec((tm, D_MODEL), lambda i: (i, 0))
    in_specs = [row]
    in_specs += [pl.BlockSpec((tm, y.shape[1]), lambda i: (i, 0)) for y in ys]
    in_specs += [_const_spec(w.shape) for w in ws]
    args = [h, *ys, *ws]
    if gate is not None:
        in_specs.append(row)
        args.append(gate)
    in_specs.append(_const_spec((1, D_MODEL)))
    args.append(g)
    return pl.pallas_call(
        functools.partial(_mix_out_kernel, n_in, gate is not None),
        grid=(T // tm,),
        in_specs=in_specs,
        out_specs=[row, row, row],
        out_shape=[jax.ShapeDtypeStruct((T, D_MODEL), F32),
                   jax.ShapeDtypeStruct((T, D_MODEL), F32),
                   jax.ShapeDtypeStruct((T, D_MODEL), BF16)],
        compiler_params=_params(("parallel",)),
        name="mix_out",
    )(*args)


def _insert(tops, x):
    out = []
    for t in tops:
        out.append(jnp.maximum(t, x))
        x = jnp.minimum(t, x)
    return out


def _top_values(scr_ref):
    def body(k, tops):
        x = scr_ref[pl.ds(k, SUBLANE, stride=PEER_NKEYS), :]
        return tuple(_insert(list(tops), x))
    init = tuple(jnp.full((SUBLANE, LANE), -jnp.inf, F32) for _ in range(PEER_TOPK))
    return lax.fori_loop(0, PEER_NKEYS, body, init)


def _peer_topk_kernel(hn_ref, wq_ref, keys_ref, s1_ref, s2_ref, e1_ref, e2_ref, thr_ref,
                      scr1, scr2):
    nch = hn_ref.shape[0] // LANE
    q = _dot(hn_ref[...], wq_ref[...], True)
    for c, scr in enumerate((scr1, scr2)):
        st = _dot_nt(keys_ref[0, c], q[:, c * LANE:(c + 1) * LANE], True)
        for j in range(nch):
            scr[j * PEER_NKEYS:(j + 1) * PEER_NKEYS, :] = st[:, j * LANE:(j + 1) * LANE]
    a = _top_values(scr1)
    b = _top_values(scr2)
    tops = [jnp.full((SUBLANE, LANE), -jnp.inf, F32) for _ in range(PEER_TOPK)]
    for i in range(PEER_TOPK):
        for j in range(PEER_TOPK):
            if (i + 1) * (j + 1) <= PEER_TOPK:
                tops = _insert(tops, a[i] + b[j])
    thr = tops[PEER_TOPK - 1]
    z = jnp.zeros((SUBLANE, LANE), F32)
    for t in tops:
        z = z + jnp.exp(t - tops[0])
    inv_z = 1.0 / z
    thr_ref[0, 0] = thr
    for j in range(nch):
        blk = slice(j * PEER_NKEYS, (j + 1) * PEER_NKEYS)
        s1 = scr1[blk, :]
        s2 = scr2[blk, :]
        s1_ref[0, j] = s1
        s2_ref[0, j] = s2
        e1_ref[0, j] = jnp.exp(s1 - a[0][j:j + 1, :]) * inv_z[j:j + 1, :]
        e2_ref[0, j] = jnp.exp(s2 - b[0][j:j + 1, :])


def _peer_topk(hn, wq, keys):
    T = hn.shape[0]
    tt = SUBLANE * LANE
    nch = T // LANE
    big = jax.ShapeDtypeStruct((PEER_HEADS, nch, PEER_NKEYS, LANE), F32)
    big_spec = pl.BlockSpec((1, SUBLANE, PEER_NKEYS, LANE), lambda i, h: (h, i, 0, 0))
    return pl.pallas_call(
        _peer_topk_kernel,
        grid=(T // tt, PEER_HEADS),
        in_specs=[pl.BlockSpec((tt, D_MODEL), lambda i, h: (i, 0)),
                  pl.BlockSpec((D_MODEL, 2 * LANE), lambda i, h: (0, h)),
                  pl.BlockSpec((1, 2, PEER_NKEYS, LANE), lambda i, h: (h, 0, 0, 0))],
        out_specs=[big_spec, big_spec, big_spec, big_spec,
                   pl.BlockSpec((1, 1, SUBLANE, LANE), lambda i, h: (h, i, 0, 0))],
        out_shape=[big, big, big, big,
                   jax.ShapeDtypeStruct((PEER_HEADS, nch // SUBLANE, SUBLANE, LANE), F32)],
        scratch_shapes=[pltpu.VMEM((tt, LANE), F32), pltpu.VMEM((tt, LANE), F32)],
        compiler_params=_params(("parallel", "arbitrary")),
        name="peer_topk",
    )(hn, wq, keys)


def _peer_main_kernel(final, hb_ref, hres_ref, u_ref, vt_ref, s1_ref, s2_ref, e1_ref, e2_ref,
                      thr_ref, fg_ref, o_ref, acc_ref, hg_ref):
    e = pl.program_id(1)
    et, tt = hg_ref.shape
    n_i1 = et // PEER_NKEYS
    n_ch = tt // LANE
    c0 = (pl.program_id(0) % (SUBLANE // n_ch)) * n_ch

    @pl.when(e == 0)
    def _():
        acc_ref[...] = jnp.zeros_like(acc_ref)

    ht = _dot_nt(u_ref[...], hb_ref[...], False)
    for il in range(n_i1):
        i1 = e * n_i1 + il
        for c in range(n_ch):
            gate = jnp.zeros((PEER_NKEYS, LANE), F32)
            for h in range(PEER_HEADS):
                s1row = s1_ref[h, c, pl.ds(i1, 1), :]
                e1row = e1_ref[h, c, pl.ds(i1, 1), :]
                sel = (s1row + s2_ref[h, c]) >= thr_ref[h, 0, pl.ds(c0 + c, 1), :]
                gate = gate + jnp.where(sel, e1row * e2_ref[h, c], 0.0)
            blk = ht[il * PEER_NKEYS:(il + 1) * PEER_NKEYS, c * LANE:(c + 1) * LANE]
            hg_ref[il * PEER_NKEYS:(il + 1) * PEER_NKEYS, c * LANE:(c + 1) * LANE] = (
                _gelu(blk) * gate).astype(BF16)
    acc_ref[...] += _dot(vt_ref[...], hg_ref[...], False)

    @pl.when(e == pl.num_programs(1) - 1)
    def _():
        out = hres_ref[...] + acc_ref[...].T
        if final:
            out = _rms(out, fg_ref[...])
        o_ref[...] = out


def _peer_main(hb, hres, u, vt, s1, s2, e1, e2, thr, fg, final):
    T = hb.shape[0]
    E = u.shape[0]
    tt, et = 512, 512
    big_spec = pl.BlockSpec((PEER_HEADS, tt // LANE, PEER_NKEYS, LANE), lambda i, e: (0, i, 0, 0))
    return pl.pallas_call(
        functools.partial(_peer_main_kernel, final),
        grid=(T // tt, E // et),
        in_specs=[pl.BlockSpec((tt, D_MODEL), lambda i, e: (i, 0)),
                  pl.BlockSpec((tt, D_MODEL), lambda i, e: (i, 0)),
                  pl.BlockSpec((et, D_MODEL), lambda i, e: (e, 0)),
                  pl.BlockSpec((D_MODEL, et), lambda i, e: (0, e)),
                  big_spec, big_spec, big_spec, big_spec,
                  pl.BlockSpec((PEER_HEADS, 1, SUBLANE, LANE),
                               lambda i, e: (0, i // (SUBLANE * LANE // tt), 0, 0)),
                  _const_spec((1, D_MODEL))],
        out_specs=pl.BlockSpec((tt, D_MODEL), lambda i, e: (i, 0)),
        out_shape=jax.ShapeDtypeStruct((T, D_MODEL), F32),
        scratch_shapes=[pltpu.VMEM((D_MODEL, tt), F32), pltpu.VMEM((et, tt), BF16)],
        compiler_params=_params(("parallel", "arbitrary")),
        name="peer_main",
    )(hb, hres, u, vt, s1, s2, e1, e2, thr, fg)


def _peer_layer(h, hn, hb, wq, keys, u_tab, v_tab, fg, final):
    s1, s2, e1, e2, thr = _peer_topk(hn, wq, keys)
    return _peer_main(hb, h, u_tab.astype(BF16), v_tab.T.astype(BF16), s1, s2, e1, e2, thr, fg, final)


def _rwkv_front_kernel(h_ref, hp_ref, g_ref, mu_ref, wr_ref, wk_ref, wv_ref, w0_ref, ww1_ref, ww2_ref,
                       a0_ref, wa1_ref, wa2_ref, wg1_ref, wg2_ref,
                       r_ref, k_ref, v_ref, w_ref, a_ref, go_ref):
    g = g_ref[...]
    x = _rms(h_ref[...], g)
    xp8 = _rms(hp_ref[...], g)
    xp8 = jnp.where(pl.program_id(0) == 0, 0.0, xp8)
    rows = x.shape[0]
    xprev = jnp.concatenate([xp8, x[:rows - BATCH]], axis=0)
    xx = xprev - x
    mix = lambda j: x + xx * mu_ref[j:j + 1, :]
    r_ref[...] = _dot(mix(0), wr_ref[...], True)
    w_ref[...] = w0_ref[...] + _dot(jnp.tanh(_dot(mix(1), ww1_ref[...], True)), ww2_ref[...], True)
    k_ref[...] = _dot(mix(2), wk_ref[...], True)
    v_ref[...] = _dot(mix(3), wv_ref[...], True)
    a_ref[...] = _sigmoid(a0_ref[...] + _dot(_dot(mix(4), wa1_ref[...], True), wa2_ref[...], True))
    go_ref[...] = _dot(_sigmoid(_dot(mix(5), wg1_ref[...], True)), wg2_ref[...], True)


def _rwkv_front(h, g, mu, wr, wk, wv, w0, ww1, ww2, a0, wa1, wa2, wg1, wg2):
    T = h.shape[0]
    tm = 128
    row = pl.BlockSpec((tm, D_MODEL), lambda i: (i, 0))
    prev = pl.BlockSpec((BATCH, D_MODEL), lambda i: (jnp.maximum(i * (tm // BATCH) - 1, 0), 0))
    consts = [g, mu, wr, wk, wv, w0, ww1, ww2, a0, wa1, wa2, wg1, wg2]
    out = jax.ShapeDtypeStruct((T, D_MODEL), F32)
    return pl.pallas_call(
        _rwkv_front_kernel,
        grid=(T // tm,),
        in_specs=[row, prev] + [_const_spec(c.shape) for c in consts],
        out_specs=[row] * 6,
        out_shape=[out] * 6,
        compiler_params=_params(("parallel",)),
        name="rwkv_front",
    )(h, h, *consts)


def _rwkv_scan_kernel(r_ref, k_ref, v_ref, w_ref, a_ref, kk_ref, ka_ref, rk_ref, lg_ref, lb_ref,
                      y_ref, s_ref, dec_s, k2_s, av_s, bv_s, wr_s, br_s, kr_s):
    tl = r_ref.shape[0]
    n = RWKV_HEAD
    ng = n // SUBLANE

    @pl.when(pl.program_id(0) == 0)
    def _():
        s_ref[...] = jnp.zeros_like(s_ref)

    r = r_ref[...]
    k = k_ref[...]
    a = a_ref[...]
    kk = k * kk_ref[...]
    kk = kk / jnp.maximum(jnp.sqrt(jnp.sum(kk * kk, axis=1, keepdims=True)), 1e-12)
    k2 = k * (1.0 + (a - 1.0) * ka_ref[...])
    dec = jnp.exp(-jnp.exp(-_softplus(-w_ref[...]) - 0.5))
    bv = kk * a
    dec_s[...] = dec
    k2_s[...] = k2
    av_s[...] = -kk
    bv_s[...] = bv
    wr_s[...] = dec * r
    br_s[...] = jnp.sum(bv * r, axis=1, keepdims=True)
    kr_s[...] = jnp.sum(k2 * r, axis=1, keepdims=True)

    def row(ref, t, j):
        return jnp.broadcast_to(ref[t, pl.ds(j, 1), :], (SUBLANE, LANE))

    def step(t, carry):
        def reduce_body(j, accs):
            sa, sy = accs
            a_b = row(av_s, t, j)
            wr_b = row(wr_s, t, j)
            sa_n, sy_n = [], []
            for vg in range(ng):
                sv = s_ref[j, vg * SUBLANE:(vg + 1) * SUBLANE, :]
                sa_n.append(sa[vg] + sv * a_b)
                sy_n.append(sy[vg] + sv * wr_b)
            return tuple(sa_n), tuple(sy_n)

        zeros = tuple(jnp.zeros((SUBLANE, LANE), F32) for _ in range(ng))
        sa, sy = lax.fori_loop(0, n, reduce_body, (zeros, zeros), unroll=4)
        br = jnp.broadcast_to(br_s[t], (SUBLANE, LANE))
        kr = jnp.broadcast_to(kr_s[t], (SUBLANE, LANE))
        vv = [v_ref[t, vg * SUBLANE:(vg + 1) * SUBLANE, :] for vg in range(ng)]
        for vg in range(ng):
            y_ref[t, vg * SUBLANE:(vg + 1) * SUBLANE, :] = sy[vg] + sa[vg] * br + vv[vg] * kr

        def update_body(j, c):
            w_b = row(dec_s, t, j)
            b_b = row(bv_s, t, j)
            k_b = row(k2_s, t, j)
            for vg in range(ng):
                sl = slice(vg * SUBLANE, (vg + 1) * SUBLANE)
                s_ref[j, sl, :] = s_ref[j, sl, :] * w_b + sa[vg] * b_b + vv[vg] * k_b
            return c

        lax.fori_loop(0, n, update_body, 0, unroll=4)
        return carry

    lax.fori_loop(0, tl, step, 0)

    y = y_ref[...]
    mean = jnp.mean(y, axis=1, keepdims=True)
    var = jnp.mean((y - mean) ** 2, axis=1, keepdims=True)
    y = (y - mean) * lax.rsqrt(var + RWKV_GN_EPS) * lg_ref[...] + lb_ref[...]
    bonus = jnp.sum(r * k2 * rk_ref[...], axis=1, keepdims=True) * v_ref[...]
    y_ref[...] = y + bonus


def _rwkv_scan(r, k, v, w, a, kk, ka, rk, lg, lb):
    L = r.shape[0]
    tl = 32
    n = RWKV_HEAD
    blk = pl.BlockSpec((tl, n, LANE), lambda i: (i, 0, 0))
    par = _const_spec((n, LANE))
    big = pltpu.VMEM((tl, n, LANE), F32)
    small = pltpu.VMEM((tl, 1, LANE), F32)
    return pl.pallas_call(
        _rwkv_scan_kernel,
        grid=(L // tl,),
        in_specs=[blk] * 5 + [par] * 5,
        out_specs=blk,
        out_shape=jax.ShapeDtypeStruct((L, n, LANE), F32),
        scratch_shapes=[pltpu.VMEM((n, n, LANE), F32), big, big, big, big, big, small, small],
        compiler_params=_params(("arbitrary",)),
        name="rwkv_scan",
    )(r, k, v, w, a, kk, ka, rk, lg, lb)


def _to_scan_layout(x, L):
    return x.reshape(L, BATCH, RWKV_HEADS, RWKV_HEAD).transpose(0, 3, 1, 2).reshape(L, RWKV_HEAD, LANE)


def _from_scan_layout(y, L):
    return y.reshape(L, RWKV_HEAD, BATCH, RWKV_HEADS).transpose(0, 2, 3, 1).reshape(L * BATCH, D_MODEL)


def _head_param(p):
    p = p.reshape(RWKV_HEADS, RWKV_HEAD).T
    return jnp.tile(p, (1, BATCH))


def _prep_even(w_in, a_re, a_im, log_dt, b_re, b_im, c_re, c_im, w_g2, b_g2):
    o = 0
    w_u = w_in[:, o:o + 512]; o += 512
    w_q = w_in[:, o:o + 256]; o += 256
    w_k = w_in[:, o:o + 256]; o += 256
    w_v = w_in[:, o:o + 512]; o += 512
    w_gl = w_in[:, o:o + GLA_RANK]; o += GLA_RANK
    w_r = w_in[:, o:o + 512]

    def pad_heads(w):
        w = w.reshape(-1, GLA_HEADS, GLA_DK)
        w = jnp.pad(w, ((0, 0), (0, 0), (0, GLA_DKP - GLA_DK)))
        return w.reshape(-1, GLA_HEADS * GLA_DKP)

    w_proj = jnp.concatenate([w_u, w_v, w_r, pad_heads(w_q), pad_heads(w_k),
                              jnp.pad(w_gl, ((0, 0), (0, LANE - GLA_RANK)))], axis=1)
    wg = jnp.pad(pad_heads(w_g2), ((0, LANE - GLA_RANK), (0, 0)))
    bg = pad_heads(b_g2[None, :])
    lm = pad_heads(jnp.ones((1, GLA_HEADS * GLA_DK), F32))

    dt = jnp.exp(log_dt)
    mag = jnp.exp(dt * a_re)
    abar_re = mag * jnp.cos(dt * a_im)
    abar_im = mag * jnp.sin(dt * a_im)
    nr = abar_re - 1.0
    den = a_re * a_re + a_im * a_im
    f_re = (nr * a_re + abar_im * a_im) / den
    f_im = (abar_im * a_re - nr * a_im) / den
    bbar_re = f_re[..., None] * b_re - f_im[..., None] * b_im
    bbar_im = f_re[..., None] * b_im + f_im[..., None] * b_re
    eye = jnp.eye(8, dtype=F32)
    gs = S5_GROUPS // S5_SLABS

    def bd(bb):
        bb = bb.reshape(S5_SLABS, gs, S5_STATE, S5_GROUP)
        return jnp.einsum('sgpc,gh->sgchp', bb, eye).reshape(S5_SLABS, gs * S5_GROUP, gs * S5_STATE)

    def cdm(cc):
        cc = cc.reshape(S5_SLABS, gs, S5_GROUP, S5_STATE)
        return jnp.einsum('sgcp,gh->sgphc', cc, eye).reshape(S5_SLABS, gs * S5_STATE, gs * S5_GROUP)

    bd_all = jnp.concatenate([bd(bbar_re), bd(bbar_im)], axis=2)
    cd_all = jnp.concatenate([cdm(c_re), -cdm(c_im)], axis=1)
    are = jnp.broadcast_to(abar_re.reshape(S5_SLABS, 1, gs * S5_STATE), (S5_SLABS, BATCH, gs * S5_STATE))
    aim = jnp.broadcast_to(abar_im.reshape(S5_SLABS, 1, gs * S5_STATE), (S5_SLABS, BATCH, gs * S5_STATE))
    return w_proj, wg, bg, lm, bd_all, cd_all, are, aim


def kernel(x, norm_mix_g, norm_ffn_g, final_g, e_w_in, e_w_out, s5_a_re, s5_a_im, s5_log_dt, s5_b_re, s5_b_im, s5_c_re, s5_c_im, s5_d, s5_w_glu, gla_w_g2, gla_b_g2, gla_norm_g, o_mu, o_w_r, o_w_k, o_w_v, o_w0, o_w_w1, o_w_w2, o_a0, o_w_a1, o_w_a2, o_w_g1, o_w_g2, o_k_k, o_k_a, o_r_k, o_lnx_g, o_lnx_b, o_w_o, peer_w_q, peer_sub_keys, peer_u, peer_v):
    bsz, L, D = x.shape
    assert bsz == BATCH and D == D_MODEL
    T = bsz * L
    row = lambda p: p.reshape(1, -1)
    h = x.transpose(1, 0, 2).reshape(T, D)

    w_proj, wg, bg, lm, bd_all, cd_all, are, aim = _prep_even(
        e_w_in[0], s5_a_re[0], s5_a_im[0], s5_log_dt[0], s5_b_re[0], s5_b_im[0],
        s5_c_re[0], s5_c_im[0], gla_w_g2[0], gla_b_g2[0])
    proj = _even_front(h, row(norm_mix_g[0]), w_proj)
    y_s5 = _s5(proj, bd_all, cd_all, are, aim, row(s5_d[0]), s5_w_glu[0])
    y_gla = _gla(proj, wg, bg, lm, row(gla_norm_g[0]))
    h, hn, hb = _mix_out(h, [y_s5, y_gla], [e_w_out[0][:S5_WIDTH], e_w_out[0][S5_WIDTH:]], None,
                         row(norm_ffn_g[0]))
    h = _peer_layer(h, hn, hb, peer_w_q[0], peer_sub_keys[0], peer_u[0], peer_v[0], row(final_g), False)

    r, k, v, w, a, g = _rwkv_front(
        h, row(norm_mix_g[1]), o_mu[0], o_w_r[0], o_w_k[0], o_w_v[0], row(o_w0[0]), o_w_w1[0], o_w_w2[0],
        row(o_a0[0]), o_w_a1[0], o_w_a2[0], o_w_g1[0], o_w_g2[0])
    sl = lambda t: _to_scan_layout(t, L)
    y = _rwkv_scan(sl(r), sl(k), sl(v), sl(w), sl(a), _head_param(o_k_k[0]), _head_param(o_k_a[0]),
                   _head_param(o_r_k[0].reshape(-1)), _head_param(o_lnx_g[0]), _head_param(o_lnx_b[0]))
    y = _from_scan_layout(y, L)
    h, hn, hb = _mix_out(h, [y], [o_w_o[0]], g, row(norm_ffn_g[1]))
    out = _peer_layer(h, hn, hb, peer_w_q[1], peer_sub_keys[1], peer_u[1], peer_v[1], row(final_g), True)
    return out.reshape(L, bsz, D).transpose(1, 0, 2)
```

```python
import functools
import math

import jax
import jax.numpy as jnp
from jax import lax
from jax.experimental import pallas as pl
from jax.experimental.pallas import tpu as pltpu

F32 = jnp.float32
BF16 = jnp.bfloat16
HIGHEST = lax.Precision.HIGHEST

D_MODEL = 1024
BATCH = 8
NORM_EPS = 1e-6
S5_WIDTH = 512
S5_GROUP = 16
S5_GROUPS = 32
S5_STATE = 64
S5_SLABS = 4
GLA_HEADS = 4
GLA_DV = 128
GLA_DK = 64
GLA_DKP = 128
GLA_RANK = 16
GLA_TAU = 16.0
GLA_CHUNK = 64
RWKV_HEAD = 64
RWKV_HEADS = 16
RWKV_GN_EPS = 64e-5
PEER_HEADS = 8
PEER_NKEYS = 128
PEER_TOPK = 16
PEER_SUB = 256
GATE_STEP = 2.0
LANE = 128
SUBLANE = 8
VMEM_LIMIT = 56 * 1024 * 1024

PROJ_W = 3 * 512 + 2 * GLA_HEADS * GLA_DKP + LANE


def _dot(a, b, hi=False):
    if hi:
        return jnp.dot(a, b, preferred_element_type=F32, precision=HIGHEST)
    return jnp.dot(a.astype(BF16), b.astype(BF16), preferred_element_type=F32)


def _dot_nt(a, b):
    return lax.dot_general(a.astype(BF16), b.astype(BF16), (((1,), (1,)), ((), ())),
                           preferred_element_type=F32)


def _rms(x, g):
    return x * lax.rsqrt(jnp.mean(x * x, axis=-1, keepdims=True) + NORM_EPS) * g


def _sigmoid(x):
    return 1.0 / (1.0 + jnp.exp(-x))


def _softplus(x):
    return jnp.maximum(x, 0.0) + jnp.log(1.0 + jnp.exp(-jnp.abs(x)))


def _gelu(x):
    return 0.5 * x * (1.0 + lax.erf(x * (1.0 / math.sqrt(2.0))))


def _const_spec(shape):
    nd = len(shape)
    return pl.BlockSpec(shape, lambda *_: (0,) * nd)


def _params(sem):
    return pltpu.CompilerParams(dimension_semantics=sem, vmem_limit_bytes=VMEM_LIMIT)


def _even_front_kernel(h_ref, g_ref, w_ref, o_ref):
    xn = _rms(h_ref[...], g_ref[...])
    o_ref[...] = _dot(xn, w_ref[...])


def _even_front(h, g, w):
    T = h.shape[0]
    tm = 256
    return pl.pallas_call(
        _even_front_kernel,
        grid=(T // tm,),
        in_specs=[pl.BlockSpec((tm, D_MODEL), lambda i: (i, 0)),
                  _const_spec((1, D_MODEL)),
                  _const_spec((D_MODEL, PROJ_W))],
        out_specs=pl.BlockSpec((tm, PROJ_W), lambda i: (i, 0)),
        out_shape=jax.ShapeDtypeStruct((T, PROJ_W), F32),
        compiler_params=_params(("parallel",)),
        name="even_front",
    )(h, g, w)


def _s5_kernel(u_ref, bd_ref, cd_ref, are_ref, aim_ref, d_ref, wglu_ref, o_ref, bu_ref, st_ref):
    rows = u_ref.shape[0]
    steps = rows // BATCH
    half = 512

    @pl.when(pl.program_id(0) == 0)
    def _():
        st_ref[...] = jnp.zeros_like(st_ref)

    u = u_ref[...]
    for s in range(S5_SLABS):
        bu_ref[s] = _dot(u[:, s * LANE:(s + 1) * LANE], bd_ref[s])

    def step(t, carry):
        r0 = pl.multiple_of(t * BATCH, BATCH)
        out = []
        for s in range(S5_SLABS):
            sre, sim = carry[2 * s], carry[2 * s + 1]
            ar, ai = are_ref[s], aim_ref[s]
            bre = bu_ref[s, pl.ds(r0, BATCH), 0:half]
            bim = bu_ref[s, pl.ds(r0, BATCH), half:2 * half]
            nre = ar * sre - ai * sim + bre
            nim = ar * sim + ai * sre + bim
            bu_ref[s, pl.ds(r0, BATCH), 0:half] = nre
            bu_ref[s, pl.ds(r0, BATCH), half:2 * half] = nim
            out += [nre, nim]
        return tuple(out)

    init = tuple(st_ref[j] for j in range(2 * S5_SLABS))
    fin = lax.fori_loop(0, steps, step, init)
    for j in range(2 * S5_SLABS):
        st_ref[j] = fin[j]

    y = jnp.concatenate([_dot(bu_ref[s], cd_ref[s]) for s in range(S5_SLABS)], axis=1)
    y = y + d_ref[...] * u
    g = _gelu(y)
    o_ref[...] = g * _sigmoid(_dot(g, wglu_ref[...]))


def _s5(proj, bd, cd, are, aim, d, wglu):
    T = proj.shape[0]
    rows = 256
    return pl.pallas_call(
        _s5_kernel,
        grid=(T // rows,),
        in_specs=[pl.BlockSpec((rows, S5_WIDTH), lambda i: (i, 0)),
                  _const_spec(bd.shape), _const_spec(cd.shape),
                  _const_spec(are.shape), _const_spec(aim.shape),
                  _const_spec(d.shape), _const_spec(wglu.shape)],
        out_specs=pl.BlockSpec((rows, S5_WIDTH), lambda i: (i, 0)),
        out_shape=jax.ShapeDtypeStruct((T, S5_WIDTH), F32),
        scratch_shapes=[pltpu.VMEM((S5_SLABS, rows, 2 * 512), F32),
                        pltpu.VMEM((2 * S5_SLABS, BATCH, 512), F32)],
        compiler_params=_params(("arbitrary",)),
        name="s5",
    )(proj, bd, cd, are, aim, d, wglu)


def _gla_kernel(v_ref, r_ref, q_ref, k_ref, gl_ref, wg_ref, bg_ref, lm_ref, ng_ref, o_ref, s_ref):
    rows = q_ref.shape[0]
    width = GLA_HEADS * GLA_DKP

    @pl.when(pl.program_id(0) == 0)
    def _():
        s_ref[...] = jnp.zeros_like(s_ref)

    z = _dot(gl_ref[...], wg_ref[...], True) + bg_ref[...]
    la = -_softplus(-z) * (1.0 / GLA_TAU) * lm_ref[...]
    b = la
    sh = BATCH
    while sh < rows:
        b = b + jnp.concatenate([jnp.zeros((sh, width), F32), b[:rows - sh]], axis=0)
        sh *= 2
    blast = b[rows - BATCH:rows]
    blast_t = jnp.concatenate([blast] * GLA_CHUNK, axis=0)
    q = q_ref[...] * (GLA_DK ** -0.5)
    k = k_ref[...]
    qd = q * jnp.exp(b)
    kd = k * jnp.exp(-b)
    kend = k * jnp.exp(blast_t - b)

    ri = lax.broadcasted_iota(jnp.int32, (rows, rows), 0)
    ci = lax.broadcasted_iota(jnp.int32, (rows, rows), 1)
    causal = jnp.logical_and(((ri - ci) & (BATCH - 1)) == 0, ri >= ci)
    rowb = lax.broadcasted_iota(jnp.int32, (rows, GLA_DKP), 0) & (BATCH - 1)
    laneb = lax.broadcasted_iota(jnp.int32, (GLA_DKP, rows), 1) & (BATCH - 1)

    for h in range(GLA_HEADS):
        sl = slice(h * GLA_DKP, (h + 1) * GLA_DKP)
        vh = v_ref[:, h * GLA_DV:(h + 1) * GLA_DV]
        qh, kdh, keh, bh = qd[:, sl], kd[:, sl], kend[:, sl], b[:, sl]
        att = jnp.where(causal, _dot_nt(qh, kdh), 0.0)
        o = _dot(att, vh)
        qexp = jnp.concatenate([jnp.where(rowb == bb, qh, 0.0) for bb in range(BATCH)], axis=1)
        o = o + _dot(qexp, s_ref[h])
        ket = keh.T
        kexp_t = jnp.concatenate([jnp.where(laneb == bb, ket, 0.0) for bb in range(BATCH)], axis=0)
        kv = _dot(kexp_t, vh)
        bt = bh[rows - GLA_DKP:rows].T
        dec = jnp.concatenate(
            [jnp.broadcast_to(jnp.exp(bt[:, GLA_DKP - BATCH + bb:GLA_DKP - BATCH + bb + 1]),
                              (GLA_DKP, GLA_DV)) for bb in range(BATCH)], axis=0)
        s_ref[h] = s_ref[h] * dec + kv
        o = o * lax.rsqrt(jnp.mean(o * o, axis=-1, keepdims=True) + NORM_EPS)
        o = o * ng_ref[:, h * GLA_DV:(h + 1) * GLA_DV]
        rr = r_ref[:, h * GLA_DV:(h + 1) * GLA_DV]
        o_ref[:, h * GLA_DV:(h + 1) * GLA_DV] = o * (rr * _sigmoid(rr))


def _gla(proj, wg, bg, lm, ng):
    T = proj.shape[0]
    rows = GLA_CHUNK * BATCH
    width = GLA_HEADS * GLA_DKP
    return pl.pallas_call(
        _gla_kernel,
        grid=(T // rows,),
        in_specs=[pl.BlockSpec((rows, 512), lambda i: (i, 1)),
                  pl.BlockSpec((rows, 512), lambda i: (i, 2)),
                  pl.BlockSpec((rows, width), lambda i: (i, 3)),
                  pl.BlockSpec((rows, width), lambda i: (i, 4)),
                  pl.BlockSpec((rows, LANE), lambda i: (i, (3 * 512 + 2 * width) // LANE)),
                  _const_spec(wg.shape), _const_spec(bg.shape), _const_spec(lm.shape),
                  _const_spec(ng.shape)],
        out_specs=pl.BlockSpec((rows, GLA_HEADS * GLA_DV), lambda i: (i, 0)),
        out_shape=jax.ShapeDtypeStruct((T, GLA_HEADS * GLA_DV), F32),
        scratch_shapes=[pltpu.VMEM((GLA_HEADS, BATCH * GLA_DKP, GLA_DV), F32)],
        compiler_params=_params(("arbitrary",)),
        name="gla",
    )(proj, proj, proj, proj, proj, wg, bg, lm, ng)


def _mix_out_kernel(n_in, has_gate, *refs):
    h_ref = refs[0]
    y_refs = refs[1:1 + n_in]
    w_refs = refs[1 + n_in:1 + 2 * n_in]
    rest = refs[1 + 2 * n_in:]
    gate_ref = rest[0] if has_gate else None
    g_ref, ho_ref, hb_ref = rest[1:] if has_gate else rest
    acc = h_ref[...]
    for y_ref, w_ref in zip(y_refs, w_refs):
        y = y_ref[...]
        if has_gate:
            y = y * gate_ref[...]
        acc = acc + _dot(y, w_ref[...])
    ho_ref[...] = acc
    hb_ref[...] = _rms(acc, g_ref[...]).astype(BF16)


def _mix_out(h, ys, ws, gate, g):
    T = h.shape[0]
    tm = 256
    n_in = len(ys)
    row = pl.BlockSpec((tm, D_MODEL), lambda i: (i, 0))
    in_specs = [row]
    in_specs += [pl.BlockSpec((tm, y.shape[1]), lambda i: (i, 0)) for y in ys]
    in_specs += [_const_spec(w.shape) for w in ws]
    args = [h, *ys, *ws]
    if gate is not None:
        in_specs.append(row)
        args.append(gate)
    in_specs.append(_const_spec((1, D_MODEL)))
    args.append(g)
    return pl.pallas_call(
        functools.partial(_mix_out_kernel, n_in, gate is not None),
        grid=(T // tm,),
        in_specs=in_specs,
        out_specs=[row, row],
        out_shape=[jax.ShapeDtypeStruct((T, D_MODEL), F32),
                   jax.ShapeDtypeStruct((T, D_MODEL), BF16)],
        compiler_params=_params(("parallel",)),
        name="mix_out",
    )(*args)


def _insert(tops, x):
    out = []
    for t in tops:
        out.append(jnp.maximum(t, x))
        x = jnp.minimum(t, x)
    return out


def _top_values(scr_ref):
    def body(k, tops):
        x = scr_ref[pl.ds(k, SUBLANE, stride=PEER_NKEYS), :]
        return tuple(_insert(list(tops), x))
    init = tuple(jnp.full((SUBLANE, LANE), -jnp.inf, F32) for _ in range(PEER_TOPK))
    return lax.fori_loop(0, PEER_NKEYS, body, init)


def _peer_topk_kernel(hn_ref, wq_ref, keys_ref, n1_ref, e1_ref, r2_ref, e2_ref,
                      scr1, scr2, scr3, scr4):
    nch = hn_ref.shape[0] // LANE
    q = _dot(hn_ref[...], wq_ref[...])
    for c, scr in enumerate((scr1, scr2)):
        st = _dot_nt(keys_ref[0, c], q[:, c * LANE:(c + 1) * LANE])
        for j in range(nch):
            scr[j * PEER_NKEYS:(j + 1) * PEER_NKEYS, :] = st[:, j * LANE:(j + 1) * LANE]
    a = _top_values(scr1)
    b = _top_values(scr2)
    pairs = [(i, j) for i in range(PEER_TOPK) for j in range(PEER_TOPK) if (i + 1) * (j + 1) <= PEER_TOPK]
    tops = [jnp.full((SUBLANE, LANE), -jnp.inf, F32) for _ in range(PEER_TOPK)]
    for i, j in pairs:
        tops = _insert(tops, a[i] + b[j])
    thr = tops[PEER_TOPK - 1]
    z = jnp.zeros((SUBLANE, LANE), F32)
    for t in tops:
        z = z + jnp.exp(t - tops[0])
    inv_z = 1.0 / z
    cnt = [jnp.zeros((SUBLANE, LANE), F32) for _ in range(PEER_TOPK)]
    for i, j in pairs:
        cnt[i] = cnt[i] + jnp.where(a[i] + b[j] >= thr, GATE_STEP, 0.0)

    def first_half(k, carry):
        rows = pl.ds(k, SUBLANE, stride=PEER_NKEYS)
        x = scr1[rows, :]
        n = jnp.zeros((SUBLANE, LANE), F32)
        for i in range(PEER_TOPK):
            n = jnp.where(x == a[i], cnt[i], n)
        scr3[rows, :] = n
        scr1[rows, :] = jnp.exp(x - a[0]) * inv_z
        return carry

    def second_half(k, carry):
        rows = pl.ds(k, SUBLANE, stride=PEER_NKEYS)
        x = scr2[rows, :]
        rank = jnp.zeros((SUBLANE, LANE), F32)
        for j in range(PEER_TOPK):
            rank = rank + jnp.where(b[j] > x, GATE_STEP, 0.0)
        scr4[rows, :] = rank
        scr2[rows, :] = jnp.exp(x - b[0])
        return carry

    lax.fori_loop(0, PEER_NKEYS, first_half, 0, unroll=2)
    lax.fori_loop(0, PEER_NKEYS, second_half, 0, unroll=2)
    shape = (nch, PEER_NKEYS, LANE)
    n1_ref[0] = scr3[...].reshape(shape)
    e1_ref[0] = scr1[...].reshape(shape)
    r2_ref[0] = scr4[...].astype(BF16).reshape(shape)
    e2_ref[0] = scr2[...].astype(BF16).reshape(shape)


def _peer_topk(hn, wq, keys):
    T = hn.shape[0]
    tt = SUBLANE * LANE
    nch = T // LANE
    shape = (PEER_HEADS, nch, PEER_NKEYS, LANE)
    spec = pl.BlockSpec((1, SUBLANE, PEER_NKEYS, LANE), lambda i, h: (h, i, 0, 0))
    return pl.pallas_call(
        _peer_topk_kernel,
        grid=(T // tt, PEER_HEADS),
        in_specs=[pl.BlockSpec((tt, D_MODEL), lambda i, h: (i, 0)),
                  pl.BlockSpec((D_MODEL, 2 * LANE), lambda i, h: (0, h)),
                  pl.BlockSpec((1, 2, PEER_NKEYS, LANE), lambda i, h: (h, 0, 0, 0))],
        out_specs=[spec, spec, spec, spec],
        out_shape=[jax.ShapeDtypeStruct(shape, F32), jax.ShapeDtypeStruct(shape, F32),
                   jax.ShapeDtypeStruct(shape, BF16), jax.ShapeDtypeStruct(shape, BF16)],
        scratch_shapes=[pltpu.VMEM((tt, LANE), F32)] * 4,
        compiler_params=_params(("parallel", "arbitrary")),
        name="peer_topk",
    )(hn, wq, keys)


def _row_bf16(ref, h, c, i1):
    packed_rows = 2 * SUBLANE
    x = jnp.broadcast_to(ref[h, c, pl.ds(i1, 1), :], (packed_rows, LANE)).astype(BF16)
    return jnp.concatenate([x] * (PEER_NKEYS // packed_rows), axis=0)


def _peer_main_kernel(final, n_sub, hb_ref, hres_ref, u_ref, vt_ref, n1_ref, e1_ref, r2_ref, e2_ref,
                      fg_ref, o_ref, acc_ref, hg_ref, ht_ref):
    e = pl.program_id(1)
    tt = hb_ref.shape[0]
    n_i1 = PEER_SUB // PEER_NKEYS
    n_ch = tt // LANE

    @pl.when(e == 0)
    def _():
        acc_ref[...] = jnp.zeros_like(acc_ref)

    zero = jnp.zeros((PEER_NKEYS, LANE), BF16)

    def first_matmul(j, slot):
        start = j * PEER_SUB if isinstance(j, int) else pl.multiple_of(j * PEER_SUB, PEER_SUB)
        ht_ref[slot] = _dot_nt(u_ref[pl.ds(start, PEER_SUB), :], hb_ref[...])

    def second_matmul(j, slot):
        acc_ref[...] += _dot(vt_ref[j], hg_ref[slot])

    def gates(j, slot):
        for il in range(n_i1):
            i1 = (e * n_sub + j) * n_i1 + il
            for c in range(n_ch):
                gate = zero
                for h in range(PEER_HEADS):
                    code = _row_bf16(n1_ref, h, c, i1) - r2_ref[h, c]
                    w = _row_bf16(e1_ref, h, c, i1) * e2_ref[h, c]
                    gate = gate + jnp.maximum(jnp.minimum(w, code), zero)
                rows = slice(il * PEER_NKEYS, (il + 1) * PEER_NKEYS)
                cols = slice(c * LANE, (c + 1) * LANE)
                hg_ref[slot, rows, cols] = _gelu(ht_ref[slot, rows, cols]).astype(BF16) * gate

    first_matmul(0, 0)
    first_matmul(1, 1)
    gates(0, 0)

    def body(p, carry):
        for j, slot in ((2 * p + 1, 1), (2 * p + 2, 0)):
            second_matmul(j - 1, 1 - slot)
            gates(j, slot)
            first_matmul(j + 1, 1 - slot)
        return carry

    assert n_sub % 2 == 0
    lax.fori_loop(0, (n_sub - 2) // 2, body, 0)
    last = n_sub - 1
    second_matmul(last - 1, 1 - last % 2)
    gates(last, last % 2)
    second_matmul(last, last % 2)

    @pl.when(e == pl.num_programs(1) - 1)
    def _():
        out = hres_ref[...] + acc_ref[...].T
        if final:
            out = _rms(out, fg_ref[...])
        o_ref[...] = out


def _peer_main(hb, hres, u, vt, n1, e1, r2, e2, fg, final):
    T = hb.shape[0]
    E = u.shape[0]
    tt, et = 512, 2048
    n_sub = et // PEER_SUB
    big_spec = pl.BlockSpec((PEER_HEADS, tt // LANE, PEER_NKEYS, LANE), lambda i, e: (0, i, 0, 0))
    return pl.pallas_call(
        functools.partial(_peer_main_kernel, final, n_sub),
        grid=(T // tt, E // et),
        in_specs=[pl.BlockSpec((tt, D_MODEL), lambda i, e: (i, 0)),
                  pl.BlockSpec((tt, D_MODEL), lambda i, e: (i, 0)),
                  pl.BlockSpec((et, D_MODEL), lambda i, e: (e, 0)),
                  pl.BlockSpec((n_sub, D_MODEL, PEER_SUB), lambda i, e: (e, 0, 0)),
                  big_spec, big_spec, big_spec, big_spec,
                  _const_spec((1, D_MODEL))],
        out_specs=pl.BlockSpec((tt, D_MODEL), lambda i, e: (i, 0)),
        out_shape=jax.ShapeDtypeStruct((T, D_MODEL), F32),
        scratch_shapes=[pltpu.VMEM((D_MODEL, tt), F32), pltpu.VMEM((2, PEER_SUB, tt), BF16),
                        pltpu.VMEM((2, PEER_SUB, tt), F32)],
        compiler_params=_params(("parallel", "arbitrary")),
        name="peer_main",
    )(hb, hres, u, vt, n1, e1, r2, e2, fg)


def _peer_layer(h, hb, wq, keys, u_tab, v_tab, fg, final):
    n1, e1, r2, e2 = _peer_topk(hb, wq.astype(BF16), keys.astype(BF16))
    vt = v_tab.astype(BF16).reshape(-1, PEER_SUB, D_MODEL).transpose(0, 2, 1)
    return _peer_main(hb, h, u_tab.astype(BF16), vt, n1, e1, r2, e2, fg, final)


def _rwkv_front_kernel(h_ref, hp_ref, g_ref, mu_ref, wr_ref, wk_ref, wv_ref, w0_ref, ww1_ref, ww2_ref,
                       a0_ref, wa1_ref, wa2_ref, wg1_ref, wg2_ref,
                       r_ref, k_ref, v_ref, w_ref, a_ref, go_ref):
    g = g_ref[...]
    x = _rms(h_ref[...], g)
    xp8 = _rms(hp_ref[...], g)
    xp8 = jnp.where(pl.program_id(0) == 0, 0.0, xp8)
    rows = x.shape[0]
    xprev = jnp.concatenate([xp8, x[:rows - BATCH]], axis=0)
    xx = xprev - x
    mix = lambda j: x + xx * mu_ref[j:j + 1, :]
    r_ref[...] = _dot(mix(0), wr_ref[...])
    w_ref[...] = w0_ref[...] + _dot(jnp.tanh(_dot(mix(1), ww1_ref[...])), ww2_ref[...])
    k_ref[...] = _dot(mix(2), wk_ref[...])
    v_ref[...] = _dot(mix(3), wv_ref[...])
    a_ref[...] = _sigmoid(a0_ref[...] + _dot(_dot(mix(4), wa1_ref[...]), wa2_ref[...]))
    go_ref[...] = _dot(_sigmoid(_dot(mix(5), wg1_ref[...])), wg2_ref[...])


def _rwkv_front(h, g, mu, wr, wk, wv, w0, ww1, ww2, a0, wa1, wa2, wg1, wg2):
    T = h.shape[0]
    tm = 128
    row = pl.BlockSpec((tm, D_MODEL), lambda i: (i, 0))
    prev = pl.BlockSpec((BATCH, D_MODEL), lambda i: (jnp.maximum(i * (tm // BATCH) - 1, 0), 0))
    consts = [g, mu, wr, wk, wv, w0, ww1, ww2, a0, wa1, wa2, wg1, wg2]
    out = jax.ShapeDtypeStruct((T, D_MODEL), F32)
    return pl.pallas_call(
        _rwkv_front_kernel,
        grid=(T // tm,),
        in_specs=[row, prev] + [_const_spec(c.shape) for c in consts],
        out_specs=[row] * 6,
        out_shape=[out] * 6,
        compiler_params=_params(("parallel",)),
        name="rwkv_front",
    )(h, h, *consts)


def _rwkv_scan_kernel(r_ref, k_ref, v_ref, w_ref, a_ref, kk_ref, ka_ref, rk_ref, lg_ref, lb_ref,
                      y_ref, s_ref, dec_s, k2_s, av_s, bv_s, wr_s, br_s, kr_s):
    tl = r_ref.shape[0]
    n = RWKV_HEAD
    ng = n // SUBLANE

    @pl.when(pl.program_id(0) == 0)
    def _():
        s_ref[...] = jnp.zeros_like(s_ref)

    r = r_ref[...]
    k = k_ref[...]
    a = a_ref[...]
    kk = k * kk_ref[...]
    kk = kk / jnp.maximum(jnp.sqrt(jnp.sum(kk * kk, axis=1, keepdims=True)), 1e-12)
    k2 = k * (1.0 + (a - 1.0) * ka_ref[...])
    dec = jnp.exp(-jnp.exp(-_softplus(-w_ref[...]) - 0.5))
    bv = kk * a
    dec_s[...] = dec
    k2_s[...] = k2
    av_s[...] = -kk
    bv_s[...] = bv
    wr_s[...] = dec * r
    br_s[...] = jnp.sum(bv * r, axis=1, keepdims=True)
    kr_s[...] = jnp.sum(k2 * r, axis=1, keepdims=True)

    def row(ref, t, j):
        return jnp.broadcast_to(ref[t, pl.ds(j, 1), :], (SUBLANE, LANE))

    def step(t, carry):
        def reduce_body(j, accs):
            sa, sy = accs
            a_b = row(av_s, t, j)
            wr_b = row(wr_s, t, j)
            sa_n, sy_n = [], []
            for vg in range(ng):
                sv = s_ref[j, vg * SUBLANE:(vg + 1) * SUBLANE, :]
                sa_n.append(sa[vg] + sv * a_b)
                sy_n.append(sy[vg] + sv * wr_b)
            return tuple(sa_n), tuple(sy_n)

        zeros = tuple(jnp.zeros((SUBLANE, LANE), F32) for _ in range(ng))
        sa, sy = lax.fori_loop(0, n, reduce_body, (zeros, zeros), unroll=4)
        br = jnp.broadcast_to(br_s[t], (SUBLANE, LANE))
        kr = jnp.broadcast_to(kr_s[t], (SUBLANE, LANE))
        vv = [v_ref[t, vg * SUBLANE:(vg + 1) * SUBLANE, :] for vg in range(ng)]
        for vg in range(ng):
            y_ref[t, vg * SUBLANE:(vg + 1) * SUBLANE, :] = sy[vg] + sa[vg] * br + vv[vg] * kr

        def update_body(j, c):
            w_b = row(dec_s, t, j)
            b_b = row(bv_s, t, j)
            k_b = row(k2_s, t, j)
            for vg in range(ng):
                sl = slice(vg * SUBLANE, (vg + 1) * SUBLANE)
                s_ref[j, sl, :] = s_ref[j, sl, :] * w_b + sa[vg] * b_b + vv[vg] * k_b
            return c

        lax.fori_loop(0, n, update_body, 0, unroll=4)
        return carry

    lax.fori_loop(0, tl, step, 0)

    y = y_ref[...]
    mean = jnp.mean(y, axis=1, keepdims=True)
    var = jnp.mean((y - mean) ** 2, axis=1, keepdims=True)
    y = (y - mean) * lax.rsqrt(var + RWKV_GN_EPS) * lg_ref[...] + lb_ref[...]
    bonus = jnp.sum(r * k2 * rk_ref[...], axis=1, keepdims=True) * v_ref[...]
    y_ref[...] = y + bonus


def _rwkv_scan(r, k, v, w, a, kk, ka, rk, lg, lb):
    L = r.shape[0]
    tl = 32
    n = RWKV_HEAD
    blk = pl.BlockSpec((tl, n, LANE), lambda i: (i, 0, 0))
    par = _const_spec((n, LANE))
    big = pltpu.VMEM((tl, n, LANE), F32)
    small = pltpu.VMEM((tl, 1, LANE), F32)
    return pl.pallas_call(
        _rwkv_scan_kernel,
        grid=(L // tl,),
        in_specs=[blk] * 5 + [par] * 5,
        out_specs=blk,
        out_shape=jax.ShapeDtypeStruct((L, n, LANE), F32),
        scratch_shapes=[pltpu.VMEM((n, n, LANE), F32), big, big, big, big, big, small, small],
        compiler_params=_params(("arbitrary",)),
        name="rwkv_scan",
    )(r, k, v, w, a, kk, ka, rk, lg, lb)


def _to_scan_layout(x, L):
    return x.reshape(L, BATCH, RWKV_HEADS, RWKV_HEAD).transpose(0, 3, 1, 2).reshape(L, RWKV_HEAD, LANE)


def _from_scan_layout(y, L):
    return y.reshape(L, RWKV_HEAD, BATCH, RWKV_HEADS).transpose(0, 2, 3, 1).reshape(L * BATCH, D_MODEL)


def _head_param(p):
    p = p.reshape(RWKV_HEADS, RWKV_HEAD).T
    return jnp.tile(p, (1, BATCH))


def _prep_even(w_in, a_re, a_im, log_dt, b_re, b_im, c_re, c_im, w_g2, b_g2):
    o = 0
    w_u = w_in[:, o:o + 512]; o += 512
    w_q = w_in[:, o:o + 256]; o += 256
    w_k = w_in[:, o:o + 256]; o += 256
    w_v = w_in[:, o:o + 512]; o += 512
    w_gl = w_in[:, o:o + GLA_RANK]; o += GLA_RANK
    w_r = w_in[:, o:o + 512]

    def pad_heads(w):
        w = w.reshape(-1, GLA_HEADS, GLA_DK)
        w = jnp.pad(w, ((0, 0), (0, 0), (0, GLA_DKP - GLA_DK)))
        return w.reshape(-1, GLA_HEADS * GLA_DKP)

    w_proj = jnp.concatenate([w_u, w_v, w_r, pad_heads(w_q), pad_heads(w_k),
                              jnp.pad(w_gl, ((0, 0), (0, LANE - GLA_RANK)))], axis=1)
    wg = jnp.pad(pad_heads(w_g2), ((0, LANE - GLA_RANK), (0, 0)))
    bg = pad_heads(b_g2[None, :])
    lm = pad_heads(jnp.ones((1, GLA_HEADS * GLA_DK), F32))

    dt = jnp.exp(log_dt)
    mag = jnp.exp(dt * a_re)
    abar_re = mag * jnp.cos(dt * a_im)
    abar_im = mag * jnp.sin(dt * a_im)
    nr = abar_re - 1.0
    den = a_re * a_re + a_im * a_im
    f_re = (nr * a_re + abar_im * a_im) / den
    f_im = (abar_im * a_re - nr * a_im) / den
    bbar_re = f_re[..., None] * b_re - f_im[..., None] * b_im
    bbar_im = f_re[..., None] * b_im + f_im[..., None] * b_re
    eye = jnp.eye(8, dtype=F32)
    gs = S5_GROUPS // S5_SLABS

    def bd(bb):
        bb = bb.reshape(S5_SLABS, gs, S5_STATE, S5_GROUP)
        return jnp.einsum('sgpc,gh->sgchp', bb, eye).reshape(S5_SLABS, gs * S5_GROUP, gs * S5_STATE)

    def cdm(cc):
        cc = cc.reshape(S5_SLABS, gs, S5_GROUP, S5_STATE)
        return jnp.einsum('sgcp,gh->sgphc', cc, eye).reshape(S5_SLABS, gs * S5_STATE, gs * S5_GROUP)

    bd_all = jnp.concatenate([bd(bbar_re), bd(bbar_im)], axis=2)
    cd_all = jnp.concatenate([cdm(c_re), -cdm(c_im)], axis=1)
    are = jnp.broadcast_to(abar_re.reshape(S5_SLABS, 1, gs * S5_STATE), (S5_SLABS, BATCH, gs * S5_STATE))
    aim = jnp.broadcast_to(abar_im.reshape(S5_SLABS, 1, gs * S5_STATE), (S5_SLABS, BATCH, gs * S5_STATE))
    return w_proj, wg, bg, lm, bd_all, cd_all, are, aim


def kernel(x, norm_mix_g, norm_ffn_g, final_g, e_w_in, e_w_out, s5_a_re, s5_a_im, s5_log_dt, s5_b_re, s5_b_im, s5_c_re, s5_c_im, s5_d, s5_w_glu, gla_w_g2, gla_b_g2, gla_norm_g, o_mu, o_w_r, o_w_k, o_w_v, o_w0, o_w_w1, o_w_w2, o_a0, o_w_a1, o_w_a2, o_w_g1, o_w_g2, o_k_k, o_k_a, o_r_k, o_lnx_g, o_lnx_b, o_w_o, peer_w_q, peer_sub_keys, peer_u, peer_v):
    bsz, L, D = x.shape
    assert bsz == BATCH and D == D_MODEL
    T = bsz * L
    row = lambda p: p.reshape(1, -1)
    h = x.transpose(1, 0, 2).reshape(T, D)

    w_proj, wg, bg, lm, bd_all, cd_all, are, aim = _prep_even(
        e_w_in[0], s5_a_re[0], s5_a_im[0], s5_log_dt[0], s5_b_re[0], s5_b_im[0],
        s5_c_re[0], s5_c_im[0], gla_w_g2[0], gla_b_g2[0])
    bf = lambda p: p.astype(BF16)
    proj = _even_front(h, row(norm_mix_g[0]), bf(w_proj))
    y_s5 = _s5(proj, bf(bd_all), bf(cd_all), are, aim, row(s5_d[0]), bf(s5_w_glu[0]))
    y_gla = _gla(proj, wg, bg, lm, row(gla_norm_g[0]))
    w_out = bf(e_w_out[0])
    h, hb = _mix_out(h, [y_s5, y_gla], [w_out[:S5_WIDTH], w_out[S5_WIDTH:]], None, row(norm_ffn_g[0]))
    h = _peer_layer(h, hb, peer_w_q[0], peer_sub_keys[0], peer_u[0], peer_v[0], row(final_g), False)

    r, k, v, w, a, g = _rwkv_front(
        h, row(norm_mix_g[1]), o_mu[0], bf(o_w_r[0]), bf(o_w_k[0]), bf(o_w_v[0]), row(o_w0[0]),
        bf(o_w_w1[0]), bf(o_w_w2[0]), row(o_a0[0]), bf(o_w_a1[0]), bf(o_w_a2[0]), bf(o_w_g1[0]), bf(o_w_g2[0]))
    sl = lambda t: _to_scan_layout(t, L)
    y = _rwkv_scan(sl(r), sl(k), sl(v), sl(w), sl(a), _head_param(o_k_k[0]), _head_param(o_k_a[0]),
                   _head_param(o_r_k[0].reshape(-1)), _head_param(o_lnx_g[0]), _head_param(o_lnx_b[0]))
    y = _from_scan_layout(y, L)
    h, hb = _mix_out(h, [y], [bf(o_w_o[0])], g, row(norm_ffn_g[1]))
    out = _peer_layer(h, hb, peer_w_q[1], peer_sub_keys[1], peer_u[1], peer_v[1], row(final_g), True)
    return out.reshape(L, bsz, D).transpose(1, 0, 2)
```

```python
import functools
import math

import jax
import jax.numpy as jnp
from jax import lax
from jax.experimental import pallas as pl
from jax.experimental.pallas import tpu as pltpu

F32 = jnp.float32
BF16 = jnp.bfloat16
HIGHEST = lax.Precision.HIGHEST

D_MODEL = 1024
BATCH = 8
NORM_EPS = 1e-6
S5_WIDTH = 512
S5_GROUP = 16
S5_GROUPS = 32
S5_STATE = 64
S5_SLABS = 4
GLA_HEADS = 4
GLA_DV = 128
GLA_DK = 64
GLA_DKP = 128
GLA_RANK = 16
GLA_TAU = 16.0
GLA_CHUNK = 64
RWKV_HEAD = 64
RWKV_HEADS = 16
RWKV_GN_EPS = 64e-5
PEER_HEADS = 8
PEER_NKEYS = 128
PEER_TOPK = 16
PEER_SUB = 256
GATE_STEP = 2.0
LANE = 128
SUBLANE = 8
VMEM_LIMIT = 56 * 1024 * 1024

PROJ_W = 3 * 512 + 2 * GLA_HEADS * GLA_DKP + LANE


def _dot(a, b, hi=False):
    if hi:
        return jnp.dot(a, b, preferred_element_type=F32, precision=HIGHEST)
    return jnp.dot(a.astype(BF16), b.astype(BF16), preferred_element_type=F32)


def _dot_nt(a, b):
    return lax.dot_general(a.astype(BF16), b.astype(BF16), (((1,), (1,)), ((), ())),
                           preferred_element_type=F32)


def _rms(x, g):
    return x * lax.rsqrt(jnp.mean(x * x, axis=-1, keepdims=True) + NORM_EPS) * g


def _sigmoid(x):
    return 1.0 / (1.0 + jnp.exp(-x))


def _softplus(x):
    return jnp.maximum(x, 0.0) + jnp.log(1.0 + jnp.exp(-jnp.abs(x)))


def _gelu(x):
    return 0.5 * x * (1.0 + lax.erf(x * (1.0 / math.sqrt(2.0))))


def _const_spec(shape):
    nd = len(shape)
    return pl.BlockSpec(shape, lambda *_: (0,) * nd)


def _params(sem):
    return pltpu.CompilerParams(dimension_semantics=sem, vmem_limit_bytes=VMEM_LIMIT)


def _even_front_kernel(h_ref, g_ref, w_ref, o_ref):
    xn = _rms(h_ref[...], g_ref[...])
    o_ref[...] = _dot(xn, w_ref[...])


def _even_front(h, g, w):
    T = h.shape[0]
    tm = 256
    return pl.pallas_call(
        _even_front_kernel,
        grid=(T // tm,),
        in_specs=[pl.BlockSpec((tm, D_MODEL), lambda i: (i, 0)),
                  _const_spec((1, D_MODEL)),
                  _const_spec((D_MODEL, PROJ_W))],
        out_specs=pl.BlockSpec((tm, PROJ_W), lambda i: (i, 0)),
        out_shape=jax.ShapeDtypeStruct((T, PROJ_W), F32),
        compiler_params=_params(("parallel",)),
        name="even_front",
    )(h, g, w)


def _s5_kernel(u_ref, bd_ref, cd_ref, are_ref, aim_ref, d_ref, wglu_ref, o_ref, bu_ref, st_ref):
    rows = u_ref.shape[0]
    steps = rows // BATCH
    half = 512

    @pl.when(pl.program_id(0) == 0)
    def _():
        st_ref[...] = jnp.zeros_like(st_ref)

    u = u_ref[...]
    for s in range(S5_SLABS):
        bu_ref[s] = _dot(u[:, s * LANE:(s + 1) * LANE], bd_ref[s])

    def step(t, carry):
        r0 = pl.multiple_of(t * BATCH, BATCH)
        out = []
        for s in range(S5_SLABS):
            sre, sim = carry[2 * s], carry[2 * s + 1]
            ar, ai = are_ref[s], aim_ref[s]
            bre = bu_ref[s, pl.ds(r0, BATCH), 0:half]
            bim = bu_ref[s, pl.ds(r0, BATCH), half:2 * half]
            nre = ar * sre - ai * sim + bre
            nim = ar * sim + ai * sre + bim
            bu_ref[s, pl.ds(r0, BATCH), 0:half] = nre
            bu_ref[s, pl.ds(r0, BATCH), half:2 * half] = nim
            out += [nre, nim]
        return tuple(out)

    init = tuple(st_ref[j] for j in range(2 * S5_SLABS))
    fin = lax.fori_loop(0, steps, step, init)
    for j in range(2 * S5_SLABS):
        st_ref[j] = fin[j]

    y = jnp.concatenate([_dot(bu_ref[s], cd_ref[s]) for s in range(S5_SLABS)], axis=1)
    y = y + d_ref[...] * u
    g = _gelu(y)
    o_ref[...] = g * _sigmoid(_dot(g, wglu_ref[...]))


def _s5(proj, bd, cd, are, aim, d, wglu):
    T = proj.shape[0]
    rows = 256
    return pl.pallas_call(
        _s5_kernel,
        grid=(T // rows,),
        in_specs=[pl.BlockSpec((rows, S5_WIDTH), lambda i: (i, 0)),
                  _const_spec(bd.shape), _const_spec(cd.shape),
                  _const_spec(are.shape), _const_spec(aim.shape),
                  _const_spec(d.shape), _const_spec(wglu.shape)],
        out_specs=pl.BlockSpec((rows, S5_WIDTH), lambda i: (i, 0)),
        out_shape=jax.ShapeDtypeStruct((T, S5_WIDTH), F32),
        scratch_shapes=[pltpu.VMEM((S5_SLABS, rows, 2 * 512), F32),
                        pltpu.VMEM((2 * S5_SLABS, BATCH, 512), F32)],
        compiler_params=_params(("arbitrary",)),
        name="s5",
    )(proj, bd, cd, are, aim, d, wglu)


def _gla_kernel(v_ref, r_ref, q_ref, k_ref, gl_ref, wg_ref, bg_ref, lm_ref, ng_ref, o_ref, s_ref):
    rows = q_ref.shape[0]
    width = GLA_HEADS * GLA_DKP

    @pl.when(pl.program_id(0) == 0)
    def _():
        s_ref[...] = jnp.zeros_like(s_ref)

    z = _dot(gl_ref[...], wg_ref[...], True) + bg_ref[...]
    la = -_softplus(-z) * (1.0 / GLA_TAU) * lm_ref[...]
    b = la
    sh = BATCH
    while sh < rows:
        b = b + jnp.concatenate([jnp.zeros((sh, width), F32), b[:rows - sh]], axis=0)
        sh *= 2
    blast = b[rows - BATCH:rows]
    blast_t = jnp.concatenate([blast] * GLA_CHUNK, axis=0)
    q = q_ref[...] * (GLA_DK ** -0.5)
    k = k_ref[...]
    qd = q * jnp.exp(b)
    kd = k * jnp.exp(-b)
    kend = k * jnp.exp(blast_t - b)

    ri = lax.broadcasted_iota(jnp.int32, (rows, rows), 0)
    ci = lax.broadcasted_iota(jnp.int32, (rows, rows), 1)
    causal = jnp.logical_and(((ri - ci) & (BATCH - 1)) == 0, ri >= ci)
    rowb = lax.broadcasted_iota(jnp.int32, (rows, GLA_DKP), 0) & (BATCH - 1)
    laneb = lax.broadcasted_iota(jnp.int32, (GLA_DKP, rows), 1) & (BATCH - 1)

    for h in range(GLA_HEADS):
        sl = slice(h * GLA_DKP, (h + 1) * GLA_DKP)
        vh = v_ref[:, h * GLA_DV:(h + 1) * GLA_DV]
        qh, kdh, keh, bh = qd[:, sl], kd[:, sl], kend[:, sl], b[:, sl]
        att = jnp.where(causal, _dot_nt(qh, kdh), 0.0)
        o = _dot(att, vh)
        qexp = jnp.concatenate([jnp.where(rowb == bb, qh, 0.0) for bb in range(BATCH)], axis=1)
        o = o + _dot(qexp, s_ref[h])
        ket = keh.T
        kexp_t = jnp.concatenate([jnp.where(laneb == bb, ket, 0.0) for bb in range(BATCH)], axis=0)
        kv = _dot(kexp_t, vh)
        bt = bh[rows - GLA_DKP:rows].T
        dec = jnp.concatenate(
            [jnp.broadcast_to(jnp.exp(bt[:, GLA_DKP - BATCH + bb:GLA_DKP - BATCH + bb + 1]),
                              (GLA_DKP, GLA_DV)) for bb in range(BATCH)], axis=0)
        s_ref[h] = s_ref[h] * dec + kv
        o = o * lax.rsqrt(jnp.mean(o * o, axis=-1, keepdims=True) + NORM_EPS)
        o = o * ng_ref[:, h * GLA_DV:(h + 1) * GLA_DV]
        rr = r_ref[:, h * GLA_DV:(h + 1) * GLA_DV]
        o_ref[:, h * GLA_DV:(h + 1) * GLA_DV] = o * (rr * _sigmoid(rr))


def _gla(proj, wg, bg, lm, ng):
    T = proj.shape[0]
    rows = GLA_CHUNK * BATCH
    width = GLA_HEADS * GLA_DKP
    return pl.pallas_call(
        _gla_kernel,
        grid=(T // rows,),
        in_specs=[pl.BlockSpec((rows, 512), lambda i: (i, 1)),
                  pl.BlockSpec((rows, 512), lambda i: (i, 2)),
                  pl.BlockSpec((rows, width), lambda i: (i, 3)),
                  pl.BlockSpec((rows, width), lambda i: (i, 4)),
                  pl.BlockSpec((rows, LANE), lambda i: (i, (3 * 512 + 2 * width) // LANE)),
                  _const_spec(wg.shape), _const_spec(bg.shape), _const_spec(lm.shape),
                  _const_spec(ng.shape)],
        out_specs=pl.BlockSpec((rows, GLA_HEADS * GLA_DV), lambda i: (i, 0)),
        out_shape=jax.ShapeDtypeStruct((T, GLA_HEADS * GLA_DV), F32),
        scratch_shapes=[pltpu.VMEM((GLA_HEADS, BATCH * GLA_DKP, GLA_DV), F32)],
        compiler_params=_params(("arbitrary",)),
        name="gla",
    )(proj, proj, proj, proj, proj, wg, bg, lm, ng)


def _mix_out_kernel(n_in, has_gate, *refs):
    h_ref = refs[0]
    y_refs = refs[1:1 + n_in]
    w_refs = refs[1 + n_in:1 + 2 * n_in]
    rest = refs[1 + 2 * n_in:]
    gate_ref = rest[0] if has_gate else None
    g_ref, ho_ref, hb_ref = rest[1:] if has_gate else rest
    acc = h_ref[...]
    for y_ref, w_ref in zip(y_refs, w_refs):
        y = y_ref[...]
        if has_gate:
            y = y * gate_ref[...]
        acc = acc + _dot(y, w_ref[...])
    ho_ref[...] = acc
    hb_ref[...] = _rms(acc, g_ref[...]).astype(BF16)


def _mix_out(h, ys, ws, gate, g):
    T = h.shape[0]
    tm = 256
    n_in = len(ys)
    row = pl.BlockSpec((tm, D_MODEL), lambda i: (i, 0))
    in_specs = [row]
    in_specs += [pl.BlockSpec((tm, y.shape[1]), lambda i: (i, 0)) for y in ys]
    in_specs += [_const_spec(w.shape) for w in ws]
    args = [h, *ys, *ws]
    if gate is not None:
        in_specs.append(row)
        args.append(gate)
    in_specs.append(_const_spec((1, D_MODEL)))
    args.append(g)
    return pl.pallas_call(
        functools.partial(_mix_out_kernel, n_in, gate is not None),
        grid=(T // tm,),
        in_specs=in_specs,
        out_specs=[row, row],
        out_shape=[jax.ShapeDtypeStruct((T, D_MODEL), F32),
                   jax.ShapeDtypeStruct((T, D_MODEL), BF16)],
        compiler_params=_params(("parallel",)),
        name="mix_out",
    )(*args)


def _tree_sum(xs):
    while len(xs) > 1:
        xs = [xs[i] + xs[i + 1] for i in range(0, len(xs) - 1, 2)] + ([xs[-1]] if len(xs) % 2 else [])
    return xs[0]


def _insert(tops, x):
    out = []
    for t in tops:
        out.append(jnp.maximum(t, x))
        x = jnp.minimum(t, x)
    return out


def _top_values(scr_ref):
    def body(k, tops):
        x = scr_ref[pl.ds(k, SUBLANE, stride=PEER_NKEYS), :]
        return tuple(_insert(list(tops), x))
    init = tuple(jnp.full((SUBLANE, LANE), -jnp.inf, F32) for _ in range(PEER_TOPK))
    return lax.fori_loop(0, PEER_NKEYS, body, init, unroll=8)


def _peer_topk_kernel(hn_ref, wq_ref, keys_ref, n1_ref, e1_ref, r2_ref, e2_ref,
                      scr1, scr2, scr3, scr4):
    nch = hn_ref.shape[0] // LANE
    q = _dot(hn_ref[...], wq_ref[...])
    for c, scr in enumerate((scr1, scr2)):
        st = _dot_nt(keys_ref[0, c], q[:, c * LANE:(c + 1) * LANE])
        for j in range(nch):
            scr[j * PEER_NKEYS:(j + 1) * PEER_NKEYS, :] = st[:, j * LANE:(j + 1) * LANE]
    a = _top_values(scr1)
    b = _top_values(scr2)
    pairs = [(i, j) for i in range(PEER_TOPK) for j in range(PEER_TOPK) if (i + 1) * (j + 1) <= PEER_TOPK]
    tops = [jnp.full((SUBLANE, LANE), -jnp.inf, F32) for _ in range(PEER_TOPK)]
    for i, j in pairs:
        tops = _insert(tops, a[i] + b[j])
    thr = tops[PEER_TOPK - 1]
    z = jnp.zeros((SUBLANE, LANE), F32)
    for t in tops:
        z = z + jnp.exp(t - tops[0])
    inv_z = 1.0 / z
    cnt = [jnp.zeros((SUBLANE, LANE), F32) for _ in range(PEER_TOPK)]
    for i, j in pairs:
        cnt[i] = cnt[i] + jnp.where(a[i] + b[j] >= thr, GATE_STEP, 0.0)

    def first_half(k, carry):
        rows = pl.ds(k, SUBLANE, stride=PEER_NKEYS)
        x = scr1[rows, :]
        scr3[rows, :] = _tree_sum([jnp.where(x == a[i], cnt[i], 0.0) for i in range(PEER_TOPK)])
        scr1[rows, :] = jnp.exp(x - a[0]) * inv_z
        return carry

    def second_half(k, carry):
        rows = pl.ds(k, SUBLANE, stride=PEER_NKEYS)
        x = scr2[rows, :]
        scr4[rows, :] = _tree_sum([jnp.where(b[j] > x, GATE_STEP, 0.0) for j in range(PEER_TOPK)])
        scr2[rows, :] = jnp.exp(x - b[0])
        return carry

    lax.fori_loop(0, PEER_NKEYS, first_half, 0, unroll=4)
    lax.fori_loop(0, PEER_NKEYS, second_half, 0, unroll=4)
    shape = (nch, PEER_NKEYS, LANE)
    n1_ref[0] = scr3[...].reshape(shape)
    e1_ref[0] = scr1[...].reshape(shape)
    r2_ref[0] = scr4[...].astype(BF16).reshape(shape)
    e2_ref[0] = scr2[...].astype(BF16).reshape(shape)


def _peer_topk(hn, wq, keys):
    T = hn.shape[0]
    tt = SUBLANE * LANE
    nch = T // LANE
    shape = (PEER_HEADS, nch, PEER_NKEYS, LANE)
    spec = pl.BlockSpec((1, SUBLANE, PEER_NKEYS, LANE), lambda i, h: (h, i, 0, 0))
    return pl.pallas_call(
        _peer_topk_kernel,
        grid=(T // tt, PEER_HEADS),
        in_specs=[pl.BlockSpec((tt, D_MODEL), lambda i, h: (i, 0)),
                  pl.BlockSpec((D_MODEL, 2 * LANE), lambda i, h: (0, h)),
                  pl.BlockSpec((1, 2, PEER_NKEYS, LANE), lambda i, h: (h, 0, 0, 0))],
        out_specs=[spec, spec, spec, spec],
        out_shape=[jax.ShapeDtypeStruct(shape, F32), jax.ShapeDtypeStruct(shape, F32),
                   jax.ShapeDtypeStruct(shape, BF16), jax.ShapeDtypeStruct(shape, BF16)],
        scratch_shapes=[pltpu.VMEM((tt, LANE), F32)] * 4,
        compiler_params=_params(("parallel", "arbitrary")),
        name="peer_topk",
    )(hn, wq, keys)


def _row_bf16(ref, h, c, i1):
    packed_rows = 2 * SUBLANE
    x = jnp.broadcast_to(ref[h, c, pl.ds(i1, 1), :], (packed_rows, LANE)).astype(BF16)
    return jnp.concatenate([x] * (PEER_NKEYS // packed_rows), axis=0)


def _peer_main_kernel(final, n_sub, hb_ref, hres_ref, u_ref, vt_ref, n1_ref, e1_ref, r2_ref, e2_ref,
                      fg_ref, o_ref, acc_ref, hg_ref, ht_ref):
    e = pl.program_id(1)
    tt = hb_ref.shape[0]
    n_i1 = PEER_SUB // PEER_NKEYS
    n_ch = tt // LANE

    @pl.when(e == 0)
    def _():
        acc_ref[...] = jnp.zeros_like(acc_ref)

    zero = jnp.zeros((PEER_NKEYS, LANE), BF16)

    def first_matmul(j, slot):
        start = j * PEER_SUB if isinstance(j, int) else pl.multiple_of(j * PEER_SUB, PEER_SUB)
        ht_ref[slot] = _dot_nt(u_ref[pl.ds(start, PEER_SUB), :], hb_ref[...])

    def second_matmul(j, slot):
        acc_ref[...] += _dot(vt_ref[j], hg_ref[slot])

    def gates(j, slot):
        for il in range(n_i1):
            i1 = (e * n_sub + j) * n_i1 + il
            for c in range(n_ch):
                gate = zero
                for h in range(PEER_HEADS):
                    code = _row_bf16(n1_ref, h, c, i1) - r2_ref[h, c]
                    w = _row_bf16(e1_ref, h, c, i1) * e2_ref[h, c]
                    gate = gate + jnp.maximum(jnp.minimum(w, code), zero)
                rows = slice(il * PEER_NKEYS, (il + 1) * PEER_NKEYS)
                cols = slice(c * LANE, (c + 1) * LANE)
                hg_ref[slot, rows, cols] = _gelu(ht_ref[slot, rows, cols]).astype(BF16) * gate

    first_matmul(0, 0)
    first_matmul(1, 1)
    gates(0, 0)

    def body(p, carry):
        for j, slot in ((2 * p + 1, 1), (2 * p + 2, 0)):
            second_matmul(j - 1, 1 - slot)
            gates(j, slot)
            first_matmul(j + 1, 1 - slot)
        return carry

    assert n_sub % 2 == 0
    lax.fori_loop(0, (n_sub - 2) // 2, body, 0)
    last = n_sub - 1
    second_matmul(last - 1, 1 - last % 2)
    gates(last, last % 2)
    second_matmul(last, last % 2)

    @pl.when(e == pl.num_programs(1) - 1)
    def _():
        out = hres_ref[...] + acc_ref[...].T
        if final:
            out = _rms(out, fg_ref[...])
        o_ref[...] = out


def _peer_main(hb, hres, u, vt, n1, e1, r2, e2, fg, final):
    T = hb.shape[0]
    E = u.shape[0]
    tt, et = 512, 2048
    n_sub = et // PEER_SUB
    big_spec = pl.BlockSpec((PEER_HEADS, tt // LANE, PEER_NKEYS, LANE), lambda i, e: (0, i, 0, 0))
    return pl.pallas_call(
        functools.partial(_peer_main_kernel, final, n_sub),
        grid=(T // tt, E // et),
        in_specs=[pl.BlockSpec((tt, D_MODEL), lambda i, e: (i, 0)),
                  pl.BlockSpec((tt, D_MODEL), lambda i, e: (i, 0)),
                  pl.BlockSpec((et, D_MODEL), lambda i, e: (e, 0)),
                  pl.BlockSpec((n_sub, D_MODEL, PEER_SUB), lambda i, e: (e, 0, 0)),
                  big_spec, big_spec, big_spec, big_spec,
                  _const_spec((1, D_MODEL))],
        out_specs=pl.BlockSpec((tt, D_MODEL), lambda i, e: (i, 0)),
        out_shape=jax.ShapeDtypeStruct((T, D_MODEL), F32),
        scratch_shapes=[pltpu.VMEM((D_MODEL, tt), F32), pltpu.VMEM((2, PEER_SUB, tt), BF16),
                        pltpu.VMEM((2, PEER_SUB, tt), F32)],
        compiler_params=_params(("parallel", "arbitrary")),
        name="peer_main",
    )(hb, hres, u, vt, n1, e1, r2, e2, fg)


def _peer_layer(h, hb, wq, keys, u_tab, v_tab, fg, final):
    n1, e1, r2, e2 = _peer_topk(hb, wq.astype(BF16), keys.astype(BF16))
    vt = v_tab.astype(BF16).reshape(-1, PEER_SUB, D_MODEL).transpose(0, 2, 1)
    return _peer_main(hb, h, u_tab.astype(BF16), vt, n1, e1, r2, e2, fg, final)


def _rwkv_front_kernel(h_ref, hp_ref, g_ref, mu_ref, wr_ref, wk_ref, wv_ref, w0_ref, ww1_ref, ww2_ref,
                       a0_ref, wa1_ref, wa2_ref, wg1_ref, wg2_ref,
                       r_ref, k_ref, v_ref, w_ref, a_ref, go_ref):
    g = g_ref[...]
    x = _rms(h_ref[...], g)
    xp8 = _rms(hp_ref[...], g)
    xp8 = jnp.where(pl.program_id(0) == 0, 0.0, xp8)
    rows = x.shape[0]
    xprev = jnp.concatenate([xp8, x[:rows - BATCH]], axis=0)
    xx = xprev - x
    mix = lambda j: x + xx * mu_ref[j:j + 1, :]
    r_ref[...] = _dot(mix(0), wr_ref[...])
    w_ref[...] = w0_ref[...] + _dot(jnp.tanh(_dot(mix(1), ww1_ref[...])), ww2_ref[...])
    k_ref[...] = _dot(mix(2), wk_ref[...])
    v_ref[...] = _dot(mix(3), wv_ref[...])
    a_ref[...] = _sigmoid(a0_ref[...] + _dot(_dot(mix(4), wa1_ref[...]), wa2_ref[...]))
    go_ref[...] = _dot(_sigmoid(_dot(mix(5), wg1_ref[...])), wg2_ref[...])


def _rwkv_front(h, g, mu, wr, wk, wv, w0, ww1, ww2, a0, wa1, wa2, wg1, wg2):
    T = h.shape[0]
    tm = 128
    row = pl.BlockSpec((tm, D_MODEL), lambda i: (i, 0))
    prev = pl.BlockSpec((BATCH, D_MODEL), lambda i: (jnp.maximum(i * (tm // BATCH) - 1, 0), 0))
    consts = [g, mu, wr, wk, wv, w0, ww1, ww2, a0, wa1, wa2, wg1, wg2]
    out = jax.ShapeDtypeStruct((T, D_MODEL), F32)
    return pl.pallas_call(
        _rwkv_front_kernel,
        grid=(T // tm,),
        in_specs=[row, prev] + [_const_spec(c.shape) for c in consts],
        out_specs=[row] * 6,
        out_shape=[out] * 6,
        compiler_params=_params(("parallel",)),
        name="rwkv_front",
    )(h, h, *consts)


def _rwkv_scan_kernel(r_ref, k_ref, v_ref, w_ref, a_ref, kk_ref, ka_ref, rk_ref, lg_ref, lb_ref,
                      y_ref, s_ref, dec_s, k2_s, av_s, bv_s, wr_s, br_s, kr_s):
    tl = r_ref.shape[0]
    n = RWKV_HEAD
    ng = n // SUBLANE

    @pl.when(pl.program_id(0) == 0)
    def _():
        s_ref[...] = jnp.zeros_like(s_ref)

    r = r_ref[...]
    k = k_ref[...]
    a = a_ref[...]
    kk = k * kk_ref[...]
    kk = kk / jnp.maximum(jnp.sqrt(jnp.sum(kk * kk, axis=1, keepdims=True)), 1e-12)
    k2 = k * (1.0 + (a - 1.0) * ka_ref[...])
    dec = jnp.exp(-jnp.exp(-_softplus(-w_ref[...]) - 0.5))
    bv = kk * a
    dec_s[...] = dec
    k2_s[...] = k2
    av_s[...] = -kk
    bv_s[...] = bv
    wr_s[...] = dec * r
    br_s[...] = jnp.sum(bv * r, axis=1, keepdims=True)
    kr_s[...] = jnp.sum(k2 * r, axis=1, keepdims=True)

    def row(ref, t, j):
        return jnp.broadcast_to(ref[t, pl.ds(j, 1), :], (SUBLANE, LANE))

    def step(t, carry):
        def reduce_body(j, accs):
            sa, sy = accs
            a_b = row(av_s, t, j)
            wr_b = row(wr_s, t, j)
            sa_n, sy_n = [], []
            for vg in range(ng):
                sv = s_ref[j, vg * SUBLANE:(vg + 1) * SUBLANE, :]
                sa_n.append(sa[vg] + sv * a_b)
                sy_n.append(sy[vg] + sv * wr_b)
            return tuple(sa_n), tuple(sy_n)

        zeros = tuple(jnp.zeros((SUBLANE, LANE), F32) for _ in range(ng))
        sa, sy = lax.fori_loop(0, n, reduce_body, (zeros, zeros), unroll=8)
        br = jnp.broadcast_to(br_s[t], (SUBLANE, LANE))
        kr = jnp.broadcast_to(kr_s[t], (SUBLANE, LANE))
        vv = [v_ref[t, vg * SUBLANE:(vg + 1) * SUBLANE, :] for vg in range(ng)]
        for vg in range(ng):
            y_ref[t, vg * SUBLANE:(vg + 1) * SUBLANE, :] = sy[vg] + sa[vg] * br + vv[vg] * kr

        def update_body(j, c):
            w_b = row(dec_s, t, j)
            b_b = row(bv_s, t, j)
            k_b = row(k2_s, t, j)
            for vg in range(ng):
                sl = slice(vg * SUBLANE, (vg + 1) * SUBLANE)
                s_ref[j, sl, :] = s_ref[j, sl, :] * w_b + sa[vg] * b_b + vv[vg] * k_b
            return c

        lax.fori_loop(0, n, update_body, 0, unroll=8)
        return carry

    lax.fori_loop(0, tl, step, 0)

    y = y_ref[...]
    mean = jnp.mean(y, axis=1, keepdims=True)
    var = jnp.mean((y - mean) ** 2, axis=1, keepdims=True)
    y = (y - mean) * lax.rsqrt(var + RWKV_GN_EPS) * lg_ref[...] + lb_ref[...]
    bonus = jnp.sum(r * k2 * rk_ref[...], axis=1, keepdims=True) * v_ref[...]
    y_ref[...] = y + bonus


def _rwkv_scan(r, k, v, w, a, kk, ka, rk, lg, lb):
    L = r.shape[0]
    tl = 32
    n = RWKV_HEAD
    blk = pl.BlockSpec((tl, n, LANE), lambda i: (i, 0, 0))
    par = _const_spec((n, LANE))
    big = pltpu.VMEM((tl, n, LANE), F32)
    small = pltpu.VMEM((tl, 1, LANE), F32)
    return pl.pallas_call(
        _rwkv_scan_kernel,
        grid=(L // tl,),
        in_specs=[blk] * 5 + [par] * 5,
        out_specs=blk,
        out_shape=jax.ShapeDtypeStruct((L, n, LANE), F32),
        scratch_shapes=[pltpu.VMEM((n, n, LANE), F32), big, big, big, big, big, small, small],
        compiler_params=_params(("arbitrary",)),
        name="rwkv_scan",
    )(r, k, v, w, a, kk, ka, rk, lg, lb)


def _to_scan_layout(x, L):
    return x.reshape(L, BATCH, RWKV_HEADS, RWKV_HEAD).transpose(0, 3, 1, 2).reshape(L, RWKV_HEAD, LANE)


def _from_scan_layout(y, L):
    return y.reshape(L, RWKV_HEAD, BATCH, RWKV_HEADS).transpose(0, 2, 3, 1).reshape(L * BATCH, D_MODEL)


def _head_param(p):
    p = p.reshape(RWKV_HEADS, RWKV_HEAD).T
    return jnp.tile(p, (1, BATCH))


def _prep_even(w_in, a_re, a_im, log_dt, b_re, b_im, c_re, c_im, w_g2, b_g2):
    o = 0
    w_u = w_in[:, o:o + 512]; o += 512
    w_q = w_in[:, o:o + 256]; o += 256
    w_k = w_in[:, o:o + 256]; o += 256
    w_v = w_in[:, o:o + 512]; o += 512
    w_gl = w_in[:, o:o + GLA_RANK]; o += GLA_RANK
    w_r = w_in[:, o:o + 512]

    def pad_heads(w):
        w = w.reshape(-1, GLA_HEADS, GLA_DK)
        w = jnp.pad(w, ((0, 0), (0, 0), (0, GLA_DKP - GLA_DK)))
        return w.reshape(-1, GLA_HEADS * GLA_DKP)

    w_proj = jnp.concatenate([w_u, w_v, w_r, pad_heads(w_q), pad_heads(w_k),
                              jnp.pad(w_gl, ((0, 0), (0, LANE - GLA_RANK)))], axis=1)
    wg = jnp.pad(pad_heads(w_g2), ((0, LANE - GLA_RANK), (0, 0)))
    bg = pad_heads(b_g2[None, :])
    lm = pad_heads(jnp.ones((1, GLA_HEADS * GLA_DK), F32))

    dt = jnp.exp(log_dt)
    mag = jnp.exp(dt * a_re)
    abar_re = mag * jnp.cos(dt * a_im)
    abar_im = mag * jnp.sin(dt * a_im)
    nr = abar_re - 1.0
    den = a_re * a_re + a_im * a_im
    f_re = (nr * a_re + abar_im * a_im) / den
    f_im = (abar_im * a_re - nr * a_im) / den
    bbar_re = f_re[..., None] * b_re - f_im[..., None] * b_im
    bbar_im = f_re[..., None] * b_im + f_im[..., None] * b_re
    eye = jnp.eye(8, dtype=F32)
    gs = S5_GROUPS // S5_SLABS

    def bd(bb):
        bb = bb.reshape(S5_SLABS, gs, S5_STATE, S5_GROUP)
        return jnp.einsum('sgpc,gh->sgchp', bb, eye).reshape(S5_SLABS, gs * S5_GROUP, gs * S5_STATE)

    def cdm(cc):
        cc = cc.reshape(S5_SLABS, gs, S5_GROUP, S5_STATE)
        return jnp.einsum('sgcp,gh->sgphc', cc, eye).reshape(S5_SLABS, gs * S5_STATE, gs * S5_GROUP)

    bd_all = jnp.concatenate([bd(bbar_re), bd(bbar_im)], axis=2)
    cd_all = jnp.concatenate([cdm(c_re), -cdm(c_im)], axis=1)
    are = jnp.broadcast_to(abar_re.reshape(S5_SLABS, 1, gs * S5_STATE), (S5_SLABS, BATCH, gs * S5_STATE))
    aim = jnp.broadcast_to(abar_im.reshape(S5_SLABS, 1, gs * S5_STATE), (S5_SLABS, BATCH, gs * S5_STATE))
    return w_proj, wg, bg, lm, bd_all, cd_all, are, aim


def kernel(x, norm_mix_g, norm_ffn_g, final_g, e_w_in, e_w_out, s5_a_re, s5_a_im, s5_log_dt, s5_b_re, s5_b_im, s5_c_re, s5_c_im, s5_d, s5_w_glu, gla_w_g2, gla_b_g2, gla_norm_g, o_mu, o_w_r, o_w_k, o_w_v, o_w0, o_w_w1, o_w_w2, o_a0, o_w_a1, o_w_a2, o_w_g1, o_w_g2, o_k_k, o_k_a, o_r_k, o_lnx_g, o_lnx_b, o_w_o, peer_w_q, peer_sub_keys, peer_u, peer_v):
    bsz, L, D = x.shape
    assert bsz == BATCH and D == D_MODEL
    T = bsz * L
    row = lambda p: p.reshape(1, -1)
    h = x.transpose(1, 0, 2).reshape(T, D)

    w_proj, wg, bg, lm, bd_all, cd_all, are, aim = _prep_even(
        e_w_in[0], s5_a_re[0], s5_a_im[0], s5_log_dt[0], s5_b_re[0], s5_b_im[0],
        s5_c_re[0], s5_c_im[0], gla_w_g2[0], gla_b_g2[0])
    bf = lambda p: p.astype(BF16)
    proj = _even_front(h, row(norm_mix_g[0]), bf(w_proj))
    y_s5 = _s5(proj, bf(bd_all), bf(cd_all), are, aim, row(s5_d[0]), bf(s5_w_glu[0]))
    y_gla = _gla(proj, wg, bg, lm, row(gla_norm_g[0]))
    w_out = bf(e_w_out[0])
    h, hb = _mix_out(h, [y_s5, y_gla], [w_out[:S5_WIDTH], w_out[S5_WIDTH:]], None, row(norm_ffn_g[0]))
    h = _peer_layer(h, hb, peer_w_q[0], peer_sub_keys[0], peer_u[0], peer_v[0], row(final_g), False)

    r, k, v, w, a, g = _rwkv_front(
        h, row(norm_mix_g[1]), o_mu[0], bf(o_w_r[0]), bf(o_w_k[0]), bf(o_w_v[0]), row(o_w0[0]),
        bf(o_w_w1[0]), bf(o_w_w2[0]), row(o_a0[0]), bf(o_w_a1[0]), bf(o_w_a2[0]), bf(o_w_g1[0]), bf(o_w_g2[0]))
    sl = lambda t: _to_scan_layout(t, L)
    y = _rwkv_scan(sl(r), sl(k), sl(v), sl(w), sl(a), _head_param(o_k_k[0]), _head_param(o_k_a[0]),
                   _head_param(o_r_k[0].reshape(-1)), _head_param(o_lnx_g[0]), _head_param(o_lnx_b[0]))
    y = _from_scan_layout(y, L)
    h, hb = _mix_out(h, [y], [bf(o_w_o[0])], g, row(norm_ffn_g[1]))
    out = _peer_layer(h, hb, peer_w_q[1], peer_sub_keys[1], peer_u[1], peer_v[1], row(final_g), True)
    return out.reshape(L, bsz, D).transpose(1, 0, 2)
```

```python
import functools
import math

import jax
import jax.numpy as jnp
from jax import lax
from jax.experimental import pallas as pl
from jax.experimental.pallas import tpu as pltpu

F32 = jnp.float32
BF16 = jnp.bfloat16
HIGHEST = lax.Precision.HIGHEST

D_MODEL = 1024
BATCH = 8
NORM_EPS = 1e-6
S5_WIDTH = 512
S5_GROUP = 16
S5_GROUPS = 32
S5_STATE = 64
S5_SLABS = 4
GLA_HEADS = 4
GLA_DV = 128
GLA_DK = 64
GLA_DKP = 128
GLA_RANK = 16
GLA_TAU = 16.0
GLA_CHUNK = 64
RWKV_HEAD = 64
RWKV_HEADS = 16
RWKV_GN_EPS = 64e-5
PEER_HEADS = 8
PEER_NKEYS = 128
PEER_TOPK = 16
PEER_TILE = 1024
GATE_STEP = 2.0
LANE = 128
SUBLANE = 8
VMEM_LIMIT = 56 * 1024 * 1024

PROJ_W = 3 * 512 + 2 * GLA_HEADS * GLA_DKP + LANE


def _dot(a, b, hi=False):
    if hi:
        return jnp.dot(a, b, preferred_element_type=F32, precision=HIGHEST)
    return jnp.dot(a.astype(BF16), b.astype(BF16), preferred_element_type=F32)


def _dot_nt(a, b):
    return lax.dot_general(a.astype(BF16), b.astype(BF16), (((1,), (1,)), ((), ())),
                           preferred_element_type=F32)


def _rms(x, g):
    return x * lax.rsqrt(jnp.mean(x * x, axis=-1, keepdims=True) + NORM_EPS) * g


def _sigmoid(x):
    return 1.0 / (1.0 + jnp.exp(-x))


def _softplus(x):
    return jnp.maximum(x, 0.0) + jnp.log(1.0 + jnp.exp(-jnp.abs(x)))


def _gelu(x):
    return 0.5 * x * (1.0 + lax.erf(x * (1.0 / math.sqrt(2.0))))


def _const_spec(shape):
    nd = len(shape)
    return pl.BlockSpec(shape, lambda *_: (0,) * nd)


def _params(sem, flags=None):
    return pltpu.CompilerParams(dimension_semantics=sem, vmem_limit_bytes=VMEM_LIMIT, flags=flags)


def _even_front_kernel(h_ref, g_ref, w_ref, o_ref):
    xn = _rms(h_ref[...], g_ref[...])
    o_ref[...] = _dot(xn, w_ref[...])


def _even_front(h, g, w):
    T = h.shape[0]
    tm = 256
    return pl.pallas_call(
        _even_front_kernel,
        grid=(T // tm,),
        in_specs=[pl.BlockSpec((tm, D_MODEL), lambda i: (i, 0)),
                  _const_spec((1, D_MODEL)),
                  _const_spec((D_MODEL, PROJ_W))],
        out_specs=pl.BlockSpec((tm, PROJ_W), lambda i: (i, 0)),
        out_shape=jax.ShapeDtypeStruct((T, PROJ_W), F32),
        compiler_params=_params(("parallel",)),
        name="even_front",
    )(h, g, w)


def _s5_kernel(u_ref, bd_ref, cd_ref, are_ref, aim_ref, d_ref, wglu_ref, o_ref, bu_ref, st_ref):
    rows = u_ref.shape[0]
    steps = rows // BATCH
    half = 512

    @pl.when(pl.program_id(0) == 0)
    def _():
        st_ref[...] = jnp.zeros_like(st_ref)

    u = u_ref[...]
    for s in range(S5_SLABS):
        bu_ref[s] = _dot(u[:, s * LANE:(s + 1) * LANE], bd_ref[s])

    def step(t, carry):
        r0 = pl.multiple_of(t * BATCH, BATCH)
        out = []
        for s in range(S5_SLABS):
            sre, sim = carry[2 * s], carry[2 * s + 1]
            ar, ai = are_ref[s], aim_ref[s]
            bre = bu_ref[s, pl.ds(r0, BATCH), 0:half]
            bim = bu_ref[s, pl.ds(r0, BATCH), half:2 * half]
            nre = ar * sre - ai * sim + bre
            nim = ar * sim + ai * sre + bim
            bu_ref[s, pl.ds(r0, BATCH), 0:half] = nre
            bu_ref[s, pl.ds(r0, BATCH), half:2 * half] = nim
            out += [nre, nim]
        return tuple(out)

    init = tuple(st_ref[j] for j in range(2 * S5_SLABS))
    fin = lax.fori_loop(0, steps, step, init)
    for j in range(2 * S5_SLABS):
        st_ref[j] = fin[j]

    y = jnp.concatenate([_dot(bu_ref[s], cd_ref[s]) for s in range(S5_SLABS)], axis=1)
    y = y + d_ref[...] * u
    g = _gelu(y)
    o_ref[...] = g * _sigmoid(_dot(g, wglu_ref[...]))


def _s5(proj, bd, cd, are, aim, d, wglu):
    T = proj.shape[0]
    rows = 256
    return pl.pallas_call(
        _s5_kernel,
        grid=(T // rows,),
        in_specs=[pl.BlockSpec((rows, S5_WIDTH), lambda i: (i, 0)),
                  _const_spec(bd.shape), _const_spec(cd.shape),
                  _const_spec(are.shape), _const_spec(aim.shape),
                  _const_spec(d.shape), _const_spec(wglu.shape)],
        out_specs=pl.BlockSpec((rows, S5_WIDTH), lambda i: (i, 0)),
        out_shape=jax.ShapeDtypeStruct((T, S5_WIDTH), F32),
        scratch_shapes=[pltpu.VMEM((S5_SLABS, rows, 2 * 512), F32),
                        pltpu.VMEM((2 * S5_SLABS, BATCH, 512), F32)],
        compiler_params=_params(("arbitrary",)),
        name="s5",
    )(proj, bd, cd, are, aim, d, wglu)


def _gla_kernel(v_ref, r_ref, q_ref, k_ref, gl_ref, wg_ref, bg_ref, lm_ref, ng_ref, o_ref, s_ref):
    rows = q_ref.shape[0]
    width = GLA_HEADS * GLA_DKP

    @pl.when(pl.program_id(0) == 0)
    def _():
        s_ref[...] = jnp.zeros_like(s_ref)

    z = _dot(gl_ref[...], wg_ref[...], True) + bg_ref[...]
    la = -_softplus(-z) * (1.0 / GLA_TAU) * lm_ref[...]
    b = la
    sh = BATCH
    while sh < rows:
        b = b + jnp.concatenate([jnp.zeros((sh, width), F32), b[:rows - sh]], axis=0)
        sh *= 2
    blast = b[rows - BATCH:rows]
    blast_t = jnp.concatenate([blast] * GLA_CHUNK, axis=0)
    q = q_ref[...] * (GLA_DK ** -0.5)
    k = k_ref[...]
    qd = q * jnp.exp(b)
    kd = k * jnp.exp(-b)
    kend = k * jnp.exp(blast_t - b)

    ri = lax.broadcasted_iota(jnp.int32, (rows, rows), 0)
    ci = lax.broadcasted_iota(jnp.int32, (rows, rows), 1)
    causal = jnp.logical_and(((ri - ci) & (BATCH - 1)) == 0, ri >= ci)
    rowb = lax.broadcasted_iota(jnp.int32, (rows, GLA_DKP), 0) & (BATCH - 1)
    laneb = lax.broadcasted_iota(jnp.int32, (GLA_DKP, rows), 1) & (BATCH - 1)

    for h in range(GLA_HEADS):
        sl = slice(h * GLA_DKP, (h + 1) * GLA_DKP)
        vh = v_ref[:, h * GLA_DV:(h + 1) * GLA_DV]
        qh, kdh, keh, bh = qd[:, sl], kd[:, sl], kend[:, sl], b[:, sl]
        att = jnp.where(causal, _dot_nt(qh, kdh), 0.0)
        o = _dot(att, vh)
        qexp = jnp.concatenate([jnp.where(rowb == bb, qh, 0.0) for bb in range(BATCH)], axis=1)
        o = o + _dot(qexp, s_ref[h])
        ket = keh.T
        kexp_t = jnp.concatenate([jnp.where(laneb == bb, ket, 0.0) for bb in range(BATCH)], axis=0)
        kv = _dot(kexp_t, vh)
        bt = bh[rows - GLA_DKP:rows].T
        dec = jnp.concatenate(
            [jnp.broadcast_to(jnp.exp(bt[:, GLA_DKP - BATCH + bb:GLA_DKP - BATCH + bb + 1]),
                              (GLA_DKP, GLA_DV)) for bb in range(BATCH)], axis=0)
        s_ref[h] = s_ref[h] * dec + kv
        o = o * lax.rsqrt(jnp.mean(o * o, axis=-1, keepdims=True) + NORM_EPS)
        o = o * ng_ref[:, h * GLA_DV:(h + 1) * GLA_DV]
        rr = r_ref[:, h * GLA_DV:(h + 1) * GLA_DV]
        o_ref[:, h * GLA_DV:(h + 1) * GLA_DV] = o * (rr * _sigmoid(rr))


def _gla(proj, wg, bg, lm, ng):
    T = proj.shape[0]
    rows = GLA_CHUNK * BATCH
    width = GLA_HEADS * GLA_DKP
    return pl.pallas_call(
        _gla_kernel,
        grid=(T // rows,),
        in_specs=[pl.BlockSpec((rows, 512), lambda i: (i, 1)),
                  pl.BlockSpec((rows, 512), lambda i: (i, 2)),
                  pl.BlockSpec((rows, width), lambda i: (i, 3)),
                  pl.BlockSpec((rows, width), lambda i: (i, 4)),
                  pl.BlockSpec((rows, LANE), lambda i: (i, (3 * 512 + 2 * width) // LANE)),
                  _const_spec(wg.shape), _const_spec(bg.shape), _const_spec(lm.shape),
                  _const_spec(ng.shape)],
        out_specs=pl.BlockSpec((rows, GLA_HEADS * GLA_DV), lambda i: (i, 0)),
        out_shape=jax.ShapeDtypeStruct((T, GLA_HEADS * GLA_DV), F32),
        scratch_shapes=[pltpu.VMEM((GLA_HEADS, BATCH * GLA_DKP, GLA_DV), F32)],
        compiler_params=_params(("arbitrary",)),
        name="gla",
    )(proj, proj, proj, proj, proj, wg, bg, lm, ng)


def _mix_out_kernel(n_in, has_gate, *refs):
    h_ref = refs[0]
    y_refs = refs[1:1 + n_in]
    w_refs = refs[1 + n_in:1 + 2 * n_in]
    rest = refs[1 + 2 * n_in:]
    gate_ref = rest[0] if has_gate else None
    g_ref, ho_ref, hb_ref, hbt_ref = rest[1:] if has_gate else rest
    acc = h_ref[...]
    for y_ref, w_ref in zip(y_refs, w_refs):
        y = y_ref[...]
        if has_gate:
            y = y * gate_ref[...]
        acc = acc + _dot(y, w_ref[...])
    ho_ref[...] = acc
    hn = _rms(acc, g_ref[...])
    hb_ref[...] = hn.astype(BF16)
    hbt_ref[...] = hn.T.astype(BF16)


def _mix_out(h, ys, ws, gate, g):
    T = h.shape[0]
    tm = 256
    n_in = len(ys)
    row = pl.BlockSpec((tm, D_MODEL), lambda i: (i, 0))
    in_specs = [row]
    in_specs += [pl.BlockSpec((tm, y.shape[1]), lambda i: (i, 0)) for y in ys]
    in_specs += [_const_spec(w.shape) for w in ws]
    args = [h, *ys, *ws]
    if gate is not None:
        in_specs.append(row)
        args.append(gate)
    in_specs.append(_const_spec((1, D_MODEL)))
    args.append(g)
    return pl.pallas_call(
        functools.partial(_mix_out_kernel, n_in, gate is not None),
        grid=(T // tm,),
        in_specs=in_specs,
        out_specs=[row, row, pl.BlockSpec((D_MODEL, tm), lambda i: (0, i))],
        out_shape=[jax.ShapeDtypeStruct((T, D_MODEL), F32),
                   jax.ShapeDtypeStruct((T, D_MODEL), BF16),
                   jax.ShapeDtypeStruct((D_MODEL, T), BF16)],
        compiler_params=_params(("parallel",)),
        name="mix_out",
    )(*args)


def _tree_sum(xs):
    while len(xs) > 1:
        xs = [xs[i] + xs[i + 1] for i in range(0, len(xs) - 1, 2)] + ([xs[-1]] if len(xs) % 2 else [])
    return xs[0]


def _insert(tops, x):
    out = []
    for t in tops:
        out.append(jnp.maximum(t, x))
        x = jnp.minimum(t, x)
    return out


def _top_values(scr_ref):
    def body(k, tops):
        x = scr_ref[pl.ds(k, SUBLANE, stride=PEER_NKEYS), :]
        return tuple(_insert(list(tops), x))
    init = tuple(jnp.full((SUBLANE, LANE), -jnp.inf, F32) for _ in range(PEER_TOPK))
    return lax.fori_loop(0, PEER_NKEYS, body, init, unroll=8)


def _peer_topk_kernel(hn_ref, wq_ref, keys_ref, n1_ref, e1_ref, r2_ref, e2_ref,
                      scr1, scr2, scr3, scr4):
    nch = hn_ref.shape[0] // LANE
    q = _dot(hn_ref[...], wq_ref[...])
    for c, scr in enumerate((scr1, scr2)):
        st = _dot_nt(keys_ref[0, c], q[:, c * LANE:(c + 1) * LANE])
        for j in range(nch):
            scr[j * PEER_NKEYS:(j + 1) * PEER_NKEYS, :] = st[:, j * LANE:(j + 1) * LANE]
    a = _top_values(scr1)
    b = _top_values(scr2)
    pairs = [(i, j) for i in range(PEER_TOPK) for j in range(PEER_TOPK) if (i + 1) * (j + 1) <= PEER_TOPK]
    tops = [jnp.full((SUBLANE, LANE), -jnp.inf, F32) for _ in range(PEER_TOPK)]
    for i, j in pairs:
        tops = _insert(tops, a[i] + b[j])
    thr = tops[PEER_TOPK - 1]
    z = jnp.zeros((SUBLANE, LANE), F32)
    for t in tops:
        z = z + jnp.exp(t - tops[0])
    inv_z = 1.0 / z
    cnt = [jnp.zeros((SUBLANE, LANE), F32) for _ in range(PEER_TOPK)]
    for i, j in pairs:
        cnt[i] = cnt[i] + jnp.where(a[i] + b[j] >= thr, GATE_STEP, 0.0)

    def first_half(k, carry):
        rows = pl.ds(k, SUBLANE, stride=PEER_NKEYS)
        x = scr1[rows, :]
        scr3[rows, :] = _tree_sum([jnp.where(x == a[i], cnt[i], 0.0) for i in range(PEER_TOPK)])
        scr1[rows, :] = jnp.exp(x - a[0]) * inv_z
        return carry

    def second_half(k, carry):
        rows = pl.ds(k, SUBLANE, stride=PEER_NKEYS)
        x = scr2[rows, :]
        scr4[rows, :] = _tree_sum([jnp.where(b[j] > x, GATE_STEP, 0.0) for j in range(PEER_TOPK)])
        scr2[rows, :] = jnp.exp(x - b[0])
        return carry

    lax.fori_loop(0, PEER_NKEYS, first_half, 0, unroll=4)
    lax.fori_loop(0, PEER_NKEYS, second_half, 0, unroll=4)
    shape = (nch, PEER_NKEYS, LANE)
    n1_ref[0] = scr3[...].reshape(shape)
    e1_ref[0] = scr1[...].reshape(shape)
    r2_ref[0] = scr4[...].astype(BF16).reshape(shape)
    e2_ref[0] = scr2[...].astype(BF16).reshape(shape)


def _peer_topk(hn, wq, keys):
    T = hn.shape[0]
    tt = SUBLANE * LANE
    nch = T // LANE
    shape = (PEER_HEADS, nch, PEER_NKEYS, LANE)
    spec = pl.BlockSpec((1, SUBLANE, PEER_NKEYS, LANE), lambda i, h: (h, i, 0, 0))
    return pl.pallas_call(
        _peer_topk_kernel,
        grid=(T // tt, PEER_HEADS),
        in_specs=[pl.BlockSpec((tt, D_MODEL), lambda i, h: (i, 0)),
                  pl.BlockSpec((D_MODEL, 2 * LANE), lambda i, h: (0, h)),
                  pl.BlockSpec((1, 2, PEER_NKEYS, LANE), lambda i, h: (h, 0, 0, 0))],
        out_specs=[spec, spec, spec, spec],
        out_shape=[jax.ShapeDtypeStruct(shape, F32), jax.ShapeDtypeStruct(shape, F32),
                   jax.ShapeDtypeStruct(shape, BF16), jax.ShapeDtypeStruct(shape, BF16)],
        scratch_shapes=[pltpu.VMEM((tt, LANE), F32)] * 4,
        compiler_params=_params(("parallel", "arbitrary")),
        name="peer_topk",
    )(hn, wq, keys)


def _row_bf16(ref, h, c, i1):
    packed_rows = 2 * SUBLANE
    x = jnp.broadcast_to(ref[h, c, pl.ds(i1, 1), :], (packed_rows, LANE)).astype(BF16)
    return jnp.concatenate([x] * (PEER_NKEYS // packed_rows), axis=0)


def _peer_main_kernel(final, hbt_ref, hres_ref, u0_ref, ub_ref, ua_ref, vta_ref, vtb_ref,
                      n1_ref, e1_ref, r2_ref, e2_ref, fg_ref, o_ref,
                      acc_ref, hga_ref, hgb_ref, hta_ref, htb_ref):
    e = pl.program_id(1)
    n_pairs = pl.num_programs(1) - 1
    tt = hbt_ref.shape[1]
    n_i1 = PEER_TILE // PEER_NKEYS
    n_ch = tt // LANE
    zero = jnp.zeros((PEER_NKEYS, LANE), BF16)

    @pl.when(e == 0)
    def _():
        acc_ref[...] = jnp.zeros_like(acc_ref)
        hgb_ref[...] = jnp.zeros_like(hgb_ref)
        hta_ref[...] = _dot(u0_ref[...], hbt_ref[...])

    def gates(tile, ht_ref, hg_ref, c):
        for il in range(n_i1):
            i1 = tile * n_i1 + il
            gate = zero
            for h in range(PEER_HEADS):
                code = _row_bf16(n1_ref, h, c, i1) - r2_ref[h, c]
                w = _row_bf16(e1_ref, h, c, i1) * e2_ref[h, c]
                gate = gate + jnp.maximum(jnp.minimum(w, code), zero)
            rows = slice(il * PEER_NKEYS, (il + 1) * PEER_NKEYS)
            cols = slice(c * LANE, (c + 1) * LANE)
            hg_ref[rows, cols] = _gelu(ht_ref[rows, cols]).astype(BF16) * gate

    def phase(tile, ht_ref, hg_ref, vt_prev_ref, hg_prev_ref, u_next_ref, ht_next_ref):
        half = tt // 2
        for piece in range(4):
            cols = slice((piece % 2) * half, (piece % 2 + 1) * half)
            if piece < 2:
                acc_ref[:, cols] += _dot(vt_prev_ref[...], hg_prev_ref[:, cols])
            else:
                ht_next_ref[:, cols] = _dot(u_next_ref[...], hbt_ref[:, cols])
            gates(tile, ht_ref, hg_ref, piece)

    @pl.when(e < n_pairs)
    def _():
        phase(2 * e, hta_ref, hga_ref, vtb_ref, hgb_ref, ub_ref, htb_ref)
        phase(2 * e + 1, htb_ref, hgb_ref, vta_ref, hga_ref, ua_ref, hta_ref)

    @pl.when(e == n_pairs)
    def _():
        acc = acc_ref[...] + _dot(vtb_ref[...], hgb_ref[...])
        out = hres_ref[...] + acc.T
        if final:
            out = _rms(out, fg_ref[...])
        o_ref[...] = out


def _peer_main(hbt, hres, u, vt, n1, e1, r2, e2, fg, final):
    T = hres.shape[0]
    E = u.shape[0]
    tt = 512
    n_tiles = E // PEER_TILE
    n_pairs = n_tiles // 2
    big_spec = pl.BlockSpec((PEER_HEADS, tt // LANE, PEER_NKEYS, LANE), lambda i, e: (0, i, 0, 0))
    tile_of = lambda f: (lambda i, e: (jnp.clip(f(e), 0, n_tiles - 1), 0, 0))
    vt_spec = lambda f: pl.BlockSpec((D_MODEL, PEER_TILE), lambda i, e: tile_of(f)(i, e)[:2])
    u_spec = lambda f: pl.BlockSpec((PEER_TILE, D_MODEL), lambda i, e: tile_of(f)(i, e)[:2])
    return pl.pallas_call(
        functools.partial(_peer_main_kernel, final),
        grid=(T // tt, n_pairs + 1),
        in_specs=[pl.BlockSpec((D_MODEL, tt), lambda i, e: (0, i)),
                  pl.BlockSpec((tt, D_MODEL), lambda i, e: (i, 0)),
                  u_spec(lambda e: 0), u_spec(lambda e: 2 * e + 1), u_spec(lambda e: 2 * e + 2),
                  vt_spec(lambda e: 2 * e), vt_spec(lambda e: 2 * e - 1),
                  big_spec, big_spec, big_spec, big_spec,
                  _const_spec((1, D_MODEL))],
        out_specs=pl.BlockSpec((tt, D_MODEL), lambda i, e: (i, 0)),
        out_shape=jax.ShapeDtypeStruct((T, D_MODEL), F32),
        scratch_shapes=[pltpu.VMEM((D_MODEL, tt), F32),
                        pltpu.VMEM((PEER_TILE, tt), BF16), pltpu.VMEM((PEER_TILE, tt), BF16),
                        pltpu.VMEM((PEER_TILE, tt), F32), pltpu.VMEM((PEER_TILE, tt), F32)],
        compiler_params=_params(("parallel", "arbitrary")),
        name="peer_main",
    )(hbt, hres, u, u, u, vt, vt, n1, e1, r2, e2, fg)


def _peer_layer(h, hb, hbt, wq, keys, u_tab, v_tab, fg, final):
    n1, e1, r2, e2 = _peer_topk(hb, wq.astype(BF16), keys.astype(BF16))
    vt = v_tab.astype(BF16).reshape(-1, PEER_TILE, D_MODEL).transpose(0, 2, 1).reshape(-1, PEER_TILE)
    return _peer_main(hbt, h, u_tab.astype(BF16), vt, n1, e1, r2, e2, fg, final)


def _rwkv_front_kernel(h_ref, hp_ref, g_ref, mu_ref, wr_ref, wk_ref, wv_ref, w0_ref, ww1_ref, ww2_ref,
                       a0_ref, wa1_ref, wa2_ref, wg1_ref, wg2_ref,
                       r_ref, k_ref, v_ref, w_ref, a_ref, go_ref):
    g = g_ref[...]
    x = _rms(h_ref[...], g)
    xp8 = _rms(hp_ref[...], g)
    xp8 = jnp.where(pl.program_id(0) == 0, 0.0, xp8)
    rows = x.shape[0]
    xprev = jnp.concatenate([xp8, x[:rows - BATCH]], axis=0)
    xx = xprev - x
    mix = lambda j: x + xx * mu_ref[j:j + 1, :]
    r_ref[...] = _dot(mix(0), wr_ref[...])
    w_ref[...] = w0_ref[...] + _dot(jnp.tanh(_dot(mix(1), ww1_ref[...])), ww2_ref[...])
    k_ref[...] = _dot(mix(2), wk_ref[...])
    v_ref[...] = _dot(mix(3), wv_ref[...])
    a_ref[...] = _sigmoid(a0_ref[...] + _dot(_dot(mix(4), wa1_ref[...]), wa2_ref[...]))
    go_ref[...] = _dot(_sigmoid(_dot(mix(5), wg1_ref[...])), wg2_ref[...])


def _rwkv_front(h, g, mu, wr, wk, wv, w0, ww1, ww2, a0, wa1, wa2, wg1, wg2):
    T = h.shape[0]
    tm = 128
    row = pl.BlockSpec((tm, D_MODEL), lambda i: (i, 0))
    prev = pl.BlockSpec((BATCH, D_MODEL), lambda i: (jnp.maximum(i * (tm // BATCH) - 1, 0), 0))
    consts = [g, mu, wr, wk, wv, w0, ww1, ww2, a0, wa1, wa2, wg1, wg2]
    out = jax.ShapeDtypeStruct((T, D_MODEL), F32)
    return pl.pallas_call(
        _rwkv_front_kernel,
        grid=(T // tm,),
        in_specs=[row, prev] + [_const_spec(c.shape) for c in consts],
        out_specs=[row] * 6,
        out_shape=[out] * 6,
        compiler_params=_params(("parallel",)),
        name="rwkv_front",
    )(h, h, *consts)


def _rwkv_scan_kernel(r_ref, k_ref, v_ref, w_ref, a_ref, kk_ref, ka_ref, rk_ref, lg_ref, lb_ref,
                      y_ref, s_ref, dec_s, k2_s, av_s, bv_s, wr_s, br_s, kr_s):
    tl = r_ref.shape[0]
    n = RWKV_HEAD
    ng = n // SUBLANE

    @pl.when(pl.program_id(0) == 0)
    def _():
        s_ref[...] = jnp.zeros_like(s_ref)

    r = r_ref[...]
    k = k_ref[...]
    a = a_ref[...]
    kk = k * kk_ref[...]
    kk = kk / jnp.maximum(jnp.sqrt(jnp.sum(kk * kk, axis=1, keepdims=True)), 1e-12)
    k2 = k * (1.0 + (a - 1.0) * ka_ref[...])
    dec = jnp.exp(-jnp.exp(-_softplus(-w_ref[...]) - 0.5))
    bv = kk * a
    dec_s[...] = dec
    k2_s[...] = k2
    av_s[...] = -kk
    bv_s[...] = bv
    wr_s[...] = dec * r
    br_s[...] = jnp.sum(bv * r, axis=1, keepdims=True)
    kr_s[...] = jnp.sum(k2 * r, axis=1, keepdims=True)

    def row(ref, t, j):
        return jnp.broadcast_to(ref[t, pl.ds(j, 1), :], (SUBLANE, LANE))

    def step(t, carry):
        def reduce_body(j, accs):
            sa, sy = accs
            a_b = row(av_s, t, j)
            wr_b = row(wr_s, t, j)
            sa_n, sy_n = [], []
            for vg in range(ng):
                sv = s_ref[j, vg * SUBLANE:(vg + 1) * SUBLANE, :]
                sa_n.append(sa[vg] + sv * a_b)
                sy_n.append(sy[vg] + sv * wr_b)
            return tuple(sa_n), tuple(sy_n)

        zeros = tuple(jnp.zeros((SUBLANE, LANE), F32) for _ in range(ng))
        sa, sy = lax.fori_loop(0, n, reduce_body, (zeros, zeros), unroll=8)
        br = jnp.broadcast_to(br_s[t], (SUBLANE, LANE))
        kr = jnp.broadcast_to(kr_s[t], (SUBLANE, LANE))
        vv = [v_ref[t, vg * SUBLANE:(vg + 1) * SUBLANE, :] for vg in range(ng)]
        for vg in range(ng):
            y_ref[t, vg * SUBLANE:(vg + 1) * SUBLANE, :] = sy[vg] + sa[vg] * br + vv[vg] * kr

        def update_body(j, c):
            w_b = row(dec_s, t, j)
            b_b = row(bv_s, t, j)
            k_b = row(k2_s, t, j)
            for vg in range(ng):
                sl = slice(vg * SUBLANE, (vg + 1) * SUBLANE)
                s_ref[j, sl, :] = s_ref[j, sl, :] * w_b + sa[vg] * b_b + vv[vg] * k_b
            return c

        lax.fori_loop(0, n, update_body, 0, unroll=8)
        return carry

    lax.fori_loop(0, tl, step, 0)

    y = y_ref[...]
    mean = jnp.mean(y, axis=1, keepdims=True)
    var = jnp.mean((y - mean) ** 2, axis=1, keepdims=True)
    y = (y - mean) * lax.rsqrt(var + RWKV_GN_EPS) * lg_ref[...] + lb_ref[...]
    bonus = jnp.sum(r * k2 * rk_ref[...], axis=1, keepdims=True) * v_ref[...]
    y_ref[...] = y + bonus


def _rwkv_scan(r, k, v, w, a, kk, ka, rk, lg, lb):
    L = r.shape[0]
    tl = 32
    n = RWKV_HEAD
    blk = pl.BlockSpec((tl, n, LANE), lambda i: (i, 0, 0))
    par = _const_spec((n, LANE))
    big = pltpu.VMEM((tl, n, LANE), F32)
    small = pltpu.VMEM((tl, 1, LANE), F32)
    return pl.pallas_call(
        _rwkv_scan_kernel,
        grid=(L // tl,),
        in_specs=[blk] * 5 + [par] * 5,
        out_specs=blk,
        out_shape=jax.ShapeDtypeStruct((L, n, LANE), F32),
        scratch_shapes=[pltpu.VMEM((n, n, LANE), F32), big, big, big, big, big, small, small],
        compiler_params=_params(("arbitrary",)),
        name="rwkv_scan",
    )(r, k, v, w, a, kk, ka, rk, lg, lb)


def _to_scan_layout(x, L):
    return x.reshape(L, BATCH, RWKV_HEADS, RWKV_HEAD).transpose(0, 3, 1, 2).reshape(L, RWKV_HEAD, LANE)


def _from_scan_layout(y, L):
    return y.reshape(L, RWKV_HEAD, BATCH, RWKV_HEADS).transpose(0, 2, 3, 1).reshape(L * BATCH, D_MODEL)


def _head_param(p):
    p = p.reshape(RWKV_HEADS, RWKV_HEAD).T
    return jnp.tile(p, (1, BATCH))


def _prep_even(w_in, a_re, a_im, log_dt, b_re, b_im, c_re, c_im, w_g2, b_g2):
    o = 0
    w_u = w_in[:, o:o + 512]; o += 512
    w_q = w_in[:, o:o + 256]; o += 256
    w_k = w_in[:, o:o + 256]; o += 256
    w_v = w_in[:, o:o + 512]; o += 512
    w_gl = w_in[:, o:o + GLA_RANK]; o += GLA_RANK
    w_r = w_in[:, o:o + 512]

    def pad_heads(w):
        w = w.reshape(-1, GLA_HEADS, GLA_DK)
        w = jnp.pad(w, ((0, 0), (0, 0), (0, GLA_DKP - GLA_DK)))
        return w.reshape(-1, GLA_HEADS * GLA_DKP)

    w_proj = jnp.concatenate([w_u, w_v, w_r, pad_heads(w_q), pad_heads(w_k),
                              jnp.pad(w_gl, ((0, 0), (0, LANE - GLA_RANK)))], axis=1)
    wg = jnp.pad(pad_heads(w_g2), ((0, LANE - GLA_RANK), (0, 0)))
    bg = pad_heads(b_g2[None, :])
    lm = pad_heads(jnp.ones((1, GLA_HEADS * GLA_DK), F32))

    dt = jnp.exp(log_dt)
    mag = jnp.exp(dt * a_re)
    abar_re = mag * jnp.cos(dt * a_im)
    abar_im = mag * jnp.sin(dt * a_im)
    nr = abar_re - 1.0
    den = a_re * a_re + a_im * a_im
    f_re = (nr * a_re + abar_im * a_im) / den
    f_im = (abar_im * a_re - nr * a_im) / den
    bbar_re = f_re[..., None] * b_re - f_im[..., None] * b_im
    bbar_im = f_re[..., None] * b_im + f_im[..., None] * b_re
    eye = jnp.eye(8, dtype=F32)
    gs = S5_GROUPS // S5_SLABS

    def bd(bb):
        bb = bb.reshape(S5_SLABS, gs, S5_STATE, S5_GROUP)
        return jnp.einsum('sgpc,gh->sgchp', bb, eye).reshape(S5_SLABS, gs * S5_GROUP, gs * S5_STATE)

    def cdm(cc):
        cc = cc.reshape(S5_SLABS, gs, S5_GROUP, S5_STATE)
        return jnp.einsum('sgcp,gh->sgphc', cc, eye).reshape(S5_SLABS, gs * S5_STATE, gs * S5_GROUP)

    bd_all = jnp.concatenate([bd(bbar_re), bd(bbar_im)], axis=2)
    cd_all = jnp.concatenate([cdm(c_re), -cdm(c_im)], axis=1)
    are = jnp.broadcast_to(abar_re.reshape(S5_SLABS, 1, gs * S5_STATE), (S5_SLABS, BATCH, gs * S5_STATE))
    aim = jnp.broadcast_to(abar_im.reshape(S5_SLABS, 1, gs * S5_STATE), (S5_SLABS, BATCH, gs * S5_STATE))
    return w_proj, wg, bg, lm, bd_all, cd_all, are, aim


def kernel(x, norm_mix_g, norm_ffn_g, final_g, e_w_in, e_w_out, s5_a_re, s5_a_im, s5_log_dt, s5_b_re, s5_b_im, s5_c_re, s5_c_im, s5_d, s5_w_glu, gla_w_g2, gla_b_g2, gla_norm_g, o_mu, o_w_r, o_w_k, o_w_v, o_w0, o_w_w1, o_w_w2, o_a0, o_w_a1, o_w_a2, o_w_g1, o_w_g2, o_k_k, o_k_a, o_r_k, o_lnx_g, o_lnx_b, o_w_o, peer_w_q, peer_sub_keys, peer_u, peer_v):
    bsz, L, D = x.shape
    assert bsz == BATCH and D == D_MODEL
    T = bsz * L
    row = lambda p: p.reshape(1, -1)
    h = x.transpose(1, 0, 2).reshape(T, D)

    w_proj, wg, bg, lm, bd_all, cd_all, are, aim = _prep_even(
        e_w_in[0], s5_a_re[0], s5_a_im[0], s5_log_dt[0], s5_b_re[0], s5_b_im[0],
        s5_c_re[0], s5_c_im[0], gla_w_g2[0], gla_b_g2[0])
    bf = lambda p: p.astype(BF16)
    proj = _even_front(h, row(norm_mix_g[0]), bf(w_proj))
    y_s5 = _s5(proj, bf(bd_all), bf(cd_all), are, aim, row(s5_d[0]), bf(s5_w_glu[0]))
    y_gla = _gla(proj, wg, bg, lm, row(gla_norm_g[0]))
    w_out = bf(e_w_out[0])
    h, hb, hbt = _mix_out(h, [y_s5, y_gla], [w_out[:S5_WIDTH], w_out[S5_WIDTH:]], None, row(norm_ffn_g[0]))
    h = _peer_layer(h, hb, hbt, peer_w_q[0], peer_sub_keys[0], peer_u[0], peer_v[0], row(final_g), False)

    r, k, v, w, a, g = _rwkv_front(
        h, row(norm_mix_g[1]), o_mu[0], bf(o_w_r[0]), bf(o_w_k[0]), bf(o_w_v[0]), row(o_w0[0]),
        bf(o_w_w1[0]), bf(o_w_w2[0]), row(o_a0[0]), bf(o_w_a1[0]), bf(o_w_a2[0]), bf(o_w_g1[0]), bf(o_w_g2[0]))
    sl = lambda t: _to_scan_layout(t, L)
    y = _rwkv_scan(sl(r), sl(k), sl(v), sl(w), sl(a), _head_param(o_k_k[0]), _head_param(o_k_a[0]),
                   _head_param(o_r_k[0].reshape(-1)), _head_param(o_lnx_g[0]), _head_param(o_lnx_b[0]))
    y = _from_scan_layout(y, L)
    h, hb, hbt = _mix_out(h, [y], [bf(o_w_o[0])], g, row(norm_ffn_g[1]))
    out = _peer_layer(h, hb, hbt, peer_w_q[1], peer_sub_keys[1], peer_u[1], peer_v[1], row(final_g), True)
    return out.reshape(L, bsz, D).transpose(1, 0, 2)
```

```python
import functools
import math

import jax
import jax.numpy as jnp
from jax import lax
from jax.experimental import pallas as pl
from jax.experimental.pallas import tpu as pltpu

F32 = jnp.float32
BF16 = jnp.bfloat16
HIGHEST = lax.Precision.HIGHEST

D_MODEL = 1024
BATCH = 8
NORM_EPS = 1e-6
S5_WIDTH = 512
S5_GROUP = 16
S5_GROUPS = 32
S5_STATE = 64
S5_SLABS = 4
GLA_HEADS = 4
GLA_DV = 128
GLA_DK = 64
GLA_DKP = 128
GLA_RANK = 16
GLA_TAU = 16.0
GLA_CHUNK = 64
RWKV_HEAD = 64
RWKV_HEADS = 16
RWKV_GN_EPS = 64e-5
PEER_HEADS = 8
PEER_NKEYS = 128
PEER_TOPK = 16
PEER_TILE = 1024
GATE_STEP = 2.0
LANE = 128
SUBLANE = 8
VMEM_LIMIT = 56 * 1024 * 1024

PROJ_W = 3 * 512 + 2 * GLA_HEADS * GLA_DKP + LANE


def _dot(a, b, hi=False):
    if hi:
        return jnp.dot(a, b, preferred_element_type=F32, precision=HIGHEST)
    return jnp.dot(a.astype(BF16), b.astype(BF16), preferred_element_type=F32)


def _dot_nt(a, b):
    return lax.dot_general(a.astype(BF16), b.astype(BF16), (((1,), (1,)), ((), ())),
                           preferred_element_type=F32)


def _rms(x, g):
    return x * lax.rsqrt(jnp.mean(x * x, axis=-1, keepdims=True) + NORM_EPS) * g


def _sigmoid(x):
    return 1.0 / (1.0 + jnp.exp(-x))


def _softplus(x):
    return jnp.maximum(x, 0.0) + jnp.log(1.0 + jnp.exp(-jnp.abs(x)))


def _gelu(x):
    return 0.5 * x * (1.0 + lax.erf(x * (1.0 / math.sqrt(2.0))))


def _const_spec(shape):
    nd = len(shape)
    return pl.BlockSpec(shape, lambda *_: (0,) * nd)


def _params(sem, flags=None):
    return pltpu.CompilerParams(dimension_semantics=sem, vmem_limit_bytes=VMEM_LIMIT, flags=flags)


def _even_front_kernel(h_ref, g_ref, w_ref, o_ref):
    xn = _rms(h_ref[...], g_ref[...])
    o_ref[...] = _dot(xn, w_ref[...])


def _even_front(h, g, w):
    T = h.shape[0]
    tm = 256
    return pl.pallas_call(
        _even_front_kernel,
        grid=(T // tm,),
        in_specs=[pl.BlockSpec((tm, D_MODEL), lambda i: (i, 0)),
                  _const_spec((1, D_MODEL)),
                  _const_spec((D_MODEL, PROJ_W))],
        out_specs=pl.BlockSpec((tm, PROJ_W), lambda i: (i, 0)),
        out_shape=jax.ShapeDtypeStruct((T, PROJ_W), F32),
        compiler_params=_params(("parallel",)),
        name="even_front",
    )(h, g, w)


def _s5_kernel(u_ref, bd_ref, cd_ref, are_ref, aim_ref, d_ref, wglu_ref, o_ref, bu_ref, st_ref):
    rows = u_ref.shape[0]
    steps = rows // BATCH
    half = 512

    @pl.when(pl.program_id(0) == 0)
    def _():
        st_ref[...] = jnp.zeros_like(st_ref)

    u = u_ref[...]
    for s in range(S5_SLABS):
        bu_ref[s] = _dot(u[:, s * LANE:(s + 1) * LANE], bd_ref[s])

    def step(t, carry):
        r0 = pl.multiple_of(t * BATCH, BATCH)
        out = []
        for s in range(S5_SLABS):
            sre, sim = carry[2 * s], carry[2 * s + 1]
            ar, ai = are_ref[s], aim_ref[s]
            bre = bu_ref[s, pl.ds(r0, BATCH), 0:half]
            bim = bu_ref[s, pl.ds(r0, BATCH), half:2 * half]
            nre = ar * sre - ai * sim + bre
            nim = ar * sim + ai * sre + bim
            bu_ref[s, pl.ds(r0, BATCH), 0:half] = nre
            bu_ref[s, pl.ds(r0, BATCH), half:2 * half] = nim
            out += [nre, nim]
        return tuple(out)

    init = tuple(st_ref[j] for j in range(2 * S5_SLABS))
    fin = lax.fori_loop(0, steps, step, init)
    for j in range(2 * S5_SLABS):
        st_ref[j] = fin[j]

    y = jnp.concatenate([_dot(bu_ref[s], cd_ref[s]) for s in range(S5_SLABS)], axis=1)
    y = y + d_ref[...] * u
    g = _gelu(y)
    o_ref[...] = g * _sigmoid(_dot(g, wglu_ref[...]))


def _s5(proj, bd, cd, are, aim, d, wglu):
    T = proj.shape[0]
    rows = 256
    return pl.pallas_call(
        _s5_kernel,
        grid=(T // rows,),
        in_specs=[pl.BlockSpec((rows, S5_WIDTH), lambda i: (i, 0)),
                  _const_spec(bd.shape), _const_spec(cd.shape),
                  _const_spec(are.shape), _const_spec(aim.shape),
                  _const_spec(d.shape), _const_spec(wglu.shape)],
        out_specs=pl.BlockSpec((rows, S5_WIDTH), lambda i: (i, 0)),
        out_shape=jax.ShapeDtypeStruct((T, S5_WIDTH), F32),
        scratch_shapes=[pltpu.VMEM((S5_SLABS, rows, 2 * 512), F32),
                        pltpu.VMEM((2 * S5_SLABS, BATCH, 512), F32)],
        compiler_params=_params(("arbitrary",)),
        name="s5",
    )(proj, bd, cd, are, aim, d, wglu)


def _gla_kernel(v_ref, r_ref, q_ref, k_ref, gl_ref, wg_ref, bg_ref, lm_ref, ng_ref, o_ref, s_ref):
    rows = q_ref.shape[0]
    width = GLA_HEADS * GLA_DKP

    @pl.when(pl.program_id(0) == 0)
    def _():
        s_ref[...] = jnp.zeros_like(s_ref)

    z = _dot(gl_ref[...], wg_ref[...], True) + bg_ref[...]
    la = -_softplus(-z) * (1.0 / GLA_TAU) * lm_ref[...]
    b = la
    sh = BATCH
    while sh < rows:
        b = b + jnp.concatenate([jnp.zeros((sh, width), F32), b[:rows - sh]], axis=0)
        sh *= 2
    blast = b[rows - BATCH:rows]
    blast_t = jnp.concatenate([blast] * GLA_CHUNK, axis=0)
    q = q_ref[...] * (GLA_DK ** -0.5)
    k = k_ref[...]
    qd = q * jnp.exp(b)
    kd = k * jnp.exp(-b)
    kend = k * jnp.exp(blast_t - b)

    ri = lax.broadcasted_iota(jnp.int32, (rows, rows), 0)
    ci = lax.broadcasted_iota(jnp.int32, (rows, rows), 1)
    causal = jnp.logical_and(((ri - ci) & (BATCH - 1)) == 0, ri >= ci)
    rowb = lax.broadcasted_iota(jnp.int32, (rows, GLA_DKP), 0) & (BATCH - 1)
    laneb = lax.broadcasted_iota(jnp.int32, (GLA_DKP, rows), 1) & (BATCH - 1)

    for h in range(GLA_HEADS):
        sl = slice(h * GLA_DKP, (h + 1) * GLA_DKP)
        vh = v_ref[:, h * GLA_DV:(h + 1) * GLA_DV]
        qh, kdh, keh, bh = qd[:, sl], kd[:, sl], kend[:, sl], b[:, sl]
        att = jnp.where(causal, _dot_nt(qh, kdh), 0.0)
        o = _dot(att, vh)
        qexp = jnp.concatenate([jnp.where(rowb == bb, qh, 0.0) for bb in range(BATCH)], axis=1)
        o = o + _dot(qexp, s_ref[h])
        ket = keh.T
        kexp_t = jnp.concatenate([jnp.where(laneb == bb, ket, 0.0) for bb in range(BATCH)], axis=0)
        kv = _dot(kexp_t, vh)
        bt = bh[rows - GLA_DKP:rows].T
        dec = jnp.concatenate(
            [jnp.broadcast_to(jnp.exp(bt[:, GLA_DKP - BATCH + bb:GLA_DKP - BATCH + bb + 1]),
                              (GLA_DKP, GLA_DV)) for bb in range(BATCH)], axis=0)
        s_ref[h] = s_ref[h] * dec + kv
        o = o * lax.rsqrt(jnp.mean(o * o, axis=-1, keepdims=True) + NORM_EPS)
        o = o * ng_ref[:, h * GLA_DV:(h + 1) * GLA_DV]
        rr = r_ref[:, h * GLA_DV:(h + 1) * GLA_DV]
        o_ref[:, h * GLA_DV:(h + 1) * GLA_DV] = o * (rr * _sigmoid(rr))


def _gla(proj, wg, bg, lm, ng):
    T = proj.shape[0]
    rows = GLA_CHUNK * BATCH
    width = GLA_HEADS * GLA_DKP
    return pl.pallas_call(
        _gla_kernel,
        grid=(T // rows,),
        in_specs=[pl.BlockSpec((rows, 512), lambda i: (i, 1)),
                  pl.BlockSpec((rows, 512), lambda i: (i, 2)),
                  pl.BlockSpec((rows, width), lambda i: (i, 3)),
                  pl.BlockSpec((rows, width), lambda i: (i, 4)),
                  pl.BlockSpec((rows, LANE), lambda i: (i, (3 * 512 + 2 * width) // LANE)),
                  _const_spec(wg.shape), _const_spec(bg.shape), _const_spec(lm.shape),
                  _const_spec(ng.shape)],
        out_specs=pl.BlockSpec((rows, GLA_HEADS * GLA_DV), lambda i: (i, 0)),
        out_shape=jax.ShapeDtypeStruct((T, GLA_HEADS * GLA_DV), F32),
        scratch_shapes=[pltpu.VMEM((GLA_HEADS, BATCH * GLA_DKP, GLA_DV), F32)],
        compiler_params=_params(("arbitrary",)),
        name="gla",
    )(proj, proj, proj, proj, proj, wg, bg, lm, ng)


def _mix_out_kernel(n_in, has_gate, *refs):
    h_ref = refs[0]
    y_refs = refs[1:1 + n_in]
    w_refs = refs[1 + n_in:1 + 2 * n_in]
    rest = refs[1 + 2 * n_in:]
    gate_ref = rest[0] if has_gate else None
    g_ref, ho_ref, hb_ref, hbt_ref = rest[1:] if has_gate else rest
    acc = h_ref[...]
    for y_ref, w_ref in zip(y_refs, w_refs):
        if len(y_ref.shape) == 3:
            y = jnp.concatenate([_scan_to_rows(y_ref, t0) for t0 in range(0, y_ref.shape[0], SCAN_BLOCK)],
                                axis=0)
        else:
            y = y_ref[...]
        if has_gate:
            y = y * gate_ref[...]
        acc = acc + _dot(y, w_ref[...])
    ho_ref[...] = acc
    hn = _rms(acc, g_ref[...])
    hb_ref[...] = hn.astype(BF16)
    hbt_ref[...] = hn.T.astype(BF16)


def _mix_out(h, ys, ws, gate, g):
    T = h.shape[0]
    tm = 256
    n_in = len(ys)
    row = pl.BlockSpec((tm, D_MODEL), lambda i: (i, 0))
    in_specs = [row]
    in_specs += [pl.BlockSpec((tm, y.shape[1]), lambda i: (i, 0)) if y.ndim == 2 else
                 pl.BlockSpec((tm // BATCH,) + y.shape[1:], lambda i: (i, 0, 0)) for y in ys]
    in_specs += [_const_spec(w.shape) for w in ws]
    args = [h, *ys, *ws]
    if gate is not None:
        in_specs.append(row)
        args.append(gate)
    in_specs.append(_const_spec((1, D_MODEL)))
    args.append(g)
    return pl.pallas_call(
        functools.partial(_mix_out_kernel, n_in, gate is not None),
        grid=(T // tm,),
        in_specs=in_specs,
        out_specs=[row, row, pl.BlockSpec((D_MODEL, tm), lambda i: (0, i))],
        out_shape=[jax.ShapeDtypeStruct((T, D_MODEL), F32),
                   jax.ShapeDtypeStruct((T, D_MODEL), BF16),
                   jax.ShapeDtypeStruct((D_MODEL, T), BF16)],
        compiler_params=_params(("parallel",)),
        name="mix_out",
    )(*args)


def _tree_sum(xs):
    while len(xs) > 1:
        xs = [xs[i] + xs[i + 1] for i in range(0, len(xs) - 1, 2)] + ([xs[-1]] if len(xs) % 2 else [])
    return xs[0]


def _insert(tops, x):
    out = []
    for t in tops:
        out.append(jnp.maximum(t, x))
        x = jnp.minimum(t, x)
    return out


def _top_values(scr_ref):
    def body(k, tops):
        x = scr_ref[pl.ds(k, SUBLANE, stride=PEER_NKEYS), :]
        return tuple(_insert(list(tops), x))
    init = tuple(jnp.full((SUBLANE, LANE), -jnp.inf, F32) for _ in range(PEER_TOPK))
    return lax.fori_loop(0, PEER_NKEYS, body, init, unroll=8)


def _peer_topk_kernel(hn_ref, wq_ref, keys_ref, n1_ref, e1_ref, r2_ref, e2_ref,
                      scr1, scr2, scr3, scr4):
    nch = hn_ref.shape[0] // LANE
    q = _dot(hn_ref[...], wq_ref[...])
    for c, scr in enumerate((scr1, scr2)):
        st = _dot_nt(keys_ref[0, c], q[:, c * LANE:(c + 1) * LANE])
        for j in range(nch):
            scr[j * PEER_NKEYS:(j + 1) * PEER_NKEYS, :] = st[:, j * LANE:(j + 1) * LANE]
    a = _top_values(scr1)
    b = _top_values(scr2)
    pairs = [(i, j) for i in range(PEER_TOPK) for j in range(PEER_TOPK) if (i + 1) * (j + 1) <= PEER_TOPK]
    tops = [jnp.full((SUBLANE, LANE), -jnp.inf, F32) for _ in range(PEER_TOPK)]
    for i, j in pairs:
        tops = _insert(tops, a[i] + b[j])
    thr = tops[PEER_TOPK - 1]
    z = jnp.zeros((SUBLANE, LANE), F32)
    for t in tops:
        z = z + jnp.exp(t - tops[0])
    inv_z = 1.0 / z
    cnt = [jnp.zeros((SUBLANE, LANE), F32) for _ in range(PEER_TOPK)]
    for i, j in pairs:
        cnt[i] = cnt[i] + jnp.where(a[i] + b[j] >= thr, GATE_STEP, 0.0)

    def first_half(k, carry):
        rows = pl.ds(k, SUBLANE, stride=PEER_NKEYS)
        x = scr1[rows, :]
        scr3[rows, :] = _tree_sum([jnp.where(x == a[i], cnt[i], 0.0) for i in range(PEER_TOPK)])
        scr1[rows, :] = jnp.exp(x - a[0]) * inv_z
        return carry

    def second_half(k, carry):
        rows = pl.ds(k, SUBLANE, stride=PEER_NKEYS)
        x = scr2[rows, :]
        scr4[rows, :] = _tree_sum([jnp.where(b[j] > x, GATE_STEP, 0.0) for j in range(PEER_TOPK)])
        scr2[rows, :] = jnp.exp(x - b[0])
        return carry

    lax.fori_loop(0, PEER_NKEYS, first_half, 0, unroll=4)
    lax.fori_loop(0, PEER_NKEYS, second_half, 0, unroll=4)
    shape = (nch, PEER_NKEYS, LANE)
    n1_ref[0] = scr3[...].reshape(shape)
    e1_ref[0] = scr1[...].reshape(shape)
    r2_ref[0] = scr4[...].astype(BF16).reshape(shape)
    e2_ref[0] = scr2[...].astype(BF16).reshape(shape)


def _peer_topk(hn, wq, keys):
    T = hn.shape[0]
    tt = SUBLANE * LANE
    nch = T // LANE
    shape = (PEER_HEADS, nch, PEER_NKEYS, LANE)
    spec = pl.BlockSpec((1, SUBLANE, PEER_NKEYS, LANE), lambda i, h: (h, i, 0, 0))
    return pl.pallas_call(
        _peer_topk_kernel,
        grid=(T // tt, PEER_HEADS),
        in_specs=[pl.BlockSpec((tt, D_MODEL), lambda i, h: (i, 0)),
                  pl.BlockSpec((D_MODEL, 2 * LANE), lambda i, h: (0, h)),
                  pl.BlockSpec((1, 2, PEER_NKEYS, LANE), lambda i, h: (h, 0, 0, 0))],
        out_specs=[spec, spec, spec, spec],
        out_shape=[jax.ShapeDtypeStruct(shape, F32), jax.ShapeDtypeStruct(shape, F32),
                   jax.ShapeDtypeStruct(shape, BF16), jax.ShapeDtypeStruct(shape, BF16)],
        scratch_shapes=[pltpu.VMEM((tt, LANE), F32)] * 4,
        compiler_params=_params(("parallel", "arbitrary")),
        name="peer_topk",
    )(hn, wq, keys)


def _row_bf16(ref, h, c, i1):
    packed_rows = 2 * SUBLANE
    x = jnp.broadcast_to(ref[h, c, pl.ds(i1, 1), :], (packed_rows, LANE)).astype(BF16)
    return jnp.concatenate([x] * (PEER_NKEYS // packed_rows), axis=0)


def _peer_main_kernel(final, hbt_ref, hres_ref, u0_ref, ub_ref, ua_ref, vta_ref, vtb_ref,
                      n1_ref, e1_ref, r2_ref, e2_ref, fg_ref, o_ref,
                      acc_ref, hga_ref, hgb_ref, hta_ref, htb_ref):
    e = pl.program_id(1)
    n_pairs = pl.num_programs(1) - 1
    tt = hbt_ref.shape[1]
    n_i1 = PEER_TILE // PEER_NKEYS
    n_ch = tt // LANE
    zero = jnp.zeros((PEER_NKEYS, LANE), BF16)

    @pl.when(e == 0)
    def _():
        acc_ref[...] = jnp.zeros_like(acc_ref)
        hgb_ref[...] = jnp.zeros_like(hgb_ref)
        hta_ref[...] = _dot(u0_ref[...], hbt_ref[...])

    def gates(tile, ht_ref, hg_ref, c):
        for il in range(n_i1):
            i1 = tile * n_i1 + il
            gate = zero
            for h in range(PEER_HEADS):
                code = _row_bf16(n1_ref, h, c, i1) - r2_ref[h, c]
                w = _row_bf16(e1_ref, h, c, i1) * e2_ref[h, c]
                gate = gate + jnp.maximum(jnp.minimum(w, code), zero)
            rows = slice(il * PEER_NKEYS, (il + 1) * PEER_NKEYS)
            cols = slice(c * LANE, (c + 1) * LANE)
            hg_ref[rows, cols] = _gelu(ht_ref[rows, cols]).astype(BF16) * gate

    def phase(tile, ht_ref, hg_ref, vt_prev_ref, hg_prev_ref, u_next_ref, ht_next_ref):
        half = tt // 2
        for piece in range(4):
            cols = slice((piece % 2) * half, (piece % 2 + 1) * half)
            if piece < 2:
                acc_ref[:, cols] += _dot(vt_prev_ref[...], hg_prev_ref[:, cols])
            else:
                ht_next_ref[:, cols] = _dot(u_next_ref[...], hbt_ref[:, cols])
            gates(tile, ht_ref, hg_ref, piece)

    @pl.when(e < n_pairs)
    def _():
        phase(2 * e, hta_ref, hga_ref, vtb_ref, hgb_ref, ub_ref, htb_ref)
        phase(2 * e + 1, htb_ref, hgb_ref, vta_ref, hga_ref, ua_ref, hta_ref)

    @pl.when(e == n_pairs)
    def _():
        acc = acc_ref[...] + _dot(vtb_ref[...], hgb_ref[...])
        out = hres_ref[...] + acc.T
        if final:
            out = _rms(out, fg_ref[...])
        o_ref[...] = out


def _peer_main(hbt, hres, u, vt, n1, e1, r2, e2, fg, final):
    T = hres.shape[0]
    E = u.shape[0]
    tt = 512
    n_tiles = E // PEER_TILE
    n_pairs = n_tiles // 2
    big_spec = pl.BlockSpec((PEER_HEADS, tt // LANE, PEER_NKEYS, LANE), lambda i, e: (0, i, 0, 0))
    tile_of = lambda f: (lambda i, e: (jnp.clip(f(e), 0, n_tiles - 1), 0, 0))
    vt_spec = lambda f: pl.BlockSpec((D_MODEL, PEER_TILE), lambda i, e: tile_of(f)(i, e)[:2])
    u_spec = lambda f: pl.BlockSpec((PEER_TILE, D_MODEL), lambda i, e: tile_of(f)(i, e)[:2])
    return pl.pallas_call(
        functools.partial(_peer_main_kernel, final),
        grid=(T // tt, n_pairs + 1),
        in_specs=[pl.BlockSpec((D_MODEL, tt), lambda i, e: (0, i)),
                  pl.BlockSpec((tt, D_MODEL), lambda i, e: (i, 0)),
                  u_spec(lambda e: 0), u_spec(lambda e: 2 * e + 1), u_spec(lambda e: 2 * e + 2),
                  vt_spec(lambda e: 2 * e), vt_spec(lambda e: 2 * e - 1),
                  big_spec, big_spec, big_spec, big_spec,
                  _const_spec((1, D_MODEL))],
        out_specs=pl.BlockSpec((tt, D_MODEL), lambda i, e: (i, 0)),
        out_shape=jax.ShapeDtypeStruct((T, D_MODEL), F32),
        scratch_shapes=[pltpu.VMEM((D_MODEL, tt), F32),
                        pltpu.VMEM((PEER_TILE, tt), BF16), pltpu.VMEM((PEER_TILE, tt), BF16),
                        pltpu.VMEM((PEER_TILE, tt), F32), pltpu.VMEM((PEER_TILE, tt), F32)],
        compiler_params=_params(("parallel", "arbitrary")),
        name="peer_main",
    )(hbt, hres, u, u, u, vt, vt, n1, e1, r2, e2, fg)


def _peer_layer(h, hb, hbt, wq, keys, u_tab, v_tab, fg, final):
    n1, e1, r2, e2 = _peer_topk(hb, wq.astype(BF16), keys.astype(BF16))
    vt = v_tab.astype(BF16).reshape(-1, PEER_TILE, D_MODEL).transpose(0, 2, 1).reshape(-1, PEER_TILE)
    return _peer_main(hbt, h, u_tab.astype(BF16), vt, n1, e1, r2, e2, fg, final)


SCAN_BLOCK = LANE // BATCH


def _swap_groups(slabs):
    a = list(slabs)
    group = lax.broadcasted_iota(jnp.int32, a[0].shape, 1) // BATCH
    s = SCAN_BLOCK // 2
    while s >= 1:
        upper = (group & s) != 0
        for i in range(SCAN_BLOCK):
            if i & s == 0:
                ai, aj = a[i], a[i + s]
                a[i] = jnp.where(upper, pltpu.roll(aj, s * BATCH, axis=1), ai)
                a[i + s] = jnp.where(upper, aj, pltpu.roll(ai, LANE - s * BATCH, axis=1))
        s //= 2
    return a


def _rows_to_scan(x, ref):
    z = x.T
    slabs = [z[h * RWKV_HEAD:(h + 1) * RWKV_HEAD, :] for h in range(RWKV_HEADS)]
    for t, blk in enumerate(_swap_groups(slabs)):
        ref[t] = blk


def _scan_to_rows(ref, t0):
    slabs = _swap_groups([ref[t0 + t] for t in range(SCAN_BLOCK)])
    return jnp.concatenate(slabs, axis=0).T


def _rwkv_front_kernel(h_ref, hp_ref, g_ref, mu_ref, wr_ref, wk_ref, wv_ref, w0_ref, ww1_ref, ww2_ref,
                       a0_ref, wa1_ref, wa2_ref, wg1_ref, wg2_ref,
                       r_ref, k_ref, v_ref, w_ref, a_ref, go_ref):
    g = g_ref[...]
    x = _rms(h_ref[...], g)
    xp8 = _rms(hp_ref[...], g)
    xp8 = jnp.where(pl.program_id(0) == 0, 0.0, xp8)
    rows = x.shape[0]
    xprev = jnp.concatenate([xp8, x[:rows - BATCH]], axis=0)
    xx = xprev - x
    mix = lambda j: x + xx * mu_ref[j:j + 1, :]
    _rows_to_scan(_dot(mix(0), wr_ref[...]), r_ref)
    _rows_to_scan(w0_ref[...] + _dot(jnp.tanh(_dot(mix(1), ww1_ref[...])), ww2_ref[...]), w_ref)
    _rows_to_scan(_dot(mix(2), wk_ref[...]), k_ref)
    _rows_to_scan(_dot(mix(3), wv_ref[...]), v_ref)
    _rows_to_scan(_sigmoid(a0_ref[...] + _dot(_dot(mix(4), wa1_ref[...]), wa2_ref[...])), a_ref)
    go_ref[...] = _dot(_sigmoid(_dot(mix(5), wg1_ref[...])), wg2_ref[...])


def _rwkv_front(h, g, mu, wr, wk, wv, w0, ww1, ww2, a0, wa1, wa2, wg1, wg2):
    assert RWKV_HEADS == SCAN_BLOCK
    T = h.shape[0]
    tm = SCAN_BLOCK * BATCH
    row = pl.BlockSpec((tm, D_MODEL), lambda i: (i, 0))
    prev = pl.BlockSpec((BATCH, D_MODEL), lambda i: (jnp.maximum(i * (tm // BATCH) - 1, 0), 0))
    scan = pl.BlockSpec((SCAN_BLOCK, RWKV_HEAD, LANE), lambda i: (i, 0, 0))
    consts = [g, mu, wr, wk, wv, w0, ww1, ww2, a0, wa1, wa2, wg1, wg2]
    scan_out = jax.ShapeDtypeStruct((T // BATCH, RWKV_HEAD, LANE), F32)
    return pl.pallas_call(
        _rwkv_front_kernel,
        grid=(T // tm,),
        in_specs=[row, prev] + [_const_spec(c.shape) for c in consts],
        out_specs=[scan] * 5 + [row],
        out_shape=[scan_out] * 5 + [jax.ShapeDtypeStruct((T, D_MODEL), F32)],
        compiler_params=_params(("parallel",)),
        name="rwkv_front",
    )(h, h, *consts)


def _rwkv_scan_kernel(r_ref, k_ref, v_ref, w_ref, a_ref, kk_ref, ka_ref, rk_ref, lg_ref, lb_ref,
                      y_ref, s_ref, dec_s, k2_s, av_s, bv_s, wr_s, br_s, kr_s):
    tl = r_ref.shape[0]
    n = RWKV_HEAD
    ng = n // SUBLANE

    @pl.when(pl.program_id(0) == 0)
    def _():
        s_ref[...] = jnp.zeros_like(s_ref)

    r = r_ref[...]
    k = k_ref[...]
    a = a_ref[...]
    kk = k * kk_ref[...]
    kk = kk / jnp.maximum(jnp.sqrt(jnp.sum(kk * kk, axis=1, keepdims=True)), 1e-12)
    k2 = k * (1.0 + (a - 1.0) * ka_ref[...])
    dec = jnp.exp(-jnp.exp(-_softplus(-w_ref[...]) - 0.5))
    bv = kk * a
    dec_s[...] = dec
    k2_s[...] = k2
    av_s[...] = -kk
    bv_s[...] = bv
    wr_s[...] = dec * r
    br_s[...] = jnp.sum(bv * r, axis=1, keepdims=True)
    kr_s[...] = jnp.sum(k2 * r, axis=1, keepdims=True)

    def row(ref, t, j):
        return jnp.broadcast_to(ref[t, pl.ds(j, 1), :], (SUBLANE, LANE))

    def step(t, carry):
        def reduce_body(j, accs):
            sa, sy = accs
            a_b = row(av_s, t, j)
            wr_b = row(wr_s, t, j)
            sa_n, sy_n = [], []
            for vg in range(ng):
                sv = s_ref[j, vg * SUBLANE:(vg + 1) * SUBLANE, :]
                sa_n.append(sa[vg] + sv * a_b)
                sy_n.append(sy[vg] + sv * wr_b)
            return tuple(sa_n), tuple(sy_n)

        zeros = tuple(jnp.zeros((SUBLANE, LANE), F32) for _ in range(ng))
        sa, sy = lax.fori_loop(0, n, reduce_body, (zeros, zeros), unroll=8)
        br = jnp.broadcast_to(br_s[t], (SUBLANE, LANE))
        kr = jnp.broadcast_to(kr_s[t], (SUBLANE, LANE))
        vv = [v_ref[t, vg * SUBLANE:(vg + 1) * SUBLANE, :] for vg in range(ng)]
        for vg in range(ng):
            y_ref[t, vg * SUBLANE:(vg + 1) * SUBLANE, :] = sy[vg] + sa[vg] * br + vv[vg] * kr

        def update_body(j, c):
            w_b = row(dec_s, t, j)
            b_b = row(bv_s, t, j)
            k_b = row(k2_s, t, j)
            for vg in range(ng):
                sl = slice(vg * SUBLANE, (vg + 1) * SUBLANE)
                s_ref[j, sl, :] = s_ref[j, sl, :] * w_b + sa[vg] * b_b + vv[vg] * k_b
            return c

        lax.fori_loop(0, n, update_body, 0, unroll=8)
        return carry

    lax.fori_loop(0, tl, step, 0)

    y = y_ref[...]
    mean = jnp.mean(y, axis=1, keepdims=True)
    var = jnp.mean((y - mean) ** 2, axis=1, keepdims=True)
    y = (y - mean) * lax.rsqrt(var + RWKV_GN_EPS) * lg_ref[...] + lb_ref[...]
    bonus = jnp.sum(r * k2 * rk_ref[...], axis=1, keepdims=True) * v_ref[...]
    y_ref[...] = y + bonus


def _rwkv_scan(r, k, v, w, a, kk, ka, rk, lg, lb):
    L = r.shape[0]
    tl = 32
    n = RWKV_HEAD
    blk = pl.BlockSpec((tl, n, LANE), lambda i: (i, 0, 0))
    par = _const_spec((n, LANE))
    big = pltpu.VMEM((tl, n, LANE), F32)
    small = pltpu.VMEM((tl, 1, LANE), F32)
    return pl.pallas_call(
        _rwkv_scan_kernel,
        grid=(L // tl,),
        in_specs=[blk] * 5 + [par] * 5,
        out_specs=blk,
        out_shape=jax.ShapeDtypeStruct((L, n, LANE), F32),
        scratch_shapes=[pltpu.VMEM((n, n, LANE), F32), big, big, big, big, big, small, small],
        compiler_params=_params(("arbitrary",)),
        name="rwkv_scan",
    )(r, k, v, w, a, kk, ka, rk, lg, lb)


def _head_param(p):
    return jnp.repeat(p.reshape(RWKV_HEADS, RWKV_HEAD).T, BATCH, axis=1)


def _prep_even(w_in, a_re, a_im, log_dt, b_re, b_im, c_re, c_im, w_g2, b_g2):
    o = 0
    w_u = w_in[:, o:o + 512]; o += 512
    w_q = w_in[:, o:o + 256]; o += 256
    w_k = w_in[:, o:o + 256]; o += 256
    w_v = w_in[:, o:o + 512]; o += 512
    w_gl = w_in[:, o:o + GLA_RANK]; o += GLA_RANK
    w_r = w_in[:, o:o + 512]

    def pad_heads(w):
        w = w.reshape(-1, GLA_HEADS, GLA_DK)
        w = jnp.pad(w, ((0, 0), (0, 0), (0, GLA_DKP - GLA_DK)))
        return w.reshape(-1, GLA_HEADS * GLA_DKP)

    w_proj = jnp.concatenate([w_u, w_v, w_r, pad_heads(w_q), pad_heads(w_k),
                              jnp.pad(w_gl, ((0, 0), (0, LANE - GLA_RANK)))], axis=1)
    wg = jnp.pad(pad_heads(w_g2), ((0, LANE - GLA_RANK), (0, 0)))
    bg = pad_heads(b_g2[None, :])
    lm = pad_heads(jnp.ones((1, GLA_HEADS * GLA_DK), F32))

    dt = jnp.exp(log_dt)
    mag = jnp.exp(dt * a_re)
    abar_re = mag * jnp.cos(dt * a_im)
    abar_im = mag * jnp.sin(dt * a_im)
    nr = abar_re - 1.0
    den = a_re * a_re + a_im * a_im
    f_re = (nr * a_re + abar_im * a_im) / den
    f_im = (abar_im * a_re - nr * a_im) / den
    bbar_re = f_re[..., None] * b_re - f_im[..., None] * b_im
    bbar_im = f_re[..., None] * b_im + f_im[..., None] * b_re
    eye = jnp.eye(8, dtype=F32)
    gs = S5_GROUPS // S5_SLABS

    def bd(bb):
        bb = bb.reshape(S5_SLABS, gs, S5_STATE, S5_GROUP)
        return jnp.einsum('sgpc,gh->sgchp', bb, eye).reshape(S5_SLABS, gs * S5_GROUP, gs * S5_STATE)

    def cdm(cc):
        cc = cc.reshape(S5_SLABS, gs, S5_GROUP, S5_STATE)
        return jnp.einsum('sgcp,gh->sgphc', cc, eye).reshape(S5_SLABS, gs * S5_STATE, gs * S5_GROUP)

    bd_all = jnp.concatenate([bd(bbar_re), bd(bbar_im)], axis=2)
    cd_all = jnp.concatenate([cdm(c_re), -cdm(c_im)], axis=1)
    are = jnp.broadcast_to(abar_re.reshape(S5_SLABS, 1, gs * S5_STATE), (S5_SLABS, BATCH, gs * S5_STATE))
    aim = jnp.broadcast_to(abar_im.reshape(S5_SLABS, 1, gs * S5_STATE), (S5_SLABS, BATCH, gs * S5_STATE))
    return w_proj, wg, bg, lm, bd_all, cd_all, are, aim


def kernel(x, norm_mix_g, norm_ffn_g, final_g, e_w_in, e_w_out, s5_a_re, s5_a_im, s5_log_dt, s5_b_re, s5_b_im, s5_c_re, s5_c_im, s5_d, s5_w_glu, gla_w_g2, gla_b_g2, gla_norm_g, o_mu, o_w_r, o_w_k, o_w_v, o_w0, o_w_w1, o_w_w2, o_a0, o_w_a1, o_w_a2, o_w_g1, o_w_g2, o_k_k, o_k_a, o_r_k, o_lnx_g, o_lnx_b, o_w_o, peer_w_q, peer_sub_keys, peer_u, peer_v):
    bsz, L, D = x.shape
    assert bsz == BATCH and D == D_MODEL
    T = bsz * L
    row = lambda p: p.reshape(1, -1)
    h = x.transpose(1, 0, 2).reshape(T, D)

    w_proj, wg, bg, lm, bd_all, cd_all, are, aim = _prep_even(
        e_w_in[0], s5_a_re[0], s5_a_im[0], s5_log_dt[0], s5_b_re[0], s5_b_im[0],
        s5_c_re[0], s5_c_im[0], gla_w_g2[0], gla_b_g2[0])
    bf = lambda p: p.astype(BF16)
    proj = _even_front(h, row(norm_mix_g[0]), bf(w_proj))
    y_s5 = _s5(proj, bf(bd_all), bf(cd_all), are, aim, row(s5_d[0]), bf(s5_w_glu[0]))
    y_gla = _gla(proj, wg, bg, lm, row(gla_norm_g[0]))
    w_out = bf(e_w_out[0])
    h, hb, hbt = _mix_out(h, [y_s5, y_gla], [w_out[:S5_WIDTH], w_out[S5_WIDTH:]], None, row(norm_ffn_g[0]))
    h = _peer_layer(h, hb, hbt, peer_w_q[0], peer_sub_keys[0], peer_u[0], peer_v[0], row(final_g), False)

    r, k, v, w, a, g = _rwkv_front(
        h, row(norm_mix_g[1]), o_mu[0], bf(o_w_r[0]), bf(o_w_k[0]), bf(o_w_v[0]), row(o_w0[0]),
        bf(o_w_w1[0]), bf(o_w_w2[0]), row(o_a0[0]), bf(o_w_a1[0]), bf(o_w_a2[0]), bf(o_w_g1[0]), bf(o_w_g2[0]))
    y = _rwkv_scan(r, k, v, w, a, _head_param(o_k_k[0]), _head_param(o_k_a[0]),
                   _head_param(o_r_k[0].reshape(-1)), _head_param(o_lnx_g[0]), _head_param(o_lnx_b[0]))
    h, hb, hbt = _mix_out(h, [y], [bf(o_w_o[0])], g, row(norm_ffn_g[1]))
    out = _peer_layer(h, hb, hbt, peer_w_q[1], peer_sub_keys[1], peer_u[1], peer_v[1], row(final_g), True)
    return out.reshape(L, bsz, D).transpose(1, 0, 2)
```

```python
import functools
import math

import jax
import jax.numpy as jnp
from jax import lax
from jax.experimental import pallas as pl
from jax.experimental.pallas import tpu as pltpu

F32 = jnp.float32
BF16 = jnp.bfloat16
HIGHEST = lax.Precision.HIGHEST

D_MODEL = 1024
BATCH = 8
NORM_EPS = 1e-6
S5_WIDTH = 512
S5_GROUP = 16
S5_GROUPS = 32
S5_STATE = 64
S5_SLABS = 4
GLA_HEADS = 4
GLA_DV = 128
GLA_DK = 64
GLA_DKP = 128
GLA_RANK = 16
GLA_TAU = 16.0
GLA_CHUNK = 64
RWKV_HEAD = 64
RWKV_HEADS = 16
RWKV_GN_EPS = 64e-5
PEER_HEADS = 8
PEER_NKEYS = 128
PEER_TOPK = 16
PEER_TILE = 1024
GATE_STEP = 2.0
LANE = 128
SUBLANE = 8
VMEM_LIMIT = 56 * 1024 * 1024

PROJ_W = 3 * 512 + 2 * GLA_HEADS * GLA_DKP + LANE


def _dot(a, b, hi=False):
    if hi:
        return jnp.dot(a, b, preferred_element_type=F32, precision=HIGHEST)
    return jnp.dot(a.astype(BF16), b.astype(BF16), preferred_element_type=F32)


def _dot_nt(a, b):
    return lax.dot_general(a.astype(BF16), b.astype(BF16), (((1,), (1,)), ((), ())),
                           preferred_element_type=F32)


def _rms(x, g):
    return x * lax.rsqrt(jnp.mean(x * x, axis=-1, keepdims=True) + NORM_EPS) * g


def _sigmoid(x):
    return 1.0 / (1.0 + jnp.exp(-x))


def _softplus(x):
    return jnp.maximum(x, 0.0) + jnp.log(1.0 + jnp.exp(-jnp.abs(x)))


def _gelu(x):
    return 0.5 * x * (1.0 + lax.erf(x * (1.0 / math.sqrt(2.0))))


def _const_spec(shape):
    nd = len(shape)
    return pl.BlockSpec(shape, lambda *_: (0,) * nd)


def _params(sem, flags=None):
    return pltpu.CompilerParams(dimension_semantics=sem, vmem_limit_bytes=VMEM_LIMIT, flags=flags)


def _even_front_kernel(h_ref, g_ref, w_ref, o_ref):
    xn = _rms(h_ref[...], g_ref[...])
    o_ref[...] = _dot(xn, w_ref[...])


def _even_front(h, g, w):
    T = h.shape[0]
    tm = 256
    return pl.pallas_call(
        _even_front_kernel,
        grid=(T // tm,),
        in_specs=[pl.BlockSpec((tm, D_MODEL), lambda i: (i, 0)),
                  _const_spec((1, D_MODEL)),
                  _const_spec((D_MODEL, PROJ_W))],
        out_specs=pl.BlockSpec((tm, PROJ_W), lambda i: (i, 0)),
        out_shape=jax.ShapeDtypeStruct((T, PROJ_W), F32),
        compiler_params=_params(("parallel",)),
        name="even_front",
    )(h, g, w)


def _s5_kernel(u_ref, bd_ref, cd_ref, are_ref, aim_ref, d_ref, wglu_ref, o_ref, bu_ref, st_ref):
    rows = u_ref.shape[0]
    steps = rows // BATCH
    half = 512

    @pl.when(pl.program_id(0) == 0)
    def _():
        st_ref[...] = jnp.zeros_like(st_ref)

    u = u_ref[...]
    for s in range(S5_SLABS):
        bu_ref[s] = _dot(u[:, s * LANE:(s + 1) * LANE], bd_ref[s])

    def step(t, carry):
        r0 = pl.multiple_of(t * BATCH, BATCH)
        out = []
        for s in range(S5_SLABS):
            sre, sim = carry[2 * s], carry[2 * s + 1]
            ar, ai = are_ref[s], aim_ref[s]
            bre = bu_ref[s, pl.ds(r0, BATCH), 0:half]
            bim = bu_ref[s, pl.ds(r0, BATCH), half:2 * half]
            nre = ar * sre - ai * sim + bre
            nim = ar * sim + ai * sre + bim
            bu_ref[s, pl.ds(r0, BATCH), 0:half] = nre
            bu_ref[s, pl.ds(r0, BATCH), half:2 * half] = nim
            out += [nre, nim]
        return tuple(out)

    init = tuple(st_ref[j] for j in range(2 * S5_SLABS))
    fin = lax.fori_loop(0, steps, step, init)
    for j in range(2 * S5_SLABS):
        st_ref[j] = fin[j]

    y = jnp.concatenate([_dot(bu_ref[s], cd_ref[s]) for s in range(S5_SLABS)], axis=1)
    y = y + d_ref[...] * u
    g = _gelu(y)
    o_ref[...] = g * _sigmoid(_dot(g, wglu_ref[...]))


def _s5(proj, bd, cd, are, aim, d, wglu):
    T = proj.shape[0]
    rows = 256
    return pl.pallas_call(
        _s5_kernel,
        grid=(T // rows,),
        in_specs=[pl.BlockSpec((rows, S5_WIDTH), lambda i: (i, 0)),
                  _const_spec(bd.shape), _const_spec(cd.shape),
                  _const_spec(are.shape), _const_spec(aim.shape),
                  _const_spec(d.shape), _const_spec(wglu.shape)],
        out_specs=pl.BlockSpec((rows, S5_WIDTH), lambda i: (i, 0)),
        out_shape=jax.ShapeDtypeStruct((T, S5_WIDTH), F32),
        scratch_shapes=[pltpu.VMEM((S5_SLABS, rows, 2 * 512), F32),
                        pltpu.VMEM((2 * S5_SLABS, BATCH, 512), F32)],
        compiler_params=_params(("arbitrary",)),
        name="s5",
    )(proj, bd, cd, are, aim, d, wglu)


def _gla_kernel(v_ref, r_ref, q_ref, k_ref, gl_ref, wg_ref, bg_ref, lm_ref, ng_ref, o_ref, s_ref):
    rows = q_ref.shape[0]
    width = GLA_HEADS * GLA_DKP

    @pl.when(pl.program_id(0) == 0)
    def _():
        s_ref[...] = jnp.zeros_like(s_ref)

    z = _dot(gl_ref[...], wg_ref[...], True) + bg_ref[...]
    la = -_softplus(-z) * (1.0 / GLA_TAU) * lm_ref[...]
    b = la
    sh = BATCH
    while sh < rows:
        b = b + jnp.concatenate([jnp.zeros((sh, width), F32), b[:rows - sh]], axis=0)
        sh *= 2
    blast = b[rows - BATCH:rows]
    blast_t = jnp.concatenate([blast] * GLA_CHUNK, axis=0)
    q = q_ref[...] * (GLA_DK ** -0.5)
    k = k_ref[...]
    qd = q * jnp.exp(b)
    kd = k * jnp.exp(-b)
    kend = k * jnp.exp(blast_t - b)

    ri = lax.broadcasted_iota(jnp.int32, (rows, rows), 0)
    ci = lax.broadcasted_iota(jnp.int32, (rows, rows), 1)
    causal = jnp.logical_and(((ri - ci) & (BATCH - 1)) == 0, ri >= ci)
    rowb = lax.broadcasted_iota(jnp.int32, (rows, GLA_DKP), 0) & (BATCH - 1)
    laneb = lax.broadcasted_iota(jnp.int32, (GLA_DKP, rows), 1) & (BATCH - 1)

    for h in range(GLA_HEADS):
        sl = slice(h * GLA_DKP, (h + 1) * GLA_DKP)
        vh = v_ref[:, h * GLA_DV:(h + 1) * GLA_DV]
        qh, kdh, keh, bh = qd[:, sl], kd[:, sl], kend[:, sl], b[:, sl]
        att = jnp.where(causal, _dot_nt(qh, kdh), 0.0)
        o = _dot(att, vh)
        qexp = jnp.concatenate([jnp.where(rowb == bb, qh, 0.0) for bb in range(BATCH)], axis=1)
        o = o + _dot(qexp, s_ref[h])
        ket = keh.T
        kexp_t = jnp.concatenate([jnp.where(laneb == bb, ket, 0.0) for bb in range(BATCH)], axis=0)
        kv = _dot(kexp_t, vh)
        bt = bh[rows - GLA_DKP:rows].T
        dec = jnp.concatenate(
            [jnp.broadcast_to(jnp.exp(bt[:, GLA_DKP - BATCH + bb:GLA_DKP - BATCH + bb + 1]),
                              (GLA_DKP, GLA_DV)) for bb in range(BATCH)], axis=0)
        s_ref[h] = s_ref[h] * dec + kv
        o = o * lax.rsqrt(jnp.mean(o * o, axis=-1, keepdims=True) + NORM_EPS)
        o = o * ng_ref[:, h * GLA_DV:(h + 1) * GLA_DV]
        rr = r_ref[:, h * GLA_DV:(h + 1) * GLA_DV]
        o_ref[:, h * GLA_DV:(h + 1) * GLA_DV] = o * (rr * _sigmoid(rr))


def _gla(proj, wg, bg, lm, ng):
    T = proj.shape[0]
    rows = GLA_CHUNK * BATCH
    width = GLA_HEADS * GLA_DKP
    return pl.pallas_call(
        _gla_kernel,
        grid=(T // rows,),
        in_specs=[pl.BlockSpec((rows, 512), lambda i: (i, 1)),
                  pl.BlockSpec((rows, 512), lambda i: (i, 2)),
                  pl.BlockSpec((rows, width), lambda i: (i, 3)),
                  pl.BlockSpec((rows, width), lambda i: (i, 4)),
                  pl.BlockSpec((rows, LANE), lambda i: (i, (3 * 512 + 2 * width) // LANE)),
                  _const_spec(wg.shape), _const_spec(bg.shape), _const_spec(lm.shape),
                  _const_spec(ng.shape)],
        out_specs=pl.BlockSpec((rows, GLA_HEADS * GLA_DV), lambda i: (i, 0)),
        out_shape=jax.ShapeDtypeStruct((T, GLA_HEADS * GLA_DV), F32),
        scratch_shapes=[pltpu.VMEM((GLA_HEADS, BATCH * GLA_DKP, GLA_DV), F32)],
        compiler_params=_params(("arbitrary",)),
        name="gla",
    )(proj, proj, proj, proj, proj, wg, bg, lm, ng)


def _mix_out_kernel(n_in, has_gate, *refs):
    h_ref = refs[0]
    y_refs = refs[1:1 + n_in]
    w_refs = refs[1 + n_in:1 + 2 * n_in]
    rest = refs[1 + 2 * n_in:]
    gate_ref = rest[0] if has_gate else None
    g_ref, ho_ref, hb_ref, hbt_ref = rest[1:] if has_gate else rest
    acc = h_ref[...]
    for y_ref, w_ref in zip(y_refs, w_refs):
        if len(y_ref.shape) == 3:
            y = jnp.concatenate([_scan_to_rows(y_ref, t0) for t0 in range(0, y_ref.shape[0], SCAN_BLOCK)],
                                axis=0)
        else:
            y = y_ref[...]
        if has_gate:
            y = y * gate_ref[...]
        acc = acc + _dot(y, w_ref[...])
    ho_ref[...] = acc
    hn = _rms(acc, g_ref[...])
    hb_ref[...] = hn.astype(BF16)
    hbt_ref[...] = hn.T.astype(BF16)


def _mix_out(h, ys, ws, gate, g):
    T = h.shape[0]
    tm = 256
    n_in = len(ys)
    row = pl.BlockSpec((tm, D_MODEL), lambda i: (i, 0))
    in_specs = [row]
    in_specs += [pl.BlockSpec((tm, y.shape[1]), lambda i: (i, 0)) if y.ndim == 2 else
                 pl.BlockSpec((tm // BATCH,) + y.shape[1:], lambda i: (i, 0, 0)) for y in ys]
    in_specs += [_const_spec(w.shape) for w in ws]
    args = [h, *ys, *ws]
    if gate is not None:
        in_specs.append(row)
        args.append(gate)
    in_specs.append(_const_spec((1, D_MODEL)))
    args.append(g)
    return pl.pallas_call(
        functools.partial(_mix_out_kernel, n_in, gate is not None),
        grid=(T // tm,),
        in_specs=in_specs,
        out_specs=[row, row, pl.BlockSpec((D_MODEL, tm), lambda i: (0, i))],
        out_shape=[jax.ShapeDtypeStruct((T, D_MODEL), F32),
                   jax.ShapeDtypeStruct((T, D_MODEL), BF16),
                   jax.ShapeDtypeStruct((D_MODEL, T), BF16)],
        compiler_params=_params(("parallel",)),
        name="mix_out",
    )(*args)


def _tree_sum(xs):
    while len(xs) > 1:
        xs = [xs[i] + xs[i + 1] for i in range(0, len(xs) - 1, 2)] + ([xs[-1]] if len(xs) % 2 else [])
    return xs[0]


def _insert(tops, x):
    out = []
    for t in tops:
        out.append(jnp.maximum(t, x))
        x = jnp.minimum(t, x)
    return out


def _compare_exchange(v, i, l, descending):
    hi, lo = jnp.maximum(v[i], v[l]), jnp.minimum(v[i], v[l])
    v[i], v[l] = (hi, lo) if descending else (lo, hi)


def _bitonic_merge_desc(v):
    v = list(v)
    j = len(v) // 2
    while j >= 1:
        for i in range(len(v)):
            if i ^ j > i:
                _compare_exchange(v, i, i ^ j, True)
        j //= 2
    return v


def _bitonic_sort_desc(v):
    v = list(v)
    k = 2
    while k <= len(v):
        j = k // 2
        while j >= 1:
            for i in range(len(v)):
                if i ^ j > i:
                    _compare_exchange(v, i, i ^ j, (i & k) == 0)
            j //= 2
        k *= 2
    return v


def _top_values(scr_ref):
    tops = None
    for g in range(PEER_NKEYS // PEER_TOPK):
        grp = _bitonic_sort_desc([scr_ref[pl.ds(g * PEER_TOPK + i, SUBLANE, stride=PEER_NKEYS), :]
                                  for i in range(PEER_TOPK)])
        if tops is None:
            tops = grp
        else:
            tops = _bitonic_merge_desc([jnp.maximum(tops[i], grp[PEER_TOPK - 1 - i])
                                        for i in range(PEER_TOPK)])
    return tops


def _pick(vals, bits):
    if not bits:
        return vals[0]
    half = len(vals) // 2
    return jnp.where(bits[0], _pick(vals[half:], bits[1:]), _pick(vals[:half], bits[1:]))


def _count_greater(x, s):
    bits = []
    step = PEER_TOPK // 2
    while step >= 1:
        vals = [s[base + step - 1] for base in range(0, PEER_TOPK, 2 * step)]
        bits.append(_pick(vals, bits) > x)
        step //= 2
    return bits, s[PEER_TOPK - 1] > x


def _peer_topk_kernel(hn_ref, wq_ref, keys_ref, n1_ref, e1_ref, r2_ref, e2_ref,
                      scr1, scr2, stat_ref):
    nch = hn_ref.shape[0] // LANE
    q = _dot(hn_ref[...], wq_ref[...])
    for c, scr in enumerate((scr1, scr2)):
        st = _dot_nt(keys_ref[0, c], q[:, c * LANE:(c + 1) * LANE])
        for j in range(nch):
            scr[j * PEER_NKEYS:(j + 1) * PEER_NKEYS, :] = st[:, j * LANE:(j + 1) * LANE]
    a = _top_values(scr1)
    b = _top_values(scr2)
    pairs = [(i, j) for i in range(PEER_TOPK) for j in range(PEER_TOPK) if (i + 1) * (j + 1) <= PEER_TOPK]
    tops = [jnp.full((SUBLANE, LANE), -jnp.inf, F32) for _ in range(PEER_TOPK)]
    for i, j in pairs:
        tops = _insert(tops, a[i] + b[j])
    thr = tops[PEER_TOPK - 1]
    z = jnp.zeros((SUBLANE, LANE), F32)
    for t in tops:
        z = z + jnp.exp(t - tops[0])
    inv_z = 1.0 / z
    cnt = [jnp.zeros((SUBLANE, LANE), F32) for _ in range(PEER_TOPK)]
    for i, j in pairs:
        cnt[i] = cnt[i] + jnp.where(a[i] + b[j] >= thr, GATE_STEP, 0.0)

    for i in range(PEER_TOPK):
        stat_ref[i] = a[i]
        stat_ref[PEER_TOPK + i] = b[i]
        stat_ref[2 * PEER_TOPK + i] = cnt[i]
    stat_ref[3 * PEER_TOPK] = inv_z
    weights = [GATE_STEP * (PEER_TOPK >> (i + 1)) for i in range(4)] + [GATE_STEP]

    def chunk(c, carry):
        def stat(i):
            return jnp.broadcast_to(stat_ref[i, pl.ds(c, 1), :], (SUBLANE, LANE))
        a_c = [stat(i) for i in range(PEER_TOPK)]
        b_c = [stat(PEER_TOPK + i) for i in range(PEER_TOPK)]
        cnt_c = [stat(2 * PEER_TOPK + i) for i in range(PEER_TOPK)]
        inv_z_c = stat(3 * PEER_TOPK)
        for g in range(PEER_NKEYS // SUBLANE):
            rows = pl.ds(pl.multiple_of(c * PEER_NKEYS, PEER_NKEYS) + g * SUBLANE, SUBLANE)
            x = scr1[rows, :]
            bits, below = _count_greater(x, a_c)
            n1_ref[0, c, g * SUBLANE:(g + 1) * SUBLANE, :] = jnp.where(below, 0.0, _pick(cnt_c, bits))
            e1_ref[0, c, g * SUBLANE:(g + 1) * SUBLANE, :] = jnp.exp(x - a_c[0]) * inv_z_c
            x = scr2[rows, :]
            bits, below = _count_greater(x, b_c)
            rank = _tree_sum([jnp.where(m, w, 0.0) for m, w in zip(bits + [below], weights)])
            scr1[rows, :] = rank
            scr2[rows, :] = jnp.exp(x - b_c[0])
        return carry

    lax.fori_loop(0, nch, chunk, 0)
    shape = (nch, PEER_NKEYS, LANE)
    r2_ref[0] = scr1[...].astype(BF16).reshape(shape)
    e2_ref[0] = scr2[...].astype(BF16).reshape(shape)


def _peer_topk(hn, wq, keys):
    T = hn.shape[0]
    tt = SUBLANE * LANE
    nch = T // LANE
    shape = (PEER_HEADS, nch, PEER_NKEYS, LANE)
    spec = pl.BlockSpec((1, SUBLANE, PEER_NKEYS, LANE), lambda i, h: (h, i, 0, 0))
    return pl.pallas_call(
        _peer_topk_kernel,
        grid=(T // tt, PEER_HEADS),
        in_specs=[pl.BlockSpec((tt, D_MODEL), lambda i, h: (i, 0)),
                  pl.BlockSpec((D_MODEL, 2 * LANE), lambda i, h: (0, h)),
                  pl.BlockSpec((1, 2, PEER_NKEYS, LANE), lambda i, h: (h, 0, 0, 0))],
        out_specs=[spec, spec, spec, spec],
        out_shape=[jax.ShapeDtypeStruct(shape, F32), jax.ShapeDtypeStruct(shape, F32),
                   jax.ShapeDtypeStruct(shape, BF16), jax.ShapeDtypeStruct(shape, BF16)],
        scratch_shapes=[pltpu.VMEM((tt, LANE), F32), pltpu.VMEM((tt, LANE), F32),
                        pltpu.VMEM((3 * PEER_TOPK + 1, SUBLANE, LANE), F32)],
        compiler_params=_params(("parallel", "arbitrary")),
        name="peer_topk",
    )(hn, wq, keys)


def _row_bf16(ref, h, c, i1):
    packed_rows = 2 * SUBLANE
    x = jnp.broadcast_to(ref[h, c, pl.ds(i1, 1), :], (packed_rows, LANE)).astype(BF16)
    return jnp.concatenate([x] * (PEER_NKEYS // packed_rows), axis=0)


def _peer_main_kernel(final, hbt_ref, hres_ref, u0_ref, ub_ref, ua_ref, vta_ref, vtb_ref,
                      n1_ref, e1_ref, r2_ref, e2_ref, fg_ref, o_ref,
                      acc_ref, hga_ref, hgb_ref, hta_ref, htb_ref):
    e = pl.program_id(1)
    n_pairs = pl.num_programs(1) - 1
    tt = hbt_ref.shape[1]
    n_i1 = PEER_TILE // PEER_NKEYS
    n_ch = tt // LANE
    zero = jnp.zeros((PEER_NKEYS, LANE), BF16)

    @pl.when(e == 0)
    def _():
        acc_ref[...] = jnp.zeros_like(acc_ref)
        hgb_ref[...] = jnp.zeros_like(hgb_ref)
        hta_ref[...] = _dot(u0_ref[...], hbt_ref[...])

    def gates(tile, ht_ref, hg_ref, c):
        for il in range(n_i1):
            i1 = tile * n_i1 + il
            gate = zero
            for h in range(PEER_HEADS):
                code = _row_bf16(n1_ref, h, c, i1) - r2_ref[h, c]
                w = _row_bf16(e1_ref, h, c, i1) * e2_ref[h, c]
                gate = gate + jnp.maximum(jnp.minimum(w, code), zero)
            rows = slice(il * PEER_NKEYS, (il + 1) * PEER_NKEYS)
            cols = slice(c * LANE, (c + 1) * LANE)
            hg_ref[rows, cols] = _gelu(ht_ref[rows, cols]).astype(BF16) * gate

    def phase(tile, ht_ref, hg_ref, vt_prev_ref, hg_prev_ref, u_next_ref, ht_next_ref):
        half = tt // 2
        for piece in range(4):
            cols = slice((piece % 2) * half, (piece % 2 + 1) * half)
            if piece < 2:
                acc_ref[:, cols] += _dot(vt_prev_ref[...], hg_prev_ref[:, cols])
            else:
                ht_next_ref[:, cols] = _dot(u_next_ref[...], hbt_ref[:, cols])
            gates(tile, ht_ref, hg_ref, piece)

    @pl.when(e < n_pairs)
    def _():
        phase(2 * e, hta_ref, hga_ref, vtb_ref, hgb_ref, ub_ref, htb_ref)
        phase(2 * e + 1, htb_ref, hgb_ref, vta_ref, hga_ref, ua_ref, hta_ref)

    @pl.when(e == n_pairs)
    def _():
        acc = acc_ref[...] + _dot(vtb_ref[...], hgb_ref[...])
        out = hres_ref[...] + acc.T
        if final:
            out = _rms(out, fg_ref[...])
        o_ref[...] = out


def _peer_main(hbt, hres, u, vt, n1, e1, r2, e2, fg, final):
    T = hres.shape[0]
    E = u.shape[0]
    tt = 512
    n_tiles = E // PEER_TILE
    n_pairs = n_tiles // 2
    big_spec = pl.BlockSpec((PEER_HEADS, tt // LANE, PEER_NKEYS, LANE), lambda i, e: (0, i, 0, 0))
    tile_of = lambda f: (lambda i, e: (jnp.clip(f(e), 0, n_tiles - 1), 0, 0))
    vt_spec = lambda f: pl.BlockSpec((D_MODEL, PEER_TILE), lambda i, e: tile_of(f)(i, e)[:2])
    u_spec = lambda f: pl.BlockSpec((PEER_TILE, D_MODEL), lambda i, e: tile_of(f)(i, e)[:2])
    return pl.pallas_call(
        functools.partial(_peer_main_kernel, final),
        grid=(T // tt, n_pairs + 1),
        in_specs=[pl.BlockSpec((D_MODEL, tt), lambda i, e: (0, i)),
                  pl.BlockSpec((tt, D_MODEL), lambda i, e: (i, 0)),
                  u_spec(lambda e: 0), u_spec(lambda e: 2 * e + 1), u_spec(lambda e: 2 * e + 2),
                  vt_spec(lambda e: 2 * e), vt_spec(lambda e: 2 * e - 1),
                  big_spec, big_spec, big_spec, big_spec,
                  _const_spec((1, D_MODEL))],
        out_specs=pl.BlockSpec((tt, D_MODEL), lambda i, e: (i, 0)),
        out_shape=jax.ShapeDtypeStruct((T, D_MODEL), F32),
        scratch_shapes=[pltpu.VMEM((D_MODEL, tt), F32),
                        pltpu.VMEM((PEER_TILE, tt), BF16), pltpu.VMEM((PEER_TILE, tt), BF16),
                        pltpu.VMEM((PEER_TILE, tt), F32), pltpu.VMEM((PEER_TILE, tt), F32)],
        compiler_params=_params(("parallel", "arbitrary")),
        name="peer_main",
    )(hbt, hres, u, u, u, vt, vt, n1, e1, r2, e2, fg)


def _peer_layer(h, hb, hbt, wq, keys, u_tab, v_tab, fg, final):
    n1, e1, r2, e2 = _peer_topk(hb, wq.astype(BF16), keys.astype(BF16))
    vt = v_tab.astype(BF16).reshape(-1, PEER_TILE, D_MODEL).transpose(0, 2, 1).reshape(-1, PEER_TILE)
    return _peer_main(hbt, h, u_tab.astype(BF16), vt, n1, e1, r2, e2, fg, final)


SCAN_BLOCK = LANE // BATCH


def _swap_groups(slabs):
    a = list(slabs)
    group = lax.broadcasted_iota(jnp.int32, a[0].shape, 1) // BATCH
    s = SCAN_BLOCK // 2
    while s >= 1:
        upper = (group & s) != 0
        for i in range(SCAN_BLOCK):
            if i & s == 0:
                ai, aj = a[i], a[i + s]
                a[i] = jnp.where(upper, pltpu.roll(aj, s * BATCH, axis=1), ai)
                a[i + s] = jnp.where(upper, aj, pltpu.roll(ai, LANE - s * BATCH, axis=1))
        s //= 2
    return a


def _rows_to_scan(x, ref):
    z = x.T
    slabs = [z[h * RWKV_HEAD:(h + 1) * RWKV_HEAD, :] for h in range(RWKV_HEADS)]
    for t, blk in enumerate(_swap_groups(slabs)):
        ref[t] = blk


def _scan_to_rows(ref, t0):
    slabs = _swap_groups([ref[t0 + t] for t in range(SCAN_BLOCK)])
    return jnp.concatenate(slabs, axis=0).T


def _rwkv_front_kernel(h_ref, hp_ref, g_ref, mu_ref, wr_ref, wk_ref, wv_ref, w0_ref, ww1_ref, ww2_ref,
                       a0_ref, wa1_ref, wa2_ref, wg1_ref, wg2_ref,
                       r_ref, k_ref, v_ref, w_ref, a_ref, go_ref):
    g = g_ref[...]
    x = _rms(h_ref[...], g)
    xp8 = _rms(hp_ref[...], g)
    xp8 = jnp.where(pl.program_id(0) == 0, 0.0, xp8)
    rows = x.shape[0]
    xprev = jnp.concatenate([xp8, x[:rows - BATCH]], axis=0)
    xx = xprev - x
    mix = lambda j: x + xx * mu_ref[j:j + 1, :]
    _rows_to_scan(_dot(mix(0), wr_ref[...]), r_ref)
    _rows_to_scan(w0_ref[...] + _dot(jnp.tanh(_dot(mix(1), ww1_ref[...])), ww2_ref[...]), w_ref)
    _rows_to_scan(_dot(mix(2), wk_ref[...]), k_ref)
    _rows_to_scan(_dot(mix(3), wv_ref[...]), v_ref)
    _rows_to_scan(_sigmoid(a0_ref[...] + _dot(_dot(mix(4), wa1_ref[...]), wa2_ref[...])), a_ref)
    go_ref[...] = _dot(_sigmoid(_dot(mix(5), wg1_ref[...])), wg2_ref[...])


def _rwkv_front(h, g, mu, wr, wk, wv, w0, ww1, ww2, a0, wa1, wa2, wg1, wg2):
    assert RWKV_HEADS == SCAN_BLOCK
    T = h.shape[0]
    tm = SCAN_BLOCK * BATCH
    row = pl.BlockSpec((tm, D_MODEL), lambda i: (i, 0))
    prev = pl.BlockSpec((BATCH, D_MODEL), lambda i: (jnp.maximum(i * (tm // BATCH) - 1, 0), 0))
    scan = pl.BlockSpec((SCAN_BLOCK, RWKV_HEAD, LANE), lambda i: (i, 0, 0))
    consts = [g, mu, wr, wk, wv, w0, ww1, ww2, a0, wa1, wa2, wg1, wg2]
    scan_out = jax.ShapeDtypeStruct((T // BATCH, RWKV_HEAD, LANE), F32)
    return pl.pallas_call(
        _rwkv_front_kernel,
        grid=(T // tm,),
        in_specs=[row, prev] + [_const_spec(c.shape) for c in consts],
        out_specs=[scan] * 5 + [row],
        out_shape=[scan_out] * 5 + [jax.ShapeDtypeStruct((T, D_MODEL), F32)],
        compiler_params=_params(("parallel",)),
        name="rwkv_front",
    )(h, h, *consts)


def _rwkv_scan_kernel(r_ref, k_ref, v_ref, w_ref, a_ref, kk_ref, ka_ref, rk_ref, lg_ref, lb_ref,
                      y_ref, s_ref, dec_s, k2_s, av_s, bv_s, wr_s, br_s, kr_s):
    tl = r_ref.shape[0]
    n = RWKV_HEAD
    ng = n // SUBLANE

    @pl.when(pl.program_id(0) == 0)
    def _():
        s_ref[...] = jnp.zeros_like(s_ref)

    r = r_ref[...]
    k = k_ref[...]
    a = a_ref[...]
    kk = k * kk_ref[...]
    kk = kk / jnp.maximum(jnp.sqrt(jnp.sum(kk * kk, axis=1, keepdims=True)), 1e-12)
    k2 = k * (1.0 + (a - 1.0) * ka_ref[...])
    dec = jnp.exp(-jnp.exp(-_softplus(-w_ref[...]) - 0.5))
    bv = kk * a
    dec_s[...] = dec
    k2_s[...] = k2
    av_s[...] = -kk
    bv_s[...] = bv
    wr_s[...] = dec * r
    br_s[...] = jnp.sum(bv * r, axis=1, keepdims=True)
    kr_s[...] = jnp.sum(k2 * r, axis=1, keepdims=True)

    def row(ref, t, j):
        return jnp.broadcast_to(ref[t, pl.ds(j, 1), :], (SUBLANE, LANE))

    def step(t, carry):
        def reduce_body(j, accs):
            sa, sy = accs
            a_b = row(av_s, t, j)
            wr_b = row(wr_s, t, j)
            sa_n, sy_n = [], []
            for vg in range(ng):
                sv = s_ref[j, vg * SUBLANE:(vg + 1) * SUBLANE, :]
                sa_n.append(sa[vg] + sv * a_b)
                sy_n.append(sy[vg] + sv * wr_b)
            return tuple(sa_n), tuple(sy_n)

        zeros = tuple(jnp.zeros((SUBLANE, LANE), F32) for _ in range(ng))
        sa, sy = lax.fori_loop(0, n, reduce_body, (zeros, zeros), unroll=8)
        br = jnp.broadcast_to(br_s[t], (SUBLANE, LANE))
        kr = jnp.broadcast_to(kr_s[t], (SUBLANE, LANE))
        vv = [v_ref[t, vg * SUBLANE:(vg + 1) * SUBLANE, :] for vg in range(ng)]
        for vg in range(ng):
            y_ref[t, vg * SUBLANE:(vg + 1) * SUBLANE, :] = sy[vg] + sa[vg] * br + vv[vg] * kr

        def update_body(j, c):
            w_b = row(dec_s, t, j)
            b_b = row(bv_s, t, j)
            k_b = row(k2_s, t, j)
            for vg in range(ng):
                sl = slice(vg * SUBLANE, (vg + 1) * SUBLANE)
                s_ref[j, sl, :] = s_ref[j, sl, :] * w_b + sa[vg] * b_b + vv[vg] * k_b
            return c

        lax.fori_loop(0, n, update_body, 0, unroll=8)
        return carry

    lax.fori_loop(0, tl, step, 0)

    y = y_ref[...]
    mean = jnp.mean(y, axis=1, keepdims=True)
    var = jnp.mean((y - mean) ** 2, axis=1, keepdims=True)
    y = (y - mean) * lax.rsqrt(var + RWKV_GN_EPS) * lg_ref[...] + lb_ref[...]
    bonus = jnp.sum(r * k2 * rk_ref[...], axis=1, keepdims=True) * v_ref[...]
    y_ref[...] = y + bonus


def _rwkv_scan(r, k, v, w, a, kk, ka, rk, lg, lb):
    L = r.shape[0]
    tl = 32
    n = RWKV_HEAD
    blk = pl.BlockSpec((tl, n, LANE), lambda i: (i, 0, 0))
    par = _const_spec((n, LANE))
    big = pltpu.VMEM((tl, n, LANE), F32)
    small = pltpu.VMEM((tl, 1, LANE), F32)
    return pl.pallas_call(
        _rwkv_scan_kernel,
        grid=(L // tl,),
        in_specs=[blk] * 5 + [par] * 5,
        out_specs=blk,
        out_shape=jax.ShapeDtypeStruct((L, n, LANE), F32),
        scratch_shapes=[pltpu.VMEM((n, n, LANE), F32), big, big, big, big, big, small, small],
        compiler_params=_params(("arbitrary",)),
        name="rwkv_scan",
    )(r, k, v, w, a, kk, ka, rk, lg, lb)


def _head_param(p):
    return jnp.repeat(p.reshape(RWKV_HEADS, RWKV_HEAD).T, BATCH, axis=1)


def _prep_even(w_in, a_re, a_im, log_dt, b_re, b_im, c_re, c_im, w_g2, b_g2):
    o = 0
    w_u = w_in[:, o:o + 512]; o += 512
    w_q = w_in[:, o:o + 256]; o += 256
    w_k = w_in[:, o:o + 256]; o += 256
    w_v = w_in[:, o:o + 512]; o += 512
    w_gl = w_in[:, o:o + GLA_RANK]; o += GLA_RANK
    w_r = w_in[:, o:o + 512]

    def pad_heads(w):
        w = w.reshape(-1, GLA_HEADS, GLA_DK)
        w = jnp.pad(w, ((0, 0), (0, 0), (0, GLA_DKP - GLA_DK)))
        return w.reshape(-1, GLA_HEADS * GLA_DKP)

    w_proj = jnp.concatenate([w_u, w_v, w_r, pad_heads(w_q), pad_heads(w_k),
                              jnp.pad(w_gl, ((0, 0), (0, LANE - GLA_RANK)))], axis=1)
    wg = jnp.pad(pad_heads(w_g2), ((0, LANE - GLA_RANK), (0, 0)))
    bg = pad_heads(b_g2[None, :])
    lm = pad_heads(jnp.ones((1, GLA_HEADS * GLA_DK), F32))

    dt = jnp.exp(log_dt)
    mag = jnp.exp(dt * a_re)
    abar_re = mag * jnp.cos(dt * a_im)
    abar_im = mag * jnp.sin(dt * a_im)
    nr = abar_re - 1.0
    den = a_re * a_re + a_im * a_im
    f_re = (nr * a_re + abar_im * a_im) / den
    f_im = (abar_im * a_re - nr * a_im) / den
    bbar_re = f_re[..., None] * b_re - f_im[..., None] * b_im
    bbar_im = f_re[..., None] * b_im + f_im[..., None] * b_re
    eye = jnp.eye(8, dtype=F32)
    gs = S5_GROUPS // S5_SLABS

    def bd(bb):
        bb = bb.reshape(S5_SLABS, gs, S5_STATE, S5_GROUP)
        return jnp.einsum('sgpc,gh->sgchp', bb, eye).reshape(S5_SLABS, gs * S5_GROUP, gs * S5_STATE)

    def cdm(cc):
        cc = cc.reshape(S5_SLABS, gs, S5_GROUP, S5_STATE)
        return jnp.einsum('sgcp,gh->sgphc', cc, eye).reshape(S5_SLABS, gs * S5_STATE, gs * S5_GROUP)

    bd_all = jnp.concatenate([bd(bbar_re), bd(bbar_im)], axis=2)
    cd_all = jnp.concatenate([cdm(c_re), -cdm(c_im)], axis=1)
    are = jnp.broadcast_to(abar_re.reshape(S5_SLABS, 1, gs * S5_STATE), (S5_SLABS, BATCH, gs * S5_STATE))
    aim = jnp.broadcast_to(abar_im.reshape(S5_SLABS, 1, gs * S5_STATE), (S5_SLABS, BATCH, gs * S5_STATE))
    return w_proj, wg, bg, lm, bd_all, cd_all, are, aim


def kernel(x, norm_mix_g, norm_ffn_g, final_g, e_w_in, e_w_out, s5_a_re, s5_a_im, s5_log_dt, s5_b_re, s5_b_im, s5_c_re, s5_c_im, s5_d, s5_w_glu, gla_w_g2, gla_b_g2, gla_norm_g, o_mu, o_w_r, o_w_k, o_w_v, o_w0, o_w_w1, o_w_w2, o_a0, o_w_a1, o_w_a2, o_w_g1, o_w_g2, o_k_k, o_k_a, o_r_k, o_lnx_g, o_lnx_b, o_w_o, peer_w_q, peer_sub_keys, peer_u, peer_v):
    bsz, L, D = x.shape
    assert bsz == BATCH and D == D_MODEL
    T = bsz * L
    row = lambda p: p.reshape(1, -1)
    h = x.transpose(1, 0, 2).reshape(T, D)

    w_proj, wg, bg, lm, bd_all, cd_all, are, aim = _prep_even(
        e_w_in[0], s5_a_re[0], s5_a_im[0], s5_log_dt[0], s5_b_re[0], s5_b_im[0],
        s5_c_re[0], s5_c_im[0], gla_w_g2[0], gla_b_g2[0])
    bf = lambda p: p.astype(BF16)
    proj = _even_front(h, row(norm_mix_g[0]), bf(w_proj))
    y_s5 = _s5(proj, bf(bd_all), bf(cd_all), are, aim, row(s5_d[0]), bf(s5_w_glu[0]))
    y_gla = _gla(proj, wg, bg, lm, row(gla_norm_g[0]))
    w_out = bf(e_w_out[0])
    h, hb, hbt = _mix_out(h, [y_s5, y_gla], [w_out[:S5_WIDTH], w_out[S5_WIDTH:]], None, row(norm_ffn_g[0]))
    h = _peer_layer(h, hb, hbt, peer_w_q[0], peer_sub_keys[0], peer_u[0], peer_v[0], row(final_g), False)

    r, k, v, w, a, g = _rwkv_front(
        h, row(norm_mix_g[1]), o_mu[0], bf(o_w_r[0]), bf(o_w_k[0]), bf(o_w_v[0]), row(o_w0[0]),
        bf(o_w_w1[0]), bf(o_w_w2[0]), row(o_a0[0]), bf(o_w_a1[0]), bf(o_w_a2[0]), bf(o_w_g1[0]), bf(o_w_g2[0]))
    y = _rwkv_scan(r, k, v, w, a, _head_param(o_k_k[0]), _head_param(o_k_a[0]),
                   _head_param(o_r_k[0].reshape(-1)), _head_param(o_lnx_g[0]), _head_param(o_lnx_b[0]))
    h, hb, hbt = _mix_out(h, [y], [bf(o_w_o[0])], g, row(norm_ffn_g[1]))
    out = _peer_layer(h, hb, hbt, peer_w_q[1], peer_sub_keys[1], peer_u[1], peer_v[1], row(final_g), True)
    return out.reshape(L, bsz, D).transpose(1, 0, 2)
```

```python
import functools
import math

import jax
import jax.numpy as jnp
from jax import lax
from jax.experimental import pallas as pl
from jax.experimental.pallas import tpu as pltpu

F32 = jnp.float32
BF16 = jnp.bfloat16
HIGHEST = lax.Precision.HIGHEST

D_MODEL = 1024
BATCH = 8
NORM_EPS = 1e-6
S5_WIDTH = 512
S5_GROUP = 16
S5_GROUPS = 32
S5_STATE = 64
S5_SLABS = 4
GLA_HEADS = 4
GLA_DV = 128
GLA_DK = 64
GLA_DKP = 128
GLA_RANK = 16
GLA_TAU = 16.0
GLA_CHUNK = 64
RWKV_HEAD = 64
RWKV_HEADS = 16
RWKV_GN_EPS = 64e-5
PEER_HEADS = 8
PEER_NKEYS = 128
PEER_TOPK = 16
PEER_TILE = 1024
GATE_STEP = 2.0
LANE = 128
SUBLANE = 8
VMEM_LIMIT = 56 * 1024 * 1024

PROJ_W = 3 * 512 + 2 * GLA_HEADS * GLA_DKP + LANE


def _dot(a, b, hi=False):
    if hi:
        return jnp.dot(a, b, preferred_element_type=F32, precision=HIGHEST)
    return jnp.dot(a.astype(BF16), b.astype(BF16), preferred_element_type=F32)


def _dot_nt(a, b):
    return lax.dot_general(a.astype(BF16), b.astype(BF16), (((1,), (1,)), ((), ())),
                           preferred_element_type=F32)


def _rms(x, g):
    return x * lax.rsqrt(jnp.mean(x * x, axis=-1, keepdims=True) + NORM_EPS) * g


def _sigmoid(x):
    return 1.0 / (1.0 + jnp.exp(-x))


def _softplus(x):
    return jnp.maximum(x, 0.0) + jnp.log(1.0 + jnp.exp(-jnp.abs(x)))


def _gelu(x):
    return 0.5 * x * (1.0 + lax.erf(x * (1.0 / math.sqrt(2.0))))


def _const_spec(shape):
    nd = len(shape)
    return pl.BlockSpec(shape, lambda *_: (0,) * nd)


def _params(sem, flags=None):
    return pltpu.CompilerParams(dimension_semantics=sem, vmem_limit_bytes=VMEM_LIMIT, flags=flags)


def _even_front_kernel(h_ref, g_ref, w_ref, o_ref):
    xn = _rms(h_ref[...], g_ref[...])
    o_ref[...] = _dot(xn, w_ref[...])


def _even_front(h, g, w):
    T = h.shape[0]
    tm = 256
    return pl.pallas_call(
        _even_front_kernel,
        grid=(T // tm,),
        in_specs=[pl.BlockSpec((tm, D_MODEL), lambda i: (i, 0)),
                  _const_spec((1, D_MODEL)),
                  _const_spec((D_MODEL, PROJ_W))],
        out_specs=pl.BlockSpec((tm, PROJ_W), lambda i: (i, 0)),
        out_shape=jax.ShapeDtypeStruct((T, PROJ_W), F32),
        compiler_params=_params(("parallel",)),
        name="even_front",
    )(h, g, w)


def _s5_kernel(u_ref, bd_ref, cd_ref, are_ref, aim_ref, d_ref, wglu_ref, o_ref, bu_ref, st_ref):
    rows = u_ref.shape[0]
    steps = rows // BATCH
    half = 512

    @pl.when(pl.program_id(0) == 0)
    def _():
        st_ref[...] = jnp.zeros_like(st_ref)

    u = u_ref[...]
    for s in range(S5_SLABS):
        bu_ref[s] = _dot(u[:, s * LANE:(s + 1) * LANE], bd_ref[s])

    def step(t, carry):
        r0 = pl.multiple_of(t * BATCH, BATCH)
        out = []
        for s in range(S5_SLABS):
            sre, sim = carry[2 * s], carry[2 * s + 1]
            ar, ai = are_ref[s], aim_ref[s]
            bre = bu_ref[s, pl.ds(r0, BATCH), 0:half]
            bim = bu_ref[s, pl.ds(r0, BATCH), half:2 * half]
            nre = ar * sre - ai * sim + bre
            nim = ar * sim + ai * sre + bim
            bu_ref[s, pl.ds(r0, BATCH), 0:half] = nre
            bu_ref[s, pl.ds(r0, BATCH), half:2 * half] = nim
            out += [nre, nim]
        return tuple(out)

    init = tuple(st_ref[j] for j in range(2 * S5_SLABS))
    fin = lax.fori_loop(0, steps, step, init)
    for j in range(2 * S5_SLABS):
        st_ref[j] = fin[j]

    y = jnp.concatenate([_dot(bu_ref[s], cd_ref[s]) for s in range(S5_SLABS)], axis=1)
    y = y + d_ref[...] * u
    g = _gelu(y)
    o_ref[...] = g * _sigmoid(_dot(g, wglu_ref[...]))


def _s5(proj, bd, cd, are, aim, d, wglu):
    T = proj.shape[0]
    rows = 256
    return pl.pallas_call(
        _s5_kernel,
        grid=(T // rows,),
        in_specs=[pl.BlockSpec((rows, S5_WIDTH), lambda i: (i, 0)),
                  _const_spec(bd.shape), _const_spec(cd.shape),
                  _const_spec(are.shape), _const_spec(aim.shape),
                  _const_spec(d.shape), _const_spec(wglu.shape)],
        out_specs=pl.BlockSpec((rows, S5_WIDTH), lambda i: (i, 0)),
        out_shape=jax.ShapeDtypeStruct((T, S5_WIDTH), F32),
        scratch_shapes=[pltpu.VMEM((S5_SLABS, rows, 2 * 512), F32),
                        pltpu.VMEM((2 * S5_SLABS, BATCH, 512), F32)],
        compiler_params=_params(("arbitrary",)),
        name="s5",
    )(proj, bd, cd, are, aim, d, wglu)


def _gla_kernel(v_ref, r_ref, q_ref, k_ref, gl_ref, wg_ref, bg_ref, lm_ref, ng_ref, o_ref, s_ref):
    rows = q_ref.shape[0]
    width = GLA_HEADS * GLA_DKP

    @pl.when(pl.program_id(0) == 0)
    def _():
        s_ref[...] = jnp.zeros_like(s_ref)

    z = _dot(gl_ref[...], wg_ref[...], True) + bg_ref[...]
    la = -_softplus(-z) * (1.0 / GLA_TAU) * lm_ref[...]
    b = la
    sh = BATCH
    while sh < rows:
        b = b + jnp.concatenate([jnp.zeros((sh, width), F32), b[:rows - sh]], axis=0)
        sh *= 2
    blast = b[rows - BATCH:rows]
    blast_t = jnp.concatenate([blast] * GLA_CHUNK, axis=0)
    q = q_ref[...] * (GLA_DK ** -0.5)
    k = k_ref[...]
    qd = q * jnp.exp(b)
    kd = k * jnp.exp(-b)
    kend = k * jnp.exp(blast_t - b)

    ri = lax.broadcasted_iota(jnp.int32, (rows, rows), 0)
    ci = lax.broadcasted_iota(jnp.int32, (rows, rows), 1)
    causal = jnp.logical_and(((ri - ci) & (BATCH - 1)) == 0, ri >= ci)
    rowb = lax.broadcasted_iota(jnp.int32, (rows, GLA_DKP), 0) & (BATCH - 1)
    laneb = lax.broadcasted_iota(jnp.int32, (GLA_DKP, rows), 1) & (BATCH - 1)

    for h in range(GLA_HEADS):
        sl = slice(h * GLA_DKP, (h + 1) * GLA_DKP)
        vh = v_ref[:, h * GLA_DV:(h + 1) * GLA_DV]
        qh, kdh, keh, bh = qd[:, sl], kd[:, sl], kend[:, sl], b[:, sl]
        att = jnp.where(causal, _dot_nt(qh, kdh), 0.0)
        o = _dot(att, vh)
        qexp = jnp.concatenate([jnp.where(rowb == bb, qh, 0.0) for bb in range(BATCH)], axis=1)
        o = o + _dot(qexp, s_ref[h])
        ket = keh.T
        kexp_t = jnp.concatenate([jnp.where(laneb == bb, ket, 0.0) for bb in range(BATCH)], axis=0)
        kv = _dot(kexp_t, vh)
        bt = bh[rows - GLA_DKP:rows].T
        dec = jnp.concatenate(
            [jnp.broadcast_to(jnp.exp(bt[:, GLA_DKP - BATCH + bb:GLA_DKP - BATCH + bb + 1]),
                              (GLA_DKP, GLA_DV)) for bb in range(BATCH)], axis=0)
        s_ref[h] = s_ref[h] * dec + kv
        o = o * lax.rsqrt(jnp.mean(o * o, axis=-1, keepdims=True) + NORM_EPS)
        o = o * ng_ref[:, h * GLA_DV:(h + 1) * GLA_DV]
        rr = r_ref[:, h * GLA_DV:(h + 1) * GLA_DV]
        o_ref[:, h * GLA_DV:(h + 1) * GLA_DV] = o * (rr * _sigmoid(rr))


def _gla(proj, wg, bg, lm, ng):
    T = proj.shape[0]
    rows = GLA_CHUNK * BATCH
    width = GLA_HEADS * GLA_DKP
    return pl.pallas_call(
        _gla_kernel,
        grid=(T // rows,),
        in_specs=[pl.BlockSpec((rows, 512), lambda i: (i, 1)),
                  pl.BlockSpec((rows, 512), lambda i: (i, 2)),
                  pl.BlockSpec((rows, width), lambda i: (i, 3)),
                  pl.BlockSpec((rows, width), lambda i: (i, 4)),
                  pl.BlockSpec((rows, LANE), lambda i: (i, (3 * 512 + 2 * width) // LANE)),
                  _const_spec(wg.shape), _const_spec(bg.shape), _const_spec(lm.shape),
                  _const_spec(ng.shape)],
        out_specs=pl.BlockSpec((rows, GLA_HEADS * GLA_DV), lambda i: (i, 0)),
        out_shape=jax.ShapeDtypeStruct((T, GLA_HEADS * GLA_DV), F32),
        scratch_shapes=[pltpu.VMEM((GLA_HEADS, BATCH * GLA_DKP, GLA_DV), F32)],
        compiler_params=_params(("arbitrary",)),
        name="gla",
    )(proj, proj, proj, proj, proj, wg, bg, lm, ng)


def _mix_out_kernel(n_in, has_gate, *refs):
    h_ref = refs[0]
    y_refs = refs[1:1 + n_in]
    w_refs = refs[1 + n_in:1 + 2 * n_in]
    rest = refs[1 + 2 * n_in:]
    gate_ref = rest[0] if has_gate else None
    g_ref, ho_ref, hb_ref, hbt_ref = rest[1:] if has_gate else rest
    acc = h_ref[...]
    for y_ref, w_ref in zip(y_refs, w_refs):
        if len(y_ref.shape) == 3:
            y = jnp.concatenate([_scan_to_rows(y_ref, t0) for t0 in range(0, y_ref.shape[0], SCAN_BLOCK)],
                                axis=0)
        else:
            y = y_ref[...]
        if has_gate:
            y = y * gate_ref[...]
        acc = acc + _dot(y, w_ref[...])
    ho_ref[...] = acc
    hn = _rms(acc, g_ref[...])
    hb_ref[...] = hn.astype(BF16)
    hbt_ref[...] = hn.T.astype(BF16)


def _mix_out(h, ys, ws, gate, g):
    T = h.shape[0]
    tm = 256
    n_in = len(ys)
    row = pl.BlockSpec((tm, D_MODEL), lambda i: (i, 0))
    in_specs = [row]
    in_specs += [pl.BlockSpec((tm, y.shape[1]), lambda i: (i, 0)) if y.ndim == 2 else
                 pl.BlockSpec((tm // BATCH,) + y.shape[1:], lambda i: (i, 0, 0)) for y in ys]
    in_specs += [_const_spec(w.shape) for w in ws]
    args = [h, *ys, *ws]
    if gate is not None:
        in_specs.append(row)
        args.append(gate)
    in_specs.append(_const_spec((1, D_MODEL)))
    args.append(g)
    return pl.pallas_call(
        functools.partial(_mix_out_kernel, n_in, gate is not None),
        grid=(T // tm,),
        in_specs=in_specs,
        out_specs=[row, row, pl.BlockSpec((D_MODEL, tm), lambda i: (0, i))],
        out_shape=[jax.ShapeDtypeStruct((T, D_MODEL), F32),
                   jax.ShapeDtypeStruct((T, D_MODEL), BF16),
                   jax.ShapeDtypeStruct((D_MODEL, T), BF16)],
        compiler_params=_params(("parallel",)),
        name="mix_out",
    )(*args)


def _tree_sum(xs):
    while len(xs) > 1:
        xs = [xs[i] + xs[i + 1] for i in range(0, len(xs) - 1, 2)] + ([xs[-1]] if len(xs) % 2 else [])
    return xs[0]


def _insert(tops, x):
    out = []
    for t in tops:
        out.append(jnp.maximum(t, x))
        x = jnp.minimum(t, x)
    return out


def _compare_exchange(v, i, l, descending):
    hi, lo = jnp.maximum(v[i], v[l]), jnp.minimum(v[i], v[l])
    v[i], v[l] = (hi, lo) if descending else (lo, hi)


def _bitonic_merge_desc(v):
    v = list(v)
    j = len(v) // 2
    while j >= 1:
        for i in range(len(v)):
            if i ^ j > i:
                _compare_exchange(v, i, i ^ j, True)
        j //= 2
    return v


def _bitonic_sort_desc(v):
    v = list(v)
    k = 2
    while k <= len(v):
        j = k // 2
        while j >= 1:
            for i in range(len(v)):
                if i ^ j > i:
                    _compare_exchange(v, i, i ^ j, (i & k) == 0)
            j //= 2
        k *= 2
    return v


def _top_values(scr_ref):
    tops = None
    for g in range(PEER_NKEYS // PEER_TOPK):
        grp = _bitonic_sort_desc([scr_ref[pl.ds(g * PEER_TOPK + i, SUBLANE, stride=PEER_NKEYS), :]
                                  for i in range(PEER_TOPK)])
        if tops is None:
            tops = grp
        else:
            tops = _bitonic_merge_desc([jnp.maximum(tops[i], grp[PEER_TOPK - 1 - i])
                                        for i in range(PEER_TOPK)])
    return tops


def _pick(vals, bits):
    if not bits:
        return vals[0]
    half = len(vals) // 2
    return jnp.where(bits[0], _pick(vals[half:], bits[1:]), _pick(vals[:half], bits[1:]))


def _count_greater(x, s):
    bits = []
    step = PEER_TOPK // 2
    while step >= 1:
        vals = [s[base + step - 1] for base in range(0, PEER_TOPK, 2 * step)]
        bits.append(_pick(vals, bits) > x)
        step //= 2
    return bits, s[PEER_TOPK - 1] > x


def _peer_topk_kernel(hn_ref, wq_ref, keys_ref, n1_ref, e1_ref, r2_ref, e2_ref,
                      scr1, scr2, stat_ref):
    nch = hn_ref.shape[0] // LANE
    q = _dot(hn_ref[...], wq_ref[...])
    for c, scr in enumerate((scr1, scr2)):
        st = _dot_nt(keys_ref[0, c], q[:, c * LANE:(c + 1) * LANE])
        for j in range(nch):
            scr[j * PEER_NKEYS:(j + 1) * PEER_NKEYS, :] = st[:, j * LANE:(j + 1) * LANE]
    a = _top_values(scr1)
    b = _top_values(scr2)
    pairs = [(i, j) for i in range(PEER_TOPK) for j in range(PEER_TOPK) if (i + 1) * (j + 1) <= PEER_TOPK]
    tops = [jnp.full((SUBLANE, LANE), -jnp.inf, F32) for _ in range(PEER_TOPK)]
    for i, j in pairs:
        tops = _insert(tops, a[i] + b[j])
    thr = tops[PEER_TOPK - 1]
    z = jnp.zeros((SUBLANE, LANE), F32)
    for t in tops:
        z = z + jnp.exp(t - tops[0])
    inv_z = 1.0 / z
    cnt = [jnp.zeros((SUBLANE, LANE), F32) for _ in range(PEER_TOPK)]
    for i, j in pairs:
        cnt[i] = cnt[i] + jnp.where(a[i] + b[j] >= thr, GATE_STEP, 0.0)

    for i in range(PEER_TOPK):
        stat_ref[i] = a[i]
        stat_ref[PEER_TOPK + i] = b[i]
        stat_ref[2 * PEER_TOPK + i] = cnt[i]
    stat_ref[3 * PEER_TOPK] = inv_z
    weights = [GATE_STEP * (PEER_TOPK >> (i + 1)) for i in range(4)] + [GATE_STEP]

    def chunk(c, carry):
        def stat(i):
            return jnp.broadcast_to(stat_ref[i, pl.ds(c, 1), :], (SUBLANE, LANE))
        a_c = [stat(i) for i in range(PEER_TOPK)]
        b_c = [stat(PEER_TOPK + i) for i in range(PEER_TOPK)]
        cnt_c = [stat(2 * PEER_TOPK + i) for i in range(PEER_TOPK)]
        inv_z_c = stat(3 * PEER_TOPK)
        for g in range(PEER_NKEYS // SUBLANE):
            rows = pl.ds(pl.multiple_of(c * PEER_NKEYS, PEER_NKEYS) + g * SUBLANE, SUBLANE)
            x = scr1[rows, :]
            bits, below = _count_greater(x, a_c)
            n1_ref[0, c, g * SUBLANE:(g + 1) * SUBLANE, :] = jnp.where(below, 0.0, _pick(cnt_c, bits))
            e1_ref[0, c, g * SUBLANE:(g + 1) * SUBLANE, :] = jnp.exp(x - a_c[0]) * inv_z_c
            x = scr2[rows, :]
            bits, below = _count_greater(x, b_c)
            rank = _tree_sum([jnp.where(m, w, 0.0) for m, w in zip(bits + [below], weights)])
            scr1[rows, :] = rank
            scr2[rows, :] = jnp.exp(x - b_c[0])
        return carry

    lax.fori_loop(0, nch, chunk, 0)
    shape = (nch, PEER_NKEYS, LANE)
    r2_ref[0] = scr1[...].astype(BF16).reshape(shape)
    e2_ref[0] = scr2[...].astype(BF16).reshape(shape)


def _peer_topk(hn, wq, keys):
    T = hn.shape[0]
    tt = SUBLANE * LANE
    nch = T // LANE
    shape = (PEER_HEADS, nch, PEER_NKEYS, LANE)
    spec = pl.BlockSpec((1, SUBLANE, PEER_NKEYS, LANE), lambda i, h: (h, i, 0, 0))
    return pl.pallas_call(
        _peer_topk_kernel,
        grid=(T // tt, PEER_HEADS),
        in_specs=[pl.BlockSpec((tt, D_MODEL), lambda i, h: (i, 0)),
                  pl.BlockSpec((D_MODEL, 2 * LANE), lambda i, h: (0, h)),
                  pl.BlockSpec((1, 2, PEER_NKEYS, LANE), lambda i, h: (h, 0, 0, 0))],
        out_specs=[spec, spec, spec, spec],
        out_shape=[jax.ShapeDtypeStruct(shape, F32), jax.ShapeDtypeStruct(shape, F32),
                   jax.ShapeDtypeStruct(shape, BF16), jax.ShapeDtypeStruct(shape, BF16)],
        scratch_shapes=[pltpu.VMEM((tt, LANE), F32), pltpu.VMEM((tt, LANE), F32),
                        pltpu.VMEM((3 * PEER_TOPK + 1, SUBLANE, LANE), F32)],
        compiler_params=_params(("parallel", "arbitrary")),
        name="peer_topk",
    )(hn, wq, keys)


def _row_bf16(ref, h, c, i1, dep=None):
    packed_rows = 2 * SUBLANE
    row = ref[h, c, pl.ds(i1, 1), :]
    if dep is not None:
        row = row + dep
    x = jnp.broadcast_to(row, (packed_rows, LANE)).astype(BF16)
    return jnp.concatenate([x] * (PEER_NKEYS // packed_rows), axis=0)


def _zero_after(x):
    bits = pltpu.bitcast(x[0:2 * SUBLANE, :], jnp.uint32)
    sixteen = jnp.uint32(16)
    zero_bits = lax.shift_right_logical(lax.shift_right_logical(bits, sixteen), sixteen)
    return pltpu.bitcast(zero_bits, F32)[0:1, :]


def _peer_main_kernel(final, hbt_ref, hres_ref, u0_ref, ub_ref, ua_ref, vta_ref, vtb_ref,
                      n1_ref, e1_ref, r2_ref, e2_ref, fg_ref, o_ref,
                      acc_ref, hga_ref, hgb_ref, hta_ref, htb_ref):
    e = pl.program_id(1)
    n_pairs = pl.num_programs(1) - 1
    tt = hbt_ref.shape[1]
    n_i1 = PEER_TILE // PEER_NKEYS
    n_ch = tt // LANE
    zero = jnp.zeros((PEER_NKEYS, LANE), BF16)

    @pl.when(e == 0)
    def _():
        acc_ref[...] = jnp.zeros_like(acc_ref)
        hgb_ref[...] = jnp.zeros_like(hgb_ref)
        hta_ref[...] = _dot(u0_ref[...], hbt_ref[...])

    def gates(tile, ht_ref, hg_ref, c, dep):
        for il in range(n_i1):
            i1 = tile * n_i1 + il
            gate = zero
            for h in range(PEER_HEADS):
                code = _row_bf16(n1_ref, h, c, i1, dep if h == 0 else None) - r2_ref[h, c]
                w = _row_bf16(e1_ref, h, c, i1) * e2_ref[h, c]
                gate = gate + jnp.maximum(jnp.minimum(w, code), zero)
            rows = slice(il * PEER_NKEYS, (il + 1) * PEER_NKEYS)
            cols = slice(c * LANE, (c + 1) * LANE)
            hg = _gelu(ht_ref[rows, cols]).astype(BF16) * gate
            hg_ref[rows, cols] = hg
            dep = _zero_after(hg)
        return dep

    def phase(tile, ht_ref, hg_ref, vt_prev_ref, hg_prev_ref, u_next_ref, ht_next_ref, dep):
        half = tt // 2
        for piece in range(4):
            cols = slice((piece % 2) * half, (piece % 2 + 1) * half)
            if piece < 2:
                acc_ref[:, cols] += _dot(vt_prev_ref[...], hg_prev_ref[:, cols])
            else:
                ht_next_ref[:, cols] = _dot(u_next_ref[...], hbt_ref[:, cols])
            dep = gates(tile, ht_ref, hg_ref, piece, dep)
        return dep

    @pl.when(e < n_pairs)
    def _():
        dep = phase(2 * e, hta_ref, hga_ref, vtb_ref, hgb_ref, ub_ref, htb_ref, None)
        phase(2 * e + 1, htb_ref, hgb_ref, vta_ref, hga_ref, ua_ref, hta_ref, dep)

    @pl.when(e == n_pairs)
    def _():
        acc = acc_ref[...] + _dot(vtb_ref[...], hgb_ref[...])
        out = hres_ref[...] + acc.T
        if final:
            out = _rms(out, fg_ref[...])
        o_ref[...] = out


def _peer_main(hbt, hres, u, vt, n1, e1, r2, e2, fg, final):
    T = hres.shape[0]
    E = u.shape[0]
    tt = 512
    n_tiles = E // PEER_TILE
    n_pairs = n_tiles // 2
    big_spec = pl.BlockSpec((PEER_HEADS, tt // LANE, PEER_NKEYS, LANE), lambda i, e: (0, i, 0, 0))
    tile_of = lambda f: (lambda i, e: (jnp.clip(f(e), 0, n_tiles - 1), 0, 0))
    vt_spec = lambda f: pl.BlockSpec((D_MODEL, PEER_TILE), lambda i, e: tile_of(f)(i, e)[:2])
    u_spec = lambda f: pl.BlockSpec((PEER_TILE, D_MODEL), lambda i, e: tile_of(f)(i, e)[:2])
    return pl.pallas_call(
        functools.partial(_peer_main_kernel, final),
        grid=(T // tt, n_pairs + 1),
        in_specs=[pl.BlockSpec((D_MODEL, tt), lambda i, e: (0, i)),
                  pl.BlockSpec((tt, D_MODEL), lambda i, e: (i, 0)),
                  u_spec(lambda e: 0), u_spec(lambda e: 2 * e + 1), u_spec(lambda e: 2 * e + 2),
                  vt_spec(lambda e: 2 * e), vt_spec(lambda e: 2 * e - 1),
                  big_spec, big_spec, big_spec, big_spec,
                  _const_spec((1, D_MODEL))],
        out_specs=pl.BlockSpec((tt, D_MODEL), lambda i, e: (i, 0)),
        out_shape=jax.ShapeDtypeStruct((T, D_MODEL), F32),
        scratch_shapes=[pltpu.VMEM((D_MODEL, tt), F32),
                        pltpu.VMEM((PEER_TILE, tt), BF16), pltpu.VMEM((PEER_TILE, tt), BF16),
                        pltpu.VMEM((PEER_TILE, tt), F32), pltpu.VMEM((PEER_TILE, tt), F32)],
        compiler_params=_params(("parallel", "arbitrary")),
        name="peer_main",
    )(hbt, hres, u, u, u, vt, vt, n1, e1, r2, e2, fg)


def _peer_layer(h, hb, hbt, wq, keys, u_tab, v_tab, fg, final):
    n1, e1, r2, e2 = _peer_topk(hb, wq.astype(BF16), keys.astype(BF16))
    vt = v_tab.astype(BF16).reshape(-1, PEER_TILE, D_MODEL).transpose(0, 2, 1).reshape(-1, PEER_TILE)
    return _peer_main(hbt, h, u_tab.astype(BF16), vt, n1, e1, r2, e2, fg, final)


SCAN_BLOCK = LANE // BATCH


def _swap_groups(slabs):
    a = list(slabs)
    group = lax.broadcasted_iota(jnp.int32, a[0].shape, 1) // BATCH
    s = SCAN_BLOCK // 2
    while s >= 1:
        upper = (group & s) != 0
        for i in range(SCAN_BLOCK):
            if i & s == 0:
                ai, aj = a[i], a[i + s]
                a[i] = jnp.where(upper, pltpu.roll(aj, s * BATCH, axis=1), ai)
                a[i + s] = jnp.where(upper, aj, pltpu.roll(ai, LANE - s * BATCH, axis=1))
        s //= 2
    return a


def _rows_to_scan(x, ref):
    z = x.T
    slabs = [z[h * RWKV_HEAD:(h + 1) * RWKV_HEAD, :] for h in range(RWKV_HEADS)]
    for t, blk in enumerate(_swap_groups(slabs)):
        ref[t] = blk


def _scan_to_rows(ref, t0):
    slabs = _swap_groups([ref[t0 + t] for t in range(SCAN_BLOCK)])
    return jnp.concatenate(slabs, axis=0).T


def _rwkv_front_kernel(h_ref, hp_ref, g_ref, mu_ref, wr_ref, wk_ref, wv_ref, w0_ref, ww1_ref, ww2_ref,
                       a0_ref, wa1_ref, wa2_ref, wg1_ref, wg2_ref,
                       r_ref, k_ref, v_ref, w_ref, a_ref, go_ref):
    g = g_ref[...]
    x = _rms(h_ref[...], g)
    xp8 = _rms(hp_ref[...], g)
    xp8 = jnp.where(pl.program_id(0) == 0, 0.0, xp8)
    rows = x.shape[0]
    xprev = jnp.concatenate([xp8, x[:rows - BATCH]], axis=0)
    xx = xprev - x
    mix = lambda j: x + xx * mu_ref[j:j + 1, :]
    _rows_to_scan(_dot(mix(0), wr_ref[...]), r_ref)
    _rows_to_scan(w0_ref[...] + _dot(jnp.tanh(_dot(mix(1), ww1_ref[...])), ww2_ref[...]), w_ref)
    _rows_to_scan(_dot(mix(2), wk_ref[...]), k_ref)
    _rows_to_scan(_dot(mix(3), wv_ref[...]), v_ref)
    _rows_to_scan(_sigmoid(a0_ref[...] + _dot(_dot(mix(4), wa1_ref[...]), wa2_ref[...])), a_ref)
    go_ref[...] = _dot(_sigmoid(_dot(mix(5), wg1_ref[...])), wg2_ref[...])


def _rwkv_front(h, g, mu, wr, wk, wv, w0, ww1, ww2, a0, wa1, wa2, wg1, wg2):
    assert RWKV_HEADS == SCAN_BLOCK
    T = h.shape[0]
    tm = SCAN_BLOCK * BATCH
    row = pl.BlockSpec((tm, D_MODEL), lambda i: (i, 0))
    prev = pl.BlockSpec((BATCH, D_MODEL), lambda i: (jnp.maximum(i * (tm // BATCH) - 1, 0), 0))
    scan = pl.BlockSpec((SCAN_BLOCK, RWKV_HEAD, LANE), lambda i: (i, 0, 0))
    consts = [g, mu, wr, wk, wv, w0, ww1, ww2, a0, wa1, wa2, wg1, wg2]
    scan_out = jax.ShapeDtypeStruct((T // BATCH, RWKV_HEAD, LANE), F32)
    return pl.pallas_call(
        _rwkv_front_kernel,
        grid=(T // tm,),
        in_specs=[row, prev] + [_const_spec(c.shape) for c in consts],
        out_specs=[scan] * 5 + [row],
        out_shape=[scan_out] * 5 + [jax.ShapeDtypeStruct((T, D_MODEL), F32)],
        compiler_params=_params(("parallel",)),
        name="rwkv_front",
    )(h, h, *consts)


def _rwkv_scan_kernel(r_ref, k_ref, v_ref, w_ref, a_ref, kk_ref, ka_ref, rk_ref, lg_ref, lb_ref,
                      y_ref, s_ref, dec_s, k2_s, av_s, bv_s, wr_s, br_s, kr_s):
    tl = r_ref.shape[0]
    n = RWKV_HEAD
    ng = n // SUBLANE

    @pl.when(pl.program_id(0) == 0)
    def _():
        s_ref[...] = jnp.zeros_like(s_ref)

    r = r_ref[...]
    k = k_ref[...]
    a = a_ref[...]
    kk = k * kk_ref[...]
    kk = kk / jnp.maximum(jnp.sqrt(jnp.sum(kk * kk, axis=1, keepdims=True)), 1e-12)
    k2 = k * (1.0 + (a - 1.0) * ka_ref[...])
    dec = jnp.exp(-jnp.exp(-_softplus(-w_ref[...]) - 0.5))
    bv = kk * a
    dec_s[...] = dec
    k2_s[...] = k2
    av_s[...] = -kk
    bv_s[...] = bv
    wr_s[...] = dec * r
    br_s[...] = jnp.sum(bv * r, axis=1, keepdims=True)
    kr_s[...] = jnp.sum(k2 * r, axis=1, keepdims=True)

    def row(ref, t, j):
        return jnp.broadcast_to(ref[t, pl.ds(j, 1), :], (SUBLANE, LANE))

    def step(t, carry):
        def reduce_body(j, accs):
            sa, sy = accs
            a_b = row(av_s, t, j)
            wr_b = row(wr_s, t, j)
            sa_n, sy_n = [], []
            for vg in range(ng):
                sv = s_ref[j, vg * SUBLANE:(vg + 1) * SUBLANE, :]
                sa_n.append(sa[vg] + sv * a_b)
                sy_n.append(sy[vg] + sv * wr_b)
            return tuple(sa_n), tuple(sy_n)

        zeros = tuple(jnp.zeros((SUBLANE, LANE), F32) for _ in range(ng))
        sa, sy = lax.fori_loop(0, n, reduce_body, (zeros, zeros), unroll=8)
        br = jnp.broadcast_to(br_s[t], (SUBLANE, LANE))
        kr = jnp.broadcast_to(kr_s[t], (SUBLANE, LANE))
        vv = [v_ref[t, vg * SUBLANE:(vg + 1) * SUBLANE, :] for vg in range(ng)]
        for vg in range(ng):
            y_ref[t, vg * SUBLANE:(vg + 1) * SUBLANE, :] = sy[vg] + sa[vg] * br + vv[vg] * kr

        def update_body(j, c):
            w_b = row(dec_s, t, j)
            b_b = row(bv_s, t, j)
            k_b = row(k2_s, t, j)
            for vg in range(ng):
                sl = slice(vg * SUBLANE, (vg + 1) * SUBLANE)
                s_ref[j, sl, :] = s_ref[j, sl, :] * w_b + sa[vg] * b_b + vv[vg] * k_b
            return c

        lax.fori_loop(0, n, update_body, 0, unroll=8)
        return carry

    lax.fori_loop(0, tl, step, 0)

    y = y_ref[...]
    mean = jnp.mean(y, axis=1, keepdims=True)
    var = jnp.mean((y - mean) ** 2, axis=1, keepdims=True)
    y = (y - mean) * lax.rsqrt(var + RWKV_GN_EPS) * lg_ref[...] + lb_ref[...]
    bonus = jnp.sum(r * k2 * rk_ref[...], axis=1, keepdims=True) * v_ref[...]
    y_ref[...] = y + bonus


def _rwkv_scan(r, k, v, w, a, kk, ka, rk, lg, lb):
    L = r.shape[0]
    tl = 32
    n = RWKV_HEAD
    blk = pl.BlockSpec((tl, n, LANE), lambda i: (i, 0, 0))
    par = _const_spec((n, LANE))
    big = pltpu.VMEM((tl, n, LANE), F32)
    small = pltpu.VMEM((tl, 1, LANE), F32)
    return pl.pallas_call(
        _rwkv_scan_kernel,
        grid=(L // tl,),
        in_specs=[blk] * 5 + [par] * 5,
        out_specs=blk,
        out_shape=jax.ShapeDtypeStruct((L, n, LANE), F32),
        scratch_shapes=[pltpu.VMEM((n, n, LANE), F32), big, big, big, big, big, small, small],
        compiler_params=_params(("arbitrary",)),
        name="rwkv_scan",
    )(r, k, v, w, a, kk, ka, rk, lg, lb)


def _head_param(p):
    return jnp.repeat(p.reshape(RWKV_HEADS, RWKV_HEAD).T, BATCH, axis=1)


def _prep_even(w_in, a_re, a_im, log_dt, b_re, b_im, c_re, c_im, w_g2, b_g2):
    o = 0
    w_u = w_in[:, o:o + 512]; o += 512
    w_q = w_in[:, o:o + 256]; o += 256
    w_k = w_in[:, o:o + 256]; o += 256
    w_v = w_in[:, o:o + 512]; o += 512
    w_gl = w_in[:, o:o + GLA_RANK]; o += GLA_RANK
    w_r = w_in[:, o:o + 512]

    def pad_heads(w):
        w = w.reshape(-1, GLA_HEADS, GLA_DK)
        w = jnp.pad(w, ((0, 0), (0, 0), (0, GLA_DKP - GLA_DK)))
        return w.reshape(-1, GLA_HEADS * GLA_DKP)

    w_proj = jnp.concatenate([w_u, w_v, w_r, pad_heads(w_q), pad_heads(w_k),
                              jnp.pad(w_gl, ((0, 0), (0, LANE - GLA_RANK)))], axis=1)
    wg = jnp.pad(pad_heads(w_g2), ((0, LANE - GLA_RANK), (0, 0)))
    bg = pad_heads(b_g2[None, :])
    lm = pad_heads(jnp.ones((1, GLA_HEADS * GLA_DK), F32))

    dt = jnp.exp(log_dt)
    mag = jnp.exp(dt * a_re)
    abar_re = mag * jnp.cos(dt * a_im)
    abar_im = mag * jnp.sin(dt * a_im)
    nr = abar_re - 1.0
    den = a_re * a_re + a_im * a_im
    f_re = (nr * a_re + abar_im * a_im) / den
    f_im = (abar_im * a_re - nr * a_im) / den
    bbar_re = f_re[..., None] * b_re - f_im[..., None] * b_im
    bbar_im = f_re[..., None] * b_im + f_im[..., None] * b_re
    eye = jnp.eye(8, dtype=F32)
    gs = S5_GROUPS // S5_SLABS

    def bd(bb):
        bb = bb.reshape(S5_SLABS, gs, S5_STATE, S5_GROUP)
        return jnp.einsum('sgpc,gh->sgchp', bb, eye).reshape(S5_SLABS, gs * S5_GROUP, gs * S5_STATE)

    def cdm(cc):
        cc = cc.reshape(S5_SLABS, gs, S5_GROUP, S5_STATE)
        return jnp.einsum('sgcp,gh->sgphc', cc, eye).reshape(S5_SLABS, gs * S5_STATE, gs * S5_GROUP)

    bd_all = jnp.concatenate([bd(bbar_re), bd(bbar_im)], axis=2)
    cd_all = jnp.concatenate([cdm(c_re), -cdm(c_im)], axis=1)
    are = jnp.broadcast_to(abar_re.reshape(S5_SLABS, 1, gs * S5_STATE), (S5_SLABS, BATCH, gs * S5_STATE))
    aim = jnp.broadcast_to(abar_im.reshape(S5_SLABS, 1, gs * S5_STATE), (S5_SLABS, BATCH, gs * S5_STATE))
    return w_proj, wg, bg, lm, bd_all, cd_all, are, aim


def kernel(x, norm_mix_g, norm_ffn_g, final_g, e_w_in, e_w_out, s5_a_re, s5_a_im, s5_log_dt, s5_b_re, s5_b_im, s5_c_re, s5_c_im, s5_d, s5_w_glu, gla_w_g2, gla_b_g2, gla_norm_g, o_mu, o_w_r, o_w_k, o_w_v, o_w0, o_w_w1, o_w_w2, o_a0, o_w_a1, o_w_a2, o_w_g1, o_w_g2, o_k_k, o_k_a, o_r_k, o_lnx_g, o_lnx_b, o_w_o, peer_w_q, peer_sub_keys, peer_u, peer_v):
    bsz, L, D = x.shape
    assert bsz == BATCH and D == D_MODEL
    T = bsz * L
    row = lambda p: p.reshape(1, -1)
    h = x.transpose(1, 0, 2).reshape(T, D)

    w_proj, wg, bg, lm, bd_all, cd_all, are, aim = _prep_even(
        e_w_in[0], s5_a_re[0], s5_a_im[0], s5_log_dt[0], s5_b_re[0], s5_b_im[0],
        s5_c_re[0], s5_c_im[0], gla_w_g2[0], gla_b_g2[0])
    bf = lambda p: p.astype(BF16)
    proj = _even_front(h, row(norm_mix_g[0]), bf(w_proj))
    y_s5 = _s5(proj, bf(bd_all), bf(cd_all), are, aim, row(s5_d[0]), bf(s5_w_glu[0]))
    y_gla = _gla(proj, wg, bg, lm, row(gla_norm_g[0]))
    w_out = bf(e_w_out[0])
    h, hb, hbt = _mix_out(h, [y_s5, y_gla], [w_out[:S5_WIDTH], w_out[S5_WIDTH:]], None, row(norm_ffn_g[0]))
    h = _peer_layer(h, hb, hbt, peer_w_q[0], peer_sub_keys[0], peer_u[0], peer_v[0], row(final_g), False)

    r, k, v, w, a, g = _rwkv_front(
        h, row(norm_mix_g[1]), o_mu[0], bf(o_w_r[0]), bf(o_w_k[0]), bf(o_w_v[0]), row(o_w0[0]),
        bf(o_w_w1[0]), bf(o_w_w2[0]), row(o_a0[0]), bf(o_w_a1[0]), bf(o_w_a2[0]), bf(o_w_g1[0]), bf(o_w_g2[0]))
    y = _rwkv_scan(r, k, v, w, a, _head_param(o_k_k[0]), _head_param(o_k_a[0]),
                   _head_param(o_r_k[0].reshape(-1)), _head_param(o_lnx_g[0]), _head_param(o_lnx_b[0]))
    h, hb, hbt = _mix_out(h, [y], [bf(o_w_o[0])], g, row(norm_ffn_g[1]))
    out = _peer_layer(h, hb, hbt, peer_w_q[1], peer_sub_keys[1], peer_u[1], peer_v[1], row(final_g), True)
    return out.reshape(L, bsz, D).transpose(1, 0, 2)
```

```python
import functools
import math

import jax
import jax.numpy as jnp
from jax import lax
from jax.experimental import pallas as pl
from jax.experimental.pallas import tpu as pltpu

F32 = jnp.float32
BF16 = jnp.bfloat16
HIGHEST = lax.Precision.HIGHEST

D_MODEL = 1024
BATCH = 8
NORM_EPS = 1e-6
S5_WIDTH = 512
S5_GROUP = 16
S5_GROUPS = 32
S5_STATE = 64
S5_SLABS = 4
GLA_HEADS = 4
GLA_DV = 128
GLA_DK = 64
GLA_DKP = 128
GLA_RANK = 16
GLA_TAU = 16.0
GLA_CHUNK = 64
RWKV_HEAD = 64
RWKV_HEADS = 16
RWKV_GN_EPS = 64e-5
PEER_HEADS = 8
PEER_NKEYS = 128
PEER_TOPK = 16
PEER_TILE = 1024
PEER_TOKENS = 512
GATE_STEP = 2.0
LANE = 128
SUBLANE = 8
VMEM_LIMIT = 56 * 1024 * 1024

PROJ_W = 3 * 512 + 2 * GLA_HEADS * GLA_DKP + LANE


def _dot(a, b, hi=False):
    if hi:
        return jnp.dot(a, b, preferred_element_type=F32, precision=HIGHEST)
    return jnp.dot(a.astype(BF16), b.astype(BF16), preferred_element_type=F32)


def _dot_nt(a, b):
    return lax.dot_general(a.astype(BF16), b.astype(BF16), (((1,), (1,)), ((), ())),
                           preferred_element_type=F32)


def _rms(x, g):
    return x * lax.rsqrt(jnp.mean(x * x, axis=-1, keepdims=True) + NORM_EPS) * g


def _sigmoid(x):
    return 1.0 / (1.0 + jnp.exp(-x))


def _softplus(x):
    return jnp.maximum(x, 0.0) + jnp.log(1.0 + jnp.exp(-jnp.abs(x)))


def _gelu(x):
    return 0.5 * x * (1.0 + lax.erf(x * (1.0 / math.sqrt(2.0))))


def _const_spec(shape):
    nd = len(shape)
    return pl.BlockSpec(shape, lambda *_: (0,) * nd)


def _params(sem, flags=None):
    return pltpu.CompilerParams(dimension_semantics=sem, vmem_limit_bytes=VMEM_LIMIT, flags=flags)


def _even_front_kernel(h_ref, g_ref, w_ref, o_ref):
    xn = _rms(h_ref[...], g_ref[...])
    o_ref[...] = _dot(xn, w_ref[...])


def _even_front(h, g, w):
    T = h.shape[0]
    tm = 256
    return pl.pallas_call(
        _even_front_kernel,
        grid=(T // tm,),
        in_specs=[pl.BlockSpec((tm, D_MODEL), lambda i: (i, 0)),
                  _const_spec((1, D_MODEL)),
                  _const_spec((D_MODEL, PROJ_W))],
        out_specs=pl.BlockSpec((tm, PROJ_W), lambda i: (i, 0)),
        out_shape=jax.ShapeDtypeStruct((T, PROJ_W), F32),
        compiler_params=_params(("parallel",)),
        name="even_front",
    )(h, g, w)


def _s5_kernel(u_ref, bd_ref, cd_ref, are_ref, aim_ref, d_ref, wglu_ref, o_ref, bu_ref, st_ref):
    rows = u_ref.shape[0]
    steps = rows // BATCH
    half = 512

    @pl.when(pl.program_id(0) == 0)
    def _():
        st_ref[...] = jnp.zeros_like(st_ref)

    u = u_ref[...]
    for s in range(S5_SLABS):
        bu_ref[s] = _dot(u[:, s * LANE:(s + 1) * LANE], bd_ref[s])

    def step(t, carry):
        r0 = pl.multiple_of(t * BATCH, BATCH)
        out = []
        for s in range(S5_SLABS):
            sre, sim = carry[2 * s], carry[2 * s + 1]
            ar, ai = are_ref[s], aim_ref[s]
            bre = bu_ref[s, pl.ds(r0, BATCH), 0:half]
            bim = bu_ref[s, pl.ds(r0, BATCH), half:2 * half]
            nre = ar * sre - ai * sim + bre
            nim = ar * sim + ai * sre + bim
            bu_ref[s, pl.ds(r0, BATCH), 0:half] = nre
            bu_ref[s, pl.ds(r0, BATCH), half:2 * half] = nim
            out += [nre, nim]
        return tuple(out)

    init = tuple(st_ref[j] for j in range(2 * S5_SLABS))
    fin = lax.fori_loop(0, steps, step, init)
    for j in range(2 * S5_SLABS):
        st_ref[j] = fin[j]

    y = jnp.concatenate([_dot(bu_ref[s], cd_ref[s]) for s in range(S5_SLABS)], axis=1)
    y = y + d_ref[...] * u
    g = _gelu(y)
    o_ref[...] = g * _sigmoid(_dot(g, wglu_ref[...]))


def _s5(proj, bd, cd, are, aim, d, wglu):
    T = proj.shape[0]
    rows = 256
    return pl.pallas_call(
        _s5_kernel,
        grid=(T // rows,),
        in_specs=[pl.BlockSpec((rows, S5_WIDTH), lambda i: (i, 0)),
                  _const_spec(bd.shape), _const_spec(cd.shape),
                  _const_spec(are.shape), _const_spec(aim.shape),
                  _const_spec(d.shape), _const_spec(wglu.shape)],
        out_specs=pl.BlockSpec((rows, S5_WIDTH), lambda i: (i, 0)),
        out_shape=jax.ShapeDtypeStruct((T, S5_WIDTH), F32),
        scratch_shapes=[pltpu.VMEM((S5_SLABS, rows, 2 * 512), F32),
                        pltpu.VMEM((2 * S5_SLABS, BATCH, 512), F32)],
        compiler_params=_params(("arbitrary",)),
        name="s5",
    )(proj, bd, cd, are, aim, d, wglu)


def _gla_kernel(v_ref, r_ref, q_ref, k_ref, gl_ref, wg_ref, bg_ref, lm_ref, ng_ref, o_ref, s_ref):
    rows = q_ref.shape[0]
    width = GLA_HEADS * GLA_DKP

    @pl.when(pl.program_id(0) == 0)
    def _():
        s_ref[...] = jnp.zeros_like(s_ref)

    z = _dot(gl_ref[...], wg_ref[...], True) + bg_ref[...]
    la = -_softplus(-z) * (1.0 / GLA_TAU) * lm_ref[...]
    b = la
    sh = BATCH
    while sh < rows:
        b = b + jnp.concatenate([jnp.zeros((sh, width), F32), b[:rows - sh]], axis=0)
        sh *= 2
    blast = b[rows - BATCH:rows]
    blast_t = jnp.concatenate([blast] * GLA_CHUNK, axis=0)
    q = q_ref[...] * (GLA_DK ** -0.5)
    k = k_ref[...]
    qd = q * jnp.exp(b)
    kd = k * jnp.exp(-b)
    kend = k * jnp.exp(blast_t - b)

    ri = lax.broadcasted_iota(jnp.int32, (rows, rows), 0)
    ci = lax.broadcasted_iota(jnp.int32, (rows, rows), 1)
    causal = jnp.logical_and(((ri - ci) & (BATCH - 1)) == 0, ri >= ci)
    rowb = lax.broadcasted_iota(jnp.int32, (rows, GLA_DKP), 0) & (BATCH - 1)
    laneb = lax.broadcasted_iota(jnp.int32, (GLA_DKP, rows), 1) & (BATCH - 1)

    for h in range(GLA_HEADS):
        sl = slice(h * GLA_DKP, (h + 1) * GLA_DKP)
        vh = v_ref[:, h * GLA_DV:(h + 1) * GLA_DV]
        qh, kdh, keh, bh = qd[:, sl], kd[:, sl], kend[:, sl], b[:, sl]
        att = jnp.where(causal, _dot_nt(qh, kdh), 0.0)
        o = _dot(att, vh)
        qexp = jnp.concatenate([jnp.where(rowb == bb, qh, 0.0) for bb in range(BATCH)], axis=1)
        o = o + _dot(qexp, s_ref[h])
        ket = keh.T
        kexp_t = jnp.concatenate([jnp.where(laneb == bb, ket, 0.0) for bb in range(BATCH)], axis=0)
        kv = _dot(kexp_t, vh)
        bt = bh[rows - GLA_DKP:rows].T
        dec = jnp.concatenate(
            [jnp.broadcast_to(jnp.exp(bt[:, GLA_DKP - BATCH + bb:GLA_DKP - BATCH + bb + 1]),
                              (GLA_DKP, GLA_DV)) for bb in range(BATCH)], axis=0)
        s_ref[h] = s_ref[h] * dec + kv
        o = o * lax.rsqrt(jnp.mean(o * o, axis=-1, keepdims=True) + NORM_EPS)
        o = o * ng_ref[:, h * GLA_DV:(h + 1) * GLA_DV]
        rr = r_ref[:, h * GLA_DV:(h + 1) * GLA_DV]
        o_ref[:, h * GLA_DV:(h + 1) * GLA_DV] = o * (rr * _sigmoid(rr))


def _gla(proj, wg, bg, lm, ng):
    T = proj.shape[0]
    rows = GLA_CHUNK * BATCH
    width = GLA_HEADS * GLA_DKP
    return pl.pallas_call(
        _gla_kernel,
        grid=(T // rows,),
        in_specs=[pl.BlockSpec((rows, 512), lambda i: (i, 1)),
                  pl.BlockSpec((rows, 512), lambda i: (i, 2)),
                  pl.BlockSpec((rows, width), lambda i: (i, 3)),
                  pl.BlockSpec((rows, width), lambda i: (i, 4)),
                  pl.BlockSpec((rows, LANE), lambda i: (i, (3 * 512 + 2 * width) // LANE)),
                  _const_spec(wg.shape), _const_spec(bg.shape), _const_spec(lm.shape),
                  _const_spec(ng.shape)],
        out_specs=pl.BlockSpec((rows, GLA_HEADS * GLA_DV), lambda i: (i, 0)),
        out_shape=jax.ShapeDtypeStruct((T, GLA_HEADS * GLA_DV), F32),
        scratch_shapes=[pltpu.VMEM((GLA_HEADS, BATCH * GLA_DKP, GLA_DV), F32)],
        compiler_params=_params(("arbitrary",)),
        name="gla",
    )(proj, proj, proj, proj, proj, wg, bg, lm, ng)


def _mix_out_kernel(n_in, has_gate, *refs):
    h_ref = refs[0]
    y_refs = refs[1:1 + n_in]
    w_refs = refs[1 + n_in:1 + 2 * n_in]
    rest = refs[1 + 2 * n_in:]
    gate_ref = rest[0] if has_gate else None
    g_ref, ho_ref, hb_ref, hbt_ref = rest[1:] if has_gate else rest
    acc = h_ref[...]
    for y_ref, w_ref in zip(y_refs, w_refs):
        if len(y_ref.shape) == 3:
            y = jnp.concatenate([_scan_to_rows(y_ref, t0) for t0 in range(0, y_ref.shape[0], SCAN_BLOCK)],
                                axis=0)
        else:
            y = y_ref[...]
        if has_gate:
            y = y * gate_ref[...]
        acc = acc + _dot(y, w_ref[...])
    ho_ref[...] = acc
    hn = _rms(acc, g_ref[...])
    hb_ref[...] = hn.astype(BF16)
    hbt_ref[...] = hn.T.astype(BF16)


def _mix_out(h, ys, ws, gate, g):
    T = h.shape[0]
    tm = 256
    n_in = len(ys)
    row = pl.BlockSpec((tm, D_MODEL), lambda i: (i, 0))
    in_specs = [row]
    in_specs += [pl.BlockSpec((tm, y.shape[1]), lambda i: (i, 0)) if y.ndim == 2 else
                 pl.BlockSpec((tm // BATCH,) + y.shape[1:], lambda i: (i, 0, 0)) for y in ys]
    in_specs += [_const_spec(w.shape) for w in ws]
    args = [h, *ys, *ws]
    if gate is not None:
        in_specs.append(row)
        args.append(gate)
    in_specs.append(_const_spec((1, D_MODEL)))
    args.append(g)
    return pl.pallas_call(
        functools.partial(_mix_out_kernel, n_in, gate is not None),
        grid=(T // tm,),
        in_specs=in_specs,
        out_specs=[row, row, pl.BlockSpec((D_MODEL, tm), lambda i: (0, i))],
        out_shape=[jax.ShapeDtypeStruct((T, D_MODEL), F32),
                   jax.ShapeDtypeStruct((T, D_MODEL), BF16),
                   jax.ShapeDtypeStruct((D_MODEL, T), BF16)],
        compiler_params=_params(("parallel",)),
        name="mix_out",
    )(*args)


def _tree_sum(xs):
    while len(xs) > 1:
        xs = [xs[i] + xs[i + 1] for i in range(0, len(xs) - 1, 2)] + ([xs[-1]] if len(xs) % 2 else [])
    return xs[0]


def _insert(tops, x):
    out = []
    for t in tops:
        out.append(jnp.maximum(t, x))
        x = jnp.minimum(t, x)
    return out


def _compare_exchange(v, i, l, descending):
    hi, lo = jnp.maximum(v[i], v[l]), jnp.minimum(v[i], v[l])
    v[i], v[l] = (hi, lo) if descending else (lo, hi)


def _bitonic_merge_desc(v):
    v = list(v)
    j = len(v) // 2
    while j >= 1:
        for i in range(len(v)):
            if i ^ j > i:
                _compare_exchange(v, i, i ^ j, True)
        j //= 2
    return v


def _bitonic_sort_desc(v):
    v = list(v)
    k = 2
    while k <= len(v):
        j = k // 2
        while j >= 1:
            for i in range(len(v)):
                if i ^ j > i:
                    _compare_exchange(v, i, i ^ j, (i & k) == 0)
            j //= 2
        k *= 2
    return v


def _top_values(scr_ref):
    tops = None
    for g in range(PEER_NKEYS // PEER_TOPK):
        grp = _bitonic_sort_desc([scr_ref[pl.ds(g * PEER_TOPK + i, SUBLANE, stride=PEER_NKEYS), :]
                                  for i in range(PEER_TOPK)])
        if tops is None:
            tops = grp
        else:
            tops = _bitonic_merge_desc([jnp.maximum(tops[i], grp[PEER_TOPK - 1 - i])
                                        for i in range(PEER_TOPK)])
    return tops


def _pick(vals, bits):
    if not bits:
        return vals[0]
    half = len(vals) // 2
    return jnp.where(bits[0], _pick(vals[half:], bits[1:]), _pick(vals[:half], bits[1:]))


def _count_greater(x, s):
    bits = []
    step = PEER_TOPK // 2
    while step >= 1:
        vals = [s[base + step - 1] for base in range(0, PEER_TOPK, 2 * step)]
        bits.append(_pick(vals, bits) > x)
        step //= 2
    return bits, s[PEER_TOPK - 1] > x


def _peer_topk_kernel(hn_ref, wq_ref, keys_ref, n1_ref, e1_ref, r2_ref, e2_ref,
                      scr1, scr2, stat_ref):
    nch = hn_ref.shape[0] // LANE
    q = _dot(hn_ref[...], wq_ref[...])
    for c, scr in enumerate((scr1, scr2)):
        st = _dot_nt(keys_ref[0, c], q[:, c * LANE:(c + 1) * LANE])
        for j in range(nch):
            scr[j * PEER_NKEYS:(j + 1) * PEER_NKEYS, :] = st[:, j * LANE:(j + 1) * LANE]
    a = _top_values(scr1)
    b = _top_values(scr2)
    pairs = [(i, j) for i in range(PEER_TOPK) for j in range(PEER_TOPK) if (i + 1) * (j + 1) <= PEER_TOPK]
    tops = [jnp.full((SUBLANE, LANE), -jnp.inf, F32) for _ in range(PEER_TOPK)]
    for i, j in pairs:
        tops = _insert(tops, a[i] + b[j])
    thr = tops[PEER_TOPK - 1]
    z = jnp.zeros((SUBLANE, LANE), F32)
    for t in tops:
        z = z + jnp.exp(t - tops[0])
    inv_z = 1.0 / z
    cnt = [jnp.zeros((SUBLANE, LANE), F32) for _ in range(PEER_TOPK)]
    for i, j in pairs:
        cnt[i] = cnt[i] + jnp.where(a[i] + b[j] >= thr, GATE_STEP, 0.0)

    for i in range(PEER_TOPK):
        stat_ref[i] = a[i]
        stat_ref[PEER_TOPK + i] = b[i]
        stat_ref[2 * PEER_TOPK + i] = cnt[i]
    stat_ref[3 * PEER_TOPK] = inv_z
    weights = [GATE_STEP * (PEER_TOPK >> (i + 1)) for i in range(4)] + [GATE_STEP]

    def chunk(c, carry):
        def stat(i):
            return jnp.broadcast_to(stat_ref[i, pl.ds(c, 1), :], (SUBLANE, LANE))
        a_c = [stat(i) for i in range(PEER_TOPK)]
        b_c = [stat(PEER_TOPK + i) for i in range(PEER_TOPK)]
        cnt_c = [stat(2 * PEER_TOPK + i) for i in range(PEER_TOPK)]
        inv_z_c = stat(3 * PEER_TOPK)
        for g in range(PEER_NKEYS // SUBLANE):
            rows = pl.ds(pl.multiple_of(c * PEER_NKEYS, PEER_NKEYS) + g * SUBLANE, SUBLANE)
            x = scr1[rows, :]
            bits, below = _count_greater(x, a_c)
            n1_ref[0, c, g * SUBLANE:(g + 1) * SUBLANE, :] = jnp.where(below, 0.0, _pick(cnt_c, bits))
            e1_ref[0, c, g * SUBLANE:(g + 1) * SUBLANE, :] = jnp.exp(x - a_c[0]) * inv_z_c
            x = scr2[rows, :]
            bits, below = _count_greater(x, b_c)
            rank = _tree_sum([jnp.where(m, w, 0.0) for m, w in zip(bits + [below], weights)])
            scr1[rows, :] = rank
            scr2[rows, :] = jnp.exp(x - b_c[0])
        return carry

    lax.fori_loop(0, nch, chunk, 0)
    shape = (nch, PEER_NKEYS, LANE)
    r2_ref[0] = scr1[...].astype(BF16).reshape(shape)
    e2_ref[0] = scr2[...].astype(BF16).reshape(shape)


def _peer_topk(hn, wq, keys):
    T = hn.shape[0]
    tt = SUBLANE * LANE
    nch = T // LANE
    shape = (PEER_HEADS, nch, PEER_NKEYS, LANE)
    spec = pl.BlockSpec((1, SUBLANE, PEER_NKEYS, LANE), lambda i, h: (h, i, 0, 0))
    return pl.pallas_call(
        _peer_topk_kernel,
        grid=(T // tt, PEER_HEADS),
        in_specs=[pl.BlockSpec((tt, D_MODEL), lambda i, h: (i, 0)),
                  pl.BlockSpec((D_MODEL, 2 * LANE), lambda i, h: (0, h)),
                  pl.BlockSpec((1, 2, PEER_NKEYS, LANE), lambda i, h: (h, 0, 0, 0))],
        out_specs=[spec, spec, spec, spec],
        out_shape=[jax.ShapeDtypeStruct(shape, F32), jax.ShapeDtypeStruct(shape, F32),
                   jax.ShapeDtypeStruct(shape, BF16), jax.ShapeDtypeStruct(shape, BF16)],
        scratch_shapes=[pltpu.VMEM((tt, LANE), F32), pltpu.VMEM((tt, LANE), F32),
                        pltpu.VMEM((3 * PEER_TOPK + 1, SUBLANE, LANE), F32)],
        compiler_params=_params(("parallel", "arbitrary")),
        name="peer_topk",
    )(hn, wq, keys)


def _row_bf16(ref, h, c, i1, dep=None):
    packed_rows = 2 * SUBLANE
    row = ref[h, c, pl.ds(i1, 1), :]
    if dep is not None:
        row = row + dep
    x = jnp.broadcast_to(row, (packed_rows, LANE)).astype(BF16)
    return jnp.concatenate([x] * (PEER_NKEYS // packed_rows), axis=0)


def _zero_after(x):
    bits = pltpu.bitcast(x[0:2 * SUBLANE, :], jnp.uint32)
    sixteen = jnp.uint32(16)
    zero_bits = lax.shift_right_logical(lax.shift_right_logical(bits, sixteen), sixteen)
    return pltpu.bitcast(zero_bits, F32)[0:1, :]


def _peer_main_kernel(final, hbt_ref, hres_ref, u0_ref, ub_ref, ua_ref, vta_ref, vtb_ref,
                      n1_ref, e1_ref, r2_ref, e2_ref, fg_ref, o_ref,
                      acc_ref, hga_ref, hgb_ref, hta_ref, htb_ref):
    e = pl.program_id(1)
    n_pairs = pl.num_programs(1) - 1
    tt = hbt_ref.shape[1]
    n_i1 = PEER_TILE // PEER_NKEYS
    n_ch = tt // LANE
    zero = jnp.zeros((PEER_NKEYS, LANE), BF16)

    @pl.when(e == 0)
    def _():
        acc_ref[...] = jnp.zeros_like(acc_ref)
        hgb_ref[...] = jnp.zeros_like(hgb_ref)
        hta_ref[...] = _dot(u0_ref[...], hbt_ref[...])

    def gates(tile, ht_ref, hg_ref, c, dep):
        for il in range(n_i1):
            i1 = tile * n_i1 + il
            gate = zero
            for h in range(PEER_HEADS):
                code = _row_bf16(n1_ref, h, c, i1, dep if h == 0 else None) - r2_ref[h, c]
                w = _row_bf16(e1_ref, h, c, i1) * e2_ref[h, c]
                gate = gate + jnp.maximum(jnp.minimum(w, code), zero)
            rows = slice(il * PEER_NKEYS, (il + 1) * PEER_NKEYS)
            cols = slice(c * LANE, (c + 1) * LANE)
            hg = _gelu(ht_ref[rows, cols]).astype(BF16) * gate
            hg_ref[rows, cols] = hg
            dep = _zero_after(hg)
        return dep

    def phase(tile, ht_ref, hg_ref, vt_prev_ref, hg_prev_ref, u_next_ref, ht_next_ref, dep):
        half = tt // 2
        for piece in range(4):
            cols = slice((piece % 2) * half, (piece % 2 + 1) * half)
            if piece < 2:
                acc_ref[:, cols] += _dot(vt_prev_ref[...], hg_prev_ref[:, cols])
            else:
                ht_next_ref[:, cols] = _dot(u_next_ref[...], hbt_ref[:, cols])
            for c in range(piece * n_ch // 4, (piece + 1) * n_ch // 4):
                dep = gates(tile, ht_ref, hg_ref, c, dep)
        return dep

    @pl.when(e < n_pairs)
    def _():
        dep = phase(2 * e, hta_ref, hga_ref, vtb_ref, hgb_ref, ub_ref, htb_ref, None)
        phase(2 * e + 1, htb_ref, hgb_ref, vta_ref, hga_ref, ua_ref, hta_ref, dep)

    @pl.when(e == n_pairs)
    def _():
        acc = acc_ref[...] + _dot(vtb_ref[...], hgb_ref[...])
        out = hres_ref[...] + acc.T
        if final:
            out = _rms(out, fg_ref[...])
        o_ref[...] = out


def _peer_main(hbt, hres, u, vt, n1, e1, r2, e2, fg, final):
    T = hres.shape[0]
    E = u.shape[0]
    tt = PEER_TOKENS
    n_tiles = E // PEER_TILE
    n_pairs = n_tiles // 2
    big_spec = pl.BlockSpec((PEER_HEADS, tt // LANE, PEER_NKEYS, LANE), lambda i, e: (0, i, 0, 0))
    tile_of = lambda f: (lambda i, e: (jnp.clip(f(e), 0, n_tiles - 1), 0, 0))
    vt_spec = lambda f: pl.BlockSpec((D_MODEL, PEER_TILE), lambda i, e: tile_of(f)(i, e)[:2])
    u_spec = lambda f: pl.BlockSpec((PEER_TILE, D_MODEL), lambda i, e: tile_of(f)(i, e)[:2])
    return pl.pallas_call(
        functools.partial(_peer_main_kernel, final),
        grid=(T // tt, n_pairs + 1),
        in_specs=[pl.BlockSpec((D_MODEL, tt), lambda i, e: (0, i)),
                  pl.BlockSpec((tt, D_MODEL), lambda i, e: (i, 0)),
                  u_spec(lambda e: 0), u_spec(lambda e: 2 * e + 1), u_spec(lambda e: 2 * e + 2),
                  vt_spec(lambda e: 2 * e), vt_spec(lambda e: 2 * e - 1),
                  big_spec, big_spec, big_spec, big_spec,
                  _const_spec((1, D_MODEL))],
        out_specs=pl.BlockSpec((tt, D_MODEL), lambda i, e: (i, 0)),
        out_shape=jax.ShapeDtypeStruct((T, D_MODEL), F32),
        scratch_shapes=[pltpu.VMEM((D_MODEL, tt), F32),
                        pltpu.VMEM((PEER_TILE, tt), BF16), pltpu.VMEM((PEER_TILE, tt), BF16),
                        pltpu.VMEM((PEER_TILE, tt), F32), pltpu.VMEM((PEER_TILE, tt), F32)],
        compiler_params=_params(("parallel", "arbitrary")),
        name="peer_main",
    )(hbt, hres, u, u, u, vt, vt, n1, e1, r2, e2, fg)


def _peer_layer(h, hb, hbt, wq, keys, u_tab, v_tab, fg, final):
    n1, e1, r2, e2 = _peer_topk(hb, wq.astype(BF16), keys.astype(BF16))
    vt = v_tab.astype(BF16).reshape(-1, PEER_TILE, D_MODEL).transpose(0, 2, 1).reshape(-1, PEER_TILE)
    return _peer_main(hbt, h, u_tab.astype(BF16), vt, n1, e1, r2, e2, fg, final)


SCAN_BLOCK = LANE // BATCH


def _swap_groups(slabs):
    a = list(slabs)
    group = lax.broadcasted_iota(jnp.int32, a[0].shape, 1) // BATCH
    s = SCAN_BLOCK // 2
    while s >= 1:
        upper = (group & s) != 0
        for i in range(SCAN_BLOCK):
            if i & s == 0:
                ai, aj = a[i], a[i + s]
                a[i] = jnp.where(upper, pltpu.roll(aj, s * BATCH, axis=1), ai)
                a[i + s] = jnp.where(upper, aj, pltpu.roll(ai, LANE - s * BATCH, axis=1))
        s //= 2
    return a


def _rows_to_scan(x, ref):
    z = x.T
    slabs = [z[h * RWKV_HEAD:(h + 1) * RWKV_HEAD, :] for h in range(RWKV_HEADS)]
    for t, blk in enumerate(_swap_groups(slabs)):
        ref[t] = blk


def _scan_to_rows(ref, t0):
    slabs = _swap_groups([ref[t0 + t] for t in range(SCAN_BLOCK)])
    return jnp.concatenate(slabs, axis=0).T


def _rwkv_front_kernel(h_ref, hp_ref, g_ref, mu_ref, wr_ref, wk_ref, wv_ref, w0_ref, ww1_ref, ww2_ref,
                       a0_ref, wa1_ref, wa2_ref, wg1_ref, wg2_ref,
                       r_ref, k_ref, v_ref, w_ref, a_ref, go_ref):
    g = g_ref[...]
    x = _rms(h_ref[...], g)
    xp8 = _rms(hp_ref[...], g)
    xp8 = jnp.where(pl.program_id(0) == 0, 0.0, xp8)
    rows = x.shape[0]
    xprev = jnp.concatenate([xp8, x[:rows - BATCH]], axis=0)
    xx = xprev - x
    mix = lambda j: x + xx * mu_ref[j:j + 1, :]
    _rows_to_scan(_dot(mix(0), wr_ref[...]), r_ref)
    _rows_to_scan(w0_ref[...] + _dot(jnp.tanh(_dot(mix(1), ww1_ref[...])), ww2_ref[...]), w_ref)
    _rows_to_scan(_dot(mix(2), wk_ref[...]), k_ref)
    _rows_to_scan(_dot(mix(3), wv_ref[...]), v_ref)
    _rows_to_scan(_sigmoid(a0_ref[...] + _dot(_dot(mix(4), wa1_ref[...]), wa2_ref[...])), a_ref)
    go_ref[...] = _dot(_sigmoid(_dot(mix(5), wg1_ref[...])), wg2_ref[...])


def _rwkv_front(h, g, mu, wr, wk, wv, w0, ww1, ww2, a0, wa1, wa2, wg1, wg2):
    assert RWKV_HEADS == SCAN_BLOCK
    T = h.shape[0]
    tm = SCAN_BLOCK * BATCH
    row = pl.BlockSpec((tm, D_MODEL), lambda i: (i, 0))
    prev = pl.BlockSpec((BATCH, D_MODEL), lambda i: (jnp.maximum(i * (tm // BATCH) - 1, 0), 0))
    scan = pl.BlockSpec((SCAN_BLOCK, RWKV_HEAD, LANE), lambda i: (i, 0, 0))
    consts = [g, mu, wr, wk, wv, w0, ww1, ww2, a0, wa1, wa2, wg1, wg2]
    scan_out = jax.ShapeDtypeStruct((T // BATCH, RWKV_HEAD, LANE), F32)
    return pl.pallas_call(
        _rwkv_front_kernel,
        grid=(T // tm,),
        in_specs=[row, prev] + [_const_spec(c.shape) for c in consts],
        out_specs=[scan] * 5 + [row],
        out_shape=[scan_out] * 5 + [jax.ShapeDtypeStruct((T, D_MODEL), F32)],
        compiler_params=_params(("parallel",)),
        name="rwkv_front",
    )(h, h, *consts)


def _rwkv_scan_kernel(r_ref, k_ref, v_ref, w_ref, a_ref, kk_ref, ka_ref, rk_ref, lg_ref, lb_ref,
                      y_ref, s_ref, pend_s, k2_s, av_s, bv_s, wr_s, br_s, kr_s):
    tl = r_ref.shape[0]
    n = RWKV_HEAD
    ng = n // SUBLANE

    @pl.when(pl.program_id(0) == 0)
    def _():
        s_ref[...] = jnp.zeros_like(s_ref)

    r = r_ref[...]
    k = k_ref[...]
    a = a_ref[...]
    kk = k * kk_ref[...]
    kk = kk / jnp.maximum(jnp.sqrt(jnp.sum(kk * kk, axis=1, keepdims=True)), 1e-12)
    k2 = k * (1.0 + (a - 1.0) * ka_ref[...])
    dec = jnp.exp(-jnp.exp(-_softplus(-w_ref[...]) - 0.5))
    bv = kk * a
    p = dec
    sh = 1
    while sh < tl:
        p = p * jnp.concatenate([jnp.ones((sh, n, LANE), F32), p[:tl - sh]], axis=0)
        sh *= 2
    p_prev = jnp.concatenate([jnp.ones((1, n, LANE), F32), p[:tl - 1]], axis=0)
    inv_p = 1.0 / p
    pend_s[...] = p[tl - 1]
    k2_s[...] = k2 * inv_p
    av_s[...] = -kk * p_prev
    bv_s[...] = bv * inv_p
    wr_s[...] = r * p
    br_s[...] = jnp.sum(bv * r, axis=1, keepdims=True)
    kr_s[...] = jnp.sum(k2 * r, axis=1, keepdims=True)

    def row(ref, t, j):
        return jnp.broadcast_to(ref[t, pl.ds(j, 1), :], (SUBLANE, LANE))

    def step(t, carry):
        def reduce_body(j, accs):
            sa, sy = accs
            a_b = row(av_s, t, j)
            wr_b = row(wr_s, t, j)
            sa_n, sy_n = [], []
            for vg in range(ng):
                sv = s_ref[j, vg * SUBLANE:(vg + 1) * SUBLANE, :]
                sa_n.append(sa[vg] + sv * a_b)
                sy_n.append(sy[vg] + sv * wr_b)
            return tuple(sa_n), tuple(sy_n)

        zeros = tuple(jnp.zeros((SUBLANE, LANE), F32) for _ in range(ng))
        sa, sy = lax.fori_loop(0, n, reduce_body, (zeros, zeros), unroll=8)
        br = jnp.broadcast_to(br_s[t], (SUBLANE, LANE))
        kr = jnp.broadcast_to(kr_s[t], (SUBLANE, LANE))
        vv = [v_ref[t, vg * SUBLANE:(vg + 1) * SUBLANE, :] for vg in range(ng)]
        for vg in range(ng):
            y_ref[t, vg * SUBLANE:(vg + 1) * SUBLANE, :] = sy[vg] + sa[vg] * br + vv[vg] * kr

        def update_body(j, c):
            b_b = row(bv_s, t, j)
            k_b = row(k2_s, t, j)
            for vg in range(ng):
                sl = slice(vg * SUBLANE, (vg + 1) * SUBLANE)
                s_ref[j, sl, :] = s_ref[j, sl, :] + sa[vg] * b_b + vv[vg] * k_b
            return c

        lax.fori_loop(0, n, update_body, 0, unroll=8)
        return carry

    lax.fori_loop(0, tl, step, 0)

    def rescale(j, c):
        s_ref[j] = s_ref[j] * pend_s[pl.ds(j, 1), :]
        return c

    lax.fori_loop(0, n, rescale, 0, unroll=8)

    y = y_ref[...]
    mean = jnp.mean(y, axis=1, keepdims=True)
    var = jnp.mean((y - mean) ** 2, axis=1, keepdims=True)
    y = (y - mean) * lax.rsqrt(var + RWKV_GN_EPS) * lg_ref[...] + lb_ref[...]
    bonus = jnp.sum(r * k2 * rk_ref[...], axis=1, keepdims=True) * v_ref[...]
    y_ref[...] = y + bonus


def _rwkv_scan(r, k, v, w, a, kk, ka, rk, lg, lb):
    L = r.shape[0]
    tl = 32
    n = RWKV_HEAD
    blk = pl.BlockSpec((tl, n, LANE), lambda i: (i, 0, 0))
    par = _const_spec((n, LANE))
    big = pltpu.VMEM((tl, n, LANE), F32)
    small = pltpu.VMEM((tl, 1, LANE), F32)
    return pl.pallas_call(
        _rwkv_scan_kernel,
        grid=(L // tl,),
        in_specs=[blk] * 5 + [par] * 5,
        out_specs=blk,
        out_shape=jax.ShapeDtypeStruct((L, n, LANE), F32),
        scratch_shapes=[pltpu.VMEM((n, n, LANE), F32), pltpu.VMEM((n, LANE), F32), big, big, big, big,
                        small, small],
        compiler_params=_params(("arbitrary",)),
        name="rwkv_scan",
    )(r, k, v, w, a, kk, ka, rk, lg, lb)


def _head_param(p):
    return jnp.repeat(p.reshape(RWKV_HEADS, RWKV_HEAD).T, BATCH, axis=1)


def _prep_even(w_in, a_re, a_im, log_dt, b_re, b_im, c_re, c_im, w_g2, b_g2):
    o = 0
    w_u = w_in[:, o:o + 512]; o += 512
    w_q = w_in[:, o:o + 256]; o += 256
    w_k = w_in[:, o:o + 256]; o += 256
    w_v = w_in[:, o:o + 512]; o += 512
    w_gl = w_in[:, o:o + GLA_RANK]; o += GLA_RANK
    w_r = w_in[:, o:o + 512]

    def pad_heads(w):
        w = w.reshape(-1, GLA_HEADS, GLA_DK)
        w = jnp.pad(w, ((0, 0), (0, 0), (0, GLA_DKP - GLA_DK)))
        return w.reshape(-1, GLA_HEADS * GLA_DKP)

    w_proj = jnp.concatenate([w_u, w_v, w_r, pad_heads(w_q), pad_heads(w_k),
                              jnp.pad(w_gl, ((0, 0), (0, LANE - GLA_RANK)))], axis=1)
    wg = jnp.pad(pad_heads(w_g2), ((0, LANE - GLA_RANK), (0, 0)))
    bg = pad_heads(b_g2[None, :])
    lm = pad_heads(jnp.ones((1, GLA_HEADS * GLA_DK), F32))

    dt = jnp.exp(log_dt)
    mag = jnp.exp(dt * a_re)
    abar_re = mag * jnp.cos(dt * a_im)
    abar_im = mag * jnp.sin(dt * a_im)
    nr = abar_re - 1.0
    den = a_re * a_re + a_im * a_im
    f_re = (nr * a_re + abar_im * a_im) / den
    f_im = (abar_im * a_re - nr * a_im) / den
    bbar_re = f_re[..., None] * b_re - f_im[..., None] * b_im
    bbar_im = f_re[..., None] * b_im + f_im[..., None] * b_re
    eye = jnp.eye(8, dtype=F32)
    gs = S5_GROUPS // S5_SLABS

    def bd(bb):
        bb = bb.reshape(S5_SLABS, gs, S5_STATE, S5_GROUP)
        return jnp.einsum('sgpc,gh->sgchp', bb, eye).reshape(S5_SLABS, gs * S5_GROUP, gs * S5_STATE)

    def cdm(cc):
        cc = cc.reshape(S5_SLABS, gs, S5_GROUP, S5_STATE)
        return jnp.einsum('sgcp,gh->sgphc', cc, eye).reshape(S5_SLABS, gs * S5_STATE, gs * S5_GROUP)

    bd_all = jnp.concatenate([bd(bbar_re), bd(bbar_im)], axis=2)
    cd_all = jnp.concatenate([cdm(c_re), -cdm(c_im)], axis=1)
    are = jnp.broadcast_to(abar_re.reshape(S5_SLABS, 1, gs * S5_STATE), (S5_SLABS, BATCH, gs * S5_STATE))
    aim = jnp.broadcast_to(abar_im.reshape(S5_SLABS, 1, gs * S5_STATE), (S5_SLABS, BATCH, gs * S5_STATE))
    return w_proj, wg, bg, lm, bd_all, cd_all, are, aim


def kernel(x, norm_mix_g, norm_ffn_g, final_g, e_w_in, e_w_out, s5_a_re, s5_a_im, s5_log_dt, s5_b_re, s5_b_im, s5_c_re, s5_c_im, s5_d, s5_w_glu, gla_w_g2, gla_b_g2, gla_norm_g, o_mu, o_w_r, o_w_k, o_w_v, o_w0, o_w_w1, o_w_w2, o_a0, o_w_a1, o_w_a2, o_w_g1, o_w_g2, o_k_k, o_k_a, o_r_k, o_lnx_g, o_lnx_b, o_w_o, peer_w_q, peer_sub_keys, peer_u, peer_v):
    bsz, L, D = x.shape
    assert bsz == BATCH and D == D_MODEL
    T = bsz * L
    row = lambda p: p.reshape(1, -1)
    h = x.transpose(1, 0, 2).reshape(T, D)

    w_proj, wg, bg, lm, bd_all, cd_all, are, aim = _prep_even(
        e_w_in[0], s5_a_re[0], s5_a_im[0], s5_log_dt[0], s5_b_re[0], s5_b_im[0],
        s5_c_re[0], s5_c_im[0], gla_w_g2[0], gla_b_g2[0])
    bf = lambda p: p.astype(BF16)
    proj = _even_front(h, row(norm_mix_g[0]), bf(w_proj))
    y_s5 = _s5(proj, bf(bd_all), bf(cd_all), are, aim, row(s5_d[0]), bf(s5_w_glu[0]))
    y_gla = _gla(proj, wg, bg, lm, row(gla_norm_g[0]))
    w_out = bf(e_w_out[0])
    h, hb, hbt = _mix_out(h, [y_s5, y_gla], [w_out[:S5_WIDTH], w_out[S5_WIDTH:]], None, row(norm_ffn_g[0]))
    h = _peer_layer(h, hb, hbt, peer_w_q[0], peer_sub_keys[0], peer_u[0], peer_v[0], row(final_g), False)

    r, k, v, w, a, g = _rwkv_front(
        h, row(norm_mix_g[1]), o_mu[0], bf(o_w_r[0]), bf(o_w_k[0]), bf(o_w_v[0]), row(o_w0[0]),
        bf(o_w_w1[0]), bf(o_w_w2[0]), row(o_a0[0]), bf(o_w_a1[0]), bf(o_w_a2[0]), bf(o_w_g1[0]), bf(o_w_g2[0]))
    y = _rwkv_scan(r, k, v, w, a, _head_param(o_k_k[0]), _head_param(o_k_a[0]),
                   _head_param(o_r_k[0].reshape(-1)), _head_param(o_lnx_g[0]), _head_param(o_lnx_b[0]))
    h, hb, hbt = _mix_out(h, [y], [bf(o_w_o[0])], g, row(norm_ffn_g[1]))
    out = _peer_layer(h, hb, hbt, peer_w_q[1], peer_sub_keys[1], peer_u[1], peer_v[1], row(final_g), True)
    return out.reshape(L, bsz, D).transpose(1, 0, 2)
```

```python
import functools
import math

import jax
import jax.numpy as jnp
from jax import lax
from jax.experimental import pallas as pl
from jax.experimental.pallas import tpu as pltpu

F32 = jnp.float32
BF16 = jnp.bfloat16
HIGHEST = lax.Precision.HIGHEST

D_MODEL = 1024
BATCH = 8
NORM_EPS = 1e-6
S5_WIDTH = 512
S5_GROUP = 16
S5_GROUPS = 32
S5_STATE = 64
S5_SLABS = 4
GLA_HEADS = 4
GLA_DV = 128
GLA_DK = 64
GLA_DKP = 128
GLA_RANK = 16
GLA_TAU = 16.0
GLA_CHUNK = 64
RWKV_HEAD = 64
RWKV_HEADS = 16
RWKV_GN_EPS = 64e-5
PEER_HEADS = 8
PEER_NKEYS = 128
PEER_TOPK = 16
PEER_TILE = 1024
PEER_TOKENS = 512
GATE_STEP = 2.0
LANE = 128
SUBLANE = 8
VMEM_LIMIT = 56 * 1024 * 1024

PROJ_W = 3 * 512 + 2 * GLA_HEADS * GLA_DKP + LANE


def _dot(a, b, hi=False):
    if hi:
        return jnp.dot(a, b, preferred_element_type=F32, precision=HIGHEST)
    return jnp.dot(a.astype(BF16), b.astype(BF16), preferred_element_type=F32)


def _dot_nt(a, b):
    return lax.dot_general(a.astype(BF16), b.astype(BF16), (((1,), (1,)), ((), ())),
                           preferred_element_type=F32)


def _rms(x, g):
    return x * lax.rsqrt(jnp.mean(x * x, axis=-1, keepdims=True) + NORM_EPS) * g


def _sigmoid(x):
    return 1.0 / (1.0 + jnp.exp(-x))


def _softplus(x):
    return jnp.maximum(x, 0.0) + jnp.log(1.0 + jnp.exp(-jnp.abs(x)))


def _gelu(x):
    return 0.5 * x * (1.0 + lax.erf(x * (1.0 / math.sqrt(2.0))))


def _const_spec(shape):
    nd = len(shape)
    return pl.BlockSpec(shape, lambda *_: (0,) * nd)


def _params(sem, flags=None):
    return pltpu.CompilerParams(dimension_semantics=sem, vmem_limit_bytes=VMEM_LIMIT, flags=flags)


def _even_front_kernel(h_ref, g_ref, w_ref, o_ref):
    xn = _rms(h_ref[...], g_ref[...])
    o_ref[...] = _dot(xn, w_ref[...])


def _even_front(h, g, w):
    T = h.shape[0]
    tm = 256
    return pl.pallas_call(
        _even_front_kernel,
        grid=(T // tm,),
        in_specs=[pl.BlockSpec((tm, D_MODEL), lambda i: (i, 0)),
                  _const_spec((1, D_MODEL)),
                  _const_spec((D_MODEL, PROJ_W))],
        out_specs=pl.BlockSpec((tm, PROJ_W), lambda i: (i, 0)),
        out_shape=jax.ShapeDtypeStruct((T, PROJ_W), F32),
        compiler_params=_params(("parallel",)),
        name="even_front",
    )(h, g, w)


def _s5_kernel(u_ref, bd_ref, cd_ref, are_ref, aim_ref, d_ref, wglu_ref, o_ref, bu_ref, st_ref):
    rows = u_ref.shape[0]
    steps = rows // BATCH
    half = 512

    @pl.when(pl.program_id(0) == 0)
    def _():
        st_ref[...] = jnp.zeros_like(st_ref)

    u = u_ref[...]
    for s in range(S5_SLABS):
        bu_ref[s] = _dot(u[:, s * LANE:(s + 1) * LANE], bd_ref[s])

    def step(t, carry):
        r0 = pl.multiple_of(t * BATCH, BATCH)
        out = []
        for s in range(S5_SLABS):
            sre, sim = carry[2 * s], carry[2 * s + 1]
            ar, ai = are_ref[s], aim_ref[s]
            bre = bu_ref[s, pl.ds(r0, BATCH), 0:half]
            bim = bu_ref[s, pl.ds(r0, BATCH), half:2 * half]
            nre = ar * sre - ai * sim + bre
            nim = ar * sim + ai * sre + bim
            bu_ref[s, pl.ds(r0, BATCH), 0:half] = nre
            bu_ref[s, pl.ds(r0, BATCH), half:2 * half] = nim
            out += [nre, nim]
        return tuple(out)

    init = tuple(st_ref[j] for j in range(2 * S5_SLABS))
    fin = lax.fori_loop(0, steps, step, init)
    for j in range(2 * S5_SLABS):
        st_ref[j] = fin[j]

    y = jnp.concatenate([_dot(bu_ref[s], cd_ref[s]) for s in range(S5_SLABS)], axis=1)
    y = y + d_ref[...] * u
    g = _gelu(y)
    o_ref[...] = g * _sigmoid(_dot(g, wglu_ref[...]))


def _s5(proj, bd, cd, are, aim, d, wglu):
    T = proj.shape[0]
    rows = 256
    return pl.pallas_call(
        _s5_kernel,
        grid=(T // rows,),
        in_specs=[pl.BlockSpec((rows, S5_WIDTH), lambda i: (i, 0)),
                  _const_spec(bd.shape), _const_spec(cd.shape),
                  _const_spec(are.shape), _const_spec(aim.shape),
                  _const_spec(d.shape), _const_spec(wglu.shape)],
        out_specs=pl.BlockSpec((rows, S5_WIDTH), lambda i: (i, 0)),
        out_shape=jax.ShapeDtypeStruct((T, S5_WIDTH), F32),
        scratch_shapes=[pltpu.VMEM((S5_SLABS, rows, 2 * 512), F32),
                        pltpu.VMEM((2 * S5_SLABS, BATCH, 512), F32)],
        compiler_params=_params(("arbitrary",)),
        name="s5",
    )(proj, bd, cd, are, aim, d, wglu)


def _gla_kernel(v_ref, r_ref, q_ref, k_ref, gl_ref, wg_ref, bg_ref, lm_ref, ng_ref, o_ref, s_ref):
    rows = q_ref.shape[0]
    width = GLA_HEADS * GLA_DKP

    @pl.when(pl.program_id(0) == 0)
    def _():
        s_ref[...] = jnp.zeros_like(s_ref)

    z = _dot(gl_ref[...], wg_ref[...], True) + bg_ref[...]
    la = -_softplus(-z) * (1.0 / GLA_TAU) * lm_ref[...]
    b = la
    sh = BATCH
    while sh < rows:
        b = b + jnp.concatenate([jnp.zeros((sh, width), F32), b[:rows - sh]], axis=0)
        sh *= 2
    blast = b[rows - BATCH:rows]
    blast_t = jnp.concatenate([blast] * GLA_CHUNK, axis=0)
    q = q_ref[...] * (GLA_DK ** -0.5)
    k = k_ref[...]
    qd = q * jnp.exp(b)
    kd = k * jnp.exp(-b)
    kend = k * jnp.exp(blast_t - b)

    ri = lax.broadcasted_iota(jnp.int32, (rows, rows), 0)
    ci = lax.broadcasted_iota(jnp.int32, (rows, rows), 1)
    causal = jnp.logical_and(((ri - ci) & (BATCH - 1)) == 0, ri >= ci)
    rowb = lax.broadcasted_iota(jnp.int32, (rows, GLA_DKP), 0) & (BATCH - 1)
    laneb = lax.broadcasted_iota(jnp.int32, (GLA_DKP, rows), 1) & (BATCH - 1)

    for h in range(GLA_HEADS):
        sl = slice(h * GLA_DKP, (h + 1) * GLA_DKP)
        vh = v_ref[:, h * GLA_DV:(h + 1) * GLA_DV]
        qh, kdh, keh, bh = qd[:, sl], kd[:, sl], kend[:, sl], b[:, sl]
        att = jnp.where(causal, _dot_nt(qh, kdh), 0.0)
        o = _dot(att, vh)
        qexp = jnp.concatenate([jnp.where(rowb == bb, qh, 0.0) for bb in range(BATCH)], axis=1)
        o = o + _dot(qexp, s_ref[h])
        ket = keh.T
        kexp_t = jnp.concatenate([jnp.where(laneb == bb, ket, 0.0) for bb in range(BATCH)], axis=0)
        kv = _dot(kexp_t, vh)
        bt = bh[rows - GLA_DKP:rows].T
        dec = jnp.concatenate(
            [jnp.broadcast_to(jnp.exp(bt[:, GLA_DKP - BATCH + bb:GLA_DKP - BATCH + bb + 1]),
                              (GLA_DKP, GLA_DV)) for bb in range(BATCH)], axis=0)
        s_ref[h] = s_ref[h] * dec + kv
        o = o * lax.rsqrt(jnp.mean(o * o, axis=-1, keepdims=True) + NORM_EPS)
        o = o * ng_ref[:, h * GLA_DV:(h + 1) * GLA_DV]
        rr = r_ref[:, h * GLA_DV:(h + 1) * GLA_DV]
        o_ref[:, h * GLA_DV:(h + 1) * GLA_DV] = o * (rr * _sigmoid(rr))


def _gla(proj, wg, bg, lm, ng):
    T = proj.shape[0]
    rows = GLA_CHUNK * BATCH
    width = GLA_HEADS * GLA_DKP
    return pl.pallas_call(
        _gla_kernel,
        grid=(T // rows,),
        in_specs=[pl.BlockSpec((rows, 512), lambda i: (i, 1)),
                  pl.BlockSpec((rows, 512), lambda i: (i, 2)),
                  pl.BlockSpec((rows, width), lambda i: (i, 3)),
                  pl.BlockSpec((rows, width), lambda i: (i, 4)),
                  pl.BlockSpec((rows, LANE), lambda i: (i, (3 * 512 + 2 * width) // LANE)),
                  _const_spec(wg.shape), _const_spec(bg.shape), _const_spec(lm.shape),
                  _const_spec(ng.shape)],
        out_specs=pl.BlockSpec((rows, GLA_HEADS * GLA_DV), lambda i: (i, 0)),
        out_shape=jax.ShapeDtypeStruct((T, GLA_HEADS * GLA_DV), F32),
        scratch_shapes=[pltpu.VMEM((GLA_HEADS, BATCH * GLA_DKP, GLA_DV), F32)],
        compiler_params=_params(("arbitrary",)),
        name="gla",
    )(proj, proj, proj, proj, proj, wg, bg, lm, ng)


def _mix_out_kernel(n_in, has_gate, *refs):
    h_ref = refs[0]
    y_refs = refs[1:1 + n_in]
    w_refs = refs[1 + n_in:1 + 2 * n_in]
    rest = refs[1 + 2 * n_in:]
    gate_ref = rest[0] if has_gate else None
    g_ref, ho_ref, hb_ref, hbt_ref = rest[1:] if has_gate else rest
    acc = h_ref[...]
    for y_ref, w_ref in zip(y_refs, w_refs):
        if len(y_ref.shape) == 3:
            y = jnp.concatenate([_scan_to_rows(y_ref, t0) for t0 in range(0, y_ref.shape[0], SCAN_BLOCK)],
                                axis=0)
        else:
            y = y_ref[...]
        if has_gate:
            y = y * gate_ref[...]
        acc = acc + _dot(y, w_ref[...])
    ho_ref[...] = acc
    hn = _rms(acc, g_ref[...])
    hb_ref[...] = hn.astype(BF16)
    hbt_ref[...] = hn.T.astype(BF16)


def _mix_out(h, ys, ws, gate, g):
    T = h.shape[0]
    tm = 256
    n_in = len(ys)
    row = pl.BlockSpec((tm, D_MODEL), lambda i: (i, 0))
    in_specs = [row]
    in_specs += [pl.BlockSpec((tm, y.shape[1]), lambda i: (i, 0)) if y.ndim == 2 else
                 pl.BlockSpec((tm // BATCH,) + y.shape[1:], lambda i: (i, 0, 0)) for y in ys]
    in_specs += [_const_spec(w.shape) for w in ws]
    args = [h, *ys, *ws]
    if gate is not None:
        in_specs.append(row)
        args.append(gate)
    in_specs.append(_const_spec((1, D_MODEL)))
    args.append(g)
    return pl.pallas_call(
        functools.partial(_mix_out_kernel, n_in, gate is not None),
        grid=(T // tm,),
        in_specs=in_specs,
        out_specs=[row, row, pl.BlockSpec((D_MODEL, tm), lambda i: (0, i))],
        out_shape=[jax.ShapeDtypeStruct((T, D_MODEL), F32),
                   jax.ShapeDtypeStruct((T, D_MODEL), BF16),
                   jax.ShapeDtypeStruct((D_MODEL, T), BF16)],
        compiler_params=_params(("parallel",)),
        name="mix_out",
    )(*args)


def _tree_sum(xs):
    while len(xs) > 1:
        xs = [xs[i] + xs[i + 1] for i in range(0, len(xs) - 1, 2)] + ([xs[-1]] if len(xs) % 2 else [])
    return xs[0]


def _insert(tops, x):
    out = []
    for t in tops:
        out.append(jnp.maximum(t, x))
        x = jnp.minimum(t, x)
    return out


def _compare_exchange(v, i, l, descending):
    hi, lo = jnp.maximum(v[i], v[l]), jnp.minimum(v[i], v[l])
    v[i], v[l] = (hi, lo) if descending else (lo, hi)


def _bitonic_merge_desc(v):
    v = list(v)
    j = len(v) // 2
    while j >= 1:
        for i in range(len(v)):
            if i ^ j > i:
                _compare_exchange(v, i, i ^ j, True)
        j //= 2
    return v


def _bitonic_sort_desc(v):
    v = list(v)
    k = 2
    while k <= len(v):
        j = k // 2
        while j >= 1:
            for i in range(len(v)):
                if i ^ j > i:
                    _compare_exchange(v, i, i ^ j, (i & k) == 0)
            j //= 2
        k *= 2
    return v


def _top_values(scr_ref):
    tops = None
    for g in range(PEER_NKEYS // PEER_TOPK):
        grp = _bitonic_sort_desc([scr_ref[pl.ds(g * PEER_TOPK + i, SUBLANE, stride=PEER_NKEYS), :]
                                  for i in range(PEER_TOPK)])
        if tops is None:
            tops = grp
        else:
            tops = _bitonic_merge_desc([jnp.maximum(tops[i], grp[PEER_TOPK - 1 - i])
                                        for i in range(PEER_TOPK)])
    return tops


def _pick(vals, bits):
    if not bits:
        return vals[0]
    half = len(vals) // 2
    return jnp.where(bits[0], _pick(vals[half:], bits[1:]), _pick(vals[:half], bits[1:]))


def _count_greater(x, s):
    bits = []
    step = PEER_TOPK // 2
    while step >= 1:
        vals = [s[base + step - 1] for base in range(0, PEER_TOPK, 2 * step)]
        bits.append(_pick(vals, bits) > x)
        step //= 2
    return bits, s[PEER_TOPK - 1] > x


def _peer_topk_kernel(hn_ref, wq_ref, keys_ref, n1_ref, e1_ref, r2_ref, e2_ref,
                      scr1, scr2, stat_ref):
    nch = hn_ref.shape[0] // LANE
    q = _dot(hn_ref[...], wq_ref[...])
    for c, scr in enumerate((scr1, scr2)):
        st = _dot_nt(keys_ref[0, c], q[:, c * LANE:(c + 1) * LANE])
        for j in range(nch):
            scr[j * PEER_NKEYS:(j + 1) * PEER_NKEYS, :] = st[:, j * LANE:(j + 1) * LANE]
    a = _top_values(scr1)
    b = _top_values(scr2)
    pairs = [(i, j) for i in range(PEER_TOPK) for j in range(PEER_TOPK) if (i + 1) * (j + 1) <= PEER_TOPK]
    tops = [jnp.full((SUBLANE, LANE), -jnp.inf, F32) for _ in range(PEER_TOPK)]
    for i, j in pairs:
        tops = _insert(tops, a[i] + b[j])
    thr = tops[PEER_TOPK - 1]
    z = jnp.zeros((SUBLANE, LANE), F32)
    for t in tops:
        z = z + jnp.exp(t - tops[0])
    inv_z = 1.0 / z
    cnt = [jnp.zeros((SUBLANE, LANE), F32) for _ in range(PEER_TOPK)]
    for i, j in pairs:
        cnt[i] = cnt[i] + jnp.where(a[i] + b[j] >= thr, GATE_STEP, 0.0)

    for i in range(PEER_TOPK):
        stat_ref[i] = a[i]
        stat_ref[PEER_TOPK + i] = b[i]
        stat_ref[2 * PEER_TOPK + i] = cnt[i]
    stat_ref[3 * PEER_TOPK] = inv_z
    weights = [GATE_STEP * (PEER_TOPK >> (i + 1)) for i in range(4)] + [GATE_STEP]

    def chunk(c, carry):
        def stat(i):
            return jnp.broadcast_to(stat_ref[i, pl.ds(c, 1), :], (SUBLANE, LANE))
        a_c = [stat(i) for i in range(PEER_TOPK)]
        b_c = [stat(PEER_TOPK + i) for i in range(PEER_TOPK)]
        cnt_c = [stat(2 * PEER_TOPK + i) for i in range(PEER_TOPK)]
        inv_z_c = stat(3 * PEER_TOPK)
        for g in range(PEER_NKEYS // SUBLANE):
            rows = pl.ds(pl.multiple_of(c * PEER_NKEYS, PEER_NKEYS) + g * SUBLANE, SUBLANE)
            x = scr1[rows, :]
            bits, below = _count_greater(x, a_c)
            n1_ref[0, c, g * SUBLANE:(g + 1) * SUBLANE, :] = jnp.where(below, 0.0, _pick(cnt_c, bits))
            e1_ref[0, c, g * SUBLANE:(g + 1) * SUBLANE, :] = jnp.exp(x - a_c[0]) * inv_z_c
            x = scr2[rows, :]
            bits, below = _count_greater(x, b_c)
            rank = _tree_sum([jnp.where(m, w, 0.0) for m, w in zip(bits + [below], weights)])
            scr1[rows, :] = rank
            scr2[rows, :] = jnp.exp(x - b_c[0])
        return carry

    lax.fori_loop(0, nch, chunk, 0)
    shape = (nch, PEER_NKEYS, LANE)
    r2_ref[0] = scr1[...].astype(BF16).reshape(shape)
    e2_ref[0] = scr2[...].astype(BF16).reshape(shape)


def _peer_topk(hn, wq, keys):
    T = hn.shape[0]
    tt = SUBLANE * LANE
    nch = T // LANE
    shape = (PEER_HEADS, nch, PEER_NKEYS, LANE)
    spec = pl.BlockSpec((1, SUBLANE, PEER_NKEYS, LANE), lambda i, h: (h, i, 0, 0))
    return pl.pallas_call(
        _peer_topk_kernel,
        grid=(T // tt, PEER_HEADS),
        in_specs=[pl.BlockSpec((tt, D_MODEL), lambda i, h: (i, 0)),
                  pl.BlockSpec((D_MODEL, 2 * LANE), lambda i, h: (0, h)),
                  pl.BlockSpec((1, 2, PEER_NKEYS, LANE), lambda i, h: (h, 0, 0, 0))],
        out_specs=[spec, spec, spec, spec],
        out_shape=[jax.ShapeDtypeStruct(shape, F32), jax.ShapeDtypeStruct(shape, F32),
                   jax.ShapeDtypeStruct(shape, BF16), jax.ShapeDtypeStruct(shape, BF16)],
        scratch_shapes=[pltpu.VMEM((tt, LANE), F32), pltpu.VMEM((tt, LANE), F32),
                        pltpu.VMEM((3 * PEER_TOPK + 1, SUBLANE, LANE), F32)],
        compiler_params=_params(("parallel", "arbitrary")),
        name="peer_topk",
    )(hn, wq, keys)


def _row_bf16(ref, h, c, i1, dep=None):
    packed_rows = 2 * SUBLANE
    row = ref[h, c, pl.ds(i1, 1), :]
    if dep is not None:
        row = row + dep
    x = jnp.broadcast_to(row, (packed_rows, LANE)).astype(BF16)
    return jnp.concatenate([x] * (PEER_NKEYS // packed_rows), axis=0)


def _zero_after(x):
    bits = pltpu.bitcast(x[0:2 * SUBLANE, :], jnp.uint32)
    sixteen = jnp.uint32(16)
    zero_bits = lax.shift_right_logical(lax.shift_right_logical(bits, sixteen), sixteen)
    return pltpu.bitcast(zero_bits, F32)[0:1, :]


def _peer_main_kernel(final, hbt_ref, hres_ref, u0_ref, ub_ref, ua_ref, vta_ref, vtb_ref,
                      n1_ref, e1_ref, r2_ref, e2_ref, fg_ref, o_ref,
                      acc_ref, hga_ref, hgb_ref, hta_ref, htb_ref):
    e = pl.program_id(1)
    n_pairs = pl.num_programs(1) - 1
    tt = hbt_ref.shape[1]
    n_i1 = PEER_TILE // PEER_NKEYS
    n_ch = tt // LANE
    zero = jnp.zeros((PEER_NKEYS, LANE), BF16)

    @pl.when(e == 0)
    def _():
        acc_ref[...] = jnp.zeros_like(acc_ref)
        hgb_ref[...] = jnp.zeros_like(hgb_ref)
        hta_ref[...] = _dot(u0_ref[...], hbt_ref[...])

    def gates(tile, ht_ref, hg_ref, c, dep):
        for il in range(n_i1):
            i1 = tile * n_i1 + il
            gate = None
            for h in range(PEER_HEADS):
                code = _row_bf16(n1_ref, h, c, i1, dep if h == 0 else None) - r2_ref[h, c]
                w = _row_bf16(e1_ref, h, c, i1) * e2_ref[h, c]
                term = jnp.maximum(jnp.minimum(w, code), zero)
                gate = term if gate is None else gate + term
            rows = slice(il * PEER_NKEYS, (il + 1) * PEER_NKEYS)
            cols = slice(c * LANE, (c + 1) * LANE)
            hg = _gelu(ht_ref[rows, cols]).astype(BF16) * gate
            hg_ref[rows, cols] = hg
            dep = _zero_after(hg)
        return dep

    def phase(tile, ht_ref, hg_ref, vt_prev_ref, hg_prev_ref, u_next_ref, ht_next_ref, dep):
        half = tt // 2
        for piece in range(4):
            cols = slice((piece % 2) * half, (piece % 2 + 1) * half)
            if piece < 2:
                acc_ref[:, cols] += _dot(vt_prev_ref[...], hg_prev_ref[:, cols])
            else:
                ht_next_ref[:, cols] = _dot(u_next_ref[...], hbt_ref[:, cols])
            for c in range(piece * n_ch // 4, (piece + 1) * n_ch // 4):
                dep = gates(tile, ht_ref, hg_ref, c, dep)
        return dep

    @pl.when(e < n_pairs)
    def _():
        dep = phase(2 * e, hta_ref, hga_ref, vtb_ref, hgb_ref, ub_ref, htb_ref, None)
        phase(2 * e + 1, htb_ref, hgb_ref, vta_ref, hga_ref, ua_ref, hta_ref, dep)

    @pl.when(e == n_pairs)
    def _():
        acc = acc_ref[...] + _dot(vtb_ref[...], hgb_ref[...])
        out = hres_ref[...] + acc.T
        if final:
            out = _rms(out, fg_ref[...])
        o_ref[...] = out


def _peer_main(hbt, hres, u, vt, n1, e1, r2, e2, fg, final):
    T = hres.shape[0]
    E = u.shape[0]
    tt = PEER_TOKENS
    n_tiles = E // PEER_TILE
    n_pairs = n_tiles // 2
    big_spec = pl.BlockSpec((PEER_HEADS, tt // LANE, PEER_NKEYS, LANE), lambda i, e: (0, i, 0, 0))
    tile_of = lambda f: (lambda i, e: (jnp.clip(f(e), 0, n_tiles - 1), 0, 0))
    vt_spec = lambda f: pl.BlockSpec((D_MODEL, PEER_TILE), lambda i, e: tile_of(f)(i, e)[:2])
    u_spec = lambda f: pl.BlockSpec((PEER_TILE, D_MODEL), lambda i, e: tile_of(f)(i, e)[:2])
    return pl.pallas_call(
        functools.partial(_peer_main_kernel, final),
        grid=(T // tt, n_pairs + 1),
        in_specs=[pl.BlockSpec((D_MODEL, tt), lambda i, e: (0, i)),
                  pl.BlockSpec((tt, D_MODEL), lambda i, e: (i, 0)),
                  u_spec(lambda e: 0), u_spec(lambda e: 2 * e + 1), u_spec(lambda e: 2 * e + 2),
                  vt_spec(lambda e: 2 * e), vt_spec(lambda e: 2 * e - 1),
                  big_spec, big_spec, big_spec, big_spec,
                  _const_spec((1, D_MODEL))],
        out_specs=pl.BlockSpec((tt, D_MODEL), lambda i, e: (i, 0)),
        out_shape=jax.ShapeDtypeStruct((T, D_MODEL), F32),
        scratch_shapes=[pltpu.VMEM((D_MODEL, tt), F32),
                        pltpu.VMEM((PEER_TILE, tt), BF16), pltpu.VMEM((PEER_TILE, tt), BF16),
                        pltpu.VMEM((PEER_TILE, tt), F32), pltpu.VMEM((PEER_TILE, tt), F32)],
        compiler_params=_params(("parallel", "arbitrary")),
        name="peer_main",
    )(hbt, hres, u, u, u, vt, vt, n1, e1, r2, e2, fg)


def _peer_layer(h, hb, hbt, wq, keys, u_tab, v_tab, fg, final):
    n1, e1, r2, e2 = _peer_topk(hb, wq.astype(BF16), keys.astype(BF16))
    vt = v_tab.astype(BF16).reshape(-1, PEER_TILE, D_MODEL).transpose(0, 2, 1).reshape(-1, PEER_TILE)
    return _peer_main(hbt, h, u_tab.astype(BF16), vt, n1, e1, r2, e2, fg, final)


SCAN_BLOCK = LANE // BATCH


def _swap_groups(slabs):
    a = list(slabs)
    group = lax.broadcasted_iota(jnp.int32, a[0].shape, 1) // BATCH
    s = SCAN_BLOCK // 2
    while s >= 1:
        upper = (group & s) != 0
        for i in range(SCAN_BLOCK):
            if i & s == 0:
                ai, aj = a[i], a[i + s]
                a[i] = jnp.where(upper, pltpu.roll(aj, s * BATCH, axis=1), ai)
                a[i + s] = jnp.where(upper, aj, pltpu.roll(ai, LANE - s * BATCH, axis=1))
        s //= 2
    return a


def _rows_to_scan(x, ref):
    block_rows = SCAN_BLOCK * BATCH
    for r0 in range(0, x.shape[0], block_rows):
        z = x[r0:r0 + block_rows].T
        slabs = [z[h * RWKV_HEAD:(h + 1) * RWKV_HEAD, :] for h in range(RWKV_HEADS)]
        for t, blk in enumerate(_swap_groups(slabs)):
            ref[r0 // BATCH + t] = blk


def _scan_to_rows(ref, t0):
    slabs = _swap_groups([ref[t0 + t] for t in range(SCAN_BLOCK)])
    return jnp.concatenate(slabs, axis=0).T


def _rwkv_front_kernel(h_ref, hp_ref, g_ref, mu_ref, wr_ref, wk_ref, wv_ref, w0_ref, ww1_ref, ww2_ref,
                       a0_ref, wa1_ref, wa2_ref, wg1_ref, wg2_ref,
                       r_ref, k_ref, v_ref, w_ref, a_ref, go_ref):
    g = g_ref[...]
    x = _rms(h_ref[...], g)
    xp8 = _rms(hp_ref[...], g)
    xp8 = jnp.where(pl.program_id(0) == 0, 0.0, xp8)
    rows = x.shape[0]
    xprev = jnp.concatenate([xp8, x[:rows - BATCH]], axis=0)
    xx = xprev - x
    mix = lambda j: x + xx * mu_ref[j:j + 1, :]
    _rows_to_scan(_dot(mix(0), wr_ref[...]), r_ref)
    _rows_to_scan(w0_ref[...] + _dot(jnp.tanh(_dot(mix(1), ww1_ref[...])), ww2_ref[...]), w_ref)
    _rows_to_scan(_dot(mix(2), wk_ref[...]), k_ref)
    _rows_to_scan(_dot(mix(3), wv_ref[...]), v_ref)
    _rows_to_scan(_sigmoid(a0_ref[...] + _dot(_dot(mix(4), wa1_ref[...]), wa2_ref[...])), a_ref)
    go_ref[...] = _dot(_sigmoid(_dot(mix(5), wg1_ref[...])), wg2_ref[...])


def _rwkv_front(h, g, mu, wr, wk, wv, w0, ww1, ww2, a0, wa1, wa2, wg1, wg2):
    assert RWKV_HEADS == SCAN_BLOCK
    T = h.shape[0]
    tm = 2 * SCAN_BLOCK * BATCH
    row = pl.BlockSpec((tm, D_MODEL), lambda i: (i, 0))
    prev = pl.BlockSpec((BATCH, D_MODEL), lambda i: (jnp.maximum(i * (tm // BATCH) - 1, 0), 0))
    scan = pl.BlockSpec((tm // BATCH, RWKV_HEAD, LANE), lambda i: (i, 0, 0))
    consts = [g, mu, wr, wk, wv, w0, ww1, ww2, a0, wa1, wa2, wg1, wg2]
    scan_out = jax.ShapeDtypeStruct((T // BATCH, RWKV_HEAD, LANE), F32)
    return pl.pallas_call(
        _rwkv_front_kernel,
        grid=(T // tm,),
        in_specs=[row, prev] + [_const_spec(c.shape) for c in consts],
        out_specs=[scan] * 5 + [row],
        out_shape=[scan_out] * 5 + [jax.ShapeDtypeStruct((T, D_MODEL), F32)],
        compiler_params=_params(("parallel",)),
        name="rwkv_front",
    )(h, h, *consts)


def _rwkv_scan_kernel(r_ref, k_ref, v_ref, w_ref, a_ref, kk_ref, ka_ref, rk_ref, lg_ref, lb_ref,
                      y_ref, s_ref, pend_s, k2_s, av_s, bv_s, wr_s, br_s, kr_s):
    tl = r_ref.shape[0]
    n = RWKV_HEAD
    ng = n // SUBLANE

    @pl.when(pl.program_id(0) == 0)
    def _():
        s_ref[...] = jnp.zeros_like(s_ref)

    r = r_ref[...]
    k = k_ref[...]
    a = a_ref[...]
    kk = k * kk_ref[...]
    kk = kk / jnp.maximum(jnp.sqrt(jnp.sum(kk * kk, axis=1, keepdims=True)), 1e-12)
    k2 = k * (1.0 + (a - 1.0) * ka_ref[...])
    dec = jnp.exp(-jnp.exp(-_softplus(-w_ref[...]) - 0.5))
    bv = kk * a
    p = dec
    sh = 1
    while sh < tl:
        p = p * jnp.concatenate([jnp.ones((sh, n, LANE), F32), p[:tl - sh]], axis=0)
        sh *= 2
    p_prev = jnp.concatenate([jnp.ones((1, n, LANE), F32), p[:tl - 1]], axis=0)
    inv_p = 1.0 / p
    pend_s[...] = p[tl - 1]
    k2_s[...] = k2 * inv_p
    av_s[...] = -kk * p_prev
    bv_s[...] = bv * inv_p
    wr_s[...] = r * p
    br_s[...] = jnp.sum(bv * r, axis=1, keepdims=True)
    kr_s[...] = jnp.sum(k2 * r, axis=1, keepdims=True)

    def row(ref, t, j):
        return jnp.broadcast_to(ref[t, pl.ds(j, 1), :], (SUBLANE, LANE))

    def step(t, carry):
        def reduce_body(j, accs):
            sa, sy = accs
            a_b = row(av_s, t, j)
            wr_b = row(wr_s, t, j)
            sa_n, sy_n = [], []
            for vg in range(ng):
                sv = s_ref[j, vg * SUBLANE:(vg + 1) * SUBLANE, :]
                sa_n.append(sa[vg] + sv * a_b)
                sy_n.append(sy[vg] + sv * wr_b)
            return tuple(sa_n), tuple(sy_n)

        zeros = tuple(jnp.zeros((SUBLANE, LANE), F32) for _ in range(ng))
        sa, sy = lax.fori_loop(0, n, reduce_body, (zeros, zeros), unroll=8)
        br = jnp.broadcast_to(br_s[t], (SUBLANE, LANE))
        kr = jnp.broadcast_to(kr_s[t], (SUBLANE, LANE))
        vv = [v_ref[t, vg * SUBLANE:(vg + 1) * SUBLANE, :] for vg in range(ng)]
        for vg in range(ng):
            y_ref[t, vg * SUBLANE:(vg + 1) * SUBLANE, :] = sy[vg] + sa[vg] * br + vv[vg] * kr

        def update_body(j, c):
            b_b = row(bv_s, t, j)
            k_b = row(k2_s, t, j)
            for vg in range(ng):
                sl = slice(vg * SUBLANE, (vg + 1) * SUBLANE)
                s_ref[j, sl, :] = s_ref[j, sl, :] + sa[vg] * b_b + vv[vg] * k_b
            return c

        lax.fori_loop(0, n, update_body, 0, unroll=8)
        return carry

    lax.fori_loop(0, tl, step, 0)

    def rescale(j, c):
        s_ref[j] = s_ref[j] * pend_s[pl.ds(j, 1), :]
        return c

    lax.fori_loop(0, n, rescale, 0, unroll=8)

    y = y_ref[...]
    mean = jnp.mean(y, axis=1, keepdims=True)
    var = jnp.mean((y - mean) ** 2, axis=1, keepdims=True)
    y = (y - mean) * lax.rsqrt(var + RWKV_GN_EPS) * lg_ref[...] + lb_ref[...]
    bonus = jnp.sum(r * k2 * rk_ref[...], axis=1, keepdims=True) * v_ref[...]
    y_ref[...] = y + bonus


def _rwkv_scan(r, k, v, w, a, kk, ka, rk, lg, lb):
    L = r.shape[0]
    tl = 32
    n = RWKV_HEAD
    blk = pl.BlockSpec((tl, n, LANE), lambda i: (i, 0, 0))
    par = _const_spec((n, LANE))
    big = pltpu.VMEM((tl, n, LANE), F32)
    small = pltpu.VMEM((tl, 1, LANE), F32)
    return pl.pallas_call(
        _rwkv_scan_kernel,
        grid=(L // tl,),
        in_specs=[blk] * 5 + [par] * 5,
        out_specs=blk,
        out_shape=jax.ShapeDtypeStruct((L, n, LANE), F32),
        scratch_shapes=[pltpu.VMEM((n, n, LANE), F32), pltpu.VMEM((n, LANE), F32), big, big, big, big,
                        small, small],
        compiler_params=_params(("arbitrary",)),
        name="rwkv_scan",
    )(r, k, v, w, a, kk, ka, rk, lg, lb)


def _head_param(p):
    return jnp.repeat(p.reshape(RWKV_HEADS, RWKV_HEAD).T, BATCH, axis=1)


def _prep_even(w_in, a_re, a_im, log_dt, b_re, b_im, c_re, c_im, w_g2, b_g2):
    o = 0
    w_u = w_in[:, o:o + 512]; o += 512
    w_q = w_in[:, o:o + 256]; o += 256
    w_k = w_in[:, o:o + 256]; o += 256
    w_v = w_in[:, o:o + 512]; o += 512
    w_gl = w_in[:, o:o + GLA_RANK]; o += GLA_RANK
    w_r = w_in[:, o:o + 512]

    def pad_heads(w):
        w = w.reshape(-1, GLA_HEADS, GLA_DK)
        w = jnp.pad(w, ((0, 0), (0, 0), (0, GLA_DKP - GLA_DK)))
        return w.reshape(-1, GLA_HEADS * GLA_DKP)

    w_proj = jnp.concatenate([w_u, w_v, w_r, pad_heads(w_q), pad_heads(w_k),
                              jnp.pad(w_gl, ((0, 0), (0, LANE - GLA_RANK)))], axis=1)
    wg = jnp.pad(pad_heads(w_g2), ((0, LANE - GLA_RANK), (0, 0)))
    bg = pad_heads(b_g2[None, :])
    lm = pad_heads(jnp.ones((1, GLA_HEADS * GLA_DK), F32))

    dt = jnp.exp(log_dt)
    mag = jnp.exp(dt * a_re)
    abar_re = mag * jnp.cos(dt * a_im)
    abar_im = mag * jnp.sin(dt * a_im)
    nr = abar_re - 1.0
    den = a_re * a_re + a_im * a_im
    f_re = (nr * a_re + abar_im * a_im) / den
    f_im = (abar_im * a_re - nr * a_im) / den
    bbar_re = f_re[..., None] * b_re - f_im[..., None] * b_im
    bbar_im = f_re[..., None] * b_im + f_im[..., None] * b_re
    eye = jnp.eye(8, dtype=F32)
    gs = S5_GROUPS // S5_SLABS

    def bd(bb):
        bb = bb.reshape(S5_SLABS, gs, S5_STATE, S5_GROUP)
        return jnp.einsum('sgpc,gh->sgchp', bb, eye).reshape(S5_SLABS, gs * S5_GROUP, gs * S5_STATE)

    def cdm(cc):
        cc = cc.reshape(S5_SLABS, gs, S5_GROUP, S5_STATE)
        return jnp.einsum('sgcp,gh->sgphc', cc, eye).reshape(S5_SLABS, gs * S5_STATE, gs * S5_GROUP)

    bd_all = jnp.concatenate([bd(bbar_re), bd(bbar_im)], axis=2)
    cd_all = jnp.concatenate([cdm(c_re), -cdm(c_im)], axis=1)
    are = jnp.broadcast_to(abar_re.reshape(S5_SLABS, 1, gs * S5_STATE), (S5_SLABS, BATCH, gs * S5_STATE))
    aim = jnp.broadcast_to(abar_im.reshape(S5_SLABS, 1, gs * S5_STATE), (S5_SLABS, BATCH, gs * S5_STATE))
    return w_proj, wg, bg, lm, bd_all, cd_all, are, aim


def kernel(x, norm_mix_g, norm_ffn_g, final_g, e_w_in, e_w_out, s5_a_re, s5_a_im, s5_log_dt, s5_b_re, s5_b_im, s5_c_re, s5_c_im, s5_d, s5_w_glu, gla_w_g2, gla_b_g2, gla_norm_g, o_mu, o_w_r, o_w_k, o_w_v, o_w0, o_w_w1, o_w_w2, o_a0, o_w_a1, o_w_a2, o_w_g1, o_w_g2, o_k_k, o_k_a, o_r_k, o_lnx_g, o_lnx_b, o_w_o, peer_w_q, peer_sub_keys, peer_u, peer_v):
    bsz, L, D = x.shape
    assert bsz == BATCH and D == D_MODEL
    T = bsz * L
    row = lambda p: p.reshape(1, -1)
    h = x.transpose(1, 0, 2).reshape(T, D)

    w_proj, wg, bg, lm, bd_all, cd_all, are, aim = _prep_even(
        e_w_in[0], s5_a_re[0], s5_a_im[0], s5_log_dt[0], s5_b_re[0], s5_b_im[0],
        s5_c_re[0], s5_c_im[0], gla_w_g2[0], gla_b_g2[0])
    bf = lambda p: p.astype(BF16)
    proj = _even_front(h, row(norm_mix_g[0]), bf(w_proj))
    y_s5 = _s5(proj, bf(bd_all), bf(cd_all), are, aim, row(s5_d[0]), bf(s5_w_glu[0]))
    y_gla = _gla(proj, wg, bg, lm, row(gla_norm_g[0]))
    w_out = bf(e_w_out[0])
    h, hb, hbt = _mix_out(h, [y_s5, y_gla], [w_out[:S5_WIDTH], w_out[S5_WIDTH:]], None, row(norm_ffn_g[0]))
    h = _peer_layer(h, hb, hbt, peer_w_q[0], peer_sub_keys[0], peer_u[0], peer_v[0], row(final_g), False)

    r, k, v, w, a, g = _rwkv_front(
        h, row(norm_mix_g[1]), o_mu[0], bf(o_w_r[0]), bf(o_w_k[0]), bf(o_w_v[0]), row(o_w0[0]),
        bf(o_w_w1[0]), bf(o_w_w2[0]), row(o_a0[0]), bf(o_w_a1[0]), bf(o_w_a2[0]), bf(o_w_g1[0]), bf(o_w_g2[0]))
    y = _rwkv_scan(r, k, v, w, a, _head_param(o_k_k[0]), _head_param(o_k_a[0]),
                   _head_param(o_r_k[0].reshape(-1)), _head_param(o_lnx_g[0]), _head_param(o_lnx_b[0]))
    h, hb, hbt = _mix_out(h, [y], [bf(o_w_o[0])], g, row(norm_ffn_g[1]))
    out = _peer_layer(h, hb, hbt, peer_w_q[1], peer_sub_keys[1], peer_u[1], peer_v[1], row(final_g), True)
    return out.reshape(L, bsz, D).transpose(1, 0, 2)
```

```python
import functools
import math

import jax
import jax.numpy as jnp
from jax import lax
from jax.experimental import pallas as pl
from jax.experimental.pallas import tpu as pltpu

F32 = jnp.float32
BF16 = jnp.bfloat16
HIGHEST = lax.Precision.HIGHEST

D_MODEL = 1024
BATCH = 8
NORM_EPS = 1e-6
S5_WIDTH = 512
S5_GROUP = 16
S5_GROUPS = 32
S5_STATE = 64
S5_SLABS = 4
S5_SLAB_STATES = S5_GROUPS // S5_SLABS * S5_STATE
GLA_HEADS = 4
GLA_DV = 128
GLA_DK = 64
GLA_DKP = 128
GLA_RANK = 16
GLA_TAU = 16.0
GLA_CHUNK = 64
RWKV_HEAD = 64
RWKV_HEADS = 16
RWKV_GN_EPS = 64e-5
PEER_HEADS = 8
PEER_NKEYS = 128
PEER_TOPK = 16
PEER_TILE = 1024
PEER_TOKENS = 512
GATE_STEP = 2.0
LANE = 128
SUBLANE = 8
VMEM_LIMIT = 56 * 1024 * 1024
ROW_TILE = 256
SCAN_STEPS = 32

GLA_WIDTH = GLA_HEADS * GLA_DV
GLA_QK = GLA_HEADS * GLA_DKP
PROJ_GATE_COL = S5_WIDTH + 2 * GLA_WIDTH + 2 * GLA_QK
PROJ_W = PROJ_GATE_COL + LANE


def _dot(a, b, hi=False):
    if hi:
        return jnp.dot(a, b, preferred_element_type=F32, precision=HIGHEST)
    return jnp.dot(a.astype(BF16), b.astype(BF16), preferred_element_type=F32)


def _dot_nt(a, b):
    return lax.dot_general(a.astype(BF16), b.astype(BF16), (((1,), (1,)), ((), ())),
                           preferred_element_type=F32)


def _rms(x, g):
    return x * lax.rsqrt(jnp.mean(x * x, axis=-1, keepdims=True) + NORM_EPS) * g


def _sigmoid(x):
    return 1.0 / (1.0 + jnp.exp(-x))


def _softplus(x):
    return jnp.maximum(x, 0.0) + jnp.log(1.0 + jnp.exp(-jnp.abs(x)))


def _gelu(x):
    return 0.5 * x * (1.0 + lax.erf(x * (1.0 / math.sqrt(2.0))))


def _const_spec(shape):
    nd = len(shape)
    return pl.BlockSpec(shape, lambda *_: (0,) * nd)


def _params(sem, flags=None):
    return pltpu.CompilerParams(dimension_semantics=sem, vmem_limit_bytes=VMEM_LIMIT, flags=flags)


def _even_front_kernel(h_ref, g_ref, w_ref, o_ref):
    xn = _rms(h_ref[...], g_ref[...])
    o_ref[...] = _dot(xn, w_ref[...])


def _even_front(h, g, w):
    T = h.shape[0]
    tm = ROW_TILE
    return pl.pallas_call(
        _even_front_kernel,
        grid=(T // tm,),
        in_specs=[pl.BlockSpec((tm, D_MODEL), lambda i: (i, 0)),
                  _const_spec((1, D_MODEL)),
                  _const_spec((D_MODEL, PROJ_W))],
        out_specs=pl.BlockSpec((tm, PROJ_W), lambda i: (i, 0)),
        out_shape=jax.ShapeDtypeStruct((T, PROJ_W), F32),
        compiler_params=_params(("parallel",)),
        name="even_front",
    )(h, g, w)


def _s5_kernel(u_ref, bd_ref, cd_ref, are_ref, aim_ref, d_ref, wglu_ref, o_ref, bu_ref, st_ref):
    rows = u_ref.shape[0]
    steps = rows // BATCH
    half = S5_SLAB_STATES

    @pl.when(pl.program_id(0) == 0)
    def _():
        st_ref[...] = jnp.zeros_like(st_ref)

    u = u_ref[...]
    for s in range(S5_SLABS):
        bu_ref[s] = _dot(u[:, s * LANE:(s + 1) * LANE], bd_ref[s])

    def step(t, carry):
        r0 = pl.multiple_of(t * BATCH, BATCH)
        out = []
        for s in range(S5_SLABS):
            sre, sim = carry[2 * s], carry[2 * s + 1]
            ar, ai = are_ref[s], aim_ref[s]
            bre = bu_ref[s, pl.ds(r0, BATCH), 0:half]
            bim = bu_ref[s, pl.ds(r0, BATCH), half:2 * half]
            nre = ar * sre - ai * sim + bre
            nim = ar * sim + ai * sre + bim
            bu_ref[s, pl.ds(r0, BATCH), 0:half] = nre
            bu_ref[s, pl.ds(r0, BATCH), half:2 * half] = nim
            out += [nre, nim]
        return tuple(out)

    init = tuple(st_ref[j] for j in range(2 * S5_SLABS))
    fin = lax.fori_loop(0, steps, step, init)
    for j in range(2 * S5_SLABS):
        st_ref[j] = fin[j]

    y = jnp.concatenate([_dot(bu_ref[s], cd_ref[s]) for s in range(S5_SLABS)], axis=1)
    y = y + d_ref[...] * u
    g = _gelu(y)
    o_ref[...] = g * _sigmoid(_dot(g, wglu_ref[...]))


def _s5(proj, bd, cd, are, aim, d, wglu):
    T = proj.shape[0]
    rows = ROW_TILE
    return pl.pallas_call(
        _s5_kernel,
        grid=(T // rows,),
        in_specs=[pl.BlockSpec((rows, S5_WIDTH), lambda i: (i, 0)),
                  _const_spec(bd.shape), _const_spec(cd.shape),
                  _const_spec(are.shape), _const_spec(aim.shape),
                  _const_spec(d.shape), _const_spec(wglu.shape)],
        out_specs=pl.BlockSpec((rows, S5_WIDTH), lambda i: (i, 0)),
        out_shape=jax.ShapeDtypeStruct((T, S5_WIDTH), F32),
        scratch_shapes=[pltpu.VMEM((S5_SLABS, rows, 2 * S5_SLAB_STATES), F32),
                        pltpu.VMEM((2 * S5_SLABS, BATCH, S5_SLAB_STATES), F32)],
        compiler_params=_params(("arbitrary",)),
        name="s5",
    )(proj, bd, cd, are, aim, d, wglu)


def _gla_kernel(v_ref, r_ref, q_ref, k_ref, gl_ref, wg_ref, bg_ref, lm_ref, ng_ref, o_ref, s_ref):
    rows = q_ref.shape[0]
    width = GLA_HEADS * GLA_DKP

    @pl.when(pl.program_id(0) == 0)
    def _():
        s_ref[...] = jnp.zeros_like(s_ref)

    z = _dot(gl_ref[...], wg_ref[...], True) + bg_ref[...]
    la = -_softplus(-z) * (1.0 / GLA_TAU) * lm_ref[...]
    b = la
    sh = BATCH
    while sh < rows:
        b = b + jnp.concatenate([jnp.zeros((sh, width), F32), b[:rows - sh]], axis=0)
        sh *= 2
    blast = b[rows - BATCH:rows]
    blast_t = jnp.concatenate([blast] * GLA_CHUNK, axis=0)
    q = q_ref[...] * (GLA_DK ** -0.5)
    k = k_ref[...]
    qd = q * jnp.exp(b)
    kd = k * jnp.exp(-b)
    kend = k * jnp.exp(blast_t - b)

    ri = lax.broadcasted_iota(jnp.int32, (rows, rows), 0)
    ci = lax.broadcasted_iota(jnp.int32, (rows, rows), 1)
    causal = jnp.logical_and(((ri - ci) & (BATCH - 1)) == 0, ri >= ci)
    rowb = lax.broadcasted_iota(jnp.int32, (rows, GLA_DKP), 0) & (BATCH - 1)
    laneb = lax.broadcasted_iota(jnp.int32, (GLA_DKP, rows), 1) & (BATCH - 1)

    for h in range(GLA_HEADS):
        sl = slice(h * GLA_DKP, (h + 1) * GLA_DKP)
        vh = v_ref[:, h * GLA_DV:(h + 1) * GLA_DV]
        qh, kdh, keh, bh = qd[:, sl], kd[:, sl], kend[:, sl], b[:, sl]
        att = jnp.where(causal, _dot_nt(qh, kdh), 0.0)
        o = _dot(att, vh)
        qexp = jnp.concatenate([jnp.where(rowb == bb, qh, 0.0) for bb in range(BATCH)], axis=1)
        o = o + _dot(qexp, s_ref[h])
        ket = keh.T
        kexp_t = jnp.concatenate([jnp.where(laneb == bb, ket, 0.0) for bb in range(BATCH)], axis=0)
        kv = _dot(kexp_t, vh)
        bt = bh[rows - GLA_DKP:rows].T
        dec = jnp.concatenate(
            [jnp.broadcast_to(jnp.exp(bt[:, GLA_DKP - BATCH + bb:GLA_DKP - BATCH + bb + 1]),
                              (GLA_DKP, GLA_DV)) for bb in range(BATCH)], axis=0)
        s_ref[h] = s_ref[h] * dec + kv
        o = o * lax.rsqrt(jnp.mean(o * o, axis=-1, keepdims=True) + NORM_EPS)
        o = o * ng_ref[:, h * GLA_DV:(h + 1) * GLA_DV]
        rr = r_ref[:, h * GLA_DV:(h + 1) * GLA_DV]
        o_ref[:, h * GLA_DV:(h + 1) * GLA_DV] = o * (rr * _sigmoid(rr))


def _gla(proj, wg, bg, lm, ng):
    T = proj.shape[0]
    rows = GLA_CHUNK * BATCH
    width = GLA_HEADS * GLA_DKP
    return pl.pallas_call(
        _gla_kernel,
        grid=(T // rows,),
        in_specs=[pl.BlockSpec((rows, GLA_WIDTH), lambda i: (i, 1)),
                  pl.BlockSpec((rows, GLA_WIDTH), lambda i: (i, 2)),
                  pl.BlockSpec((rows, width), lambda i: (i, 3)),
                  pl.BlockSpec((rows, width), lambda i: (i, 4)),
                  pl.BlockSpec((rows, LANE), lambda i: (i, PROJ_GATE_COL // LANE)),
                  _const_spec(wg.shape), _const_spec(bg.shape), _const_spec(lm.shape),
                  _const_spec(ng.shape)],
        out_specs=pl.BlockSpec((rows, GLA_HEADS * GLA_DV), lambda i: (i, 0)),
        out_shape=jax.ShapeDtypeStruct((T, GLA_HEADS * GLA_DV), F32),
        scratch_shapes=[pltpu.VMEM((GLA_HEADS, BATCH * GLA_DKP, GLA_DV), F32)],
        compiler_params=_params(("arbitrary",)),
        name="gla",
    )(proj, proj, proj, proj, proj, wg, bg, lm, ng)


def _mix_out_kernel(n_in, has_gate, *refs):
    h_ref = refs[0]
    y_refs = refs[1:1 + n_in]
    w_refs = refs[1 + n_in:1 + 2 * n_in]
    rest = refs[1 + 2 * n_in:]
    gate_ref = rest[0] if has_gate else None
    g_ref, ho_ref, hb_ref, hbt_ref = rest[1:] if has_gate else rest
    acc = h_ref[...]
    for y_ref, w_ref in zip(y_refs, w_refs):
        if len(y_ref.shape) == 3:
            y = jnp.concatenate([_scan_to_rows(y_ref, t0) for t0 in range(0, y_ref.shape[0], SCAN_BLOCK)],
                                axis=0)
        else:
            y = y_ref[...]
        if has_gate:
            y = y * gate_ref[...]
        acc = acc + _dot(y, w_ref[...])
    ho_ref[...] = acc
    hn = _rms(acc, g_ref[...])
    hb_ref[...] = hn.astype(BF16)
    hbt_ref[...] = hn.T.astype(BF16)


def _mix_out(h, ys, ws, gate, g):
    T = h.shape[0]
    tm = ROW_TILE
    n_in = len(ys)
    row = pl.BlockSpec((tm, D_MODEL), lambda i: (i, 0))
    in_specs = [row]
    in_specs += [pl.BlockSpec((tm, y.shape[1]), lambda i: (i, 0)) if y.ndim == 2 else
                 pl.BlockSpec((tm // BATCH,) + y.shape[1:], lambda i: (i, 0, 0)) for y in ys]
    in_specs += [_const_spec(w.shape) for w in ws]
    args = [h, *ys, *ws]
    if gate is not None:
        in_specs.append(row)
        args.append(gate)
    in_specs.append(_const_spec((1, D_MODEL)))
    args.append(g)
    return pl.pallas_call(
        functools.partial(_mix_out_kernel, n_in, gate is not None),
        grid=(T // tm,),
        in_specs=in_specs,
        out_specs=[row, row, pl.BlockSpec((D_MODEL, tm), lambda i: (0, i))],
        out_shape=[jax.ShapeDtypeStruct((T, D_MODEL), F32),
                   jax.ShapeDtypeStruct((T, D_MODEL), BF16),
                   jax.ShapeDtypeStruct((D_MODEL, T), BF16)],
        compiler_params=_params(("parallel",)),
        name="mix_out",
    )(*args)


def _tree_sum(xs):
    while len(xs) > 1:
        xs = [xs[i] + xs[i + 1] for i in range(0, len(xs) - 1, 2)] + ([xs[-1]] if len(xs) % 2 else [])
    return xs[0]


def _insert(tops, x):
    out = []
    for t in tops:
        out.append(jnp.maximum(t, x))
        x = jnp.minimum(t, x)
    return out


def _compare_exchange(v, i, l, descending):
    hi, lo = jnp.maximum(v[i], v[l]), jnp.minimum(v[i], v[l])
    v[i], v[l] = (hi, lo) if descending else (lo, hi)


def _bitonic_merge_desc(v):
    v = list(v)
    j = len(v) // 2
    while j >= 1:
        for i in range(len(v)):
            if i ^ j > i:
                _compare_exchange(v, i, i ^ j, True)
        j //= 2
    return v


def _bitonic_sort_desc(v):
    v = list(v)
    k = 2
    while k <= len(v):
        j = k // 2
        while j >= 1:
            for i in range(len(v)):
                if i ^ j > i:
                    _compare_exchange(v, i, i ^ j, (i & k) == 0)
            j //= 2
        k *= 2
    return v


def _top_values(scr_ref):
    tops = None
    for g in range(PEER_NKEYS // PEER_TOPK):
        grp = _bitonic_sort_desc([scr_ref[pl.ds(g * PEER_TOPK + i, SUBLANE, stride=PEER_NKEYS), :]
                                  for i in range(PEER_TOPK)])
        if tops is None:
            tops = grp
        else:
            tops = _bitonic_merge_desc([jnp.maximum(tops[i], grp[PEER_TOPK - 1 - i])
                                        for i in range(PEER_TOPK)])
    return tops


def _pick(vals, bits):
    if not bits:
        return vals[0]
    half = len(vals) // 2
    return jnp.where(bits[0], _pick(vals[half:], bits[1:]), _pick(vals[:half], bits[1:]))


def _count_greater(x, s):
    bits = []
    step = PEER_TOPK // 2
    while step >= 1:
        vals = [s[base + step - 1] for base in range(0, PEER_TOPK, 2 * step)]
        bits.append(_pick(vals, bits) > x)
        step //= 2
    return bits, s[PEER_TOPK - 1] > x


def _peer_topk_kernel(hn_ref, wq_ref, keys_ref, n1_ref, e1_ref, r2_ref, e2_ref,
                      scr1, scr2, stat_ref):
    nch = hn_ref.shape[0] // LANE
    q = _dot(hn_ref[...], wq_ref[...])
    for c, scr in enumerate((scr1, scr2)):
        st = _dot_nt(keys_ref[0, c], q[:, c * LANE:(c + 1) * LANE])
        for j in range(nch):
            scr[j * PEER_NKEYS:(j + 1) * PEER_NKEYS, :] = st[:, j * LANE:(j + 1) * LANE]
    a = _top_values(scr1)
    b = _top_values(scr2)
    pairs = [(i, j) for i in range(PEER_TOPK) for j in range(PEER_TOPK) if (i + 1) * (j + 1) <= PEER_TOPK]
    tops = [jnp.full((SUBLANE, LANE), -jnp.inf, F32) for _ in range(PEER_TOPK)]
    for i, j in pairs:
        tops = _insert(tops, a[i] + b[j])
    thr = tops[PEER_TOPK - 1]
    z = jnp.zeros((SUBLANE, LANE), F32)
    for t in tops:
        z = z + jnp.exp(t - tops[0])
    inv_z = 1.0 / z
    cnt = [jnp.zeros((SUBLANE, LANE), F32) for _ in range(PEER_TOPK)]
    for i, j in pairs:
        cnt[i] = cnt[i] + jnp.where(a[i] + b[j] >= thr, GATE_STEP, 0.0)

    for i in range(PEER_TOPK):
        stat_ref[i] = a[i]
        stat_ref[PEER_TOPK + i] = b[i]
        stat_ref[2 * PEER_TOPK + i] = cnt[i]
    stat_ref[3 * PEER_TOPK] = inv_z
    weights = [GATE_STEP * (PEER_TOPK >> (i + 1)) for i in range(4)] + [GATE_STEP]

    def chunk(c, carry):
        def stat(i):
            return jnp.broadcast_to(stat_ref[i, pl.ds(c, 1), :], (SUBLANE, LANE))
        a_c = [stat(i) for i in range(PEER_TOPK)]
        b_c = [stat(PEER_TOPK + i) for i in range(PEER_TOPK)]
        cnt_c = [stat(2 * PEER_TOPK + i) for i in range(PEER_TOPK)]
        inv_z_c = stat(3 * PEER_TOPK)
        for g in range(PEER_NKEYS // SUBLANE):
            rows = pl.ds(pl.multiple_of(c * PEER_NKEYS, PEER_NKEYS) + g * SUBLANE, SUBLANE)
            x = scr1[rows, :]
            bits, below = _count_greater(x, a_c)
            n1_ref[0, c, g * SUBLANE:(g + 1) * SUBLANE, :] = jnp.where(below, 0.0, _pick(cnt_c, bits))
            e1_ref[0, c, g * SUBLANE:(g + 1) * SUBLANE, :] = jnp.exp(x - a_c[0]) * inv_z_c
            x = scr2[rows, :]
            bits, below = _count_greater(x, b_c)
            rank = _tree_sum([jnp.where(m, w, 0.0) for m, w in zip(bits + [below], weights)])
            scr1[rows, :] = rank
            scr2[rows, :] = jnp.exp(x - b_c[0])
        return carry

    lax.fori_loop(0, nch, chunk, 0)
    shape = (nch, PEER_NKEYS, LANE)
    r2_ref[0] = scr1[...].astype(BF16).reshape(shape)
    e2_ref[0] = scr2[...].astype(BF16).reshape(shape)


def _peer_topk(hn, wq, keys):
    T = hn.shape[0]
    tt = SUBLANE * LANE
    nch = T // LANE
    shape = (PEER_HEADS, nch, PEER_NKEYS, LANE)
    spec = pl.BlockSpec((1, SUBLANE, PEER_NKEYS, LANE), lambda i, h: (h, i, 0, 0))
    return pl.pallas_call(
        _peer_topk_kernel,
        grid=(T // tt, PEER_HEADS),
        in_specs=[pl.BlockSpec((tt, D_MODEL), lambda i, h: (i, 0)),
                  pl.BlockSpec((D_MODEL, 2 * LANE), lambda i, h: (0, h)),
                  pl.BlockSpec((1, 2, PEER_NKEYS, LANE), lambda i, h: (h, 0, 0, 0))],
        out_specs=[spec, spec, spec, spec],
        out_shape=[jax.ShapeDtypeStruct(shape, F32), jax.ShapeDtypeStruct(shape, F32),
                   jax.ShapeDtypeStruct(shape, BF16), jax.ShapeDtypeStruct(shape, BF16)],
        scratch_shapes=[pltpu.VMEM((tt, LANE), F32), pltpu.VMEM((tt, LANE), F32),
                        pltpu.VMEM((3 * PEER_TOPK + 1, SUBLANE, LANE), F32)],
        compiler_params=_params(("parallel", "arbitrary")),
        name="peer_topk",
    )(hn, wq, keys)


def _row_bf16(ref, h, c, i1, dep=None):
    packed_rows = 2 * SUBLANE
    row = ref[h, c, pl.ds(i1, 1), :]
    if dep is not None:
        row = row + dep
    x = jnp.broadcast_to(row, (packed_rows, LANE)).astype(BF16)
    return jnp.concatenate([x] * (PEER_NKEYS // packed_rows), axis=0)


def _zero_after(x):
    bits = pltpu.bitcast(x[0:2 * SUBLANE, :], jnp.uint32)
    sixteen = jnp.uint32(16)
    zero_bits = lax.shift_right_logical(lax.shift_right_logical(bits, sixteen), sixteen)
    return pltpu.bitcast(zero_bits, F32)[0:1, :]


def _peer_main_kernel(final, hbt_ref, hres_ref, u0_ref, ub_ref, ua_ref, vta_ref, vtb_ref,
                      n1_ref, e1_ref, r2_ref, e2_ref, fg_ref, o_ref,
                      acc_ref, hga_ref, hgb_ref, hta_ref, htb_ref):
    e = pl.program_id(1)
    n_pairs = pl.num_programs(1) - 1
    tt = hbt_ref.shape[1]
    n_i1 = PEER_TILE // PEER_NKEYS
    n_ch = tt // LANE
    zero = jnp.zeros((PEER_NKEYS, LANE), BF16)

    @pl.when(e == 0)
    def _():
        acc_ref[...] = jnp.zeros_like(acc_ref)
        hgb_ref[...] = jnp.zeros_like(hgb_ref)
        hta_ref[...] = _dot(u0_ref[...], hbt_ref[...])

    def gates(tile, ht_ref, hg_ref, c, dep):
        for il in range(n_i1):
            i1 = tile * n_i1 + il
            gate = None
            for h in range(PEER_HEADS):
                code = _row_bf16(n1_ref, h, c, i1, dep if h == 0 else None) - r2_ref[h, c]
                w = _row_bf16(e1_ref, h, c, i1) * e2_ref[h, c]
                term = jnp.maximum(jnp.minimum(w, code), zero)
                gate = term if gate is None else gate + term
            rows = slice(il * PEER_NKEYS, (il + 1) * PEER_NKEYS)
            cols = slice(c * LANE, (c + 1) * LANE)
            hg = _gelu(ht_ref[rows, cols]).astype(BF16) * gate
            hg_ref[rows, cols] = hg
            dep = _zero_after(hg)
        return dep

    def phase(tile, ht_ref, hg_ref, vt_prev_ref, hg_prev_ref, u_next_ref, ht_next_ref, dep):
        half = tt // 2
        for piece in range(4):
            cols = slice((piece % 2) * half, (piece % 2 + 1) * half)
            if piece < 2:
                acc_ref[:, cols] += _dot(vt_prev_ref[...], hg_prev_ref[:, cols])
            else:
                ht_next_ref[:, cols] = _dot(u_next_ref[...], hbt_ref[:, cols])
            for c in range(piece * n_ch // 4, (piece + 1) * n_ch // 4):
                dep = gates(tile, ht_ref, hg_ref, c, dep)
        return dep

    @pl.when(e < n_pairs)
    def _():
        dep = phase(2 * e, hta_ref, hga_ref, vtb_ref, hgb_ref, ub_ref, htb_ref, None)
        phase(2 * e + 1, htb_ref, hgb_ref, vta_ref, hga_ref, ua_ref, hta_ref, dep)

    @pl.when(e == n_pairs)
    def _():
        acc = acc_ref[...] + _dot(vtb_ref[...], hgb_ref[...])
        out = hres_ref[...] + acc.T
        if final:
            out = _rms(out, fg_ref[...])
        o_ref[...] = out


def _peer_main(hbt, hres, u, vt, n1, e1, r2, e2, fg, final):
    T = hres.shape[0]
    E = u.shape[0]
    tt = PEER_TOKENS
    n_tiles = E // PEER_TILE
    n_pairs = n_tiles // 2
    big_spec = pl.BlockSpec((PEER_HEADS, tt // LANE, PEER_NKEYS, LANE), lambda i, e: (0, i, 0, 0))
    tile_of = lambda f: (lambda i, e: (jnp.clip(f(e), 0, n_tiles - 1), 0, 0))
    vt_spec = lambda f: pl.BlockSpec((D_MODEL, PEER_TILE), lambda i, e: tile_of(f)(i, e)[:2])
    u_spec = lambda f: pl.BlockSpec((PEER_TILE, D_MODEL), lambda i, e: tile_of(f)(i, e)[:2])
    return pl.pallas_call(
        functools.partial(_peer_main_kernel, final),
        grid=(T // tt, n_pairs + 1),
        in_specs=[pl.BlockSpec((D_MODEL, tt), lambda i, e: (0, i)),
                  pl.BlockSpec((tt, D_MODEL), lambda i, e: (i, 0)),
                  u_spec(lambda e: 0), u_spec(lambda e: 2 * e + 1), u_spec(lambda e: 2 * e + 2),
                  vt_spec(lambda e: 2 * e), vt_spec(lambda e: 2 * e - 1),
                  big_spec, big_spec, big_spec, big_spec,
                  _const_spec((1, D_MODEL))],
        out_specs=pl.BlockSpec((tt, D_MODEL), lambda i, e: (i, 0)),
        out_shape=jax.ShapeDtypeStruct((T, D_MODEL), F32),
        scratch_shapes=[pltpu.VMEM((D_MODEL, tt), F32),
                        pltpu.VMEM((PEER_TILE, tt), BF16), pltpu.VMEM((PEER_TILE, tt), BF16),
                        pltpu.VMEM((PEER_TILE, tt), F32), pltpu.VMEM((PEER_TILE, tt), F32)],
        compiler_params=_params(("parallel", "arbitrary")),
        name="peer_main",
    )(hbt, hres, u, u, u, vt, vt, n1, e1, r2, e2, fg)


def _peer_layer(h, hb, hbt, wq, keys, u_tab, v_tab, fg, final):
    n1, e1, r2, e2 = _peer_topk(hb, wq.astype(BF16), keys.astype(BF16))
    vt = v_tab.astype(BF16).reshape(-1, PEER_TILE, D_MODEL).transpose(0, 2, 1).reshape(-1, PEER_TILE)
    return _peer_main(hbt, h, u_tab.astype(BF16), vt, n1, e1, r2, e2, fg, final)


SCAN_BLOCK = LANE // BATCH


def _swap_groups(slabs):
    a = list(slabs)
    group = lax.broadcasted_iota(jnp.int32, a[0].shape, 1) // BATCH
    s = SCAN_BLOCK // 2
    while s >= 1:
        upper = (group & s) != 0
        for i in range(SCAN_BLOCK):
            if i & s == 0:
                ai, aj = a[i], a[i + s]
                a[i] = jnp.where(upper, pltpu.roll(aj, s * BATCH, axis=1), ai)
                a[i + s] = jnp.where(upper, aj, pltpu.roll(ai, LANE - s * BATCH, axis=1))
        s //= 2
    return a


def _rows_to_scan(x, ref):
    block_rows = SCAN_BLOCK * BATCH
    for r0 in range(0, x.shape[0], block_rows):
        z = x[r0:r0 + block_rows].T
        slabs = [z[h * RWKV_HEAD:(h + 1) * RWKV_HEAD, :] for h in range(RWKV_HEADS)]
        for t, blk in enumerate(_swap_groups(slabs)):
            ref[r0 // BATCH + t] = blk


def _scan_to_rows(ref, t0):
    slabs = _swap_groups([ref[t0 + t] for t in range(SCAN_BLOCK)])
    return jnp.concatenate(slabs, axis=0).T


def _rwkv_front_kernel(h_ref, hp_ref, g_ref, mu_ref, wr_ref, wk_ref, wv_ref, w0_ref, ww1_ref, ww2_ref,
                       a0_ref, wa1_ref, wa2_ref, wg1_ref, wg2_ref,
                       r_ref, k_ref, v_ref, w_ref, a_ref, go_ref):
    g = g_ref[...]
    x = _rms(h_ref[...], g)
    xp8 = _rms(hp_ref[...], g)
    xp8 = jnp.where(pl.program_id(0) == 0, 0.0, xp8)
    rows = x.shape[0]
    xprev = jnp.concatenate([xp8, x[:rows - BATCH]], axis=0)
    xx = xprev - x
    mix = lambda j: x + xx * mu_ref[j:j + 1, :]
    _rows_to_scan(_dot(mix(0), wr_ref[...]), r_ref)
    _rows_to_scan(w0_ref[...] + _dot(jnp.tanh(_dot(mix(1), ww1_ref[...])), ww2_ref[...]), w_ref)
    _rows_to_scan(_dot(mix(2), wk_ref[...]), k_ref)
    _rows_to_scan(_dot(mix(3), wv_ref[...]), v_ref)
    _rows_to_scan(_sigmoid(a0_ref[...] + _dot(_dot(mix(4), wa1_ref[...]), wa2_ref[...])), a_ref)
    go_ref[...] = _dot(_sigmoid(_dot(mix(5), wg1_ref[...])), wg2_ref[...])


def _rwkv_front(h, g, mu, wr, wk, wv, w0, ww1, ww2, a0, wa1, wa2, wg1, wg2):
    assert RWKV_HEADS == SCAN_BLOCK
    T = h.shape[0]
    tm = 2 * SCAN_BLOCK * BATCH
    row = pl.BlockSpec((tm, D_MODEL), lambda i: (i, 0))
    prev = pl.BlockSpec((BATCH, D_MODEL), lambda i: (jnp.maximum(i * (tm // BATCH) - 1, 0), 0))
    scan = pl.BlockSpec((tm // BATCH, RWKV_HEAD, LANE), lambda i: (i, 0, 0))
    consts = [g, mu, wr, wk, wv, w0, ww1, ww2, a0, wa1, wa2, wg1, wg2]
    scan_out = jax.ShapeDtypeStruct((T // BATCH, RWKV_HEAD, LANE), F32)
    return pl.pallas_call(
        _rwkv_front_kernel,
        grid=(T // tm,),
        in_specs=[row, prev] + [_const_spec(c.shape) for c in consts],
        out_specs=[scan] * 5 + [row],
        out_shape=[scan_out] * 5 + [jax.ShapeDtypeStruct((T, D_MODEL), F32)],
        compiler_params=_params(("parallel",)),
        name="rwkv_front",
    )(h, h, *consts)


def _rwkv_scan_kernel(r_ref, k_ref, v_ref, w_ref, a_ref, kk_ref, ka_ref, rk_ref, lg_ref, lb_ref,
                      y_ref, s_ref, pend_s, k2_s, av_s, bv_s, wr_s, br_s, kr_s):
    tl = r_ref.shape[0]
    n = RWKV_HEAD
    ng = n // SUBLANE

    @pl.when(pl.program_id(0) == 0)
    def _():
        s_ref[...] = jnp.zeros_like(s_ref)

    r = r_ref[...]
    k = k_ref[...]
    a = a_ref[...]
    kk = k * kk_ref[...]
    kk = kk / jnp.maximum(jnp.sqrt(jnp.sum(kk * kk, axis=1, keepdims=True)), 1e-12)
    k2 = k * (1.0 + (a - 1.0) * ka_ref[...])
    dec = jnp.exp(-jnp.exp(-_softplus(-w_ref[...]) - 0.5))
    bv = kk * a
    p = dec
    sh = 1
    while sh < tl:
        p = p * jnp.concatenate([jnp.ones((sh, n, LANE), F32), p[:tl - sh]], axis=0)
        sh *= 2
    p_prev = jnp.concatenate([jnp.ones((1, n, LANE), F32), p[:tl - 1]], axis=0)
    inv_p = 1.0 / p
    pend_s[...] = p[tl - 1]
    k2_s[...] = k2 * inv_p
    av_s[...] = -kk * p_prev
    bv_s[...] = bv * inv_p
    wr_s[...] = r * p
    br_s[...] = jnp.sum(bv * r, axis=1, keepdims=True)
    kr_s[...] = jnp.sum(k2 * r, axis=1, keepdims=True)

    def row(ref, t, j):
        return jnp.broadcast_to(ref[t, pl.ds(j, 1), :], (SUBLANE, LANE))

    def step(t, carry):
        def reduce_body(j, accs):
            sa, sy = accs
            a_b = row(av_s, t, j)
            wr_b = row(wr_s, t, j)
            sa_n, sy_n = [], []
            for vg in range(ng):
                sv = s_ref[j, vg * SUBLANE:(vg + 1) * SUBLANE, :]
                sa_n.append(sa[vg] + sv * a_b)
                sy_n.append(sy[vg] + sv * wr_b)
            return tuple(sa_n), tuple(sy_n)

        zeros = tuple(jnp.zeros((SUBLANE, LANE), F32) for _ in range(ng))
        sa, sy = lax.fori_loop(0, n, reduce_body, (zeros, zeros), unroll=8)
        br = jnp.broadcast_to(br_s[t], (SUBLANE, LANE))
        kr = jnp.broadcast_to(kr_s[t], (SUBLANE, LANE))
        vv = [v_ref[t, vg * SUBLANE:(vg + 1) * SUBLANE, :] for vg in range(ng)]
        for vg in range(ng):
            y_ref[t, vg * SUBLANE:(vg + 1) * SUBLANE, :] = sy[vg] + sa[vg] * br + vv[vg] * kr

        def update_body(j, c):
            b_b = row(bv_s, t, j)
            k_b = row(k2_s, t, j)
            for vg in range(ng):
                sl = slice(vg * SUBLANE, (vg + 1) * SUBLANE)
                s_ref[j, sl, :] = s_ref[j, sl, :] + sa[vg] * b_b + vv[vg] * k_b
            return c

        lax.fori_loop(0, n, update_body, 0, unroll=8)
        return carry

    lax.fori_loop(0, tl, step, 0)

    def rescale(j, c):
        s_ref[j] = s_ref[j] * pend_s[pl.ds(j, 1), :]
        return c

    lax.fori_loop(0, n, rescale, 0, unroll=8)

    y = y_ref[...]
    mean = jnp.mean(y, axis=1, keepdims=True)
    var = jnp.mean((y - mean) ** 2, axis=1, keepdims=True)
    y = (y - mean) * lax.rsqrt(var + RWKV_GN_EPS) * lg_ref[...] + lb_ref[...]
    bonus = jnp.sum(r * k2 * rk_ref[...], axis=1, keepdims=True) * v_ref[...]
    y_ref[...] = y + bonus


def _rwkv_scan(r, k, v, w, a, kk, ka, rk, lg, lb):
    L = r.shape[0]
    tl = SCAN_STEPS
    n = RWKV_HEAD
    blk = pl.BlockSpec((tl, n, LANE), lambda i: (i, 0, 0))
    par = _const_spec((n, LANE))
    big = pltpu.VMEM((tl, n, LANE), F32)
    small = pltpu.VMEM((tl, 1, LANE), F32)
    return pl.pallas_call(
        _rwkv_scan_kernel,
        grid=(L // tl,),
        in_specs=[blk] * 5 + [par] * 5,
        out_specs=blk,
        out_shape=jax.ShapeDtypeStruct((L, n, LANE), F32),
        scratch_shapes=[pltpu.VMEM((n, n, LANE), F32), pltpu.VMEM((n, LANE), F32), big, big, big, big,
                        small, small],
        compiler_params=_params(("arbitrary",)),
        name="rwkv_scan",
    )(r, k, v, w, a, kk, ka, rk, lg, lb)


def _head_param(p):
    return jnp.repeat(p.reshape(RWKV_HEADS, RWKV_HEAD).T, BATCH, axis=1)


def _prep_even(w_in, a_re, a_im, log_dt, b_re, b_im, c_re, c_im, w_g2, b_g2):
    assert S5_WIDTH == GLA_WIDTH == GLA_QK
    qk = GLA_HEADS * GLA_DK
    widths = (S5_WIDTH, qk, qk, GLA_WIDTH, GLA_RANK, GLA_WIDTH)
    edges = [sum(widths[:j]) for j in range(len(widths) + 1)]
    w_u, w_q, w_k, w_v, w_gl, w_r = (w_in[:, edges[j]:edges[j + 1]] for j in range(len(widths)))

    def pad_heads(w):
        w = w.reshape(-1, GLA_HEADS, GLA_DK)
        w = jnp.pad(w, ((0, 0), (0, 0), (0, GLA_DKP - GLA_DK)))
        return w.reshape(-1, GLA_HEADS * GLA_DKP)

    w_proj = jnp.concatenate([w_u, w_v, w_r, pad_heads(w_q), pad_heads(w_k),
                              jnp.pad(w_gl, ((0, 0), (0, LANE - GLA_RANK)))], axis=1)
    wg = jnp.pad(pad_heads(w_g2), ((0, LANE - GLA_RANK), (0, 0)))
    bg = pad_heads(b_g2[None, :])
    lm = pad_heads(jnp.ones((1, GLA_HEADS * GLA_DK), F32))

    dt = jnp.exp(log_dt)
    mag = jnp.exp(dt * a_re)
    abar_re = mag * jnp.cos(dt * a_im)
    abar_im = mag * jnp.sin(dt * a_im)
    nr = abar_re - 1.0
    den = a_re * a_re + a_im * a_im
    f_re = (nr * a_re + abar_im * a_im) / den
    f_im = (abar_im * a_re - nr * a_im) / den
    bbar_re = f_re[..., None] * b_re - f_im[..., None] * b_im
    bbar_im = f_re[..., None] * b_im + f_im[..., None] * b_re
    eye = jnp.eye(8, dtype=F32)
    gs = S5_GROUPS // S5_SLABS

    def bd(bb):
        bb = bb.reshape(S5_SLABS, gs, S5_STATE, S5_GROUP)
        return jnp.einsum('sgpc,gh->sgchp', bb, eye).reshape(S5_SLABS, gs * S5_GROUP, gs * S5_STATE)

    def cdm(cc):
        cc = cc.reshape(S5_SLABS, gs, S5_GROUP, S5_STATE)
        return jnp.einsum('sgcp,gh->sgphc', cc, eye).reshape(S5_SLABS, gs * S5_STATE, gs * S5_GROUP)

    bd_all = jnp.concatenate([bd(bbar_re), bd(bbar_im)], axis=2)
    cd_all = jnp.concatenate([cdm(c_re), -cdm(c_im)], axis=1)
    are = jnp.broadcast_to(abar_re.reshape(S5_SLABS, 1, gs * S5_STATE), (S5_SLABS, BATCH, gs * S5_STATE))
    aim = jnp.broadcast_to(abar_im.reshape(S5_SLABS, 1, gs * S5_STATE), (S5_SLABS, BATCH, gs * S5_STATE))
    return w_proj, wg, bg, lm, bd_all, cd_all, are, aim


def kernel(x, norm_mix_g, norm_ffn_g, final_g, e_w_in, e_w_out, s5_a_re, s5_a_im, s5_log_dt, s5_b_re, s5_b_im, s5_c_re, s5_c_im, s5_d, s5_w_glu, gla_w_g2, gla_b_g2, gla_norm_g, o_mu, o_w_r, o_w_k, o_w_v, o_w0, o_w_w1, o_w_w2, o_a0, o_w_a1, o_w_a2, o_w_g1, o_w_g2, o_k_k, o_k_a, o_r_k, o_lnx_g, o_lnx_b, o_w_o, peer_w_q, peer_sub_keys, peer_u, peer_v):
    bsz, L, D = x.shape
    assert bsz == BATCH and D == D_MODEL
    T = bsz * L
    row = lambda p: p.reshape(1, -1)
    h = x.transpose(1, 0, 2).reshape(T, D)

    w_proj, wg, bg, lm, bd_all, cd_all, are, aim = _prep_even(
        e_w_in[0], s5_a_re[0], s5_a_im[0], s5_log_dt[0], s5_b_re[0], s5_b_im[0],
        s5_c_re[0], s5_c_im[0], gla_w_g2[0], gla_b_g2[0])
    bf = lambda p: p.astype(BF16)
    proj = _even_front(h, row(norm_mix_g[0]), bf(w_proj))
    y_s5 = _s5(proj, bf(bd_all), bf(cd_all), are, aim, row(s5_d[0]), bf(s5_w_glu[0]))
    y_gla = _gla(proj, wg, bg, lm, row(gla_norm_g[0]))
    w_out = bf(e_w_out[0])
    h, hb, hbt = _mix_out(h, [y_s5, y_gla], [w_out[:S5_WIDTH], w_out[S5_WIDTH:]], None, row(norm_ffn_g[0]))
    h = _peer_layer(h, hb, hbt, peer_w_q[0], peer_sub_keys[0], peer_u[0], peer_v[0], row(final_g), False)

    r, k, v, w, a, g = _rwkv_front(
        h, row(norm_mix_g[1]), o_mu[0], bf(o_w_r[0]), bf(o_w_k[0]), bf(o_w_v[0]), row(o_w0[0]),
        bf(o_w_w1[0]), bf(o_w_w2[0]), row(o_a0[0]), bf(o_w_a1[0]), bf(o_w_a2[0]), bf(o_w_g1[0]), bf(o_w_g2[0]))
    y = _rwkv_scan(r, k, v, w, a, _head_param(o_k_k[0]), _head_param(o_k_a[0]),
                   _head_param(o_r_k[0].reshape(-1)), _head_param(o_lnx_g[0]), _head_param(o_lnx_b[0]))
    h, hb, hbt = _mix_out(h, [y], [bf(o_w_o[0])], g, row(norm_ffn_g[1]))
    out = _peer_layer(h, hb, hbt, peer_w_q[1], peer_sub_keys[1], peer_u[1], peer_v[1], row(final_g), True)
    return out.reshape(L, bsz, D).transpose(1, 0, 2)
```

```python
import functools
import math

import jax
import jax.numpy as jnp
from jax import lax
from jax.experimental import pallas as pl
from jax.experimental.pallas import tpu as pltpu

F32 = jnp.float32
BF16 = jnp.bfloat16
HIGHEST = lax.Precision.HIGHEST

D_MODEL = 1024
BATCH = 8
NORM_EPS = 1e-6
S5_WIDTH = 512
S5_GROUP = 16
S5_GROUPS = 32
S5_STATE = 64
S5_SLABS = 4
S5_SLAB_STATES = S5_GROUPS // S5_SLABS * S5_STATE
GLA_HEADS = 4
GLA_DV = 128
GLA_DK = 64
GLA_DKP = 128
GLA_RANK = 16
GLA_TAU = 16.0
GLA_CHUNK = 64
RWKV_HEAD = 64
RWKV_HEADS = 16
RWKV_GN_EPS = 64e-5
PEER_HEADS = 8
PEER_NKEYS = 128
PEER_TOPK = 16
PEER_TILE = 1024
PEER_TOKENS = 512
GATE_STEP = 2.0
LANE = 128
SUBLANE = 8
VMEM_LIMIT = 56 * 1024 * 1024
CHUNK_PITCH = PEER_NKEYS + SUBLANE
ROW_TILE = 256
SCAN_STEPS = 32

GLA_WIDTH = GLA_HEADS * GLA_DV
GLA_QK = GLA_HEADS * GLA_DKP
PROJ_GATE_COL = S5_WIDTH + 2 * GLA_WIDTH + 2 * GLA_QK
PROJ_W = PROJ_GATE_COL + LANE


def _dot(a, b, hi=False):
    if hi:
        return jnp.dot(a, b, preferred_element_type=F32, precision=HIGHEST)
    return jnp.dot(a.astype(BF16), b.astype(BF16), preferred_element_type=F32)


def _dot_nt(a, b):
    return lax.dot_general(a.astype(BF16), b.astype(BF16), (((1,), (1,)), ((), ())),
                           preferred_element_type=F32)


def _rms(x, g):
    return x * lax.rsqrt(jnp.mean(x * x, axis=-1, keepdims=True) + NORM_EPS) * g


def _sigmoid(x):
    return 1.0 / (1.0 + jnp.exp(-x))


def _softplus(x):
    return jnp.maximum(x, 0.0) + jnp.log(1.0 + jnp.exp(-jnp.abs(x)))


def _gelu(x):
    return 0.5 * x * (1.0 + lax.erf(x * (1.0 / math.sqrt(2.0))))


def _const_spec(shape):
    nd = len(shape)
    return pl.BlockSpec(shape, lambda *_: (0,) * nd)


def _params(sem, flags=None):
    return pltpu.CompilerParams(dimension_semantics=sem, vmem_limit_bytes=VMEM_LIMIT, flags=flags)


def _even_front_kernel(h_ref, g_ref, w_ref, o_ref):
    xn = _rms(h_ref[...], g_ref[...])
    o_ref[...] = _dot(xn, w_ref[...])


def _even_front(h, g, w):
    T = h.shape[0]
    tm = ROW_TILE
    return pl.pallas_call(
        _even_front_kernel,
        grid=(T // tm,),
        in_specs=[pl.BlockSpec((tm, D_MODEL), lambda i: (i, 0)),
                  _const_spec((1, D_MODEL)),
                  _const_spec((D_MODEL, PROJ_W))],
        out_specs=pl.BlockSpec((tm, PROJ_W), lambda i: (i, 0)),
        out_shape=jax.ShapeDtypeStruct((T, PROJ_W), F32),
        compiler_params=_params(("parallel",)),
        name="even_front",
    )(h, g, w)


def _s5_kernel(u_ref, bd_ref, cd_ref, are_ref, aim_ref, d_ref, wglu_ref, o_ref, bu_ref, st_ref):
    rows = u_ref.shape[0]
    steps = rows // BATCH
    half = S5_SLAB_STATES

    @pl.when(pl.program_id(0) == 0)
    def _():
        st_ref[...] = jnp.zeros_like(st_ref)

    u = u_ref[...]
    for s in range(S5_SLABS):
        bu_ref[s] = _dot(u[:, s * LANE:(s + 1) * LANE], bd_ref[s])

    def step(t, carry):
        r0 = pl.multiple_of(t * BATCH, BATCH)
        out = []
        for s in range(S5_SLABS):
            sre, sim = carry[2 * s], carry[2 * s + 1]
            ar, ai = are_ref[s], aim_ref[s]
            bre = bu_ref[s, pl.ds(r0, BATCH), 0:half]
            bim = bu_ref[s, pl.ds(r0, BATCH), half:2 * half]
            nre = ar * sre - ai * sim + bre
            nim = ar * sim + ai * sre + bim
            bu_ref[s, pl.ds(r0, BATCH), 0:half] = nre
            bu_ref[s, pl.ds(r0, BATCH), half:2 * half] = nim
            out += [nre, nim]
        return tuple(out)

    init = tuple(st_ref[j] for j in range(2 * S5_SLABS))
    fin = lax.fori_loop(0, steps, step, init)
    for j in range(2 * S5_SLABS):
        st_ref[j] = fin[j]

    y = jnp.concatenate([_dot(bu_ref[s], cd_ref[s]) for s in range(S5_SLABS)], axis=1)
    y = y + d_ref[...] * u
    g = _gelu(y)
    o_ref[...] = g * _sigmoid(_dot(g, wglu_ref[...]))


def _s5(proj, bd, cd, are, aim, d, wglu):
    T = proj.shape[0]
    rows = ROW_TILE
    return pl.pallas_call(
        _s5_kernel,
        grid=(T // rows,),
        in_specs=[pl.BlockSpec((rows, S5_WIDTH), lambda i: (i, 0)),
                  _const_spec(bd.shape), _const_spec(cd.shape),
                  _const_spec(are.shape), _const_spec(aim.shape),
                  _const_spec(d.shape), _const_spec(wglu.shape)],
        out_specs=pl.BlockSpec((rows, S5_WIDTH), lambda i: (i, 0)),
        out_shape=jax.ShapeDtypeStruct((T, S5_WIDTH), F32),
        scratch_shapes=[pltpu.VMEM((S5_SLABS, rows, 2 * S5_SLAB_STATES), F32),
                        pltpu.VMEM((2 * S5_SLABS, BATCH, S5_SLAB_STATES), F32)],
        compiler_params=_params(("arbitrary",)),
        name="s5",
    )(proj, bd, cd, are, aim, d, wglu)


def _gla_kernel(v_ref, r_ref, q_ref, k_ref, gl_ref, wg_ref, bg_ref, lm_ref, ng_ref, o_ref, s_ref):
    rows = q_ref.shape[0]
    width = GLA_HEADS * GLA_DKP

    @pl.when(pl.program_id(0) == 0)
    def _():
        s_ref[...] = jnp.zeros_like(s_ref)

    z = _dot(gl_ref[...], wg_ref[...], True) + bg_ref[...]
    la = -_softplus(-z) * (1.0 / GLA_TAU) * lm_ref[...]
    b = la
    sh = BATCH
    while sh < rows:
        b = b + jnp.concatenate([jnp.zeros((sh, width), F32), b[:rows - sh]], axis=0)
        sh *= 2
    blast = b[rows - BATCH:rows]
    blast_t = jnp.concatenate([blast] * GLA_CHUNK, axis=0)
    q = q_ref[...] * (GLA_DK ** -0.5)
    k = k_ref[...]
    qd = q * jnp.exp(b)
    kd = k * jnp.exp(-b)
    kend = k * jnp.exp(blast_t - b)

    ri = lax.broadcasted_iota(jnp.int32, (rows, rows), 0)
    ci = lax.broadcasted_iota(jnp.int32, (rows, rows), 1)
    causal = jnp.logical_and(((ri - ci) & (BATCH - 1)) == 0, ri >= ci)
    rowb = lax.broadcasted_iota(jnp.int32, (rows, GLA_DKP), 0) & (BATCH - 1)
    laneb = lax.broadcasted_iota(jnp.int32, (GLA_DKP, rows), 1) & (BATCH - 1)

    for h in range(GLA_HEADS):
        sl = slice(h * GLA_DKP, (h + 1) * GLA_DKP)
        vh = v_ref[:, h * GLA_DV:(h + 1) * GLA_DV]
        qh, kdh, keh, bh = qd[:, sl], kd[:, sl], kend[:, sl], b[:, sl]
        att = jnp.where(causal, _dot_nt(qh, kdh), 0.0)
        o = _dot(att, vh)
        qexp = jnp.concatenate([jnp.where(rowb == bb, qh, 0.0) for bb in range(BATCH)], axis=1)
        o = o + _dot(qexp, s_ref[h])
        ket = keh.T
        kexp_t = jnp.concatenate([jnp.where(laneb == bb, ket, 0.0) for bb in range(BATCH)], axis=0)
        kv = _dot(kexp_t, vh)
        bt = bh[rows - GLA_DKP:rows].T
        dec = jnp.concatenate(
            [jnp.broadcast_to(jnp.exp(bt[:, GLA_DKP - BATCH + bb:GLA_DKP - BATCH + bb + 1]),
                              (GLA_DKP, GLA_DV)) for bb in range(BATCH)], axis=0)
        s_ref[h] = s_ref[h] * dec + kv
        o = o * lax.rsqrt(jnp.mean(o * o, axis=-1, keepdims=True) + NORM_EPS)
        o = o * ng_ref[:, h * GLA_DV:(h + 1) * GLA_DV]
        rr = r_ref[:, h * GLA_DV:(h + 1) * GLA_DV]
        o_ref[:, h * GLA_DV:(h + 1) * GLA_DV] = o * (rr * _sigmoid(rr))


def _gla(proj, wg, bg, lm, ng):
    T = proj.shape[0]
    rows = GLA_CHUNK * BATCH
    width = GLA_HEADS * GLA_DKP
    return pl.pallas_call(
        _gla_kernel,
        grid=(T // rows,),
        in_specs=[pl.BlockSpec((rows, GLA_WIDTH), lambda i: (i, 1)),
                  pl.BlockSpec((rows, GLA_WIDTH), lambda i: (i, 2)),
                  pl.BlockSpec((rows, width), lambda i: (i, 3)),
                  pl.BlockSpec((rows, width), lambda i: (i, 4)),
                  pl.BlockSpec((rows, LANE), lambda i: (i, PROJ_GATE_COL // LANE)),
                  _const_spec(wg.shape), _const_spec(bg.shape), _const_spec(lm.shape),
                  _const_spec(ng.shape)],
        out_specs=pl.BlockSpec((rows, GLA_HEADS * GLA_DV), lambda i: (i, 0)),
        out_shape=jax.ShapeDtypeStruct((T, GLA_HEADS * GLA_DV), F32),
        scratch_shapes=[pltpu.VMEM((GLA_HEADS, BATCH * GLA_DKP, GLA_DV), F32)],
        compiler_params=_params(("arbitrary",)),
        name="gla",
    )(proj, proj, proj, proj, proj, wg, bg, lm, ng)


def _mix_out_kernel(n_in, has_gate, *refs):
    h_ref = refs[0]
    y_refs = refs[1:1 + n_in]
    w_refs = refs[1 + n_in:1 + 2 * n_in]
    rest = refs[1 + 2 * n_in:]
    gate_ref = rest[0] if has_gate else None
    g_ref, ho_ref, hb_ref, hbt_ref = rest[1:] if has_gate else rest
    acc = h_ref[...]
    for y_ref, w_ref in zip(y_refs, w_refs):
        if len(y_ref.shape) == 3:
            y = jnp.concatenate([_scan_to_rows(y_ref, t0) for t0 in range(0, y_ref.shape[0], SCAN_BLOCK)],
                                axis=0)
        else:
            y = y_ref[...]
        if has_gate:
            y = y * gate_ref[...]
        acc = acc + _dot(y, w_ref[...])
    ho_ref[...] = acc
    hn = _rms(acc, g_ref[...])
    hb_ref[...] = hn.astype(BF16)
    hbt_ref[...] = hn.T.astype(BF16)


def _mix_out(h, ys, ws, gate, g):
    T = h.shape[0]
    tm = ROW_TILE
    n_in = len(ys)
    row = pl.BlockSpec((tm, D_MODEL), lambda i: (i, 0))
    in_specs = [row]
    in_specs += [pl.BlockSpec((tm, y.shape[1]), lambda i: (i, 0)) if y.ndim == 2 else
                 pl.BlockSpec((tm // BATCH,) + y.shape[1:], lambda i: (i, 0, 0)) for y in ys]
    in_specs += [_const_spec(w.shape) for w in ws]
    args = [h, *ys, *ws]
    if gate is not None:
        in_specs.append(row)
        args.append(gate)
    in_specs.append(_const_spec((1, D_MODEL)))
    args.append(g)
    return pl.pallas_call(
        functools.partial(_mix_out_kernel, n_in, gate is not None),
        grid=(T // tm,),
        in_specs=in_specs,
        out_specs=[row, row, pl.BlockSpec((D_MODEL, tm), lambda i: (0, i))],
        out_shape=[jax.ShapeDtypeStruct((T, D_MODEL), F32),
                   jax.ShapeDtypeStruct((T, D_MODEL), BF16),
                   jax.ShapeDtypeStruct((D_MODEL, T), BF16)],
        compiler_params=_params(("parallel",)),
        name="mix_out",
    )(*args)


def _tree_sum(xs):
    while len(xs) > 1:
        xs = [xs[i] + xs[i + 1] for i in range(0, len(xs) - 1, 2)] + ([xs[-1]] if len(xs) % 2 else [])
    return xs[0]


def _insert(tops, x):
    out = []
    for t in tops:
        out.append(jnp.maximum(t, x))
        x = jnp.minimum(t, x)
    return out


def _compare_exchange(v, i, l, descending):
    hi, lo = jnp.maximum(v[i], v[l]), jnp.minimum(v[i], v[l])
    v[i], v[l] = (hi, lo) if descending else (lo, hi)


def _bitonic_merge_desc(v):
    v = list(v)
    j = len(v) // 2
    while j >= 1:
        for i in range(len(v)):
            if i ^ j > i:
                _compare_exchange(v, i, i ^ j, True)
        j //= 2
    return v


def _bitonic_sort_desc(v):
    v = list(v)
    k = 2
    while k <= len(v):
        j = k // 2
        while j >= 1:
            for i in range(len(v)):
                if i ^ j > i:
                    _compare_exchange(v, i, i ^ j, (i & k) == 0)
            j //= 2
        k *= 2
    return v


def _top_values(scr_ref):
    tops = None
    for g in range(PEER_NKEYS // PEER_TOPK):
        grp = _bitonic_sort_desc([scr_ref[pl.ds(g * PEER_TOPK + i, SUBLANE, stride=CHUNK_PITCH), :]
                                  for i in range(PEER_TOPK)])
        if tops is None:
            tops = grp
        else:
            tops = _bitonic_merge_desc([jnp.maximum(tops[i], grp[PEER_TOPK - 1 - i])
                                        for i in range(PEER_TOPK)])
    return tops


def _pick(vals, bits):
    if not bits:
        return vals[0]
    half = len(vals) // 2
    return jnp.where(bits[0], _pick(vals[half:], bits[1:]), _pick(vals[:half], bits[1:]))


def _count_greater(x, s):
    bits = []
    step = PEER_TOPK // 2
    while step >= 1:
        vals = [s[base + step - 1] for base in range(0, PEER_TOPK, 2 * step)]
        bits.append(_pick(vals, bits) > x)
        step //= 2
    return bits, s[PEER_TOPK - 1] > x


def _peer_topk_kernel(hn_ref, wq_ref, keys_ref, n1_ref, e1_ref, r2_ref, e2_ref,
                      scr1, scr2, stat_ref):
    nch = hn_ref.shape[0] // LANE
    q = _dot(hn_ref[...], wq_ref[...])
    for c, scr in enumerate((scr1, scr2)):
        st = _dot_nt(keys_ref[0, c], q[:, c * LANE:(c + 1) * LANE])
        for j in range(nch):
            scr[j * CHUNK_PITCH:j * CHUNK_PITCH + PEER_NKEYS, :] = st[:, j * LANE:(j + 1) * LANE]
    a = _top_values(scr1)
    b = _top_values(scr2)
    pairs = [(i, j) for i in range(PEER_TOPK) for j in range(PEER_TOPK) if (i + 1) * (j + 1) <= PEER_TOPK]
    tops = [jnp.full((SUBLANE, LANE), -jnp.inf, F32) for _ in range(PEER_TOPK)]
    for i, j in pairs:
        tops = _insert(tops, a[i] + b[j])
    thr = tops[PEER_TOPK - 1]
    z = jnp.zeros((SUBLANE, LANE), F32)
    for t in tops:
        z = z + jnp.exp(t - tops[0])
    inv_z = 1.0 / z
    cnt = [jnp.zeros((SUBLANE, LANE), F32) for _ in range(PEER_TOPK)]
    for i, j in pairs:
        cnt[i] = cnt[i] + jnp.where(a[i] + b[j] >= thr, GATE_STEP, 0.0)

    for i in range(PEER_TOPK):
        stat_ref[i] = a[i]
        stat_ref[PEER_TOPK + i] = b[i]
        stat_ref[2 * PEER_TOPK + i] = cnt[i]
    stat_ref[3 * PEER_TOPK] = inv_z
    weights = [GATE_STEP * (PEER_TOPK >> (i + 1)) for i in range(4)] + [GATE_STEP]

    def chunk(c, carry):
        def stat(i):
            return jnp.broadcast_to(stat_ref[i, pl.ds(c, 1), :], (SUBLANE, LANE))
        a_c = [stat(i) for i in range(PEER_TOPK)]
        b_c = [stat(PEER_TOPK + i) for i in range(PEER_TOPK)]
        cnt_c = [stat(2 * PEER_TOPK + i) for i in range(PEER_TOPK)]
        inv_z_c = stat(3 * PEER_TOPK)
        for g in range(PEER_NKEYS // SUBLANE):
            rows = pl.ds(pl.multiple_of(c * CHUNK_PITCH, SUBLANE) + g * SUBLANE, SUBLANE)
            x = scr1[rows, :]
            bits, below = _count_greater(x, a_c)
            n1_ref[0, c, g * SUBLANE:(g + 1) * SUBLANE, :] = jnp.where(below, 0.0, _pick(cnt_c, bits))
            e1_ref[0, c, g * SUBLANE:(g + 1) * SUBLANE, :] = jnp.exp(x - a_c[0]) * inv_z_c
            x = scr2[rows, :]
            bits, below = _count_greater(x, b_c)
            rank = _tree_sum([jnp.where(m, w, 0.0) for m, w in zip(bits + [below], weights)])
            scr1[rows, :] = rank
            scr2[rows, :] = jnp.exp(x - b_c[0])
        return carry

    lax.fori_loop(0, nch, chunk, 0)
    for j in range(nch):
        rows = slice(j * CHUNK_PITCH, j * CHUNK_PITCH + PEER_NKEYS)
        r2_ref[0, j] = scr1[rows, :].astype(BF16)
        e2_ref[0, j] = scr2[rows, :].astype(BF16)


def _peer_topk(hn, wq, keys):
    T = hn.shape[0]
    tt = SUBLANE * LANE
    nch = T // LANE
    shape = (PEER_HEADS, nch, PEER_NKEYS, LANE)
    spec = pl.BlockSpec((1, SUBLANE, PEER_NKEYS, LANE), lambda i, h: (h, i, 0, 0))
    return pl.pallas_call(
        _peer_topk_kernel,
        grid=(T // tt, PEER_HEADS),
        in_specs=[pl.BlockSpec((tt, D_MODEL), lambda i, h: (i, 0)),
                  pl.BlockSpec((D_MODEL, 2 * LANE), lambda i, h: (0, h)),
                  pl.BlockSpec((1, 2, PEER_NKEYS, LANE), lambda i, h: (h, 0, 0, 0))],
        out_specs=[spec, spec, spec, spec],
        out_shape=[jax.ShapeDtypeStruct(shape, F32), jax.ShapeDtypeStruct(shape, F32),
                   jax.ShapeDtypeStruct(shape, BF16), jax.ShapeDtypeStruct(shape, BF16)],
        scratch_shapes=[pltpu.VMEM((SUBLANE * CHUNK_PITCH, LANE), F32),
                        pltpu.VMEM((SUBLANE * CHUNK_PITCH, LANE), F32),
                        pltpu.VMEM((3 * PEER_TOPK + 1, SUBLANE, LANE), F32)],
        compiler_params=_params(("parallel", "arbitrary")),
        name="peer_topk",
    )(hn, wq, keys)


def _row_bf16(ref, h, c, i1, dep=None):
    packed_rows = 2 * SUBLANE
    row = ref[h, c, pl.ds(i1, 1), :]
    if dep is not None:
        row = row + dep
    x = jnp.broadcast_to(row, (packed_rows, LANE)).astype(BF16)
    return jnp.concatenate([x] * (PEER_NKEYS // packed_rows), axis=0)


def _zero_after(x):
    bits = pltpu.bitcast(x[0:2 * SUBLANE, :], jnp.uint32)
    sixteen = jnp.uint32(16)
    zero_bits = lax.shift_right_logical(lax.shift_right_logical(bits, sixteen), sixteen)
    return pltpu.bitcast(zero_bits, F32)[0:1, :]


def _peer_main_kernel(final, hbt_ref, hres_ref, u0_ref, ub_ref, ua_ref, vta_ref, vtb_ref,
                      n1_ref, e1_ref, r2_ref, e2_ref, fg_ref, o_ref,
                      acc_ref, hga_ref, hgb_ref, hta_ref, htb_ref):
    e = pl.program_id(1)
    n_pairs = pl.num_programs(1) - 1
    tt = hbt_ref.shape[1]
    n_i1 = PEER_TILE // PEER_NKEYS
    n_ch = tt // LANE
    zero = jnp.zeros((PEER_NKEYS, LANE), BF16)

    @pl.when(e == 0)
    def _():
        acc_ref[...] = jnp.zeros_like(acc_ref)
        hgb_ref[...] = jnp.zeros_like(hgb_ref)
        hta_ref[...] = _dot(u0_ref[...], hbt_ref[...])

    def gates(tile, ht_ref, hg_ref, c, dep):
        for il in range(n_i1):
            i1 = tile * n_i1 + il
            gate = None
            for h in range(PEER_HEADS):
                code = _row_bf16(n1_ref, h, c, i1, dep if h == 0 else None) - r2_ref[h, c]
                w = _row_bf16(e1_ref, h, c, i1) * e2_ref[h, c]
                term = jnp.maximum(jnp.minimum(w, code), zero)
                gate = term if gate is None else gate + term
            rows = slice(il * PEER_NKEYS, (il + 1) * PEER_NKEYS)
            cols = slice(c * LANE, (c + 1) * LANE)
            hg = _gelu(ht_ref[rows, cols]).astype(BF16) * gate
            hg_ref[rows, cols] = hg
            dep = _zero_after(hg)
        return dep

    def phase(tile, ht_ref, hg_ref, vt_prev_ref, hg_prev_ref, u_next_ref, ht_next_ref, dep):
        half = tt // 2
        for piece in range(4):
            cols = slice((piece % 2) * half, (piece % 2 + 1) * half)
            if piece < 2:
                acc_ref[:, cols] += _dot(vt_prev_ref[...], hg_prev_ref[:, cols])
            else:
                ht_next_ref[:, cols] = _dot(u_next_ref[...], hbt_ref[:, cols])
            for c in range(piece * n_ch // 4, (piece + 1) * n_ch // 4):
                dep = gates(tile, ht_ref, hg_ref, c, dep)
        return dep

    @pl.when(e < n_pairs)
    def _():
        dep = phase(2 * e, hta_ref, hga_ref, vtb_ref, hgb_ref, ub_ref, htb_ref, None)
        phase(2 * e + 1, htb_ref, hgb_ref, vta_ref, hga_ref, ua_ref, hta_ref, dep)

    @pl.when(e == n_pairs)
    def _():
        acc = acc_ref[...] + _dot(vtb_ref[...], hgb_ref[...])
        out = hres_ref[...] + acc.T
        if final:
            out = _rms(out, fg_ref[...])
        o_ref[...] = out


def _peer_main(hbt, hres, u, vt, n1, e1, r2, e2, fg, final):
    T = hres.shape[0]
    E = u.shape[0]
    tt = PEER_TOKENS
    n_tiles = E // PEER_TILE
    n_pairs = n_tiles // 2
    big_spec = pl.BlockSpec((PEER_HEADS, tt // LANE, PEER_NKEYS, LANE), lambda i, e: (0, i, 0, 0))
    tile_of = lambda f: (lambda i, e: (jnp.clip(f(e), 0, n_tiles - 1), 0, 0))
    vt_spec = lambda f: pl.BlockSpec((D_MODEL, PEER_TILE), lambda i, e: tile_of(f)(i, e)[:2])
    u_spec = lambda f: pl.BlockSpec((PEER_TILE, D_MODEL), lambda i, e: tile_of(f)(i, e)[:2])
    return pl.pallas_call(
        functools.partial(_peer_main_kernel, final),
        grid=(T // tt, n_pairs + 1),
        in_specs=[pl.BlockSpec((D_MODEL, tt), lambda i, e: (0, i)),
                  pl.BlockSpec((tt, D_MODEL), lambda i, e: (i, 0)),
                  u_spec(lambda e: 0), u_spec(lambda e: 2 * e + 1), u_spec(lambda e: 2 * e + 2),
                  vt_spec(lambda e: 2 * e), vt_spec(lambda e: 2 * e - 1),
                  big_spec, big_spec, big_spec, big_spec,
                  _const_spec((1, D_MODEL))],
        out_specs=pl.BlockSpec((tt, D_MODEL), lambda i, e: (i, 0)),
        out_shape=jax.ShapeDtypeStruct((T, D_MODEL), F32),
        scratch_shapes=[pltpu.VMEM((D_MODEL, tt), F32),
                        pltpu.VMEM((PEER_TILE, tt), BF16), pltpu.VMEM((PEER_TILE, tt), BF16),
                        pltpu.VMEM((PEER_TILE, tt), F32), pltpu.VMEM((PEER_TILE, tt), F32)],
        compiler_params=_params(("parallel", "arbitrary")),
        name="peer_main",
    )(hbt, hres, u, u, u, vt, vt, n1, e1, r2, e2, fg)


def _peer_layer(h, hb, hbt, wq, keys, u_tab, v_tab, fg, final):
    n1, e1, r2, e2 = _peer_topk(hb, wq.astype(BF16), keys.astype(BF16))
    vt = v_tab.astype(BF16).reshape(-1, PEER_TILE, D_MODEL).transpose(0, 2, 1).reshape(-1, PEER_TILE)
    return _peer_main(hbt, h, u_tab.astype(BF16), vt, n1, e1, r2, e2, fg, final)


SCAN_BLOCK = LANE // BATCH


def _swap_groups(slabs):
    a = list(slabs)
    group = lax.broadcasted_iota(jnp.int32, a[0].shape, 1) // BATCH
    s = SCAN_BLOCK // 2
    while s >= 1:
        upper = (group & s) != 0
        for i in range(SCAN_BLOCK):
            if i & s == 0:
                ai, aj = a[i], a[i + s]
                a[i] = jnp.where(upper, pltpu.roll(aj, s * BATCH, axis=1), ai)
                a[i + s] = jnp.where(upper, aj, pltpu.roll(ai, LANE - s * BATCH, axis=1))
        s //= 2
    return a


def _rows_to_scan(x, ref):
    block_rows = SCAN_BLOCK * BATCH
    for r0 in range(0, x.shape[0], block_rows):
        z = x[r0:r0 + block_rows].T
        slabs = [z[h * RWKV_HEAD:(h + 1) * RWKV_HEAD, :] for h in range(RWKV_HEADS)]
        for t, blk in enumerate(_swap_groups(slabs)):
            ref[r0 // BATCH + t] = blk


def _scan_to_rows(ref, t0):
    slabs = _swap_groups([ref[t0 + t] for t in range(SCAN_BLOCK)])
    return jnp.concatenate(slabs, axis=0).T


def _rwkv_front_kernel(h_ref, hp_ref, g_ref, mu_ref, wr_ref, wk_ref, wv_ref, w0_ref, ww1_ref, ww2_ref,
                       a0_ref, wa1_ref, wa2_ref, wg1_ref, wg2_ref,
                       r_ref, k_ref, v_ref, w_ref, a_ref, go_ref):
    g = g_ref[...]
    x = _rms(h_ref[...], g)
    xp8 = _rms(hp_ref[...], g)
    xp8 = jnp.where(pl.program_id(0) == 0, 0.0, xp8)
    rows = x.shape[0]
    xprev = jnp.concatenate([xp8, x[:rows - BATCH]], axis=0)
    xx = xprev - x
    mix = lambda j: x + xx * mu_ref[j:j + 1, :]
    _rows_to_scan(_dot(mix(0), wr_ref[...]), r_ref)
    _rows_to_scan(w0_ref[...] + _dot(jnp.tanh(_dot(mix(1), ww1_ref[...])), ww2_ref[...]), w_ref)
    _rows_to_scan(_dot(mix(2), wk_ref[...]), k_ref)
    _rows_to_scan(_dot(mix(3), wv_ref[...]), v_ref)
    _rows_to_scan(_sigmoid(a0_ref[...] + _dot(_dot(mix(4), wa1_ref[...]), wa2_ref[...])), a_ref)
    go_ref[...] = _dot(_sigmoid(_dot(mix(5), wg1_ref[...])), wg2_ref[...])


def _rwkv_front(h, g, mu, wr, wk, wv, w0, ww1, ww2, a0, wa1, wa2, wg1, wg2):
    assert RWKV_HEADS == SCAN_BLOCK
    T = h.shape[0]
    tm = 2 * SCAN_BLOCK * BATCH
    row = pl.BlockSpec((tm, D_MODEL), lambda i: (i, 0))
    prev = pl.BlockSpec((BATCH, D_MODEL), lambda i: (jnp.maximum(i * (tm // BATCH) - 1, 0), 0))
    scan = pl.BlockSpec((tm // BATCH, RWKV_HEAD, LANE), lambda i: (i, 0, 0))
    consts = [g, mu, wr, wk, wv, w0, ww1, ww2, a0, wa1, wa2, wg1, wg2]
    scan_out = jax.ShapeDtypeStruct((T // BATCH, RWKV_HEAD, LANE), F32)
    return pl.pallas_call(
        _rwkv_front_kernel,
        grid=(T // tm,),
        in_specs=[row, prev] + [_const_spec(c.shape) for c in consts],
        out_specs=[scan] * 5 + [row],
        out_shape=[scan_out] * 5 + [jax.ShapeDtypeStruct((T, D_MODEL), F32)],
        compiler_params=_params(("parallel",)),
        name="rwkv_front",
    )(h, h, *consts)


def _rwkv_scan_kernel(r_ref, k_ref, v_ref, w_ref, a_ref, kk_ref, ka_ref, rk_ref, lg_ref, lb_ref,
                      y_ref, s_ref, pend_s, k2_s, av_s, bv_s, wr_s, br_s, kr_s):
    tl = r_ref.shape[0]
    n = RWKV_HEAD
    ng = n // SUBLANE

    @pl.when(pl.program_id(0) == 0)
    def _():
        s_ref[...] = jnp.zeros_like(s_ref)

    r = r_ref[...]
    k = k_ref[...]
    a = a_ref[...]
    kk = k * kk_ref[...]
    kk = kk / jnp.maximum(jnp.sqrt(jnp.sum(kk * kk, axis=1, keepdims=True)), 1e-12)
    k2 = k * (1.0 + (a - 1.0) * ka_ref[...])
    dec = jnp.exp(-jnp.exp(-_softplus(-w_ref[...]) - 0.5))
    bv = kk * a
    p = dec
    sh = 1
    while sh < tl:
        p = p * jnp.concatenate([jnp.ones((sh, n, LANE), F32), p[:tl - sh]], axis=0)
        sh *= 2
    p_prev = jnp.concatenate([jnp.ones((1, n, LANE), F32), p[:tl - 1]], axis=0)
    inv_p = 1.0 / p
    pend_s[...] = p[tl - 1]
    k2_s[...] = k2 * inv_p
    av_s[...] = -kk * p_prev
    bv_s[...] = bv * inv_p
    wr_s[...] = r * p
    br_s[...] = jnp.sum(bv * r, axis=1, keepdims=True)
    kr_s[...] = jnp.sum(k2 * r, axis=1, keepdims=True)

    def row(ref, t, j):
        return jnp.broadcast_to(ref[t, pl.ds(j, 1), :], (SUBLANE, LANE))

    def step(t, carry):
        def reduce_body(j, accs):
            sa, sy = accs
            a_b = row(av_s, t, j)
            wr_b = row(wr_s, t, j)
            sa_n, sy_n = [], []
            for vg in range(ng):
                sv = s_ref[j, vg * SUBLANE:(vg + 1) * SUBLANE, :]
                sa_n.append(sa[vg] + sv * a_b)
                sy_n.append(sy[vg] + sv * wr_b)
            return tuple(sa_n), tuple(sy_n)

        zeros = tuple(jnp.zeros((SUBLANE, LANE), F32) for _ in range(ng))
        sa, sy = lax.fori_loop(0, n, reduce_body, (zeros, zeros), unroll=8)
        br = jnp.broadcast_to(br_s[t], (SUBLANE, LANE))
        kr = jnp.broadcast_to(kr_s[t], (SUBLANE, LANE))
        vv = [v_ref[t, vg * SUBLANE:(vg + 1) * SUBLANE, :] for vg in range(ng)]
        for vg in range(ng):
            y_ref[t, vg * SUBLANE:(vg + 1) * SUBLANE, :] = sy[vg] + sa[vg] * br + vv[vg] * kr

        def update_body(j, c):
            b_b = row(bv_s, t, j)
            k_b = row(k2_s, t, j)
            for vg in range(ng):
                sl = slice(vg * SUBLANE, (vg + 1) * SUBLANE)
                s_ref[j, sl, :] = s_ref[j, sl, :] + sa[vg] * b_b + vv[vg] * k_b
            return c

        lax.fori_loop(0, n, update_body, 0, unroll=8)
        return carry

    lax.fori_loop(0, tl, step, 0)

    def rescale(j, c):
        s_ref[j] = s_ref[j] * pend_s[pl.ds(j, 1), :]
        return c

    lax.fori_loop(0, n, rescale, 0, unroll=8)

    y = y_ref[...]
    mean = jnp.mean(y, axis=1, keepdims=True)
    var = jnp.mean((y - mean) ** 2, axis=1, keepdims=True)
    y = (y - mean) * lax.rsqrt(var + RWKV_GN_EPS) * lg_ref[...] + lb_ref[...]
    bonus = jnp.sum(r * k2 * rk_ref[...], axis=1, keepdims=True) * v_ref[...]
    y_ref[...] = y + bonus


def _rwkv_scan(r, k, v, w, a, kk, ka, rk, lg, lb):
    L = r.shape[0]
    tl = SCAN_STEPS
    n = RWKV_HEAD
    blk = pl.BlockSpec((tl, n, LANE), lambda i: (i, 0, 0))
    par = _const_spec((n, LANE))
    big = pltpu.VMEM((tl, n, LANE), F32)
    small = pltpu.VMEM((tl, 1, LANE), F32)
    return pl.pallas_call(
        _rwkv_scan_kernel,
        grid=(L // tl,),
        in_specs=[blk] * 5 + [par] * 5,
        out_specs=blk,
        out_shape=jax.ShapeDtypeStruct((L, n, LANE), F32),
        scratch_shapes=[pltpu.VMEM((n, n, LANE), F32), pltpu.VMEM((n, LANE), F32), big, big, big, big,
                        small, small],
        compiler_params=_params(("arbitrary",)),
        name="rwkv_scan",
    )(r, k, v, w, a, kk, ka, rk, lg, lb)


def _head_param(p):
    return jnp.repeat(p.reshape(RWKV_HEADS, RWKV_HEAD).T, BATCH, axis=1)


def _prep_even(w_in, a_re, a_im, log_dt, b_re, b_im, c_re, c_im, w_g2, b_g2):
    assert S5_WIDTH == GLA_WIDTH == GLA_QK
    qk = GLA_HEADS * GLA_DK
    widths = (S5_WIDTH, qk, qk, GLA_WIDTH, GLA_RANK, GLA_WIDTH)
    edges = [sum(widths[:j]) for j in range(len(widths) + 1)]
    w_u, w_q, w_k, w_v, w_gl, w_r = (w_in[:, edges[j]:edges[j + 1]] for j in range(len(widths)))

    def pad_heads(w):
        w = w.reshape(-1, GLA_HEADS, GLA_DK)
        w = jnp.pad(w, ((0, 0), (0, 0), (0, GLA_DKP - GLA_DK)))
        return w.reshape(-1, GLA_HEADS * GLA_DKP)

    w_proj = jnp.concatenate([w_u, w_v, w_r, pad_heads(w_q), pad_heads(w_k),
                              jnp.pad(w_gl, ((0, 0), (0, LANE - GLA_RANK)))], axis=1)
    wg = jnp.pad(pad_heads(w_g2), ((0, LANE - GLA_RANK), (0, 0)))
    bg = pad_heads(b_g2[None, :])
    lm = pad_heads(jnp.ones((1, GLA_HEADS * GLA_DK), F32))

    dt = jnp.exp(log_dt)
    mag = jnp.exp(dt * a_re)
    abar_re = mag * jnp.cos(dt * a_im)
    abar_im = mag * jnp.sin(dt * a_im)
    nr = abar_re - 1.0
    den = a_re * a_re + a_im * a_im
    f_re = (nr * a_re + abar_im * a_im) / den
    f_im = (abar_im * a_re - nr * a_im) / den
    bbar_re = f_re[..., None] * b_re - f_im[..., None] * b_im
    bbar_im = f_re[..., None] * b_im + f_im[..., None] * b_re
    eye = jnp.eye(8, dtype=F32)
    gs = S5_GROUPS // S5_SLABS

    def bd(bb):
        bb = bb.reshape(S5_SLABS, gs, S5_STATE, S5_GROUP)
        return jnp.einsum('sgpc,gh->sgchp', bb, eye).reshape(S5_SLABS, gs * S5_GROUP, gs * S5_STATE)

    def cdm(cc):
        cc = cc.reshape(S5_SLABS, gs, S5_GROUP, S5_STATE)
        return jnp.einsum('sgcp,gh->sgphc', cc, eye).reshape(S5_SLABS, gs * S5_STATE, gs * S5_GROUP)

    bd_all = jnp.concatenate([bd(bbar_re), bd(bbar_im)], axis=2)
    cd_all = jnp.concatenate([cdm(c_re), -cdm(c_im)], axis=1)
    are = jnp.broadcast_to(abar_re.reshape(S5_SLABS, 1, gs * S5_STATE), (S5_SLABS, BATCH, gs * S5_STATE))
    aim = jnp.broadcast_to(abar_im.reshape(S5_SLABS, 1, gs * S5_STATE), (S5_SLABS, BATCH, gs * S5_STATE))
    return w_proj, wg, bg, lm, bd_all, cd_all, are, aim


def kernel(x, norm_mix_g, norm_ffn_g, final_g, e_w_in, e_w_out, s5_a_re, s5_a_im, s5_log_dt, s5_b_re, s5_b_im, s5_c_re, s5_c_im, s5_d, s5_w_glu, gla_w_g2, gla_b_g2, gla_norm_g, o_mu, o_w_r, o_w_k, o_w_v, o_w0, o_w_w1, o_w_w2, o_a0, o_w_a1, o_w_a2, o_w_g1, o_w_g2, o_k_k, o_k_a, o_r_k, o_lnx_g, o_lnx_b, o_w_o, peer_w_q, peer_sub_keys, peer_u, peer_v):
    bsz, L, D = x.shape
    assert bsz == BATCH and D == D_MODEL
    T = bsz * L
    row = lambda p: p.reshape(1, -1)
    h = x.transpose(1, 0, 2).reshape(T, D)

    w_proj, wg, bg, lm, bd_all, cd_all, are, aim = _prep_even(
        e_w_in[0], s5_a_re[0], s5_a_im[0], s5_log_dt[0], s5_b_re[0], s5_b_im[0],
        s5_c_re[0], s5_c_im[0], gla_w_g2[0], gla_b_g2[0])
    bf = lambda p: p.astype(BF16)
    proj = _even_front(h, row(norm_mix_g[0]), bf(w_proj))
    y_s5 = _s5(proj, bf(bd_all), bf(cd_all), are, aim, row(s5_d[0]), bf(s5_w_glu[0]))
    y_gla = _gla(proj, wg, bg, lm, row(gla_norm_g[0]))
    w_out = bf(e_w_out[0])
    h, hb, hbt = _mix_out(h, [y_s5, y_gla], [w_out[:S5_WIDTH], w_out[S5_WIDTH:]], None, row(norm_ffn_g[0]))
    h = _peer_layer(h, hb, hbt, peer_w_q[0], peer_sub_keys[0], peer_u[0], peer_v[0], row(final_g), False)

    r, k, v, w, a, g = _rwkv_front(
        h, row(norm_mix_g[1]), o_mu[0], bf(o_w_r[0]), bf(o_w_k[0]), bf(o_w_v[0]), row(o_w0[0]),
        bf(o_w_w1[0]), bf(o_w_w2[0]), row(o_a0[0]), bf(o_w_a1[0]), bf(o_w_a2[0]), bf(o_w_g1[0]), bf(o_w_g2[0]))
    y = _rwkv_scan(r, k, v, w, a, _head_param(o_k_k[0]), _head_param(o_k_a[0]),
                   _head_param(o_r_k[0].reshape(-1)), _head_param(o_lnx_g[0]), _head_param(o_lnx_b[0]))
    h, hb, hbt = _mix_out(h, [y], [bf(o_w_o[0])], g, row(norm_ffn_g[1]))
    out = _peer_layer(h, hb, hbt, peer_w_q[1], peer_sub_keys[1], peer_u[1], peer_v[1], row(final_g), True)
    return out.reshape(L, bsz, D).transpose(1, 0, 2)
```

```python
import functools
import math

import jax
import jax.numpy as jnp
from jax import lax
from jax.experimental import pallas as pl
from jax.experimental.pallas import tpu as pltpu

F32 = jnp.float32
BF16 = jnp.bfloat16
HIGHEST = lax.Precision.HIGHEST

D_MODEL = 1024
BATCH = 8
NORM_EPS = 1e-6
S5_WIDTH = 512
S5_GROUP = 16
S5_GROUPS = 32
S5_STATE = 64
S5_SLABS = 4
S5_SLAB_STATES = S5_GROUPS // S5_SLABS * S5_STATE
GLA_HEADS = 4
GLA_DV = 128
GLA_DK = 64
GLA_DKP = 128
GLA_RANK = 16
GLA_TAU = 16.0
GLA_CHUNK = 64
RWKV_HEAD = 64
RWKV_HEADS = 16
RWKV_GN_EPS = 64e-5
PEER_HEADS = 8
PEER_NKEYS = 128
PEER_TOPK = 16
PEER_TILE = 1024
PEER_TOKENS = 512
GATE_STEP = 2.0
LANE = 128
SUBLANE = 8
VMEM_LIMIT = 56 * 1024 * 1024
CHUNK_PITCH = PEER_NKEYS + SUBLANE
E2_ROWS = PEER_NKEYS + 2 * SUBLANE
ROW_TILE = 256
SCAN_STEPS = 32

GLA_WIDTH = GLA_HEADS * GLA_DV
GLA_QK = GLA_HEADS * GLA_DKP
PROJ_GATE_COL = S5_WIDTH + 2 * GLA_WIDTH + 2 * GLA_QK
PROJ_W = PROJ_GATE_COL + LANE


def _dot(a, b, hi=False):
    if hi:
        return jnp.dot(a, b, preferred_element_type=F32, precision=HIGHEST)
    return jnp.dot(a.astype(BF16), b.astype(BF16), preferred_element_type=F32)


def _dot_nt(a, b):
    return lax.dot_general(a.astype(BF16), b.astype(BF16), (((1,), (1,)), ((), ())),
                           preferred_element_type=F32)


def _rms(x, g):
    return x * lax.rsqrt(jnp.mean(x * x, axis=-1, keepdims=True) + NORM_EPS) * g


def _sigmoid(x):
    return 1.0 / (1.0 + jnp.exp(-x))


def _softplus(x):
    return jnp.maximum(x, 0.0) + jnp.log(1.0 + jnp.exp(-jnp.abs(x)))


def _gelu(x):
    return 0.5 * x * (1.0 + lax.erf(x * (1.0 / math.sqrt(2.0))))


def _const_spec(shape):
    nd = len(shape)
    return pl.BlockSpec(shape, lambda *_: (0,) * nd)


def _params(sem, flags=None):
    return pltpu.CompilerParams(dimension_semantics=sem, vmem_limit_bytes=VMEM_LIMIT, flags=flags)


def _even_front_kernel(h_ref, g_ref, w_ref, o_ref):
    xn = _rms(h_ref[...], g_ref[...])
    o_ref[...] = _dot(xn, w_ref[...])


def _even_front(h, g, w):
    T = h.shape[0]
    tm = ROW_TILE
    return pl.pallas_call(
        _even_front_kernel,
        grid=(T // tm,),
        in_specs=[pl.BlockSpec((tm, D_MODEL), lambda i: (i, 0)),
                  _const_spec((1, D_MODEL)),
                  _const_spec((D_MODEL, PROJ_W))],
        out_specs=pl.BlockSpec((tm, PROJ_W), lambda i: (i, 0)),
        out_shape=jax.ShapeDtypeStruct((T, PROJ_W), F32),
        compiler_params=_params(("parallel",)),
        name="even_front",
    )(h, g, w)


def _s5_kernel(u_ref, bd_ref, cd_ref, are_ref, aim_ref, d_ref, wglu_ref, o_ref, bu_ref, st_ref):
    rows = u_ref.shape[0]
    steps = rows // BATCH
    half = S5_SLAB_STATES

    @pl.when(pl.program_id(0) == 0)
    def _():
        st_ref[...] = jnp.zeros_like(st_ref)

    u = u_ref[...]
    for s in range(S5_SLABS):
        bu_ref[s] = _dot(u[:, s * LANE:(s + 1) * LANE], bd_ref[s])

    def step(t, carry):
        r0 = pl.multiple_of(t * BATCH, BATCH)
        out = []
        for s in range(S5_SLABS):
            sre, sim = carry[2 * s], carry[2 * s + 1]
            ar, ai = are_ref[s], aim_ref[s]
            bre = bu_ref[s, pl.ds(r0, BATCH), 0:half]
            bim = bu_ref[s, pl.ds(r0, BATCH), half:2 * half]
            nre = ar * sre - ai * sim + bre
            nim = ar * sim + ai * sre + bim
            bu_ref[s, pl.ds(r0, BATCH), 0:half] = nre
            bu_ref[s, pl.ds(r0, BATCH), half:2 * half] = nim
            out += [nre, nim]
        return tuple(out)

    init = tuple(st_ref[j] for j in range(2 * S5_SLABS))
    fin = lax.fori_loop(0, steps, step, init)
    for j in range(2 * S5_SLABS):
        st_ref[j] = fin[j]

    y = jnp.concatenate([_dot(bu_ref[s], cd_ref[s]) for s in range(S5_SLABS)], axis=1)
    y = y + d_ref[...] * u
    g = _gelu(y)
    o_ref[...] = g * _sigmoid(_dot(g, wglu_ref[...]))


def _s5(proj, bd, cd, are, aim, d, wglu):
    T = proj.shape[0]
    rows = ROW_TILE
    return pl.pallas_call(
        _s5_kernel,
        grid=(T // rows,),
        in_specs=[pl.BlockSpec((rows, S5_WIDTH), lambda i: (i, 0)),
                  _const_spec(bd.shape), _const_spec(cd.shape),
                  _const_spec(are.shape), _const_spec(aim.shape),
                  _const_spec(d.shape), _const_spec(wglu.shape)],
        out_specs=pl.BlockSpec((rows, S5_WIDTH), lambda i: (i, 0)),
        out_shape=jax.ShapeDtypeStruct((T, S5_WIDTH), F32),
        scratch_shapes=[pltpu.VMEM((S5_SLABS, rows, 2 * S5_SLAB_STATES), F32),
                        pltpu.VMEM((2 * S5_SLABS, BATCH, S5_SLAB_STATES), F32)],
        compiler_params=_params(("arbitrary",)),
        name="s5",
    )(proj, bd, cd, are, aim, d, wglu)


def _gla_kernel(v_ref, r_ref, q_ref, k_ref, gl_ref, wg_ref, bg_ref, lm_ref, ng_ref, o_ref, s_ref):
    rows = q_ref.shape[0]
    width = GLA_HEADS * GLA_DKP

    @pl.when(pl.program_id(0) == 0)
    def _():
        s_ref[...] = jnp.zeros_like(s_ref)

    z = _dot(gl_ref[...], wg_ref[...], True) + bg_ref[...]
    la = -_softplus(-z) * (1.0 / GLA_TAU) * lm_ref[...]
    b = la
    sh = BATCH
    while sh < rows:
        b = b + jnp.concatenate([jnp.zeros((sh, width), F32), b[:rows - sh]], axis=0)
        sh *= 2
    blast = b[rows - BATCH:rows]
    blast_t = jnp.concatenate([blast] * GLA_CHUNK, axis=0)
    q = q_ref[...] * (GLA_DK ** -0.5)
    k = k_ref[...]
    qd = q * jnp.exp(b)
    kd = k * jnp.exp(-b)
    kend = k * jnp.exp(blast_t - b)

    ri = lax.broadcasted_iota(jnp.int32, (rows, rows), 0)
    ci = lax.broadcasted_iota(jnp.int32, (rows, rows), 1)
    causal = jnp.logical_and(((ri - ci) & (BATCH - 1)) == 0, ri >= ci)
    rowb = lax.broadcasted_iota(jnp.int32, (rows, GLA_DKP), 0) & (BATCH - 1)
    laneb = lax.broadcasted_iota(jnp.int32, (GLA_DKP, rows), 1) & (BATCH - 1)

    for h in range(GLA_HEADS):
        sl = slice(h * GLA_DKP, (h + 1) * GLA_DKP)
        vh = v_ref[:, h * GLA_DV:(h + 1) * GLA_DV]
        qh, kdh, keh, bh = qd[:, sl], kd[:, sl], kend[:, sl], b[:, sl]
        att = jnp.where(causal, _dot_nt(qh, kdh), 0.0)
        o = _dot(att, vh)
        qexp = jnp.concatenate([jnp.where(rowb == bb, qh, 0.0) for bb in range(BATCH)], axis=1)
        o = o + _dot(qexp, s_ref[h])
        ket = keh.T
        kexp_t = jnp.concatenate([jnp.where(laneb == bb, ket, 0.0) for bb in range(BATCH)], axis=0)
        kv = _dot(kexp_t, vh)
        bt = bh[rows - GLA_DKP:rows].T
        dec = jnp.concatenate(
            [jnp.broadcast_to(jnp.exp(bt[:, GLA_DKP - BATCH + bb:GLA_DKP - BATCH + bb + 1]),
                              (GLA_DKP, GLA_DV)) for bb in range(BATCH)], axis=0)
        s_ref[h] = s_ref[h] * dec + kv
        o = o * lax.rsqrt(jnp.mean(o * o, axis=-1, keepdims=True) + NORM_EPS)
        o = o * ng_ref[:, h * GLA_DV:(h + 1) * GLA_DV]
        rr = r_ref[:, h * GLA_DV:(h + 1) * GLA_DV]
        o_ref[:, h * GLA_DV:(h + 1) * GLA_DV] = o * (rr * _sigmoid(rr))


def _gla(proj, wg, bg, lm, ng):
    T = proj.shape[0]
    rows = GLA_CHUNK * BATCH
    width = GLA_HEADS * GLA_DKP
    return pl.pallas_call(
        _gla_kernel,
        grid=(T // rows,),
        in_specs=[pl.BlockSpec((rows, GLA_WIDTH), lambda i: (i, 1)),
                  pl.BlockSpec((rows, GLA_WIDTH), lambda i: (i, 2)),
                  pl.BlockSpec((rows, width), lambda i: (i, 3)),
                  pl.BlockSpec((rows, width), lambda i: (i, 4)),
                  pl.BlockSpec((rows, LANE), lambda i: (i, PROJ_GATE_COL // LANE)),
                  _const_spec(wg.shape), _const_spec(bg.shape), _const_spec(lm.shape),
                  _const_spec(ng.shape)],
        out_specs=pl.BlockSpec((rows, GLA_HEADS * GLA_DV), lambda i: (i, 0)),
        out_shape=jax.ShapeDtypeStruct((T, GLA_HEADS * GLA_DV), F32),
        scratch_shapes=[pltpu.VMEM((GLA_HEADS, BATCH * GLA_DKP, GLA_DV), F32)],
        compiler_params=_params(("arbitrary",)),
        name="gla",
    )(proj, proj, proj, proj, proj, wg, bg, lm, ng)


def _mix_out_kernel(n_in, has_gate, *refs):
    h_ref = refs[0]
    y_refs = refs[1:1 + n_in]
    w_refs = refs[1 + n_in:1 + 2 * n_in]
    rest = refs[1 + 2 * n_in:]
    gate_ref = rest[0] if has_gate else None
    g_ref, ho_ref, hb_ref, hbt_ref = rest[1:] if has_gate else rest
    acc = h_ref[...]
    for y_ref, w_ref in zip(y_refs, w_refs):
        if len(y_ref.shape) == 3:
            y = jnp.concatenate([_scan_to_rows(y_ref, t0) for t0 in range(0, y_ref.shape[0], SCAN_BLOCK)],
                                axis=0)
        else:
            y = y_ref[...]
        if has_gate:
            y = y * gate_ref[...]
        acc = acc + _dot(y, w_ref[...])
    ho_ref[...] = acc
    hn = _rms(acc, g_ref[...])
    hb_ref[...] = hn.astype(BF16)
    hbt_ref[...] = hn.T.astype(BF16)


def _mix_out(h, ys, ws, gate, g):
    T = h.shape[0]
    tm = ROW_TILE
    n_in = len(ys)
    row = pl.BlockSpec((tm, D_MODEL), lambda i: (i, 0))
    in_specs = [row]
    in_specs += [pl.BlockSpec((tm, y.shape[1]), lambda i: (i, 0)) if y.ndim == 2 else
                 pl.BlockSpec((tm // BATCH,) + y.shape[1:], lambda i: (i, 0, 0)) for y in ys]
    in_specs += [_const_spec(w.shape) for w in ws]
    args = [h, *ys, *ws]
    if gate is not None:
        in_specs.append(row)
        args.append(gate)
    in_specs.append(_const_spec((1, D_MODEL)))
    args.append(g)
    return pl.pallas_call(
        functools.partial(_mix_out_kernel, n_in, gate is not None),
        grid=(T // tm,),
        in_specs=in_specs,
        out_specs=[row, row, pl.BlockSpec((D_MODEL, tm), lambda i: (0, i))],
        out_shape=[jax.ShapeDtypeStruct((T, D_MODEL), F32),
                   jax.ShapeDtypeStruct((T, D_MODEL), BF16),
                   jax.ShapeDtypeStruct((D_MODEL, T), BF16)],
        compiler_params=_params(("parallel",)),
        name="mix_out",
    )(*args)


def _tree_sum(xs):
    while len(xs) > 1:
        xs = [xs[i] + xs[i + 1] for i in range(0, len(xs) - 1, 2)] + ([xs[-1]] if len(xs) % 2 else [])
    return xs[0]


def _insert(tops, x):
    out = []
    for t in tops:
        out.append(jnp.maximum(t, x))
        x = jnp.minimum(t, x)
    return out


def _compare_exchange(v, i, l, descending):
    hi, lo = jnp.maximum(v[i], v[l]), jnp.minimum(v[i], v[l])
    v[i], v[l] = (hi, lo) if descending else (lo, hi)


def _bitonic_merge_desc(v):
    v = list(v)
    j = len(v) // 2
    while j >= 1:
        for i in range(len(v)):
            if i ^ j > i:
                _compare_exchange(v, i, i ^ j, True)
        j //= 2
    return v


def _bitonic_sort_desc(v):
    v = list(v)
    k = 2
    while k <= len(v):
        j = k // 2
        while j >= 1:
            for i in range(len(v)):
                if i ^ j > i:
                    _compare_exchange(v, i, i ^ j, (i & k) == 0)
            j //= 2
        k *= 2
    return v


def _top_values(scr_ref):
    tops = None
    for g in range(PEER_NKEYS // PEER_TOPK):
        grp = _bitonic_sort_desc([scr_ref[pl.ds(g * PEER_TOPK + i, SUBLANE, stride=CHUNK_PITCH), :]
                                  for i in range(PEER_TOPK)])
        if tops is None:
            tops = grp
        else:
            tops = _bitonic_merge_desc([jnp.maximum(tops[i], grp[PEER_TOPK - 1 - i])
                                        for i in range(PEER_TOPK)])
    return tops


def _pick(vals, bits):
    if not bits:
        return vals[0]
    half = len(vals) // 2
    return jnp.where(bits[0], _pick(vals[half:], bits[1:]), _pick(vals[:half], bits[1:]))


def _count_greater(x, s):
    bits = []
    step = PEER_TOPK // 2
    while step >= 1:
        vals = [s[base + step - 1] for base in range(0, PEER_TOPK, 2 * step)]
        bits.append(_pick(vals, bits) > x)
        step //= 2
    return bits, s[PEER_TOPK - 1] > x


def _peer_topk_kernel(hn_ref, wq_ref, keys_ref, n1_ref, e1_ref, r2_ref, e2_ref,
                      scr1, scr2, stat_ref):
    nch = hn_ref.shape[0] // LANE
    q = _dot(hn_ref[...], wq_ref[...])
    for c, scr in enumerate((scr1, scr2)):
        st = _dot_nt(keys_ref[0, c], q[:, c * LANE:(c + 1) * LANE])
        for j in range(nch):
            scr[j * CHUNK_PITCH:j * CHUNK_PITCH + PEER_NKEYS, :] = st[:, j * LANE:(j + 1) * LANE]
    a = _top_values(scr1)
    b = _top_values(scr2)
    pairs = [(i, j) for i in range(PEER_TOPK) for j in range(PEER_TOPK) if (i + 1) * (j + 1) <= PEER_TOPK]
    tops = [jnp.full((SUBLANE, LANE), -jnp.inf, F32) for _ in range(PEER_TOPK)]
    for i, j in pairs:
        tops = _insert(tops, a[i] + b[j])
    thr = tops[PEER_TOPK - 1]
    z = jnp.zeros((SUBLANE, LANE), F32)
    for t in tops:
        z = z + jnp.exp(t - tops[0])
    inv_z = 1.0 / z
    cnt = [jnp.zeros((SUBLANE, LANE), F32) for _ in range(PEER_TOPK)]
    for i, j in pairs:
        cnt[i] = cnt[i] + jnp.where(a[i] + b[j] >= thr, GATE_STEP, 0.0)

    for i in range(PEER_TOPK):
        stat_ref[i] = a[i]
        stat_ref[PEER_TOPK + i] = b[i]
        stat_ref[2 * PEER_TOPK + i] = cnt[i]
    stat_ref[3 * PEER_TOPK] = inv_z
    weights = [GATE_STEP * (PEER_TOPK >> (i + 1)) for i in range(4)] + [GATE_STEP]

    def chunk(c, carry):
        def stat(i):
            return jnp.broadcast_to(stat_ref[i, pl.ds(c, 1), :], (SUBLANE, LANE))
        a_c = [stat(i) for i in range(PEER_TOPK)]
        b_c = [stat(PEER_TOPK + i) for i in range(PEER_TOPK)]
        cnt_c = [stat(2 * PEER_TOPK + i) for i in range(PEER_TOPK)]
        inv_z_c = stat(3 * PEER_TOPK)
        for g in range(PEER_NKEYS // SUBLANE):
            rows = pl.ds(pl.multiple_of(c * CHUNK_PITCH, SUBLANE) + g * SUBLANE, SUBLANE)
            x = scr1[rows, :]
            bits, below = _count_greater(x, a_c)
            n1_ref[0, c, g * SUBLANE:(g + 1) * SUBLANE, :] = jnp.where(below, 0.0, _pick(cnt_c, bits))
            e1_ref[0, c, g * SUBLANE:(g + 1) * SUBLANE, :] = jnp.exp(x - a_c[0]) * inv_z_c
            x = scr2[rows, :]
            bits, below = _count_greater(x, b_c)
            rank = _tree_sum([jnp.where(m, w, 0.0) for m, w in zip(bits + [below], weights)])
            scr1[rows, :] = rank
            scr2[rows, :] = jnp.exp(x - b_c[0])
        return carry

    lax.fori_loop(0, nch, chunk, 0)
    for j in range(nch):
        rows = slice(j * CHUNK_PITCH, j * CHUNK_PITCH + PEER_NKEYS)
        r2_ref[0, j] = scr1[rows, :].astype(BF16)
        e2_ref[0, j, 0:PEER_NKEYS, :] = scr2[rows, :].astype(BF16)
        e2_ref[0, j, PEER_NKEYS:E2_ROWS, :] = jnp.zeros((E2_ROWS - PEER_NKEYS, LANE), BF16)


def _peer_topk(hn, wq, keys):
    T = hn.shape[0]
    tt = SUBLANE * LANE
    nch = T // LANE
    shape = (PEER_HEADS, nch, PEER_NKEYS, LANE)
    spec = pl.BlockSpec((1, SUBLANE, PEER_NKEYS, LANE), lambda i, h: (h, i, 0, 0))
    return pl.pallas_call(
        _peer_topk_kernel,
        grid=(T // tt, PEER_HEADS),
        in_specs=[pl.BlockSpec((tt, D_MODEL), lambda i, h: (i, 0)),
                  pl.BlockSpec((D_MODEL, 2 * LANE), lambda i, h: (0, h)),
                  pl.BlockSpec((1, 2, PEER_NKEYS, LANE), lambda i, h: (h, 0, 0, 0))],
        out_specs=[spec, spec, spec,
                   pl.BlockSpec((1, SUBLANE, E2_ROWS, LANE), lambda i, h: (h, i, 0, 0))],
        out_shape=[jax.ShapeDtypeStruct(shape, F32), jax.ShapeDtypeStruct(shape, F32),
                   jax.ShapeDtypeStruct(shape, BF16),
                   jax.ShapeDtypeStruct((PEER_HEADS, nch, E2_ROWS, LANE), BF16)],
        scratch_shapes=[pltpu.VMEM((SUBLANE * CHUNK_PITCH, LANE), F32),
                        pltpu.VMEM((SUBLANE * CHUNK_PITCH, LANE), F32),
                        pltpu.VMEM((3 * PEER_TOPK + 1, SUBLANE, LANE), F32)],
        compiler_params=_params(("parallel", "arbitrary")),
        name="peer_topk",
    )(hn, wq, keys)


def _row_bf16(ref, h, c, i1, dep=None):
    packed_rows = 2 * SUBLANE
    row = ref[h, c, pl.ds(i1, 1), :]
    if dep is not None:
        row = row + dep
    x = jnp.broadcast_to(row, (packed_rows, LANE)).astype(BF16)
    return jnp.concatenate([x] * (PEER_NKEYS // packed_rows), axis=0)


def _zero_after(x):
    bits = pltpu.bitcast(x[0:2 * SUBLANE, :], jnp.uint32)
    sixteen = jnp.uint32(16)
    zero_bits = lax.shift_right_logical(lax.shift_right_logical(bits, sixteen), sixteen)
    return pltpu.bitcast(zero_bits, F32)[0:1, :]


def _peer_main_kernel(final, hbt_ref, hres_ref, u0_ref, ub_ref, ua_ref, vta_ref, vtb_ref,
                      n1_ref, e1_ref, r2_ref, e2_ref, fg_ref, o_ref,
                      acc_ref, hga_ref, hgb_ref, hta_ref, htb_ref):
    e = pl.program_id(1)
    n_pairs = pl.num_programs(1) - 1
    tt = hbt_ref.shape[1]
    n_i1 = PEER_TILE // PEER_NKEYS
    n_ch = tt // LANE
    zero = jnp.zeros((PEER_NKEYS, LANE), BF16)

    @pl.when(e == 0)
    def _():
        acc_ref[...] = jnp.zeros_like(acc_ref)
        hgb_ref[...] = jnp.zeros_like(hgb_ref)
        hta_ref[...] = _dot(u0_ref[...], hbt_ref[...])

    def gates(tile, ht_ref, hg_ref, c, dep):
        for il in range(n_i1):
            i1 = tile * n_i1 + il
            gate = None
            for h in range(PEER_HEADS):
                code = _row_bf16(n1_ref, h, c, i1, dep if h == 0 else None) - r2_ref[h, c]
                w = _row_bf16(e1_ref, h, c, i1) * e2_ref[h, c, 0:PEER_NKEYS, :]
                term = jnp.maximum(jnp.minimum(w, code), zero)
                gate = term if gate is None else gate + term
            rows = slice(il * PEER_NKEYS, (il + 1) * PEER_NKEYS)
            cols = slice(c * LANE, (c + 1) * LANE)
            hg = _gelu(ht_ref[rows, cols]).astype(BF16) * gate
            hg_ref[rows, cols] = hg
            dep = _zero_after(hg)
        return dep

    def phase(tile, ht_ref, hg_ref, vt_prev_ref, hg_prev_ref, u_next_ref, ht_next_ref, dep):
        half = tt // 2
        for piece in range(4):
            cols = slice((piece % 2) * half, (piece % 2 + 1) * half)
            if piece < 2:
                acc_ref[:, cols] += _dot(vt_prev_ref[...], hg_prev_ref[:, cols])
            else:
                ht_next_ref[:, cols] = _dot(u_next_ref[...], hbt_ref[:, cols])
            for c in range(piece * n_ch // 4, (piece + 1) * n_ch // 4):
                dep = gates(tile, ht_ref, hg_ref, c, dep)
        return dep

    @pl.when(e < n_pairs)
    def _():
        dep = phase(2 * e, hta_ref, hga_ref, vtb_ref, hgb_ref, ub_ref, htb_ref, None)
        phase(2 * e + 1, htb_ref, hgb_ref, vta_ref, hga_ref, ua_ref, hta_ref, dep)

    @pl.when(e == n_pairs)
    def _():
        acc = acc_ref[...] + _dot(vtb_ref[...], hgb_ref[...])
        out = hres_ref[...] + acc.T
        if final:
            out = _rms(out, fg_ref[...])
        o_ref[...] = out


def _peer_main(hbt, hres, u, vt, n1, e1, r2, e2, fg, final):
    T = hres.shape[0]
    E = u.shape[0]
    tt = PEER_TOKENS
    n_tiles = E // PEER_TILE
    n_pairs = n_tiles // 2
    big_spec = pl.BlockSpec((PEER_HEADS, tt // LANE, PEER_NKEYS, LANE), lambda i, e: (0, i, 0, 0))
    tile_of = lambda f: (lambda i, e: (jnp.clip(f(e), 0, n_tiles - 1), 0, 0))
    vt_spec = lambda f: pl.BlockSpec((D_MODEL, PEER_TILE), lambda i, e: tile_of(f)(i, e)[:2])
    u_spec = lambda f: pl.BlockSpec((PEER_TILE, D_MODEL), lambda i, e: tile_of(f)(i, e)[:2])
    return pl.pallas_call(
        functools.partial(_peer_main_kernel, final),
        grid=(T // tt, n_pairs + 1),
        in_specs=[pl.BlockSpec((D_MODEL, tt), lambda i, e: (0, i)),
                  pl.BlockSpec((tt, D_MODEL), lambda i, e: (i, 0)),
                  u_spec(lambda e: 0), u_spec(lambda e: 2 * e + 1), u_spec(lambda e: 2 * e + 2),
                  vt_spec(lambda e: 2 * e), vt_spec(lambda e: 2 * e - 1),
                  big_spec, big_spec, big_spec,
                  pl.BlockSpec((PEER_HEADS, tt // LANE, E2_ROWS, LANE), lambda i, e: (0, i, 0, 0)),
                  _const_spec((1, D_MODEL))],
        out_specs=pl.BlockSpec((tt, D_MODEL), lambda i, e: (i, 0)),
        out_shape=jax.ShapeDtypeStruct((T, D_MODEL), F32),
        scratch_shapes=[pltpu.VMEM((D_MODEL, tt), F32),
                        pltpu.VMEM((PEER_TILE, tt), BF16), pltpu.VMEM((PEER_TILE, tt), BF16),
                        pltpu.VMEM((PEER_TILE, tt), F32), pltpu.VMEM((PEER_TILE, tt), F32)],
        compiler_params=_params(("parallel", "arbitrary")),
        name="peer_main",
    )(hbt, hres, u, u, u, vt, vt, n1, e1, r2, e2, fg)


def _peer_layer(h, hb, hbt, wq, keys, u_tab, v_tab, fg, final):
    n1, e1, r2, e2 = _peer_topk(hb, wq.astype(BF16), keys.astype(BF16))
    vt = v_tab.astype(BF16).reshape(-1, PEER_TILE, D_MODEL).transpose(0, 2, 1).reshape(-1, PEER_TILE)
    return _peer_main(hbt, h, u_tab.astype(BF16), vt, n1, e1, r2, e2, fg, final)


SCAN_BLOCK = LANE // BATCH


def _swap_groups(slabs):
    a = list(slabs)
    group = lax.broadcasted_iota(jnp.int32, a[0].shape, 1) // BATCH
    s = SCAN_BLOCK // 2
    while s >= 1:
        upper = (group & s) != 0
        for i in range(SCAN_BLOCK):
            if i & s == 0:
                ai, aj = a[i], a[i + s]
                a[i] = jnp.where(upper, pltpu.roll(aj, s * BATCH, axis=1), ai)
                a[i + s] = jnp.where(upper, aj, pltpu.roll(ai, LANE - s * BATCH, axis=1))
        s //= 2
    return a


def _rows_to_scan(x, ref):
    block_rows = SCAN_BLOCK * BATCH
    for r0 in range(0, x.shape[0], block_rows):
        z = x[r0:r0 + block_rows].T
        slabs = [z[h * RWKV_HEAD:(h + 1) * RWKV_HEAD, :] for h in range(RWKV_HEADS)]
        for t, blk in enumerate(_swap_groups(slabs)):
            ref[r0 // BATCH + t] = blk


def _scan_to_rows(ref, t0):
    slabs = _swap_groups([ref[t0 + t] for t in range(SCAN_BLOCK)])
    return jnp.concatenate(slabs, axis=0).T


def _rwkv_front_kernel(h_ref, hp_ref, g_ref, mu_ref, wr_ref, wk_ref, wv_ref, w0_ref, ww1_ref, ww2_ref,
                       a0_ref, wa1_ref, wa2_ref, wg1_ref, wg2_ref,
                       r_ref, k_ref, v_ref, w_ref, a_ref, go_ref):
    g = g_ref[...]
    x = _rms(h_ref[...], g)
    xp8 = _rms(hp_ref[...], g)
    xp8 = jnp.where(pl.program_id(0) == 0, 0.0, xp8)
    rows = x.shape[0]
    xprev = jnp.concatenate([xp8, x[:rows - BATCH]], axis=0)
    xx = xprev - x
    mix = lambda j: x + xx * mu_ref[j:j + 1, :]
    _rows_to_scan(_dot(mix(0), wr_ref[...]), r_ref)
    _rows_to_scan(w0_ref[...] + _dot(jnp.tanh(_dot(mix(1), ww1_ref[...])), ww2_ref[...]), w_ref)
    _rows_to_scan(_dot(mix(2), wk_ref[...]), k_ref)
    _rows_to_scan(_dot(mix(3), wv_ref[...]), v_ref)
    _rows_to_scan(_sigmoid(a0_ref[...] + _dot(_dot(mix(4), wa1_ref[...]), wa2_ref[...])), a_ref)
    go_ref[...] = _dot(_sigmoid(_dot(mix(5), wg1_ref[...])), wg2_ref[...])


def _rwkv_front(h, g, mu, wr, wk, wv, w0, ww1, ww2, a0, wa1, wa2, wg1, wg2):
    assert RWKV_HEADS == SCAN_BLOCK
    T = h.shape[0]
    tm = 2 * SCAN_BLOCK * BATCH
    row = pl.BlockSpec((tm, D_MODEL), lambda i: (i, 0))
    prev = pl.BlockSpec((BATCH, D_MODEL), lambda i: (jnp.maximum(i * (tm // BATCH) - 1, 0), 0))
    scan = pl.BlockSpec((tm // BATCH, RWKV_HEAD, LANE), lambda i: (i, 0, 0))
    consts = [g, mu, wr, wk, wv, w0, ww1, ww2, a0, wa1, wa2, wg1, wg2]
    scan_out = jax.ShapeDtypeStruct((T // BATCH, RWKV_HEAD, LANE), F32)
    return pl.pallas_call(
        _rwkv_front_kernel,
        grid=(T // tm,),
        in_specs=[row, prev] + [_const_spec(c.shape) for c in consts],
        out_specs=[scan] * 5 + [row],
        out_shape=[scan_out] * 5 + [jax.ShapeDtypeStruct((T, D_MODEL), F32)],
        compiler_params=_params(("parallel",)),
        name="rwkv_front",
    )(h, h, *consts)


def _rwkv_scan_kernel(r_ref, k_ref, v_ref, w_ref, a_ref, kk_ref, ka_ref, rk_ref, lg_ref, lb_ref,
                      y_ref, s_ref, pend_s, k2_s, av_s, bv_s, wr_s, br_s, kr_s):
    tl = r_ref.shape[0]
    n = RWKV_HEAD
    ng = n // SUBLANE

    @pl.when(pl.program_id(0) == 0)
    def _():
        s_ref[...] = jnp.zeros_like(s_ref)

    r = r_ref[...]
    k = k_ref[...]
    a = a_ref[...]
    kk = k * kk_ref[...]
    kk = kk / jnp.maximum(jnp.sqrt(jnp.sum(kk * kk, axis=1, keepdims=True)), 1e-12)
    k2 = k * (1.0 + (a - 1.0) * ka_ref[...])
    dec = jnp.exp(-jnp.exp(-_softplus(-w_ref[...]) - 0.5))
    bv = kk * a
    p = dec
    sh = 1
    while sh < tl:
        p = p * jnp.concatenate([jnp.ones((sh, n, LANE), F32), p[:tl - sh]], axis=0)
        sh *= 2
    p_prev = jnp.concatenate([jnp.ones((1, n, LANE), F32), p[:tl - 1]], axis=0)
    inv_p = 1.0 / p
    pend_s[...] = p[tl - 1]
    k2_s[...] = k2 * inv_p
    av_s[...] = -kk * p_prev
    bv_s[...] = bv * inv_p
    wr_s[...] = r * p
    br_s[...] = jnp.sum(bv * r, axis=1, keepdims=True)
    kr_s[...] = jnp.sum(k2 * r, axis=1, keepdims=True)

    def row(ref, t, j):
        return jnp.broadcast_to(ref[t, pl.ds(j, 1), :], (SUBLANE, LANE))

    def step(t, carry):
        def reduce_body(j, accs):
            sa, sy = accs
            a_b = row(av_s, t, j)
            wr_b = row(wr_s, t, j)
            sa_n, sy_n = [], []
            for vg in range(ng):
                sv = s_ref[j, vg * SUBLANE:(vg + 1) * SUBLANE, :]
                sa_n.append(sa[vg] + sv * a_b)
                sy_n.append(sy[vg] + sv * wr_b)
            return tuple(sa_n), tuple(sy_n)

        zeros = tuple(jnp.zeros((SUBLANE, LANE), F32) for _ in range(ng))
        sa, sy = lax.fori_loop(0, n, reduce_body, (zeros, zeros), unroll=8)
        br = jnp.broadcast_to(br_s[t], (SUBLANE, LANE))
        kr = jnp.broadcast_to(kr_s[t], (SUBLANE, LANE))
        vv = [v_ref[t, vg * SUBLANE:(vg + 1) * SUBLANE, :] for vg in range(ng)]
        for vg in range(ng):
            y_ref[t, vg * SUBLANE:(vg + 1) * SUBLANE, :] = sy[vg] + sa[vg] * br + vv[vg] * kr

        def update_body(j, c):
            b_b = row(bv_s, t, j)
            k_b = row(k2_s, t, j)
            for vg in range(ng):
                sl = slice(vg * SUBLANE, (vg + 1) * SUBLANE)
                s_ref[j, sl, :] = s_ref[j, sl, :] + sa[vg] * b_b + vv[vg] * k_b
            return c

        lax.fori_loop(0, n, update_body, 0, unroll=8)
        return carry

    lax.fori_loop(0, tl, step, 0)

    def rescale(j, c):
        s_ref[j] = s_ref[j] * pend_s[pl.ds(j, 1), :]
        return c

    lax.fori_loop(0, n, rescale, 0, unroll=8)

    y = y_ref[...]
    mean = jnp.mean(y, axis=1, keepdims=True)
    var = jnp.mean((y - mean) ** 2, axis=1, keepdims=True)
    y = (y - mean) * lax.rsqrt(var + RWKV_GN_EPS) * lg_ref[...] + lb_ref[...]
    bonus = jnp.sum(r * k2 * rk_ref[...], axis=1, keepdims=True) * v_ref[...]
    y_ref[...] = y + bonus


def _rwkv_scan(r, k, v, w, a, kk, ka, rk, lg, lb):
    L = r.shape[0]
    tl = SCAN_STEPS
    n = RWKV_HEAD
    blk = pl.BlockSpec((tl, n, LANE), lambda i: (i, 0, 0))
    par = _const_spec((n, LANE))
    big = pltpu.VMEM((tl, n, LANE), F32)
    small = pltpu.VMEM((tl, 1, LANE), F32)
    return pl.pallas_call(
        _rwkv_scan_kernel,
        grid=(L // tl,),
        in_specs=[blk] * 5 + [par] * 5,
        out_specs=blk,
        out_shape=jax.ShapeDtypeStruct((L, n, LANE), F32),
        scratch_shapes=[pltpu.VMEM((n, n, LANE), F32), pltpu.VMEM((n, LANE), F32), big, big, big, big,
                        small, small],
        compiler_params=_params(("arbitrary",)),
        name="rwkv_scan",
    )(r, k, v, w, a, kk, ka, rk, lg, lb)


def _head_param(p):
    return jnp.repeat(p.reshape(RWKV_HEADS, RWKV_HEAD).T, BATCH, axis=1)


def _prep_even(w_in, a_re, a_im, log_dt, b_re, b_im, c_re, c_im, w_g2, b_g2):
    assert S5_WIDTH == GLA_WIDTH == GLA_QK
    qk = GLA_HEADS * GLA_DK
    widths = (S5_WIDTH, qk, qk, GLA_WIDTH, GLA_RANK, GLA_WIDTH)
    edges = [sum(widths[:j]) for j in range(len(widths) + 1)]
    w_u, w_q, w_k, w_v, w_gl, w_r = (w_in[:, edges[j]:edges[j + 1]] for j in range(len(widths)))

    def pad_heads(w):
        w = w.reshape(-1, GLA_HEADS, GLA_DK)
        w = jnp.pad(w, ((0, 0), (0, 0), (0, GLA_DKP - GLA_DK)))
        return w.reshape(-1, GLA_HEADS * GLA_DKP)

    w_proj = jnp.concatenate([w_u, w_v, w_r, pad_heads(w_q), pad_heads(w_k),
                              jnp.pad(w_gl, ((0, 0), (0, LANE - GLA_RANK)))], axis=1)
    wg = jnp.pad(pad_heads(w_g2), ((0, LANE - GLA_RANK), (0, 0)))
    bg = pad_heads(b_g2[None, :])
    lm = pad_heads(jnp.ones((1, GLA_HEADS * GLA_DK), F32))

    dt = jnp.exp(log_dt)
    mag = jnp.exp(dt * a_re)
    abar_re = mag * jnp.cos(dt * a_im)
    abar_im = mag * jnp.sin(dt * a_im)
    nr = abar_re - 1.0
    den = a_re * a_re + a_im * a_im
    f_re = (nr * a_re + abar_im * a_im) / den
    f_im = (abar_im * a_re - nr * a_im) / den
    bbar_re = f_re[..., None] * b_re - f_im[..., None] * b_im
    bbar_im = f_re[..., None] * b_im + f_im[..., None] * b_re
    eye = jnp.eye(8, dtype=F32)
    gs = S5_GROUPS // S5_SLABS

    def bd(bb):
        bb = bb.reshape(S5_SLABS, gs, S5_STATE, S5_GROUP)
        return jnp.einsum('sgpc,gh->sgchp', bb, eye).reshape(S5_SLABS, gs * S5_GROUP, gs * S5_STATE)

    def cdm(cc):
        cc = cc.reshape(S5_SLABS, gs, S5_GROUP, S5_STATE)
        return jnp.einsum('sgcp,gh->sgphc', cc, eye).reshape(S5_SLABS, gs * S5_STATE, gs * S5_GROUP)

    bd_all = jnp.concatenate([bd(bbar_re), bd(bbar_im)], axis=2)
    cd_all = jnp.concatenate([cdm(c_re), -cdm(c_im)], axis=1)
    are = jnp.broadcast_to(abar_re.reshape(S5_SLABS, 1, gs * S5_STATE), (S5_SLABS, BATCH, gs * S5_STATE))
    aim = jnp.broadcast_to(abar_im.reshape(S5_SLABS, 1, gs * S5_STATE), (S5_SLABS, BATCH, gs * S5_STATE))
    return w_proj, wg, bg, lm, bd_all, cd_all, are, aim


def kernel(x, norm_mix_g, norm_ffn_g, final_g, e_w_in, e_w_out, s5_a_re, s5_a_im, s5_log_dt, s5_b_re, s5_b_im, s5_c_re, s5_c_im, s5_d, s5_w_glu, gla_w_g2, gla_b_g2, gla_norm_g, o_mu, o_w_r, o_w_k, o_w_v, o_w0, o_w_w1, o_w_w2, o_a0, o_w_a1, o_w_a2, o_w_g1, o_w_g2, o_k_k, o_k_a, o_r_k, o_lnx_g, o_lnx_b, o_w_o, peer_w_q, peer_sub_keys, peer_u, peer_v):
    bsz, L, D = x.shape
    assert bsz == BATCH and D == D_MODEL
    T = bsz * L
    row = lambda p: p.reshape(1, -1)
    h = x.transpose(1, 0, 2).reshape(T, D)

    w_proj, wg, bg, lm, bd_all, cd_all, are, aim = _prep_even(
        e_w_in[0], s5_a_re[0], s5_a_im[0], s5_log_dt[0], s5_b_re[0], s5_b_im[0],
        s5_c_re[0], s5_c_im[0], gla_w_g2[0], gla_b_g2[0])
    bf = lambda p: p.astype(BF16)
    proj = _even_front(h, row(norm_mix_g[0]), bf(w_proj))
    y_s5 = _s5(proj, bf(bd_all), bf(cd_all), are, aim, row(s5_d[0]), bf(s5_w_glu[0]))
    y_gla = _gla(proj, wg, bg, lm, row(gla_norm_g[0]))
    w_out = bf(e_w_out[0])
    h, hb, hbt = _mix_out(h, [y_s5, y_gla], [w_out[:S5_WIDTH], w_out[S5_WIDTH:]], None, row(norm_ffn_g[0]))
    h = _peer_layer(h, hb, hbt, peer_w_q[0], peer_sub_keys[0], peer_u[0], peer_v[0], row(final_g), False)

    r, k, v, w, a, g = _rwkv_front(
        h, row(norm_mix_g[1]), o_mu[0], bf(o_w_r[0]), bf(o_w_k[0]), bf(o_w_v[0]), row(o_w0[0]),
        bf(o_w_w1[0]), bf(o_w_w2[0]), row(o_a0[0]), bf(o_w_a1[0]), bf(o_w_a2[0]), bf(o_w_g1[0]), bf(o_w_g2[0]))
    y = _rwkv_scan(r, k, v, w, a, _head_param(o_k_k[0]), _head_param(o_k_a[0]),
                   _head_param(o_r_k[0].reshape(-1)), _head_param(o_lnx_g[0]), _head_param(o_lnx_b[0]))
    h, hb, hbt = _mix_out(h, [y], [bf(o_w_o[0])], g, row(norm_ffn_g[1]))
    out = _peer_layer(h, hb, hbt, peer_w_q[1], peer_sub_keys[1], peer_u[1], peer_v[1], row(final_g), True)
    return out.reshape(L, bsz, D).transpose(1, 0, 2)
```

```python
import functools
import math

import jax
import jax.numpy as jnp
from jax import lax
from jax.experimental import pallas as pl
from jax.experimental.pallas import tpu as pltpu

F32 = jnp.float32
BF16 = jnp.bfloat16
HIGHEST = lax.Precision.HIGHEST

D_MODEL = 1024
BATCH = 8
NORM_EPS = 1e-6
S5_WIDTH = 512
S5_GROUP = 16
S5_GROUPS = 32
S5_STATE = 64
S5_SLABS = 4
S5_SLAB_STATES = S5_GROUPS // S5_SLABS * S5_STATE
GLA_HEADS = 4
GLA_DV = 128
GLA_DK = 64
GLA_DKP = 128
GLA_RANK = 16
GLA_TAU = 16.0
GLA_CHUNK = 64
RWKV_HEAD = 64
RWKV_HEADS = 16
RWKV_GN_EPS = 64e-5
PEER_HEADS = 8
PEER_NKEYS = 128
PEER_TOPK = 16
PEER_TILE = 1024
PEER_TOKENS = 512
GATE_STEP = 2.0
LANE = 128
SUBLANE = 8
VMEM_LIMIT = 56 * 1024 * 1024
CHUNK_PITCH = PEER_NKEYS + SUBLANE
E2_ROWS = PEER_NKEYS + 4 * SUBLANE
E2_ROW0 = 2 * SUBLANE
E1_ROWS = PEER_NKEYS + 2 * SUBLANE
E1_ROW0 = SUBLANE
ROW_TILE = 256
SCAN_STEPS = 32

GLA_WIDTH = GLA_HEADS * GLA_DV
GLA_QK = GLA_HEADS * GLA_DKP
PROJ_GATE_COL = S5_WIDTH + 2 * GLA_WIDTH + 2 * GLA_QK
PROJ_W = PROJ_GATE_COL + LANE


def _dot(a, b, hi=False):
    if hi:
        return jnp.dot(a, b, preferred_element_type=F32, precision=HIGHEST)
    return jnp.dot(a.astype(BF16), b.astype(BF16), preferred_element_type=F32)


def _dot_nt(a, b):
    return lax.dot_general(a.astype(BF16), b.astype(BF16), (((1,), (1,)), ((), ())),
                           preferred_element_type=F32)


def _rms(x, g):
    return x * lax.rsqrt(jnp.mean(x * x, axis=-1, keepdims=True) + NORM_EPS) * g


def _sigmoid(x):
    return 1.0 / (1.0 + jnp.exp(-x))


def _softplus(x):
    return jnp.maximum(x, 0.0) + jnp.log(1.0 + jnp.exp(-jnp.abs(x)))


def _gelu(x):
    return 0.5 * x * (1.0 + lax.erf(x * (1.0 / math.sqrt(2.0))))


def _const_spec(shape):
    nd = len(shape)
    return pl.BlockSpec(shape, lambda *_: (0,) * nd)


def _params(sem, flags=None):
    return pltpu.CompilerParams(dimension_semantics=sem, vmem_limit_bytes=VMEM_LIMIT, flags=flags)


def _even_front_kernel(h_ref, g_ref, w_ref, o_ref):
    xn = _rms(h_ref[...], g_ref[...])
    o_ref[...] = _dot(xn, w_ref[...])


def _even_front(h, g, w):
    T = h.shape[0]
    tm = ROW_TILE
    return pl.pallas_call(
        _even_front_kernel,
        grid=(T // tm,),
        in_specs=[pl.BlockSpec((tm, D_MODEL), lambda i: (i, 0)),
                  _const_spec((1, D_MODEL)),
                  _const_spec((D_MODEL, PROJ_W))],
        out_specs=pl.BlockSpec((tm, PROJ_W), lambda i: (i, 0)),
        out_shape=jax.ShapeDtypeStruct((T, PROJ_W), F32),
        compiler_params=_params(("parallel",)),
        name="even_front",
    )(h, g, w)


def _s5_kernel(u_ref, bd_ref, cd_ref, are_ref, aim_ref, d_ref, wglu_ref, o_ref, bu_ref, st_ref):
    rows = u_ref.shape[0]
    steps = rows // BATCH
    half = S5_SLAB_STATES

    @pl.when(pl.program_id(0) == 0)
    def _():
        st_ref[...] = jnp.zeros_like(st_ref)

    u = u_ref[...]
    for s in range(S5_SLABS):
        bu_ref[s] = _dot(u[:, s * LANE:(s + 1) * LANE], bd_ref[s])

    def step(t, carry):
        r0 = pl.multiple_of(t * BATCH, BATCH)
        out = []
        for s in range(S5_SLABS):
            sre, sim = carry[2 * s], carry[2 * s + 1]
            ar, ai = are_ref[s], aim_ref[s]
            bre = bu_ref[s, pl.ds(r0, BATCH), 0:half]
            bim = bu_ref[s, pl.ds(r0, BATCH), half:2 * half]
            nre = ar * sre - ai * sim + bre
            nim = ar * sim + ai * sre + bim
            bu_ref[s, pl.ds(r0, BATCH), 0:half] = nre
            bu_ref[s, pl.ds(r0, BATCH), half:2 * half] = nim
            out += [nre, nim]
        return tuple(out)

    init = tuple(st_ref[j] for j in range(2 * S5_SLABS))
    fin = lax.fori_loop(0, steps, step, init)
    for j in range(2 * S5_SLABS):
        st_ref[j] = fin[j]

    y = jnp.concatenate([_dot(bu_ref[s], cd_ref[s]) for s in range(S5_SLABS)], axis=1)
    y = y + d_ref[...] * u
    g = _gelu(y)
    o_ref[...] = g * _sigmoid(_dot(g, wglu_ref[...]))


def _s5(proj, bd, cd, are, aim, d, wglu):
    T = proj.shape[0]
    rows = ROW_TILE
    return pl.pallas_call(
        _s5_kernel,
        grid=(T // rows,),
        in_specs=[pl.BlockSpec((rows, S5_WIDTH), lambda i: (i, 0)),
                  _const_spec(bd.shape), _const_spec(cd.shape),
                  _const_spec(are.shape), _const_spec(aim.shape),
                  _const_spec(d.shape), _const_spec(wglu.shape)],
        out_specs=pl.BlockSpec((rows, S5_WIDTH), lambda i: (i, 0)),
        out_shape=jax.ShapeDtypeStruct((T, S5_WIDTH), F32),
        scratch_shapes=[pltpu.VMEM((S5_SLABS, rows, 2 * S5_SLAB_STATES), F32),
                        pltpu.VMEM((2 * S5_SLABS, BATCH, S5_SLAB_STATES), F32)],
        compiler_params=_params(("arbitrary",)),
        name="s5",
    )(proj, bd, cd, are, aim, d, wglu)


def _gla_kernel(v_ref, r_ref, q_ref, k_ref, gl_ref, wg_ref, bg_ref, lm_ref, ng_ref, o_ref, s_ref):
    rows = q_ref.shape[0]
    width = GLA_HEADS * GLA_DKP

    @pl.when(pl.program_id(0) == 0)
    def _():
        s_ref[...] = jnp.zeros_like(s_ref)

    z = _dot(gl_ref[...], wg_ref[...], True) + bg_ref[...]
    la = -_softplus(-z) * (1.0 / GLA_TAU) * lm_ref[...]
    b = la
    sh = BATCH
    while sh < rows:
        b = b + jnp.concatenate([jnp.zeros((sh, width), F32), b[:rows - sh]], axis=0)
        sh *= 2
    blast = b[rows - BATCH:rows]
    blast_t = jnp.concatenate([blast] * GLA_CHUNK, axis=0)
    q = q_ref[...] * (GLA_DK ** -0.5)
    k = k_ref[...]
    qd = q * jnp.exp(b)
    kd = k * jnp.exp(-b)
    kend = k * jnp.exp(blast_t - b)

    ri = lax.broadcasted_iota(jnp.int32, (rows, rows), 0)
    ci = lax.broadcasted_iota(jnp.int32, (rows, rows), 1)
    causal = jnp.logical_and(((ri - ci) & (BATCH - 1)) == 0, ri >= ci)
    rowb = lax.broadcasted_iota(jnp.int32, (rows, GLA_DKP), 0) & (BATCH - 1)
    laneb = lax.broadcasted_iota(jnp.int32, (GLA_DKP, rows), 1) & (BATCH - 1)

    for h in range(GLA_HEADS):
        sl = slice(h * GLA_DKP, (h + 1) * GLA_DKP)
        vh = v_ref[:, h * GLA_DV:(h + 1) * GLA_DV]
        qh, kdh, keh, bh = qd[:, sl], kd[:, sl], kend[:, sl], b[:, sl]
        att = jnp.where(causal, _dot_nt(qh, kdh), 0.0)
        o = _dot(att, vh)
        qexp = jnp.concatenate([jnp.where(rowb == bb, qh, 0.0) for bb in range(BATCH)], axis=1)
        o = o + _dot(qexp, s_ref[h])
        ket = keh.T
        kexp_t = jnp.concatenate([jnp.where(laneb == bb, ket, 0.0) for bb in range(BATCH)], axis=0)
        kv = _dot(kexp_t, vh)
        bt = bh[rows - GLA_DKP:rows].T
        dec = jnp.concatenate(
            [jnp.broadcast_to(jnp.exp(bt[:, GLA_DKP - BATCH + bb:GLA_DKP - BATCH + bb + 1]),
                              (GLA_DKP, GLA_DV)) for bb in range(BATCH)], axis=0)
        s_ref[h] = s_ref[h] * dec + kv
        o = o * lax.rsqrt(jnp.mean(o * o, axis=-1, keepdims=True) + NORM_EPS)
        o = o * ng_ref[:, h * GLA_DV:(h + 1) * GLA_DV]
        rr = r_ref[:, h * GLA_DV:(h + 1) * GLA_DV]
        o_ref[:, h * GLA_DV:(h + 1) * GLA_DV] = o * (rr * _sigmoid(rr))


def _gla(proj, wg, bg, lm, ng):
    T = proj.shape[0]
    rows = GLA_CHUNK * BATCH
    width = GLA_HEADS * GLA_DKP
    return pl.pallas_call(
        _gla_kernel,
        grid=(T // rows,),
        in_specs=[pl.BlockSpec((rows, GLA_WIDTH), lambda i: (i, 1)),
                  pl.BlockSpec((rows, GLA_WIDTH), lambda i: (i, 2)),
                  pl.BlockSpec((rows, width), lambda i: (i, 3)),
                  pl.BlockSpec((rows, width), lambda i: (i, 4)),
                  pl.BlockSpec((rows, LANE), lambda i: (i, PROJ_GATE_COL // LANE)),
                  _const_spec(wg.shape), _const_spec(bg.shape), _const_spec(lm.shape),
                  _const_spec(ng.shape)],
        out_specs=pl.BlockSpec((rows, GLA_HEADS * GLA_DV), lambda i: (i, 0)),
        out_shape=jax.ShapeDtypeStruct((T, GLA_HEADS * GLA_DV), F32),
        scratch_shapes=[pltpu.VMEM((GLA_HEADS, BATCH * GLA_DKP, GLA_DV), F32)],
        compiler_params=_params(("arbitrary",)),
        name="gla",
    )(proj, proj, proj, proj, proj, wg, bg, lm, ng)


def _mix_out_kernel(n_in, has_gate, *refs):
    h_ref = refs[0]
    y_refs = refs[1:1 + n_in]
    w_refs = refs[1 + n_in:1 + 2 * n_in]
    rest = refs[1 + 2 * n_in:]
    gate_ref = rest[0] if has_gate else None
    g_ref, ho_ref, hb_ref, hbt_ref = rest[1:] if has_gate else rest
    acc = h_ref[...]
    for y_ref, w_ref in zip(y_refs, w_refs):
        if len(y_ref.shape) == 3:
            y = jnp.concatenate([_scan_to_rows(y_ref, t0) for t0 in range(0, y_ref.shape[0], SCAN_BLOCK)],
                                axis=0)
        else:
            y = y_ref[...]
        if has_gate:
            y = y * gate_ref[...]
        acc = acc + _dot(y, w_ref[...])
    ho_ref[...] = acc
    hn = _rms(acc, g_ref[...])
    hb_ref[...] = hn.astype(BF16)
    hbt_ref[...] = hn.T.astype(BF16)


def _mix_out(h, ys, ws, gate, g):
    T = h.shape[0]
    tm = ROW_TILE
    n_in = len(ys)
    row = pl.BlockSpec((tm, D_MODEL), lambda i: (i, 0))
    in_specs = [row]
    in_specs += [pl.BlockSpec((tm, y.shape[1]), lambda i: (i, 0)) if y.ndim == 2 else
                 pl.BlockSpec((tm // BATCH,) + y.shape[1:], lambda i: (i, 0, 0)) for y in ys]
    in_specs += [_const_spec(w.shape) for w in ws]
    args = [h, *ys, *ws]
    if gate is not None:
        in_specs.append(row)
        args.append(gate)
    in_specs.append(_const_spec((1, D_MODEL)))
    args.append(g)
    return pl.pallas_call(
        functools.partial(_mix_out_kernel, n_in, gate is not None),
        grid=(T // tm,),
        in_specs=in_specs,
        out_specs=[row, row, pl.BlockSpec((D_MODEL, tm), lambda i: (0, i))],
        out_shape=[jax.ShapeDtypeStruct((T, D_MODEL), F32),
                   jax.ShapeDtypeStruct((T, D_MODEL), BF16),
                   jax.ShapeDtypeStruct((D_MODEL, T), BF16)],
        compiler_params=_params(("parallel",)),
        name="mix_out",
    )(*args)


def _tree_sum(xs):
    while len(xs) > 1:
        xs = [xs[i] + xs[i + 1] for i in range(0, len(xs) - 1, 2)] + ([xs[-1]] if len(xs) % 2 else [])
    return xs[0]


def _insert(tops, x):
    out = []
    for t in tops:
        out.append(jnp.maximum(t, x))
        x = jnp.minimum(t, x)
    return out


def _compare_exchange(v, i, l, descending):
    hi, lo = jnp.maximum(v[i], v[l]), jnp.minimum(v[i], v[l])
    v[i], v[l] = (hi, lo) if descending else (lo, hi)


def _bitonic_merge_desc(v):
    v = list(v)
    j = len(v) // 2
    while j >= 1:
        for i in range(len(v)):
            if i ^ j > i:
                _compare_exchange(v, i, i ^ j, True)
        j //= 2
    return v


def _bitonic_sort_desc(v):
    v = list(v)
    k = 2
    while k <= len(v):
        j = k // 2
        while j >= 1:
            for i in range(len(v)):
                if i ^ j > i:
                    _compare_exchange(v, i, i ^ j, (i & k) == 0)
            j //= 2
        k *= 2
    return v


def _top_values(scr_ref):
    tops = None
    for g in range(PEER_NKEYS // PEER_TOPK):
        grp = _bitonic_sort_desc([scr_ref[pl.ds(g * PEER_TOPK + i, SUBLANE, stride=CHUNK_PITCH), :]
                                  for i in range(PEER_TOPK)])
        if tops is None:
            tops = grp
        else:
            tops = _bitonic_merge_desc([jnp.maximum(tops[i], grp[PEER_TOPK - 1 - i])
                                        for i in range(PEER_TOPK)])
    return tops


def _pick(vals, bits):
    if not bits:
        return vals[0]
    half = len(vals) // 2
    return jnp.where(bits[0], _pick(vals[half:], bits[1:]), _pick(vals[:half], bits[1:]))


def _count_greater(x, s):
    bits = []
    step = PEER_TOPK // 2
    while step >= 1:
        vals = [s[base + step - 1] for base in range(0, PEER_TOPK, 2 * step)]
        bits.append(_pick(vals, bits) > x)
        step //= 2
    return bits, s[PEER_TOPK - 1] > x


def _peer_topk_kernel(hn_ref, wq_ref, keys_ref, n1_ref, e1_ref, r2_ref, e2_ref,
                      scr1, scr2, stat_ref):
    nch = hn_ref.shape[0] // LANE
    q = _dot(hn_ref[...], wq_ref[...])
    for c, scr in enumerate((scr1, scr2)):
        st = _dot_nt(keys_ref[0, c], q[:, c * LANE:(c + 1) * LANE])
        for j in range(nch):
            scr[j * CHUNK_PITCH:j * CHUNK_PITCH + PEER_NKEYS, :] = st[:, j * LANE:(j + 1) * LANE]
    a = _top_values(scr1)
    b = _top_values(scr2)
    pairs = [(i, j) for i in range(PEER_TOPK) for j in range(PEER_TOPK) if (i + 1) * (j + 1) <= PEER_TOPK]
    tops = [jnp.full((SUBLANE, LANE), -jnp.inf, F32) for _ in range(PEER_TOPK)]
    for i, j in pairs:
        tops = _insert(tops, a[i] + b[j])
    thr = tops[PEER_TOPK - 1]
    z = jnp.zeros((SUBLANE, LANE), F32)
    for t in tops:
        z = z + jnp.exp(t - tops[0])
    inv_z = 1.0 / z
    cnt = [jnp.zeros((SUBLANE, LANE), F32) for _ in range(PEER_TOPK)]
    for i, j in pairs:
        cnt[i] = cnt[i] + jnp.where(a[i] + b[j] >= thr, GATE_STEP, 0.0)

    for i in range(PEER_TOPK):
        stat_ref[i] = a[i]
        stat_ref[PEER_TOPK + i] = b[i]
        stat_ref[2 * PEER_TOPK + i] = cnt[i]
    stat_ref[3 * PEER_TOPK] = inv_z
    weights = [GATE_STEP * (PEER_TOPK >> (i + 1)) for i in range(4)] + [GATE_STEP]

    def chunk(c, carry):
        def stat(i):
            return jnp.broadcast_to(stat_ref[i, pl.ds(c, 1), :], (SUBLANE, LANE))
        a_c = [stat(i) for i in range(PEER_TOPK)]
        b_c = [stat(PEER_TOPK + i) for i in range(PEER_TOPK)]
        cnt_c = [stat(2 * PEER_TOPK + i) for i in range(PEER_TOPK)]
        inv_z_c = stat(3 * PEER_TOPK)
        e1_ref[0, c, 0:E1_ROW0, :] = jnp.zeros((E1_ROW0, LANE), F32)
        e1_ref[0, c, E1_ROW0 + PEER_NKEYS:E1_ROWS, :] = jnp.zeros((E1_ROWS - E1_ROW0 - PEER_NKEYS, LANE), F32)
        for g in range(PEER_NKEYS // SUBLANE):
            rows = pl.ds(pl.multiple_of(c * CHUNK_PITCH, SUBLANE) + g * SUBLANE, SUBLANE)
            x = scr1[rows, :]
            bits, below = _count_greater(x, a_c)
            n1_ref[0, c, g * SUBLANE:(g + 1) * SUBLANE, :] = jnp.where(below, 0.0, _pick(cnt_c, bits))
            e1_ref[0, c, E1_ROW0 + g * SUBLANE:E1_ROW0 + (g + 1) * SUBLANE, :] = jnp.exp(x - a_c[0]) * inv_z_c
            x = scr2[rows, :]
            bits, below = _count_greater(x, b_c)
            rank = _tree_sum([jnp.where(m, w, 0.0) for m, w in zip(bits + [below], weights)])
            scr1[rows, :] = rank
            scr2[rows, :] = jnp.exp(x - b_c[0])
        return carry

    lax.fori_loop(0, nch, chunk, 0)
    for j in range(nch):
        rows = slice(j * CHUNK_PITCH, j * CHUNK_PITCH + PEER_NKEYS)
        r2_ref[0, j] = scr1[rows, :].astype(BF16)
        e2_ref[0, j, 0:E2_ROW0, :] = jnp.zeros((E2_ROW0, LANE), BF16)
        e2_ref[0, j, E2_ROW0:E2_ROW0 + PEER_NKEYS, :] = scr2[rows, :].astype(BF16)
        e2_ref[0, j, E2_ROW0 + PEER_NKEYS:E2_ROWS, :] = jnp.zeros((E2_ROWS - E2_ROW0 - PEER_NKEYS, LANE), BF16)


def _peer_topk(hn, wq, keys):
    T = hn.shape[0]
    tt = SUBLANE * LANE
    nch = T // LANE
    shape = (PEER_HEADS, nch, PEER_NKEYS, LANE)
    spec = pl.BlockSpec((1, SUBLANE, PEER_NKEYS, LANE), lambda i, h: (h, i, 0, 0))
    return pl.pallas_call(
        _peer_topk_kernel,
        grid=(T // tt, PEER_HEADS),
        in_specs=[pl.BlockSpec((tt, D_MODEL), lambda i, h: (i, 0)),
                  pl.BlockSpec((D_MODEL, 2 * LANE), lambda i, h: (0, h)),
                  pl.BlockSpec((1, 2, PEER_NKEYS, LANE), lambda i, h: (h, 0, 0, 0))],
        out_specs=[spec, pl.BlockSpec((1, SUBLANE, E1_ROWS, LANE), lambda i, h: (h, i, 0, 0)), spec,
                   pl.BlockSpec((1, SUBLANE, E2_ROWS, LANE), lambda i, h: (h, i, 0, 0))],
        out_shape=[jax.ShapeDtypeStruct(shape, F32),
                   jax.ShapeDtypeStruct((PEER_HEADS, nch, E1_ROWS, LANE), F32),
                   jax.ShapeDtypeStruct(shape, BF16),
                   jax.ShapeDtypeStruct((PEER_HEADS, nch, E2_ROWS, LANE), BF16)],
        scratch_shapes=[pltpu.VMEM((SUBLANE * CHUNK_PITCH, LANE), F32),
                        pltpu.VMEM((SUBLANE * CHUNK_PITCH, LANE), F32),
                        pltpu.VMEM((3 * PEER_TOPK + 1, SUBLANE, LANE), F32)],
        compiler_params=_params(("parallel", "arbitrary")),
        name="peer_topk",
    )(hn, wq, keys)


def _row_bf16(ref, h, c, i1, dep=None):
    packed_rows = 2 * SUBLANE
    row = ref[h, c, pl.ds(i1, 1), :]
    if dep is not None:
        row = row + dep
    x = jnp.broadcast_to(row, (packed_rows, LANE)).astype(BF16)
    return jnp.concatenate([x] * (PEER_NKEYS // packed_rows), axis=0)


def _zero_after(x):
    bits = pltpu.bitcast(x[0:2 * SUBLANE, :], jnp.uint32)
    sixteen = jnp.uint32(16)
    zero_bits = lax.shift_right_logical(lax.shift_right_logical(bits, sixteen), sixteen)
    return pltpu.bitcast(zero_bits, F32)[0:1, :]


def _peer_main_kernel(final, hbt_ref, hres_ref, u0_ref, ub_ref, ua_ref, vta_ref, vtb_ref,
                      n1_ref, e1_ref, r2_ref, e2_ref, fg_ref, o_ref,
                      acc_ref, hga_ref, hgb_ref, hta_ref, htb_ref):
    e = pl.program_id(1)
    n_pairs = pl.num_programs(1) - 1
    tt = hbt_ref.shape[1]
    n_i1 = PEER_TILE // PEER_NKEYS
    n_ch = tt // LANE
    zero = jnp.zeros((PEER_NKEYS, LANE), BF16)

    @pl.when(e == 0)
    def _():
        acc_ref[...] = jnp.zeros_like(acc_ref)
        hgb_ref[...] = jnp.zeros_like(hgb_ref)
        hta_ref[...] = _dot(u0_ref[...], hbt_ref[...])

    def gates(tile, ht_ref, hg_ref, c, dep):
        for il in range(n_i1):
            i1 = tile * n_i1 + il
            gate = None
            for h in range(PEER_HEADS):
                code = _row_bf16(n1_ref, h, c, i1, dep if h == 0 else None) - r2_ref[h, c]
                w = _row_bf16(e1_ref, h, c, i1 + E1_ROW0) * e2_ref[h, c, E2_ROW0:E2_ROW0 + PEER_NKEYS, :]
                term = jnp.maximum(jnp.minimum(w, code), zero)
                gate = term if gate is None else gate + term
            rows = slice(il * PEER_NKEYS, (il + 1) * PEER_NKEYS)
            cols = slice(c * LANE, (c + 1) * LANE)
            hg = _gelu(ht_ref[rows, cols]).astype(BF16) * gate
            hg_ref[rows, cols] = hg
            dep = _zero_after(hg)
        return dep

    def phase(tile, ht_ref, hg_ref, vt_prev_ref, hg_prev_ref, u_next_ref, ht_next_ref, dep):
        half = tt // 2
        for piece in range(4):
            cols = slice((piece % 2) * half, (piece % 2 + 1) * half)
            if piece < 2:
                acc_ref[:, cols] += _dot(vt_prev_ref[...], hg_prev_ref[:, cols])
            else:
                ht_next_ref[:, cols] = _dot(u_next_ref[...], hbt_ref[:, cols])
            for c in range(piece * n_ch // 4, (piece + 1) * n_ch // 4):
                dep = gates(tile, ht_ref, hg_ref, c, dep)
        return dep

    @pl.when(e < n_pairs)
    def _():
        dep = phase(2 * e, hta_ref, hga_ref, vtb_ref, hgb_ref, ub_ref, htb_ref, None)
        phase(2 * e + 1, htb_ref, hgb_ref, vta_ref, hga_ref, ua_ref, hta_ref, dep)

    @pl.when(e == n_pairs)
    def _():
        acc = acc_ref[...] + _dot(vtb_ref[...], hgb_ref[...])
        out = hres_ref[...] + acc.T
        if final:
            out = _rms(out, fg_ref[...])
        o_ref[...] = out


def _peer_main(hbt, hres, u, vt, n1, e1, r2, e2, fg, final):
    T = hres.shape[0]
    E = u.shape[0]
    tt = PEER_TOKENS
    n_tiles = E // PEER_TILE
    n_pairs = n_tiles // 2
    big_spec = pl.BlockSpec((PEER_HEADS, tt // LANE, PEER_NKEYS, LANE), lambda i, e: (0, i, 0, 0))
    tile_of = lambda f: (lambda i, e: (jnp.clip(f(e), 0, n_tiles - 1), 0, 0))
    vt_spec = lambda f: pl.BlockSpec((D_MODEL, PEER_TILE), lambda i, e: tile_of(f)(i, e)[:2])
    u_spec = lambda f: pl.BlockSpec((PEER_TILE, D_MODEL), lambda i, e: tile_of(f)(i, e)[:2])
    return pl.pallas_call(
        functools.partial(_peer_main_kernel, final),
        grid=(T // tt, n_pairs + 1),
        in_specs=[pl.BlockSpec((D_MODEL, tt), lambda i, e: (0, i)),
                  pl.BlockSpec((tt, D_MODEL), lambda i, e: (i, 0)),
                  u_spec(lambda e: 0), u_spec(lambda e: 2 * e + 1), u_spec(lambda e: 2 * e + 2),
                  vt_spec(lambda e: 2 * e), vt_spec(lambda e: 2 * e - 1),
                  big_spec,
                  pl.BlockSpec((PEER_HEADS, tt // LANE, E1_ROWS, LANE), lambda i, e: (0, i, 0, 0)),
                  big_spec,
                  pl.BlockSpec((PEER_HEADS, tt // LANE, E2_ROWS, LANE), lambda i, e: (0, i, 0, 0)),
                  _const_spec((1, D_MODEL))],
        out_specs=pl.BlockSpec((tt, D_MODEL), lambda i, e: (i, 0)),
        out_shape=jax.ShapeDtypeStruct((T, D_MODEL), F32),
        scratch_shapes=[pltpu.VMEM((D_MODEL, tt), F32),
                        pltpu.VMEM((PEER_TILE, tt), BF16), pltpu.VMEM((PEER_TILE, tt), BF16),
                        pltpu.VMEM((PEER_TILE, tt), F32), pltpu.VMEM((PEER_TILE, tt), F32)],
        compiler_params=_params(("parallel", "arbitrary")),
        name="peer_main",
    )(hbt, hres, u, u, u, vt, vt, n1, e1, r2, e2, fg)


def _peer_layer(h, hb, hbt, wq, keys, u_tab, v_tab, fg, final):
    n1, e1, r2, e2 = _peer_topk(hb, wq.astype(BF16), keys.astype(BF16))
    vt = v_tab.astype(BF16).reshape(-1, PEER_TILE, D_MODEL).transpose(0, 2, 1).reshape(-1, PEER_TILE)
    return _peer_main(hbt, h, u_tab.astype(BF16), vt, n1, e1, r2, e2, fg, final)


SCAN_BLOCK = LANE // BATCH


def _swap_groups(slabs):
    a = list(slabs)
    group = lax.broadcasted_iota(jnp.int32, a[0].shape, 1) // BATCH
    s = SCAN_BLOCK // 2
    while s >= 1:
        upper = (group & s) != 0
        for i in range(SCAN_BLOCK):
            if i & s == 0:
                ai, aj = a[i], a[i + s]
                a[i] = jnp.where(upper, pltpu.roll(aj, s * BATCH, axis=1), ai)
                a[i + s] = jnp.where(upper, aj, pltpu.roll(ai, LANE - s * BATCH, axis=1))
        s //= 2
    return a


def _rows_to_scan(x, ref):
    block_rows = SCAN_BLOCK * BATCH
    for r0 in range(0, x.shape[0], block_rows):
        z = x[r0:r0 + block_rows].T
        slabs = [z[h * RWKV_HEAD:(h + 1) * RWKV_HEAD, :] for h in range(RWKV_HEADS)]
        for t, blk in enumerate(_swap_groups(slabs)):
            ref[r0 // BATCH + t] = blk


def _scan_to_rows(ref, t0):
    slabs = _swap_groups([ref[t0 + t] for t in range(SCAN_BLOCK)])
    return jnp.concatenate(slabs, axis=0).T


def _rwkv_front_kernel(h_ref, hp_ref, g_ref, mu_ref, wr_ref, wk_ref, wv_ref, w0_ref, ww1_ref, ww2_ref,
                       a0_ref, wa1_ref, wa2_ref, wg1_ref, wg2_ref,
                       r_ref, k_ref, v_ref, w_ref, a_ref, go_ref):
    g = g_ref[...]
    x = _rms(h_ref[...], g)
    xp8 = _rms(hp_ref[...], g)
    xp8 = jnp.where(pl.program_id(0) == 0, 0.0, xp8)
    rows = x.shape[0]
    xprev = jnp.concatenate([xp8, x[:rows - BATCH]], axis=0)
    xx = xprev - x
    mix = lambda j: x + xx * mu_ref[j:j + 1, :]
    _rows_to_scan(_dot(mix(0), wr_ref[...]), r_ref)
    _rows_to_scan(w0_ref[...] + _dot(jnp.tanh(_dot(mix(1), ww1_ref[...])), ww2_ref[...]), w_ref)
    _rows_to_scan(_dot(mix(2), wk_ref[...]), k_ref)
    _rows_to_scan(_dot(mix(3), wv_ref[...]), v_ref)
    _rows_to_scan(_sigmoid(a0_ref[...] + _dot(_dot(mix(4), wa1_ref[...]), wa2_ref[...])), a_ref)
    go_ref[...] = _dot(_sigmoid(_dot(mix(5), wg1_ref[...])), wg2_ref[...])


def _rwkv_front(h, g, mu, wr, wk, wv, w0, ww1, ww2, a0, wa1, wa2, wg1, wg2):
    assert RWKV_HEADS == SCAN_BLOCK
    T = h.shape[0]
    tm = 2 * SCAN_BLOCK * BATCH
    row = pl.BlockSpec((tm, D_MODEL), lambda i: (i, 0))
    prev = pl.BlockSpec((BATCH, D_MODEL), lambda i: (jnp.maximum(i * (tm // BATCH) - 1, 0), 0))
    scan = pl.BlockSpec((tm // BATCH, RWKV_HEAD, LANE), lambda i: (i, 0, 0))
    consts = [g, mu, wr, wk, wv, w0, ww1, ww2, a0, wa1, wa2, wg1, wg2]
    scan_out = jax.ShapeDtypeStruct((T // BATCH, RWKV_HEAD, LANE), F32)
    return pl.pallas_call(
        _rwkv_front_kernel,
        grid=(T // tm,),
        in_specs=[row, prev] + [_const_spec(c.shape) for c in consts],
        out_specs=[scan] * 5 + [row],
        out_shape=[scan_out] * 5 + [jax.ShapeDtypeStruct((T, D_MODEL), F32)],
        compiler_params=_params(("parallel",)),
        name="rwkv_front",
    )(h, h, *consts)


def _rwkv_scan_kernel(r_ref, k_ref, v_ref, w_ref, a_ref, kk_ref, ka_ref, rk_ref, lg_ref, lb_ref,
                      y_ref, s_ref, pend_s, k2_s, av_s, bv_s, wr_s, br_s, kr_s):
    tl = r_ref.shape[0]
    n = RWKV_HEAD
    ng = n // SUBLANE

    @pl.when(pl.program_id(0) == 0)
    def _():
        s_ref[...] = jnp.zeros_like(s_ref)

    r = r_ref[...]
    k = k_ref[...]
    a = a_ref[...]
    kk = k * kk_ref[...]
    kk = kk / jnp.maximum(jnp.sqrt(jnp.sum(kk * kk, axis=1, keepdims=True)), 1e-12)
    k2 = k * (1.0 + (a - 1.0) * ka_ref[...])
    dec = jnp.exp(-jnp.exp(-_softplus(-w_ref[...]) - 0.5))
    bv = kk * a
    p = dec
    sh = 1
    while sh < tl:
        p = p * jnp.concatenate([jnp.ones((sh, n, LANE), F32), p[:tl - sh]], axis=0)
        sh *= 2
    p_prev = jnp.concatenate([jnp.ones((1, n, LANE), F32), p[:tl - 1]], axis=0)
    inv_p = 1.0 / p
    pend_s[...] = p[tl - 1]
    k2_s[...] = k2 * inv_p
    av_s[...] = -kk * p_prev
    bv_s[...] = bv * inv_p
    wr_s[...] = r * p
    br_s[...] = jnp.sum(bv * r, axis=1, keepdims=True)
    kr_s[...] = jnp.sum(k2 * r, axis=1, keepdims=True)

    def row(ref, t, j):
        return jnp.broadcast_to(ref[t, pl.ds(j, 1), :], (SUBLANE, LANE))

    def step(t, carry):
        def reduce_body(j, accs):
            sa, sy = accs
            a_b = row(av_s, t, j)
            wr_b = row(wr_s, t, j)
            sa_n, sy_n = [], []
            for vg in range(ng):
                sv = s_ref[j, vg * SUBLANE:(vg + 1) * SUBLANE, :]
                sa_n.append(sa[vg] + sv * a_b)
                sy_n.append(sy[vg] + sv * wr_b)
            return tuple(sa_n), tuple(sy_n)

        zeros = tuple(jnp.zeros((SUBLANE, LANE), F32) for _ in range(ng))
        sa, sy = lax.fori_loop(0, n, reduce_body, (zeros, zeros), unroll=8)
        br = jnp.broadcast_to(br_s[t], (SUBLANE, LANE))
        kr = jnp.broadcast_to(kr_s[t], (SUBLANE, LANE))
        vv = [v_ref[t, vg * SUBLANE:(vg + 1) * SUBLANE, :] for vg in range(ng)]
        for vg in range(ng):
            y_ref[t, vg * SUBLANE:(vg + 1) * SUBLANE, :] = sy[vg] + sa[vg] * br + vv[vg] * kr

        def update_body(j, c):
            b_b = row(bv_s, t, j)
            k_b = row(k2_s, t, j)
            for vg in range(ng):
                sl = slice(vg * SUBLANE, (vg + 1) * SUBLANE)
                s_ref[j, sl, :] = s_ref[j, sl, :] + sa[vg] * b_b + vv[vg] * k_b
            return c

        lax.fori_loop(0, n, update_body, 0, unroll=8)
        return carry

    lax.fori_loop(0, tl, step, 0)

    def rescale(j, c):
        s_ref[j] = s_ref[j] * pend_s[pl.ds(j, 1), :]
        return c

    lax.fori_loop(0, n, rescale, 0, unroll=8)

    y = y_ref[...]
    mean = jnp.mean(y, axis=1, keepdims=True)
    var = jnp.mean((y - mean) ** 2, axis=1, keepdims=True)
    y = (y - mean) * lax.rsqrt(var + RWKV_GN_EPS) * lg_ref[...] + lb_ref[...]
    bonus = jnp.sum(r * k2 * rk_ref[...], axis=1, keepdims=True) * v_ref[...]
    y_ref[...] = y + bonus


def _rwkv_scan(r, k, v, w, a, kk, ka, rk, lg, lb):
    L = r.shape[0]
    tl = SCAN_STEPS
    n = RWKV_HEAD
    blk = pl.BlockSpec((tl, n, LANE), lambda i: (i, 0, 0))
    par = _const_spec((n, LANE))
    big = pltpu.VMEM((tl, n, LANE), F32)
    small = pltpu.VMEM((tl, 1, LANE), F32)
    return pl.pallas_call(
        _rwkv_scan_kernel,
        grid=(L // tl,),
        in_specs=[blk] * 5 + [par] * 5,
        out_specs=blk,
        out_shape=jax.ShapeDtypeStruct((L, n, LANE), F32),
        scratch_shapes=[pltpu.VMEM((n, n, LANE), F32), pltpu.VMEM((n, LANE), F32), big, big, big, big,
                        small, small],
        compiler_params=_params(("arbitrary",)),
        name="rwkv_scan",
    )(r, k, v, w, a, kk, ka, rk, lg, lb)


def _head_param(p):
    return jnp.repeat(p.reshape(RWKV_HEADS, RWKV_HEAD).T, BATCH, axis=1)


def _prep_even(w_in, a_re, a_im, log_dt, b_re, b_im, c_re, c_im, w_g2, b_g2):
    assert S5_WIDTH == GLA_WIDTH == GLA_QK
    qk = GLA_HEADS * GLA_DK
    widths = (S5_WIDTH, qk, qk, GLA_WIDTH, GLA_RANK, GLA_WIDTH)
    edges = [sum(widths[:j]) for j in range(len(widths) + 1)]
    w_u, w_q, w_k, w_v, w_gl, w_r = (w_in[:, edges[j]:edges[j + 1]] for j in range(len(widths)))

    def pad_heads(w):
        w = w.reshape(-1, GLA_HEADS, GLA_DK)
        w = jnp.pad(w, ((0, 0), (0, 0), (0, GLA_DKP - GLA_DK)))
        return w.reshape(-1, GLA_HEADS * GLA_DKP)

    w_proj = jnp.concatenate([w_u, w_v, w_r, pad_heads(w_q), pad_heads(w_k),
                              jnp.pad(w_gl, ((0, 0), (0, LANE - GLA_RANK)))], axis=1)
    wg = jnp.pad(pad_heads(w_g2), ((0, LANE - GLA_RANK), (0, 0)))
    bg = pad_heads(b_g2[None, :])
    lm = pad_heads(jnp.ones((1, GLA_HEADS * GLA_DK), F32))

    dt = jnp.exp(log_dt)
    mag = jnp.exp(dt * a_re)
    abar_re = mag * jnp.cos(dt * a_im)
    abar_im = mag * jnp.sin(dt * a_im)
    nr = abar_re - 1.0
    den = a_re * a_re + a_im * a_im
    f_re = (nr * a_re + abar_im * a_im) / den
    f_im = (abar_im * a_re - nr * a_im) / den
    bbar_re = f_re[..., None] * b_re - f_im[..., None] * b_im
    bbar_im = f_re[..., None] * b_im + f_im[..., None] * b_re
    eye = jnp.eye(8, dtype=F32)
    gs = S5_GROUPS // S5_SLABS

    def bd(bb):
        bb = bb.reshape(S5_SLABS, gs, S5_STATE, S5_GROUP)
        return jnp.einsum('sgpc,gh->sgchp', bb, eye).reshape(S5_SLABS, gs * S5_GROUP, gs * S5_STATE)

    def cdm(cc):
        cc = cc.reshape(S5_SLABS, gs, S5_GROUP, S5_STATE)
        return jnp.einsum('sgcp,gh->sgphc', cc, eye).reshape(S5_SLABS, gs * S5_STATE, gs * S5_GROUP)

    bd_all = jnp.concatenate([bd(bbar_re), bd(bbar_im)], axis=2)
    cd_all = jnp.concatenate([cdm(c_re), -cdm(c_im)], axis=1)
    are = jnp.broadcast_to(abar_re.reshape(S5_SLABS, 1, gs * S5_STATE), (S5_SLABS, BATCH, gs * S5_STATE))
    aim = jnp.broadcast_to(abar_im.reshape(S5_SLABS, 1, gs * S5_STATE), (S5_SLABS, BATCH, gs * S5_STATE))
    return w_proj, wg, bg, lm, bd_all, cd_all, are, aim


def kernel(x, norm_mix_g, norm_ffn_g, final_g, e_w_in, e_w_out, s5_a_re, s5_a_im, s5_log_dt, s5_b_re, s5_b_im, s5_c_re, s5_c_im, s5_d, s5_w_glu, gla_w_g2, gla_b_g2, gla_norm_g, o_mu, o_w_r, o_w_k, o_w_v, o_w0, o_w_w1, o_w_w2, o_a0, o_w_a1, o_w_a2, o_w_g1, o_w_g2, o_k_k, o_k_a, o_r_k, o_lnx_g, o_lnx_b, o_w_o, peer_w_q, peer_sub_keys, peer_u, peer_v):
    bsz, L, D = x.shape
    assert bsz == BATCH and D == D_MODEL
    T = bsz * L
    row = lambda p: p.reshape(1, -1)
    h = x.transpose(1, 0, 2).reshape(T, D)

    w_proj, wg, bg, lm, bd_all, cd_all, are, aim = _prep_even(
        e_w_in[0], s5_a_re[0], s5_a_im[0], s5_log_dt[0], s5_b_re[0], s5_b_im[0],
        s5_c_re[0], s5_c_im[0], gla_w_g2[0], gla_b_g2[0])
    bf = lambda p: p.astype(BF16)
    proj = _even_front(h, row(norm_mix_g[0]), bf(w_proj))
    y_s5 = _s5(proj, bf(bd_all), bf(cd_all), are, aim, row(s5_d[0]), bf(s5_w_glu[0]))
    y_gla = _gla(proj, wg, bg, lm, row(gla_norm_g[0]))
    w_out = bf(e_w_out[0])
    h, hb, hbt = _mix_out(h, [y_s5, y_gla], [w_out[:S5_WIDTH], w_out[S5_WIDTH:]], None, row(norm_ffn_g[0]))
    h = _peer_layer(h, hb, hbt, peer_w_q[0], peer_sub_keys[0], peer_u[0], peer_v[0], row(final_g), False)

    r, k, v, w, a, g = _rwkv_front(
        h, row(norm_mix_g[1]), o_mu[0], bf(o_w_r[0]), bf(o_w_k[0]), bf(o_w_v[0]), row(o_w0[0]),
        bf(o_w_w1[0]), bf(o_w_w2[0]), row(o_a0[0]), bf(o_w_a1[0]), bf(o_w_a2[0]), bf(o_w_g1[0]), bf(o_w_g2[0]))
    y = _rwkv_scan(r, k, v, w, a, _head_param(o_k_k[0]), _head_param(o_k_a[0]),
                   _head_param(o_r_k[0].reshape(-1)), _head_param(o_lnx_g[0]), _head_param(o_lnx_b[0]))
    h, hb, hbt = _mix_out(h, [y], [bf(o_w_o[0])], g, row(norm_ffn_g[1]))
    out = _peer_layer(h, hb, hbt, peer_w_q[1], peer_sub_keys[1], peer_u[1], peer_v[1], row(final_g), True)
    return out.reshape(L, bsz, D).transpose(1, 0, 2)
```

```python
import functools
import math

import jax
import jax.numpy as jnp
from jax import lax
from jax.experimental import pallas as pl
from jax.experimental.pallas import tpu as pltpu

F32 = jnp.float32
BF16 = jnp.bfloat16
HIGHEST = lax.Precision.HIGHEST

D_MODEL = 1024
BATCH = 8
NORM_EPS = 1e-6
S5_WIDTH = 512
S5_GROUP = 16
S5_GROUPS = 32
S5_STATE = 64
S5_SLABS = 4
S5_SLAB_STATES = S5_GROUPS // S5_SLABS * S5_STATE
GLA_HEADS = 4
GLA_DV = 128
GLA_DK = 64
GLA_DKP = 128
GLA_RANK = 16
GLA_TAU = 16.0
GLA_CHUNK = 64
RWKV_HEAD = 64
RWKV_HEADS = 16
RWKV_GN_EPS = 64e-5
PEER_HEADS = 8
PEER_NKEYS = 128
PEER_TOPK = 16
PEER_TILE = 1024
PEER_TOKENS = 512
GATE_STEP = 2.0
LANE = 128
SUBLANE = 8
VMEM_LIMIT = 56 * 1024 * 1024
CHUNK_PITCH = PEER_NKEYS + SUBLANE
E2_ROWS = PEER_NKEYS + 4 * SUBLANE
E2_ROW0 = 2 * SUBLANE
E1_ROWS = PEER_NKEYS + 2 * SUBLANE
E1_ROW0 = SUBLANE
ROW_TILE = 256
SCAN_STEPS = 32

GLA_WIDTH = GLA_HEADS * GLA_DV
GLA_QK = GLA_HEADS * GLA_DKP
PROJ_GATE_COL = S5_WIDTH + 2 * GLA_WIDTH + 2 * GLA_QK
PROJ_W = PROJ_GATE_COL + LANE


def _dot(a, b, hi=False):
    if hi:
        return jnp.dot(a, b, preferred_element_type=F32, precision=HIGHEST)
    return jnp.dot(a.astype(BF16), b.astype(BF16), preferred_element_type=F32)


def _dot_nt(a, b):
    return lax.dot_general(a.astype(BF16), b.astype(BF16), (((1,), (1,)), ((), ())),
                           preferred_element_type=F32)


def _rms(x, g):
    return x * lax.rsqrt(jnp.mean(x * x, axis=-1, keepdims=True) + NORM_EPS) * g


def _sigmoid(x):
    return 1.0 / (1.0 + jnp.exp(-x))


def _softplus(x):
    return jnp.maximum(x, 0.0) + jnp.log(1.0 + jnp.exp(-jnp.abs(x)))


def _gelu(x):
    return 0.5 * x * (1.0 + lax.erf(x * (1.0 / math.sqrt(2.0))))


def _const_spec(shape):
    nd = len(shape)
    return pl.BlockSpec(shape, lambda *_: (0,) * nd)


def _params(sem, flags=None):
    return pltpu.CompilerParams(dimension_semantics=sem, vmem_limit_bytes=VMEM_LIMIT, flags=flags)


def _even_front_kernel(h_ref, g_ref, w_ref, o_ref):
    xn = _rms(h_ref[...], g_ref[...])
    o_ref[...] = _dot(xn, w_ref[...])


def _even_front(h, g, w):
    T = h.shape[0]
    tm = ROW_TILE
    return pl.pallas_call(
        _even_front_kernel,
        grid=(T // tm,),
        in_specs=[pl.BlockSpec((tm, D_MODEL), lambda i: (i, 0)),
                  _const_spec((1, D_MODEL)),
                  _const_spec((D_MODEL, PROJ_W))],
        out_specs=pl.BlockSpec((tm, PROJ_W), lambda i: (i, 0)),
        out_shape=jax.ShapeDtypeStruct((T, PROJ_W), F32),
        compiler_params=_params(("parallel",)),
        name="even_front",
    )(h, g, w)


def _s5_kernel(u_ref, bd_ref, cd_ref, are_ref, aim_ref, d_ref, wglu_ref, o_ref, bu_ref, st_ref):
    rows = u_ref.shape[0]
    steps = rows // BATCH
    half = S5_SLAB_STATES

    @pl.when(pl.program_id(0) == 0)
    def _():
        st_ref[...] = jnp.zeros_like(st_ref)

    u = u_ref[...]
    for s in range(S5_SLABS):
        bu_ref[s] = _dot(u[:, s * LANE:(s + 1) * LANE], bd_ref[s])

    def step(t, carry):
        r0 = pl.multiple_of(t * BATCH, BATCH)
        out = []
        for s in range(S5_SLABS):
            sre, sim = carry[2 * s], carry[2 * s + 1]
            ar, ai = are_ref[s], aim_ref[s]
            bre = bu_ref[s, pl.ds(r0, BATCH), 0:half]
            bim = bu_ref[s, pl.ds(r0, BATCH), half:2 * half]
            nre = ar * sre - ai * sim + bre
            nim = ar * sim + ai * sre + bim
            bu_ref[s, pl.ds(r0, BATCH), 0:half] = nre
            bu_ref[s, pl.ds(r0, BATCH), half:2 * half] = nim
            out += [nre, nim]
        return tuple(out)

    init = tuple(st_ref[j] for j in range(2 * S5_SLABS))
    fin = lax.fori_loop(0, steps, step, init)
    for j in range(2 * S5_SLABS):
        st_ref[j] = fin[j]

    y = jnp.concatenate([_dot(bu_ref[s], cd_ref[s]) for s in range(S5_SLABS)], axis=1)
    y = y + d_ref[...] * u
    g = _gelu(y)
    o_ref[...] = g * _sigmoid(_dot(g, wglu_ref[...]))


def _s5(proj, bd, cd, are, aim, d, wglu):
    T = proj.shape[0]
    rows = ROW_TILE
    return pl.pallas_call(
        _s5_kernel,
        grid=(T // rows,),
        in_specs=[pl.BlockSpec((rows, S5_WIDTH), lambda i: (i, 0)),
                  _const_spec(bd.shape), _const_spec(cd.shape),
                  _const_spec(are.shape), _const_spec(aim.shape),
                  _const_spec(d.shape), _const_spec(wglu.shape)],
        out_specs=pl.BlockSpec((rows, S5_WIDTH), lambda i: (i, 0)),
        out_shape=jax.ShapeDtypeStruct((T, S5_WIDTH), F32),
        scratch_shapes=[pltpu.VMEM((S5_SLABS, rows, 2 * S5_SLAB_STATES), F32),
                        pltpu.VMEM((2 * S5_SLABS, BATCH, S5_SLAB_STATES), F32)],
        compiler_params=_params(("arbitrary",)),
        name="s5",
    )(proj, bd, cd, are, aim, d, wglu)


def _gla_kernel(v_ref, r_ref, q_ref, k_ref, gl_ref, wg_ref, bg_ref, lm_ref, ng_ref, o_ref, s_ref):
    rows = q_ref.shape[0]
    width = GLA_HEADS * GLA_DKP

    @pl.when(pl.program_id(0) == 0)
    def _():
        s_ref[...] = jnp.zeros_like(s_ref)

    z = _dot(gl_ref[...], wg_ref[...], True) + bg_ref[...]
    la = -_softplus(-z) * (1.0 / GLA_TAU) * lm_ref[...]
    b = la
    sh = BATCH
    while sh < rows:
        b = b + jnp.concatenate([jnp.zeros((sh, width), F32), b[:rows - sh]], axis=0)
        sh *= 2
    blast = b[rows - BATCH:rows]
    blast_t = jnp.concatenate([blast] * GLA_CHUNK, axis=0)
    q = q_ref[...] * (GLA_DK ** -0.5)
    k = k_ref[...]
    qd = q * jnp.exp(b)
    kd = k * jnp.exp(-b)
    kend = k * jnp.exp(blast_t - b)

    ri = lax.broadcasted_iota(jnp.int32, (rows, rows), 0)
    ci = lax.broadcasted_iota(jnp.int32, (rows, rows), 1)
    causal = jnp.logical_and(((ri - ci) & (BATCH - 1)) == 0, ri >= ci)
    rowb = lax.broadcasted_iota(jnp.int32, (rows, GLA_DKP), 0) & (BATCH - 1)
    laneb = lax.broadcasted_iota(jnp.int32, (GLA_DKP, rows), 1) & (BATCH - 1)

    for h in range(GLA_HEADS):
        sl = slice(h * GLA_DKP, (h + 1) * GLA_DKP)
        vh = v_ref[:, h * GLA_DV:(h + 1) * GLA_DV]
        qh, kdh, keh, bh = qd[:, sl], kd[:, sl], kend[:, sl], b[:, sl]
        att = jnp.where(causal, _dot_nt(qh, kdh), 0.0)
        o = _dot(att, vh)
        qexp = jnp.concatenate([jnp.where(rowb == bb, qh, 0.0) for bb in range(BATCH)], axis=1)
        o = o + _dot(qexp, s_ref[h])
        ket = keh.T
        kexp_t = jnp.concatenate([jnp.where(laneb == bb, ket, 0.0) for bb in range(BATCH)], axis=0)
        kv = _dot(kexp_t, vh)
        bt = bh[rows - GLA_DKP:rows].T
        dec = jnp.concatenate(
            [jnp.broadcast_to(jnp.exp(bt[:, GLA_DKP - BATCH + bb:GLA_DKP - BATCH + bb + 1]),
                              (GLA_DKP, GLA_DV)) for bb in range(BATCH)], axis=0)
        s_ref[h] = s_ref[h] * dec + kv
        o = o * lax.rsqrt(jnp.mean(o * o, axis=-1, keepdims=True) + NORM_EPS)
        o = o * ng_ref[:, h * GLA_DV:(h + 1) * GLA_DV]
        rr = r_ref[:, h * GLA_DV:(h + 1) * GLA_DV]
        o_ref[:, h * GLA_DV:(h + 1) * GLA_DV] = o * (rr * _sigmoid(rr))


def _gla(proj, wg, bg, lm, ng):
    T = proj.shape[0]
    rows = GLA_CHUNK * BATCH
    width = GLA_HEADS * GLA_DKP
    return pl.pallas_call(
        _gla_kernel,
        grid=(T // rows,),
        in_specs=[pl.BlockSpec((rows, GLA_WIDTH), lambda i: (i, 1)),
                  pl.BlockSpec((rows, GLA_WIDTH), lambda i: (i, 2)),
                  pl.BlockSpec((rows, width), lambda i: (i, 3)),
                  pl.BlockSpec((rows, width), lambda i: (i, 4)),
                  pl.BlockSpec((rows, LANE), lambda i: (i, PROJ_GATE_COL // LANE)),
                  _const_spec(wg.shape), _const_spec(bg.shape), _const_spec(lm.shape),
                  _const_spec(ng.shape)],
        out_specs=pl.BlockSpec((rows, GLA_HEADS * GLA_DV), lambda i: (i, 0)),
        out_shape=jax.ShapeDtypeStruct((T, GLA_HEADS * GLA_DV), F32),
        scratch_shapes=[pltpu.VMEM((GLA_HEADS, BATCH * GLA_DKP, GLA_DV), F32)],
        compiler_params=_params(("arbitrary",)),
        name="gla",
    )(proj, proj, proj, proj, proj, wg, bg, lm, ng)


def _mix_out_kernel(n_in, has_gate, *refs):
    h_ref = refs[0]
    y_refs = refs[1:1 + n_in]
    w_refs = refs[1 + n_in:1 + 2 * n_in]
    rest = refs[1 + 2 * n_in:]
    gate_ref = rest[0] if has_gate else None
    g_ref, ho_ref, hbt_ref = rest[1:] if has_gate else rest
    acc = h_ref[...]
    for y_ref, w_ref in zip(y_refs, w_refs):
        if len(y_ref.shape) == 3:
            y = jnp.concatenate([_scan_to_rows(y_ref, t0) for t0 in range(0, y_ref.shape[0], SCAN_BLOCK)],
                                axis=0)
        else:
            y = y_ref[...]
        if has_gate:
            y = y * gate_ref[...]
        acc = acc + _dot(y, w_ref[...])
    ho_ref[...] = acc
    hbt_ref[...] = _rms(acc, g_ref[...]).T.astype(BF16)


def _mix_out(h, ys, ws, gate, g):
    T = h.shape[0]
    tm = ROW_TILE
    n_in = len(ys)
    row = pl.BlockSpec((tm, D_MODEL), lambda i: (i, 0))
    in_specs = [row]
    in_specs += [pl.BlockSpec((tm, y.shape[1]), lambda i: (i, 0)) if y.ndim == 2 else
                 pl.BlockSpec((tm // BATCH,) + y.shape[1:], lambda i: (i, 0, 0)) for y in ys]
    in_specs += [_const_spec(w.shape) for w in ws]
    args = [h, *ys, *ws]
    if gate is not None:
        in_specs.append(row)
        args.append(gate)
    in_specs.append(_const_spec((1, D_MODEL)))
    args.append(g)
    return pl.pallas_call(
        functools.partial(_mix_out_kernel, n_in, gate is not None),
        grid=(T // tm,),
        in_specs=in_specs,
        out_specs=[row, pl.BlockSpec((D_MODEL, tm), lambda i: (0, i))],
        out_shape=[jax.ShapeDtypeStruct((T, D_MODEL), F32),
                   jax.ShapeDtypeStruct((D_MODEL, T), BF16)],
        compiler_params=_params(("parallel",)),
        name="mix_out",
    )(*args)


def _tree_sum(xs):
    while len(xs) > 1:
        xs = [xs[i] + xs[i + 1] for i in range(0, len(xs) - 1, 2)] + ([xs[-1]] if len(xs) % 2 else [])
    return xs[0]


def _insert(tops, x):
    out = []
    for t in tops:
        out.append(jnp.maximum(t, x))
        x = jnp.minimum(t, x)
    return out


def _compare_exchange(v, i, l, descending):
    hi, lo = jnp.maximum(v[i], v[l]), jnp.minimum(v[i], v[l])
    v[i], v[l] = (hi, lo) if descending else (lo, hi)


def _bitonic_merge_desc(v):
    v = list(v)
    j = len(v) // 2
    while j >= 1:
        for i in range(len(v)):
            if i ^ j > i:
                _compare_exchange(v, i, i ^ j, True)
        j //= 2
    return v


def _bitonic_sort_desc(v):
    v = list(v)
    k = 2
    while k <= len(v):
        j = k // 2
        while j >= 1:
            for i in range(len(v)):
                if i ^ j > i:
                    _compare_exchange(v, i, i ^ j, (i & k) == 0)
            j //= 2
        k *= 2
    return v


def _top_values(scr_ref):
    tops = None
    for g in range(PEER_NKEYS // PEER_TOPK):
        grp = _bitonic_sort_desc([scr_ref[pl.ds(g * PEER_TOPK + i, SUBLANE, stride=CHUNK_PITCH), :]
                                  for i in range(PEER_TOPK)])
        if tops is None:
            tops = grp
        else:
            tops = _bitonic_merge_desc([jnp.maximum(tops[i], grp[PEER_TOPK - 1 - i])
                                        for i in range(PEER_TOPK)])
    return tops


def _pick(vals, bits):
    if not bits:
        return vals[0]
    half = len(vals) // 2
    return jnp.where(bits[0], _pick(vals[half:], bits[1:]), _pick(vals[:half], bits[1:]))


def _count_greater(x, s):
    bits = []
    step = PEER_TOPK // 2
    while step >= 1:
        vals = [s[base + step - 1] for base in range(0, PEER_TOPK, 2 * step)]
        bits.append(_pick(vals, bits) > x)
        step //= 2
    return bits, s[PEER_TOPK - 1] > x


def _peer_topk_kernel(hn_ref, wq_ref, keys_ref, n1_ref, e1_ref, r2_ref, e2_ref,
                      scr1, scr2, stat_ref):
    nch = hn_ref.shape[1] // LANE
    qt = _dot(wq_ref[...], hn_ref[...])
    for c, scr in enumerate((scr1, scr2)):
        st = _dot(keys_ref[0, c], qt[c * LANE:(c + 1) * LANE, :])
        for j in range(nch):
            scr[j * CHUNK_PITCH:j * CHUNK_PITCH + PEER_NKEYS, :] = st[:, j * LANE:(j + 1) * LANE]
    a = _top_values(scr1)
    b = _top_values(scr2)
    pairs = [(i, j) for i in range(PEER_TOPK) for j in range(PEER_TOPK) if (i + 1) * (j + 1) <= PEER_TOPK]
    tops = [jnp.full((SUBLANE, LANE), -jnp.inf, F32) for _ in range(PEER_TOPK)]
    for i, j in pairs:
        tops = _insert(tops, a[i] + b[j])
    thr = tops[PEER_TOPK - 1]
    z = jnp.zeros((SUBLANE, LANE), F32)
    for t in tops:
        z = z + jnp.exp(t - tops[0])
    inv_z = 1.0 / z
    cnt = [jnp.zeros((SUBLANE, LANE), F32) for _ in range(PEER_TOPK)]
    for i, j in pairs:
        cnt[i] = cnt[i] + jnp.where(a[i] + b[j] >= thr, GATE_STEP, 0.0)

    for i in range(PEER_TOPK):
        stat_ref[i] = a[i]
        stat_ref[PEER_TOPK + i] = b[i]
        stat_ref[2 * PEER_TOPK + i] = cnt[i]
    stat_ref[3 * PEER_TOPK] = inv_z
    weights = [GATE_STEP * (PEER_TOPK >> (i + 1)) for i in range(4)] + [GATE_STEP]

    def chunk(c, carry):
        def stat(i):
            return jnp.broadcast_to(stat_ref[i, pl.ds(c, 1), :], (SUBLANE, LANE))
        a_c = [stat(i) for i in range(PEER_TOPK)]
        b_c = [stat(PEER_TOPK + i) for i in range(PEER_TOPK)]
        cnt_c = [stat(2 * PEER_TOPK + i) for i in range(PEER_TOPK)]
        inv_z_c = stat(3 * PEER_TOPK)
        e1_ref[0, c, 0:E1_ROW0, :] = jnp.zeros((E1_ROW0, LANE), F32)
        e1_ref[0, c, E1_ROW0 + PEER_NKEYS:E1_ROWS, :] = jnp.zeros((E1_ROWS - E1_ROW0 - PEER_NKEYS, LANE), F32)
        for g in range(PEER_NKEYS // SUBLANE):
            rows = pl.ds(pl.multiple_of(c * CHUNK_PITCH, SUBLANE) + g * SUBLANE, SUBLANE)
            x = scr1[rows, :]
            bits, below = _count_greater(x, a_c)
            n1_ref[0, c, g * SUBLANE:(g + 1) * SUBLANE, :] = jnp.where(below, 0.0, _pick(cnt_c, bits))
            e1_ref[0, c, E1_ROW0 + g * SUBLANE:E1_ROW0 + (g + 1) * SUBLANE, :] = jnp.exp(x - a_c[0]) * inv_z_c
            x = scr2[rows, :]
            bits, below = _count_greater(x, b_c)
            rank = _tree_sum([jnp.where(m, w, 0.0) for m, w in zip(bits + [below], weights)])
            scr1[rows, :] = rank
            scr2[rows, :] = jnp.exp(x - b_c[0])
        return carry

    lax.fori_loop(0, nch, chunk, 0)
    for j in range(nch):
        rows = slice(j * CHUNK_PITCH, j * CHUNK_PITCH + PEER_NKEYS)
        r2_ref[0, j] = scr1[rows, :].astype(BF16)
        e2_ref[0, j, 0:E2_ROW0, :] = jnp.zeros((E2_ROW0, LANE), BF16)
        e2_ref[0, j, E2_ROW0:E2_ROW0 + PEER_NKEYS, :] = scr2[rows, :].astype(BF16)
        e2_ref[0, j, E2_ROW0 + PEER_NKEYS:E2_ROWS, :] = jnp.zeros((E2_ROWS - E2_ROW0 - PEER_NKEYS, LANE), BF16)


def _peer_topk(hn, wq, keys):
    T = hn.shape[1]
    tt = SUBLANE * LANE
    nch = T // LANE
    shape = (PEER_HEADS, nch, PEER_NKEYS, LANE)
    spec = pl.BlockSpec((1, SUBLANE, PEER_NKEYS, LANE), lambda i, h: (h, i, 0, 0))
    return pl.pallas_call(
        _peer_topk_kernel,
        grid=(T // tt, PEER_HEADS),
        in_specs=[pl.BlockSpec((D_MODEL, tt), lambda i, h: (0, i)),
                  pl.BlockSpec((2 * LANE, D_MODEL), lambda i, h: (h, 0)),
                  pl.BlockSpec((1, 2, PEER_NKEYS, LANE), lambda i, h: (h, 0, 0, 0))],
        out_specs=[spec, pl.BlockSpec((1, SUBLANE, E1_ROWS, LANE), lambda i, h: (h, i, 0, 0)), spec,
                   pl.BlockSpec((1, SUBLANE, E2_ROWS, LANE), lambda i, h: (h, i, 0, 0))],
        out_shape=[jax.ShapeDtypeStruct(shape, F32),
                   jax.ShapeDtypeStruct((PEER_HEADS, nch, E1_ROWS, LANE), F32),
                   jax.ShapeDtypeStruct(shape, BF16),
                   jax.ShapeDtypeStruct((PEER_HEADS, nch, E2_ROWS, LANE), BF16)],
        scratch_shapes=[pltpu.VMEM((SUBLANE * CHUNK_PITCH, LANE), F32),
                        pltpu.VMEM((SUBLANE * CHUNK_PITCH, LANE), F32),
                        pltpu.VMEM((3 * PEER_TOPK + 1, SUBLANE, LANE), F32)],
        compiler_params=_params(("parallel", "arbitrary")),
        name="peer_topk",
    )(hn, wq, keys)


def _row_bf16(ref, h, c, i1, dep=None):
    packed_rows = 2 * SUBLANE
    row = ref[h, c, pl.ds(i1, 1), :]
    if dep is not None:
        row = row + dep
    x = jnp.broadcast_to(row, (packed_rows, LANE)).astype(BF16)
    return jnp.concatenate([x] * (PEER_NKEYS // packed_rows), axis=0)


def _zero_after(x):
    bits = pltpu.bitcast(x[0:2 * SUBLANE, :], jnp.uint32)
    sixteen = jnp.uint32(16)
    zero_bits = lax.shift_right_logical(lax.shift_right_logical(bits, sixteen), sixteen)
    return pltpu.bitcast(zero_bits, F32)[0:1, :]


def _peer_main_kernel(final, hbt_ref, hres_ref, u0_ref, ub_ref, ua_ref, vta_ref, vtb_ref,
                      n1_ref, e1_ref, r2_ref, e2_ref, fg_ref, o_ref,
                      acc_ref, hga_ref, hgb_ref, hta_ref, htb_ref):
    e = pl.program_id(1)
    n_pairs = pl.num_programs(1) - 1
    tt = hbt_ref.shape[1]
    n_i1 = PEER_TILE // PEER_NKEYS
    n_ch = tt // LANE
    zero = jnp.zeros((PEER_NKEYS, LANE), BF16)

    @pl.when(e == 0)
    def _():
        acc_ref[...] = jnp.zeros_like(acc_ref)
        hgb_ref[...] = jnp.zeros_like(hgb_ref)
        hta_ref[...] = _dot(u0_ref[...], hbt_ref[...])

    def gates(tile, ht_ref, hg_ref, c, dep):
        for il in range(n_i1):
            i1 = tile * n_i1 + il
            gate = None
            for h in range(PEER_HEADS):
                code = _row_bf16(n1_ref, h, c, i1, dep if h == 0 else None) - r2_ref[h, c]
                w = _row_bf16(e1_ref, h, c, i1 + E1_ROW0) * e2_ref[h, c, E2_ROW0:E2_ROW0 + PEER_NKEYS, :]
                term = jnp.maximum(jnp.minimum(w, code), zero)
                gate = term if gate is None else gate + term
            rows = slice(il * PEER_NKEYS, (il + 1) * PEER_NKEYS)
            cols = slice(c * LANE, (c + 1) * LANE)
            hg = _gelu(ht_ref[rows, cols]).astype(BF16) * gate
            hg_ref[rows, cols] = hg
            dep = _zero_after(hg)
        return dep

    def phase(tile, ht_ref, hg_ref, vt_prev_ref, hg_prev_ref, u_next_ref, ht_next_ref, dep):
        half = tt // 2
        for piece in range(4):
            cols = slice((piece % 2) * half, (piece % 2 + 1) * half)
            if piece < 2:
                acc_ref[:, cols] += _dot(vt_prev_ref[...], hg_prev_ref[:, cols])
            else:
                ht_next_ref[:, cols] = _dot(u_next_ref[...], hbt_ref[:, cols])
            for c in range(piece * n_ch // 4, (piece + 1) * n_ch // 4):
                dep = gates(tile, ht_ref, hg_ref, c, dep)
        return dep

    @pl.when(e < n_pairs)
    def _():
        dep = phase(2 * e, hta_ref, hga_ref, vtb_ref, hgb_ref, ub_ref, htb_ref, None)
        phase(2 * e + 1, htb_ref, hgb_ref, vta_ref, hga_ref, ua_ref, hta_ref, dep)

    @pl.when(e == n_pairs)
    def _():
        acc = acc_ref[...] + _dot(vtb_ref[...], hgb_ref[...])
        out = hres_ref[...] + acc.T
        if final:
            out = _rms(out, fg_ref[...])
        o_ref[...] = out


def _peer_main(hbt, hres, u, vt, n1, e1, r2, e2, fg, final):
    T = hres.shape[0]
    E = u.shape[0]
    tt = PEER_TOKENS
    n_tiles = E // PEER_TILE
    n_pairs = n_tiles // 2
    big_spec = pl.BlockSpec((PEER_HEADS, tt // LANE, PEER_NKEYS, LANE), lambda i, e: (0, i, 0, 0))
    tile_of = lambda f: (lambda i, e: (jnp.clip(f(e), 0, n_tiles - 1), 0, 0))
    vt_spec = lambda f: pl.BlockSpec((D_MODEL, PEER_TILE), lambda i, e: tile_of(f)(i, e)[:2])
    u_spec = lambda f: pl.BlockSpec((PEER_TILE, D_MODEL), lambda i, e: tile_of(f)(i, e)[:2])
    return pl.pallas_call(
        functools.partial(_peer_main_kernel, final),
        grid=(T // tt, n_pairs + 1),
        in_specs=[pl.BlockSpec((D_MODEL, tt), lambda i, e: (0, i)),
                  pl.BlockSpec((tt, D_MODEL), lambda i, e: (i, 0)),
                  u_spec(lambda e: 0), u_spec(lambda e: 2 * e + 1), u_spec(lambda e: 2 * e + 2),
                  vt_spec(lambda e: 2 * e), vt_spec(lambda e: 2 * e - 1),
                  big_spec,
                  pl.BlockSpec((PEER_HEADS, tt // LANE, E1_ROWS, LANE), lambda i, e: (0, i, 0, 0)),
                  big_spec,
                  pl.BlockSpec((PEER_HEADS, tt // LANE, E2_ROWS, LANE), lambda i, e: (0, i, 0, 0)),
                  _const_spec((1, D_MODEL))],
        out_specs=pl.BlockSpec((tt, D_MODEL), lambda i, e: (i, 0)),
        out_shape=jax.ShapeDtypeStruct((T, D_MODEL), F32),
        scratch_shapes=[pltpu.VMEM((D_MODEL, tt), F32),
                        pltpu.VMEM((PEER_TILE, tt), BF16), pltpu.VMEM((PEER_TILE, tt), BF16),
                        pltpu.VMEM((PEER_TILE, tt), F32), pltpu.VMEM((PEER_TILE, tt), F32)],
        compiler_params=_params(("parallel", "arbitrary")),
        name="peer_main",
    )(hbt, hres, u, u, u, vt, vt, n1, e1, r2, e2, fg)


def _peer_layer(h, hbt, wq, keys, u_tab, v_tab, fg, final):
    n1, e1, r2, e2 = _peer_topk(hbt, wq.T.astype(BF16), keys.astype(BF16))
    vt = v_tab.astype(BF16).reshape(-1, PEER_TILE, D_MODEL).transpose(0, 2, 1).reshape(-1, PEER_TILE)
    return _peer_main(hbt, h, u_tab.astype(BF16), vt, n1, e1, r2, e2, fg, final)


SCAN_BLOCK = LANE // BATCH


def _swap_groups(slabs):
    a = list(slabs)
    group = lax.broadcasted_iota(jnp.int32, a[0].shape, 1) // BATCH
    s = SCAN_BLOCK // 2
    while s >= 1:
        upper = (group & s) != 0
        for i in range(SCAN_BLOCK):
            if i & s == 0:
                ai, aj = a[i], a[i + s]
                a[i] = jnp.where(upper, pltpu.roll(aj, s * BATCH, axis=1), ai)
                a[i + s] = jnp.where(upper, aj, pltpu.roll(ai, LANE - s * BATCH, axis=1))
        s //= 2
    return a


def _rows_to_scan(x, ref):
    block_rows = SCAN_BLOCK * BATCH
    for r0 in range(0, x.shape[0], block_rows):
        z = x[r0:r0 + block_rows].T
        slabs = [z[h * RWKV_HEAD:(h + 1) * RWKV_HEAD, :] for h in range(RWKV_HEADS)]
        for t, blk in enumerate(_swap_groups(slabs)):
            ref[r0 // BATCH + t] = blk


def _scan_to_rows(ref, t0):
    slabs = _swap_groups([ref[t0 + t] for t in range(SCAN_BLOCK)])
    return jnp.concatenate(slabs, axis=0).T


def _rwkv_front_kernel(h_ref, hp_ref, g_ref, mu_ref, wr_ref, wk_ref, wv_ref, w0_ref, ww1_ref, ww2_ref,
                       a0_ref, wa1_ref, wa2_ref, wg1_ref, wg2_ref,
                       r_ref, k_ref, v_ref, w_ref, a_ref, go_ref):
    g = g_ref[...]
    x = _rms(h_ref[...], g)
    xp8 = _rms(hp_ref[...], g)
    xp8 = jnp.where(pl.program_id(0) == 0, 0.0, xp8)
    rows = x.shape[0]
    xprev = jnp.concatenate([xp8, x[:rows - BATCH]], axis=0)
    xx = xprev - x
    mix = lambda j: x + xx * mu_ref[j:j + 1, :]
    _rows_to_scan(_dot(mix(0), wr_ref[...]), r_ref)
    _rows_to_scan(w0_ref[...] + _dot(jnp.tanh(_dot(mix(1), ww1_ref[...])), ww2_ref[...]), w_ref)
    _rows_to_scan(_dot(mix(2), wk_ref[...]), k_ref)
    _rows_to_scan(_dot(mix(3), wv_ref[...]), v_ref)
    _rows_to_scan(_sigmoid(a0_ref[...] + _dot(_dot(mix(4), wa1_ref[...]), wa2_ref[...])), a_ref)
    go_ref[...] = _dot(_sigmoid(_dot(mix(5), wg1_ref[...])), wg2_ref[...])


def _rwkv_front(h, g, mu, wr, wk, wv, w0, ww1, ww2, a0, wa1, wa2, wg1, wg2):
    assert RWKV_HEADS == SCAN_BLOCK
    T = h.shape[0]
    tm = 2 * SCAN_BLOCK * BATCH
    row = pl.BlockSpec((tm, D_MODEL), lambda i: (i, 0))
    prev = pl.BlockSpec((BATCH, D_MODEL), lambda i: (jnp.maximum(i * (tm // BATCH) - 1, 0), 0))
    scan = pl.BlockSpec((tm // BATCH, RWKV_HEAD, LANE), lambda i: (i, 0, 0))
    consts = [g, mu, wr, wk, wv, w0, ww1, ww2, a0, wa1, wa2, wg1, wg2]
    scan_out = jax.ShapeDtypeStruct((T // BATCH, RWKV_HEAD, LANE), F32)
    return pl.pallas_call(
        _rwkv_front_kernel,
        grid=(T // tm,),
        in_specs=[row, prev] + [_const_spec(c.shape) for c in consts],
        out_specs=[scan] * 5 + [row],
        out_shape=[scan_out] * 5 + [jax.ShapeDtypeStruct((T, D_MODEL), F32)],
        compiler_params=_params(("parallel",)),
        name="rwkv_front",
    )(h, h, *consts)


def _rwkv_scan_kernel(r_ref, k_ref, v_ref, w_ref, a_ref, kk_ref, ka_ref, rk_ref, lg_ref, lb_ref,
                      y_ref, s_ref, pend_s, k2_s, av_s, bv_s, wr_s, br_s, kr_s):
    tl = r_ref.shape[0]
    n = RWKV_HEAD
    ng = n // SUBLANE

    @pl.when(pl.program_id(0) == 0)
    def _():
        s_ref[...] = jnp.zeros_like(s_ref)

    r = r_ref[...]
    k = k_ref[...]
    a = a_ref[...]
    kk = k * kk_ref[...]
    kk = kk / jnp.maximum(jnp.sqrt(jnp.sum(kk * kk, axis=1, keepdims=True)), 1e-12)
    k2 = k * (1.0 + (a - 1.0) * ka_ref[...])
    dec = jnp.exp(-jnp.exp(-_softplus(-w_ref[...]) - 0.5))
    bv = kk * a
    p = dec
    sh = 1
    while sh < tl:
        p = p * jnp.concatenate([jnp.ones((sh, n, LANE), F32), p[:tl - sh]], axis=0)
        sh *= 2
    p_prev = jnp.concatenate([jnp.ones((1, n, LANE), F32), p[:tl - 1]], axis=0)
    inv_p = 1.0 / p
    pend_s[...] = p[tl - 1]
    k2_s[...] = k2 * inv_p
    av_s[...] = -kk * p_prev
    bv_s[...] = bv * inv_p
    wr_s[...] = r * p
    br_s[...] = jnp.sum(bv * r, axis=1, keepdims=True)
    kr_s[...] = jnp.sum(k2 * r, axis=1, keepdims=True)

    def row(ref, t, j):
        return jnp.broadcast_to(ref[t, pl.ds(j, 1), :], (SUBLANE, LANE))

    def step(t, carry):
        def reduce_body(j, accs):
            sa, sy = accs
            a_b = row(av_s, t, j)
            wr_b = row(wr_s, t, j)
            sa_n, sy_n = [], []
            for vg in range(ng):
                sv = s_ref[j, vg * SUBLANE:(vg + 1) * SUBLANE, :]
                sa_n.append(sa[vg] + sv * a_b)
                sy_n.append(sy[vg] + sv * wr_b)
            return tuple(sa_n), tuple(sy_n)

        zeros = tuple(jnp.zeros((SUBLANE, LANE), F32) for _ in range(ng))
        sa, sy = lax.fori_loop(0, n, reduce_body, (zeros, zeros), unroll=8)
        br = jnp.broadcast_to(br_s[t], (SUBLANE, LANE))
        kr = jnp.broadcast_to(kr_s[t], (SUBLANE, LANE))
        vv = [v_ref[t, vg * SUBLANE:(vg + 1) * SUBLANE, :] for vg in range(ng)]
        for vg in range(ng):
            y_ref[t, vg * SUBLANE:(vg + 1) * SUBLANE, :] = sy[vg] + sa[vg] * br + vv[vg] * kr

        def update_body(j, c):
            b_b = row(bv_s, t, j)
            k_b = row(k2_s, t, j)
            for vg in range(ng):
                sl = slice(vg * SUBLANE, (vg + 1) * SUBLANE)
                s_ref[j, sl, :] = s_ref[j, sl, :] + sa[vg] * b_b + vv[vg] * k_b
            return c

        lax.fori_loop(0, n, update_body, 0, unroll=8)
        return carry

    lax.fori_loop(0, tl, step, 0)

    def rescale(j, c):
        s_ref[j] = s_ref[j] * pend_s[pl.ds(j, 1), :]
        return c

    lax.fori_loop(0, n, rescale, 0, unroll=8)

    y = y_ref[...]
    mean = jnp.mean(y, axis=1, keepdims=True)
    var = jnp.mean((y - mean) ** 2, axis=1, keepdims=True)
    y = (y - mean) * lax.rsqrt(var + RWKV_GN_EPS) * lg_ref[...] + lb_ref[...]
    bonus = jnp.sum(r * k2 * rk_ref[...], axis=1, keepdims=True) * v_ref[...]
    y_ref[...] = y + bonus


def _rwkv_scan(r, k, v, w, a, kk, ka, rk, lg, lb):
    L = r.shape[0]
    tl = SCAN_STEPS
    n = RWKV_HEAD
    blk = pl.BlockSpec((tl, n, LANE), lambda i: (i, 0, 0))
    par = _const_spec((n, LANE))
    big = pltpu.VMEM((tl, n, LANE), F32)
    small = pltpu.VMEM((tl, 1, LANE), F32)
    return pl.pallas_call(
        _rwkv_scan_kernel,
        grid=(L // tl,),
        in_specs=[blk] * 5 + [par] * 5,
        out_specs=blk,
        out_shape=jax.ShapeDtypeStruct((L, n, LANE), F32),
        scratch_shapes=[pltpu.VMEM((n, n, LANE), F32), pltpu.VMEM((n, LANE), F32), big, big, big, big,
                        small, small],
        compiler_params=_params(("arbitrary",)),
        name="rwkv_scan",
    )(r, k, v, w, a, kk, ka, rk, lg, lb)


def _head_param(p):
    return jnp.repeat(p.reshape(RWKV_HEADS, RWKV_HEAD).T, BATCH, axis=1)


def _prep_even(w_in, a_re, a_im, log_dt, b_re, b_im, c_re, c_im, w_g2, b_g2):
    assert S5_WIDTH == GLA_WIDTH == GLA_QK
    qk = GLA_HEADS * GLA_DK
    widths = (S5_WIDTH, qk, qk, GLA_WIDTH, GLA_RANK, GLA_WIDTH)
    edges = [sum(widths[:j]) for j in range(len(widths) + 1)]
    w_u, w_q, w_k, w_v, w_gl, w_r = (w_in[:, edges[j]:edges[j + 1]] for j in range(len(widths)))

    def pad_heads(w):
        w = w.reshape(-1, GLA_HEADS, GLA_DK)
        w = jnp.pad(w, ((0, 0), (0, 0), (0, GLA_DKP - GLA_DK)))
        return w.reshape(-1, GLA_HEADS * GLA_DKP)

    w_proj = jnp.concatenate([w_u, w_v, w_r, pad_heads(w_q), pad_heads(w_k),
                              jnp.pad(w_gl, ((0, 0), (0, LANE - GLA_RANK)))], axis=1)
    wg = jnp.pad(pad_heads(w_g2), ((0, LANE - GLA_RANK), (0, 0)))
    bg = pad_heads(b_g2[None, :])
    lm = pad_heads(jnp.ones((1, GLA_HEADS * GLA_DK), F32))

    dt = jnp.exp(log_dt)
    mag = jnp.exp(dt * a_re)
    abar_re = mag * jnp.cos(dt * a_im)
    abar_im = mag * jnp.sin(dt * a_im)
    nr = abar_re - 1.0
    den = a_re * a_re + a_im * a_im
    f_re = (nr * a_re + abar_im * a_im) / den
    f_im = (abar_im * a_re - nr * a_im) / den
    bbar_re = f_re[..., None] * b_re - f_im[..., None] * b_im
    bbar_im = f_re[..., None] * b_im + f_im[..., None] * b_re
    eye = jnp.eye(8, dtype=F32)
    gs = S5_GROUPS // S5_SLABS

    def bd(bb):
        bb = bb.reshape(S5_SLABS, gs, S5_STATE, S5_GROUP)
        return jnp.einsum('sgpc,gh->sgchp', bb, eye).reshape(S5_SLABS, gs * S5_GROUP, gs * S5_STATE)

    def cdm(cc):
        cc = cc.reshape(S5_SLABS, gs, S5_GROUP, S5_STATE)
        return jnp.einsum('sgcp,gh->sgphc', cc, eye).reshape(S5_SLABS, gs * S5_STATE, gs * S5_GROUP)

    bd_all = jnp.concatenate([bd(bbar_re), bd(bbar_im)], axis=2)
    cd_all = jnp.concatenate([cdm(c_re), -cdm(c_im)], axis=1)
    are = jnp.broadcast_to(abar_re.reshape(S5_SLABS, 1, gs * S5_STATE), (S5_SLABS, BATCH, gs * S5_STATE))
    aim = jnp.broadcast_to(abar_im.reshape(S5_SLABS, 1, gs * S5_STATE), (S5_SLABS, BATCH, gs * S5_STATE))
    return w_proj, wg, bg, lm, bd_all, cd_all, are, aim


def kernel(x, norm_mix_g, norm_ffn_g, final_g, e_w_in, e_w_out, s5_a_re, s5_a_im, s5_log_dt, s5_b_re, s5_b_im, s5_c_re, s5_c_im, s5_d, s5_w_glu, gla_w_g2, gla_b_g2, gla_norm_g, o_mu, o_w_r, o_w_k, o_w_v, o_w0, o_w_w1, o_w_w2, o_a0, o_w_a1, o_w_a2, o_w_g1, o_w_g2, o_k_k, o_k_a, o_r_k, o_lnx_g, o_lnx_b, o_w_o, peer_w_q, peer_sub_keys, peer_u, peer_v):
    bsz, L, D = x.shape
    assert bsz == BATCH and D == D_MODEL
    T = bsz * L
    row = lambda p: p.reshape(1, -1)
    h = x.transpose(1, 0, 2).reshape(T, D)

    w_proj, wg, bg, lm, bd_all, cd_all, are, aim = _prep_even(
        e_w_in[0], s5_a_re[0], s5_a_im[0], s5_log_dt[0], s5_b_re[0], s5_b_im[0],
        s5_c_re[0], s5_c_im[0], gla_w_g2[0], gla_b_g2[0])
    bf = lambda p: p.astype(BF16)
    proj = _even_front(h, row(norm_mix_g[0]), bf(w_proj))
    y_s5 = _s5(proj, bf(bd_all), bf(cd_all), are, aim, row(s5_d[0]), bf(s5_w_glu[0]))
    y_gla = _gla(proj, wg, bg, lm, row(gla_norm_g[0]))
    w_out = bf(e_w_out[0])
    h, hbt = _mix_out(h, [y_s5, y_gla], [w_out[:S5_WIDTH], w_out[S5_WIDTH:]], None, row(norm_ffn_g[0]))
    h = _peer_layer(h, hbt, peer_w_q[0], peer_sub_keys[0], peer_u[0], peer_v[0], row(final_g), False)

    r, k, v, w, a, g = _rwkv_front(
        h, row(norm_mix_g[1]), o_mu[0], bf(o_w_r[0]), bf(o_w_k[0]), bf(o_w_v[0]), row(o_w0[0]),
        bf(o_w_w1[0]), bf(o_w_w2[0]), row(o_a0[0]), bf(o_w_a1[0]), bf(o_w_a2[0]), bf(o_w_g1[0]), bf(o_w_g2[0]))
    y = _rwkv_scan(r, k, v, w, a, _head_param(o_k_k[0]), _head_param(o_k_a[0]),
                   _head_param(o_r_k[0].reshape(-1)), _head_param(o_lnx_g[0]), _head_param(o_lnx_b[0]))
    h, hbt = _mix_out(h, [y], [bf(o_w_o[0])], g, row(norm_ffn_g[1]))
    out = _peer_layer(h, hbt, peer_w_q[1], peer_sub_keys[1], peer_u[1], peer_v[1], row(final_g), True)
    return out.reshape(L, bsz, D).transpose(1, 0, 2)
```

```python
import functools
import math

import jax
import jax.numpy as jnp
from jax import lax
from jax.experimental import pallas as pl
from jax.experimental.pallas import tpu as pltpu

F32 = jnp.float32
BF16 = jnp.bfloat16
HIGHEST = lax.Precision.HIGHEST

D_MODEL = 1024
BATCH = 8
NORM_EPS = 1e-6
S5_WIDTH = 512
S5_GROUP = 16
S5_GROUPS = 32
S5_STATE = 64
S5_SLABS = 4
S5_SLAB_STATES = S5_GROUPS // S5_SLABS * S5_STATE
GLA_HEADS = 4
GLA_DV = 128
GLA_DK = 64
GLA_DKP = 128
GLA_RANK = 16
GLA_TAU = 16.0
GLA_CHUNK = 64
RWKV_HEAD = 64
RWKV_HEADS = 16
RWKV_GN_EPS = 64e-5
PEER_HEADS = 8
PEER_NKEYS = 128
PEER_TOPK = 16
PEER_TILE = 1024
PEER_TOKENS = 512
GATE_STEP = 2.0
LANE = 128
SUBLANE = 8
VMEM_LIMIT = 56 * 1024 * 1024
CHUNK_PITCH = PEER_NKEYS + SUBLANE
E2_ROWS = PEER_NKEYS + 4 * SUBLANE
E2_ROW0 = 2 * SUBLANE
E1_ROWS = PEER_NKEYS + 2 * SUBLANE
E1_ROW0 = SUBLANE
ROW_TILE = 512
SCAN_STEPS = 32

GLA_WIDTH = GLA_HEADS * GLA_DV
GLA_QK = GLA_HEADS * GLA_DKP
PROJ_GATE_COL = S5_WIDTH + 2 * GLA_WIDTH + 2 * GLA_QK
PROJ_W = PROJ_GATE_COL + LANE


def _dot(a, b, hi=False):
    if hi:
        return jnp.dot(a, b, preferred_element_type=F32, precision=HIGHEST)
    return jnp.dot(a.astype(BF16), b.astype(BF16), preferred_element_type=F32)


def _dot_nt(a, b):
    return lax.dot_general(a.astype(BF16), b.astype(BF16), (((1,), (1,)), ((), ())),
                           preferred_element_type=F32)


def _rms(x, g):
    return x * lax.rsqrt(jnp.mean(x * x, axis=-1, keepdims=True) + NORM_EPS) * g


def _sigmoid(x):
    return 1.0 / (1.0 + jnp.exp(-x))


def _softplus(x):
    return jnp.maximum(x, 0.0) + jnp.log(1.0 + jnp.exp(-jnp.abs(x)))


def _gelu(x):
    return 0.5 * x * (1.0 + lax.erf(x * (1.0 / math.sqrt(2.0))))


def _const_spec(shape):
    nd = len(shape)
    return pl.BlockSpec(shape, lambda *_: (0,) * nd)


def _params(sem, flags=None):
    return pltpu.CompilerParams(dimension_semantics=sem, vmem_limit_bytes=VMEM_LIMIT, flags=flags)


def _even_front_kernel(h_ref, g_ref, w_ref, o_ref):
    xn = _rms(h_ref[...], g_ref[...])
    o_ref[...] = _dot(xn, w_ref[...])


def _even_front(h, g, w):
    T = h.shape[0]
    tm = ROW_TILE
    return pl.pallas_call(
        _even_front_kernel,
        grid=(T // tm,),
        in_specs=[pl.BlockSpec((tm, D_MODEL), lambda i: (i, 0)),
                  _const_spec((1, D_MODEL)),
                  _const_spec((D_MODEL, PROJ_W))],
        out_specs=pl.BlockSpec((tm, PROJ_W), lambda i: (i, 0)),
        out_shape=jax.ShapeDtypeStruct((T, PROJ_W), F32),
        compiler_params=_params(("parallel",)),
        name="even_front",
    )(h, g, w)


def _s5_kernel(u_ref, bd_ref, cd_ref, are_ref, aim_ref, d_ref, wglu_ref, o_ref, bu_ref, st_ref):
    rows = u_ref.shape[0]
    steps = rows // BATCH
    half = S5_SLAB_STATES

    @pl.when(pl.program_id(0) == 0)
    def _():
        st_ref[...] = jnp.zeros_like(st_ref)

    u = u_ref[...]
    for s in range(S5_SLABS):
        bu_ref[s] = _dot(u[:, s * LANE:(s + 1) * LANE], bd_ref[s])

    def step(t, carry):
        r0 = pl.multiple_of(t * BATCH, BATCH)
        out = []
        for s in range(S5_SLABS):
            sre, sim = carry[2 * s], carry[2 * s + 1]
            ar, ai = are_ref[s], aim_ref[s]
            bre = bu_ref[s, pl.ds(r0, BATCH), 0:half]
            bim = bu_ref[s, pl.ds(r0, BATCH), half:2 * half]
            nre = ar * sre - ai * sim + bre
            nim = ar * sim + ai * sre + bim
            bu_ref[s, pl.ds(r0, BATCH), 0:half] = nre
            bu_ref[s, pl.ds(r0, BATCH), half:2 * half] = nim
            out += [nre, nim]
        return tuple(out)

    init = tuple(st_ref[j] for j in range(2 * S5_SLABS))
    fin = lax.fori_loop(0, steps, step, init)
    for j in range(2 * S5_SLABS):
        st_ref[j] = fin[j]

    y = jnp.concatenate([_dot(bu_ref[s], cd_ref[s]) for s in range(S5_SLABS)], axis=1)
    y = y + d_ref[...] * u
    g = _gelu(y)
    o_ref[...] = g * _sigmoid(_dot(g, wglu_ref[...]))


def _s5(proj, bd, cd, are, aim, d, wglu):
    T = proj.shape[0]
    rows = ROW_TILE
    return pl.pallas_call(
        _s5_kernel,
        grid=(T // rows,),
        in_specs=[pl.BlockSpec((rows, S5_WIDTH), lambda i: (i, 0)),
                  _const_spec(bd.shape), _const_spec(cd.shape),
                  _const_spec(are.shape), _const_spec(aim.shape),
                  _const_spec(d.shape), _const_spec(wglu.shape)],
        out_specs=pl.BlockSpec((rows, S5_WIDTH), lambda i: (i, 0)),
        out_shape=jax.ShapeDtypeStruct((T, S5_WIDTH), F32),
        scratch_shapes=[pltpu.VMEM((S5_SLABS, rows, 2 * S5_SLAB_STATES), F32),
                        pltpu.VMEM((2 * S5_SLABS, BATCH, S5_SLAB_STATES), F32)],
        compiler_params=_params(("arbitrary",)),
        name="s5",
    )(proj, bd, cd, are, aim, d, wglu)


def _gla_kernel(v_ref, r_ref, q_ref, k_ref, gl_ref, wg_ref, bg_ref, lm_ref, ng_ref, o_ref, s_ref):
    rows = q_ref.shape[0]
    width = GLA_HEADS * GLA_DKP

    @pl.when(pl.program_id(0) == 0)
    def _():
        s_ref[...] = jnp.zeros_like(s_ref)

    z = _dot(gl_ref[...], wg_ref[...], True) + bg_ref[...]
    la = -_softplus(-z) * (1.0 / GLA_TAU) * lm_ref[...]
    b = la
    sh = BATCH
    while sh < rows:
        b = b + jnp.concatenate([jnp.zeros((sh, width), F32), b[:rows - sh]], axis=0)
        sh *= 2
    blast = b[rows - BATCH:rows]
    blast_t = jnp.concatenate([blast] * GLA_CHUNK, axis=0)
    q = q_ref[...] * (GLA_DK ** -0.5)
    k = k_ref[...]
    qd = q * jnp.exp(b)
    kd = k * jnp.exp(-b)
    kend = k * jnp.exp(blast_t - b)

    ri = lax.broadcasted_iota(jnp.int32, (rows, rows), 0)
    ci = lax.broadcasted_iota(jnp.int32, (rows, rows), 1)
    causal = jnp.logical_and(((ri - ci) & (BATCH - 1)) == 0, ri >= ci)
    rowb = lax.broadcasted_iota(jnp.int32, (rows, GLA_DKP), 0) & (BATCH - 1)
    laneb = lax.broadcasted_iota(jnp.int32, (GLA_DKP, rows), 1) & (BATCH - 1)

    for h in range(GLA_HEADS):
        sl = slice(h * GLA_DKP, (h + 1) * GLA_DKP)
        vh = v_ref[:, h * GLA_DV:(h + 1) * GLA_DV]
        qh, kdh, keh, bh = qd[:, sl], kd[:, sl], kend[:, sl], b[:, sl]
        att = jnp.where(causal, _dot_nt(qh, kdh), 0.0)
        o = _dot(att, vh)
        qexp = jnp.concatenate([jnp.where(rowb == bb, qh, 0.0) for bb in range(BATCH)], axis=1)
        o = o + _dot(qexp, s_ref[h])
        ket = keh.T
        kexp_t = jnp.concatenate([jnp.where(laneb == bb, ket, 0.0) for bb in range(BATCH)], axis=0)
        kv = _dot(kexp_t, vh)
        bt = bh[rows - GLA_DKP:rows].T
        dec = jnp.concatenate(
            [jnp.broadcast_to(jnp.exp(bt[:, GLA_DKP - BATCH + bb:GLA_DKP - BATCH + bb + 1]),
                              (GLA_DKP, GLA_DV)) for bb in range(BATCH)], axis=0)
        s_ref[h] = s_ref[h] * dec + kv
        o = o * lax.rsqrt(jnp.mean(o * o, axis=-1, keepdims=True) + NORM_EPS)
        o = o * ng_ref[:, h * GLA_DV:(h + 1) * GLA_DV]
        rr = r_ref[:, h * GLA_DV:(h + 1) * GLA_DV]
        o_ref[:, h * GLA_DV:(h + 1) * GLA_DV] = o * (rr * _sigmoid(rr))


def _gla(proj, wg, bg, lm, ng):
    T = proj.shape[0]
    rows = GLA_CHUNK * BATCH
    width = GLA_HEADS * GLA_DKP
    return pl.pallas_call(
        _gla_kernel,
        grid=(T // rows,),
        in_specs=[pl.BlockSpec((rows, GLA_WIDTH), lambda i: (i, 1)),
                  pl.BlockSpec((rows, GLA_WIDTH), lambda i: (i, 2)),
                  pl.BlockSpec((rows, width), lambda i: (i, 3)),
                  pl.BlockSpec((rows, width), lambda i: (i, 4)),
                  pl.BlockSpec((rows, LANE), lambda i: (i, PROJ_GATE_COL // LANE)),
                  _const_spec(wg.shape), _const_spec(bg.shape), _const_spec(lm.shape),
                  _const_spec(ng.shape)],
        out_specs=pl.BlockSpec((rows, GLA_HEADS * GLA_DV), lambda i: (i, 0)),
        out_shape=jax.ShapeDtypeStruct((T, GLA_HEADS * GLA_DV), F32),
        scratch_shapes=[pltpu.VMEM((GLA_HEADS, BATCH * GLA_DKP, GLA_DV), F32)],
        compiler_params=_params(("arbitrary",)),
        name="gla",
    )(proj, proj, proj, proj, proj, wg, bg, lm, ng)


def _mix_out_kernel(n_in, has_gate, *refs):
    h_ref = refs[0]
    y_refs = refs[1:1 + n_in]
    w_refs = refs[1 + n_in:1 + 2 * n_in]
    rest = refs[1 + 2 * n_in:]
    gate_ref = rest[0] if has_gate else None
    g_ref, ho_ref, hbt_ref = rest[1:] if has_gate else rest
    acc = h_ref[...]
    for y_ref, w_ref in zip(y_refs, w_refs):
        if len(y_ref.shape) == 3:
            y = jnp.concatenate([_scan_to_rows(y_ref, t0) for t0 in range(0, y_ref.shape[0], SCAN_BLOCK)],
                                axis=0)
        else:
            y = y_ref[...]
        if has_gate:
            y = y * gate_ref[...]
        acc = acc + _dot(y, w_ref[...])
    ho_ref[...] = acc
    hbt_ref[...] = _rms(acc, g_ref[...]).T.astype(BF16)


def _mix_out(h, ys, ws, gate, g):
    T = h.shape[0]
    tm = ROW_TILE
    n_in = len(ys)
    row = pl.BlockSpec((tm, D_MODEL), lambda i: (i, 0))
    in_specs = [row]
    in_specs += [pl.BlockSpec((tm, y.shape[1]), lambda i: (i, 0)) if y.ndim == 2 else
                 pl.BlockSpec((tm // BATCH,) + y.shape[1:], lambda i: (i, 0, 0)) for y in ys]
    in_specs += [_const_spec(w.shape) for w in ws]
    args = [h, *ys, *ws]
    if gate is not None:
        in_specs.append(row)
        args.append(gate)
    in_specs.append(_const_spec((1, D_MODEL)))
    args.append(g)
    return pl.pallas_call(
        functools.partial(_mix_out_kernel, n_in, gate is not None),
        grid=(T // tm,),
        in_specs=in_specs,
        out_specs=[row, pl.BlockSpec((D_MODEL, tm), lambda i: (0, i))],
        out_shape=[jax.ShapeDtypeStruct((T, D_MODEL), F32),
                   jax.ShapeDtypeStruct((D_MODEL, T), BF16)],
        compiler_params=_params(("parallel",)),
        name="mix_out",
    )(*args)


def _tree_sum(xs):
    while len(xs) > 1:
        xs = [xs[i] + xs[i + 1] for i in range(0, len(xs) - 1, 2)] + ([xs[-1]] if len(xs) % 2 else [])
    return xs[0]


def _insert(tops, x):
    out = []
    for t in tops:
        out.append(jnp.maximum(t, x))
        x = jnp.minimum(t, x)
    return out


def _compare_exchange(v, i, l, descending):
    hi, lo = jnp.maximum(v[i], v[l]), jnp.minimum(v[i], v[l])
    v[i], v[l] = (hi, lo) if descending else (lo, hi)


def _bitonic_merge_desc(v):
    v = list(v)
    j = len(v) // 2
    while j >= 1:
        for i in range(len(v)):
            if i ^ j > i:
                _compare_exchange(v, i, i ^ j, True)
        j //= 2
    return v


def _bitonic_sort_desc(v):
    v = list(v)
    k = 2
    while k <= len(v):
        j = k // 2
        while j >= 1:
            for i in range(len(v)):
                if i ^ j > i:
                    _compare_exchange(v, i, i ^ j, (i & k) == 0)
            j //= 2
        k *= 2
    return v


def _top_values(scr_ref):
    tops = None
    for g in range(PEER_NKEYS // PEER_TOPK):
        grp = _bitonic_sort_desc([scr_ref[pl.ds(g * PEER_TOPK + i, SUBLANE, stride=CHUNK_PITCH), :]
                                  for i in range(PEER_TOPK)])
        if tops is None:
            tops = grp
        else:
            tops = _bitonic_merge_desc([jnp.maximum(tops[i], grp[PEER_TOPK - 1 - i])
                                        for i in range(PEER_TOPK)])
    return tops


def _pick(vals, bits):
    if not bits:
        return vals[0]
    half = len(vals) // 2
    return jnp.where(bits[0], _pick(vals[half:], bits[1:]), _pick(vals[:half], bits[1:]))


def _count_greater(x, s):
    bits = []
    step = PEER_TOPK // 2
    while step >= 1:
        vals = [s[base + step - 1] for base in range(0, PEER_TOPK, 2 * step)]
        bits.append(_pick(vals, bits) > x)
        step //= 2
    return bits, s[PEER_TOPK - 1] > x


def _peer_topk_kernel(hn_ref, wq_ref, keys_ref, n1_ref, e1_ref, r2_ref, e2_ref,
                      scr1, scr2, stat_ref):
    nch = hn_ref.shape[1] // LANE
    qt = _dot(wq_ref[...], hn_ref[...])
    for c, scr in enumerate((scr1, scr2)):
        st = _dot(keys_ref[0, c], qt[c * LANE:(c + 1) * LANE, :])
        for j in range(nch):
            scr[j * CHUNK_PITCH:j * CHUNK_PITCH + PEER_NKEYS, :] = st[:, j * LANE:(j + 1) * LANE]
    a = _top_values(scr1)
    b = _top_values(scr2)
    pairs = [(i, j) for i in range(PEER_TOPK) for j in range(PEER_TOPK) if (i + 1) * (j + 1) <= PEER_TOPK]
    tops = [jnp.full((SUBLANE, LANE), -jnp.inf, F32) for _ in range(PEER_TOPK)]
    for i, j in pairs:
        tops = _insert(tops, a[i] + b[j])
    thr = tops[PEER_TOPK - 1]
    z = jnp.zeros((SUBLANE, LANE), F32)
    for t in tops:
        z = z + jnp.exp(t - tops[0])
    inv_z = 1.0 / z
    cnt = [jnp.zeros((SUBLANE, LANE), F32) for _ in range(PEER_TOPK)]
    for i, j in pairs:
        cnt[i] = cnt[i] + jnp.where(a[i] + b[j] >= thr, GATE_STEP, 0.0)

    for i in range(PEER_TOPK):
        stat_ref[i] = a[i]
        stat_ref[PEER_TOPK + i] = b[i]
        stat_ref[2 * PEER_TOPK + i] = cnt[i]
    stat_ref[3 * PEER_TOPK] = inv_z
    weights = [GATE_STEP * (PEER_TOPK >> (i + 1)) for i in range(4)] + [GATE_STEP]

    def chunk(c, carry):
        def stat(i):
            return jnp.broadcast_to(stat_ref[i, pl.ds(c, 1), :], (SUBLANE, LANE))
        a_c = [stat(i) for i in range(PEER_TOPK)]
        b_c = [stat(PEER_TOPK + i) for i in range(PEER_TOPK)]
        cnt_c = [stat(2 * PEER_TOPK + i) for i in range(PEER_TOPK)]
        inv_z_c = stat(3 * PEER_TOPK)
        e1_ref[0, c, 0:E1_ROW0, :] = jnp.zeros((E1_ROW0, LANE), F32)
        e1_ref[0, c, E1_ROW0 + PEER_NKEYS:E1_ROWS, :] = jnp.zeros((E1_ROWS - E1_ROW0 - PEER_NKEYS, LANE), F32)
        for g in range(PEER_NKEYS // SUBLANE):
            rows = pl.ds(pl.multiple_of(c * CHUNK_PITCH, SUBLANE) + g * SUBLANE, SUBLANE)
            x = scr1[rows, :]
            bits, below = _count_greater(x, a_c)
            n1_ref[0, c, g * SUBLANE:(g + 1) * SUBLANE, :] = jnp.where(below, 0.0, _pick(cnt_c, bits))
            e1_ref[0, c, E1_ROW0 + g * SUBLANE:E1_ROW0 + (g + 1) * SUBLANE, :] = jnp.exp(x - a_c[0]) * inv_z_c
            x = scr2[rows, :]
            bits, below = _count_greater(x, b_c)
            rank = _tree_sum([jnp.where(m, w, 0.0) for m, w in zip(bits + [below], weights)])
            scr1[rows, :] = rank
            scr2[rows, :] = jnp.exp(x - b_c[0])
        return carry

    lax.fori_loop(0, nch, chunk, 0)
    for j in range(nch):
        rows = slice(j * CHUNK_PITCH, j * CHUNK_PITCH + PEER_NKEYS)
        r2_ref[0, j] = scr1[rows, :].astype(BF16)
        e2_ref[0, j, 0:E2_ROW0, :] = jnp.zeros((E2_ROW0, LANE), BF16)
        e2_ref[0, j, E2_ROW0:E2_ROW0 + PEER_NKEYS, :] = scr2[rows, :].astype(BF16)
        e2_ref[0, j, E2_ROW0 + PEER_NKEYS:E2_ROWS, :] = jnp.zeros((E2_ROWS - E2_ROW0 - PEER_NKEYS, LANE), BF16)


def _peer_topk(hn, wq, keys):
    T = hn.shape[1]
    tt = SUBLANE * LANE
    nch = T // LANE
    shape = (PEER_HEADS, nch, PEER_NKEYS, LANE)
    spec = pl.BlockSpec((1, SUBLANE, PEER_NKEYS, LANE), lambda i, h: (h, i, 0, 0))
    return pl.pallas_call(
        _peer_topk_kernel,
        grid=(T // tt, PEER_HEADS),
        in_specs=[pl.BlockSpec((D_MODEL, tt), lambda i, h: (0, i)),
                  pl.BlockSpec((2 * LANE, D_MODEL), lambda i, h: (h, 0)),
                  pl.BlockSpec((1, 2, PEER_NKEYS, LANE), lambda i, h: (h, 0, 0, 0))],
        out_specs=[spec, pl.BlockSpec((1, SUBLANE, E1_ROWS, LANE), lambda i, h: (h, i, 0, 0)), spec,
                   pl.BlockSpec((1, SUBLANE, E2_ROWS, LANE), lambda i, h: (h, i, 0, 0))],
        out_shape=[jax.ShapeDtypeStruct(shape, F32),
                   jax.ShapeDtypeStruct((PEER_HEADS, nch, E1_ROWS, LANE), F32),
                   jax.ShapeDtypeStruct(shape, BF16),
                   jax.ShapeDtypeStruct((PEER_HEADS, nch, E2_ROWS, LANE), BF16)],
        scratch_shapes=[pltpu.VMEM((SUBLANE * CHUNK_PITCH, LANE), F32),
                        pltpu.VMEM((SUBLANE * CHUNK_PITCH, LANE), F32),
                        pltpu.VMEM((3 * PEER_TOPK + 1, SUBLANE, LANE), F32)],
        compiler_params=_params(("parallel", "arbitrary")),
        name="peer_topk",
    )(hn, wq, keys)


def _row_bf16(ref, h, c, i1, dep=None):
    packed_rows = 2 * SUBLANE
    row = ref[h, c, pl.ds(i1, 1), :]
    if dep is not None:
        row = row + dep
    x = jnp.broadcast_to(row, (packed_rows, LANE)).astype(BF16)
    return jnp.concatenate([x] * (PEER_NKEYS // packed_rows), axis=0)


def _zero_after(x):
    bits = pltpu.bitcast(x[0:2 * SUBLANE, :], jnp.uint32)
    sixteen = jnp.uint32(16)
    zero_bits = lax.shift_right_logical(lax.shift_right_logical(bits, sixteen), sixteen)
    return pltpu.bitcast(zero_bits, F32)[0:1, :]


def _peer_main_kernel(final, hbt_ref, hres_ref, u0_ref, ub_ref, ua_ref, vta_ref, vtb_ref,
                      n1_ref, e1_ref, r2_ref, e2_ref, fg_ref, o_ref,
                      acc_ref, hga_ref, hgb_ref, hta_ref, htb_ref):
    e = pl.program_id(1)
    n_pairs = pl.num_programs(1) - 1
    tt = hbt_ref.shape[1]
    n_i1 = PEER_TILE // PEER_NKEYS
    n_ch = tt // LANE
    zero = jnp.zeros((PEER_NKEYS, LANE), BF16)

    @pl.when(e == 0)
    def _():
        acc_ref[...] = jnp.zeros_like(acc_ref)
        hgb_ref[...] = jnp.zeros_like(hgb_ref)
        hta_ref[...] = _dot(u0_ref[...], hbt_ref[...])

    def gates(tile, ht_ref, hg_ref, c, dep):
        for il in range(n_i1):
            i1 = tile * n_i1 + il
            gate = None
            for h in range(PEER_HEADS):
                code = _row_bf16(n1_ref, h, c, i1, dep if h == 0 else None) - r2_ref[h, c]
                w = _row_bf16(e1_ref, h, c, i1 + E1_ROW0) * e2_ref[h, c, E2_ROW0:E2_ROW0 + PEER_NKEYS, :]
                term = jnp.maximum(jnp.minimum(w, code), zero)
                gate = term if gate is None else gate + term
            rows = slice(il * PEER_NKEYS, (il + 1) * PEER_NKEYS)
            cols = slice(c * LANE, (c + 1) * LANE)
            hg = _gelu(ht_ref[rows, cols]).astype(BF16) * gate
            hg_ref[rows, cols] = hg
            dep = _zero_after(hg)
        return dep

    def phase(tile, ht_ref, hg_ref, vt_prev_ref, hg_prev_ref, u_next_ref, ht_next_ref, dep):
        half = tt // 2
        for piece in range(4):
            cols = slice((piece % 2) * half, (piece % 2 + 1) * half)
            if piece < 2:
                acc_ref[:, cols] += _dot(vt_prev_ref[...], hg_prev_ref[:, cols])
            else:
                ht_next_ref[:, cols] = _dot(u_next_ref[...], hbt_ref[:, cols])
            for c in range(piece * n_ch // 4, (piece + 1) * n_ch // 4):
                dep = gates(tile, ht_ref, hg_ref, c, dep)
        return dep

    @pl.when(e < n_pairs)
    def _():
        dep = phase(2 * e, hta_ref, hga_ref, vtb_ref, hgb_ref, ub_ref, htb_ref, None)
        phase(2 * e + 1, htb_ref, hgb_ref, vta_ref, hga_ref, ua_ref, hta_ref, dep)

    @pl.when(e == n_pairs)
    def _():
        acc = acc_ref[...] + _dot(vtb_ref[...], hgb_ref[...])
        out = hres_ref[...] + acc.T
        if final:
            out = _rms(out, fg_ref[...])
        o_ref[...] = out


def _peer_main(hbt, hres, u, vt, n1, e1, r2, e2, fg, final):
    T = hres.shape[0]
    E = u.shape[0]
    tt = PEER_TOKENS
    n_tiles = E // PEER_TILE
    n_pairs = n_tiles // 2
    big_spec = pl.BlockSpec((PEER_HEADS, tt // LANE, PEER_NKEYS, LANE), lambda i, e: (0, i, 0, 0))
    tile_of = lambda f: (lambda i, e: (jnp.clip(f(e), 0, n_tiles - 1), 0, 0))
    vt_spec = lambda f: pl.BlockSpec((D_MODEL, PEER_TILE), lambda i, e: tile_of(f)(i, e)[:2])
    u_spec = lambda f: pl.BlockSpec((PEER_TILE, D_MODEL), lambda i, e: tile_of(f)(i, e)[:2])
    return pl.pallas_call(
        functools.partial(_peer_main_kernel, final),
        grid=(T // tt, n_pairs + 1),
        in_specs=[pl.BlockSpec((D_MODEL, tt), lambda i, e: (0, i)),
                  pl.BlockSpec((tt, D_MODEL), lambda i, e: (i, 0)),
                  u_spec(lambda e: 0), u_spec(lambda e: 2 * e + 1), u_spec(lambda e: 2 * e + 2),
                  vt_spec(lambda e: 2 * e), vt_spec(lambda e: 2 * e - 1),
                  big_spec,
                  pl.BlockSpec((PEER_HEADS, tt // LANE, E1_ROWS, LANE), lambda i, e: (0, i, 0, 0)),
                  big_spec,
                  pl.BlockSpec((PEER_HEADS, tt // LANE, E2_ROWS, LANE), lambda i, e: (0, i, 0, 0)),
                  _const_spec((1, D_MODEL))],
        out_specs=pl.BlockSpec((tt, D_MODEL), lambda i, e: (i, 0)),
        out_shape=jax.ShapeDtypeStruct((T, D_MODEL), F32),
        scratch_shapes=[pltpu.VMEM((D_MODEL, tt), F32),
                        pltpu.VMEM((PEER_TILE, tt), BF16), pltpu.VMEM((PEER_TILE, tt), BF16),
                        pltpu.VMEM((PEER_TILE, tt), F32), pltpu.VMEM((PEER_TILE, tt), F32)],
        compiler_params=_params(("parallel", "arbitrary")),
        name="peer_main",
    )(hbt, hres, u, u, u, vt, vt, n1, e1, r2, e2, fg)


def _peer_layer(h, hbt, wq, keys, u_tab, v_tab, fg, final):
    n1, e1, r2, e2 = _peer_topk(hbt, wq.T.astype(BF16), keys.astype(BF16))
    vt = v_tab.astype(BF16).reshape(-1, PEER_TILE, D_MODEL).transpose(0, 2, 1).reshape(-1, PEER_TILE)
    return _peer_main(hbt, h, u_tab.astype(BF16), vt, n1, e1, r2, e2, fg, final)


SCAN_BLOCK = LANE // BATCH


def _swap_groups(slabs):
    a = list(slabs)
    group = lax.broadcasted_iota(jnp.int32, a[0].shape, 1) // BATCH
    s = SCAN_BLOCK // 2
    while s >= 1:
        upper = (group & s) != 0
        for i in range(SCAN_BLOCK):
            if i & s == 0:
                ai, aj = a[i], a[i + s]
                a[i] = jnp.where(upper, pltpu.roll(aj, s * BATCH, axis=1), ai)
                a[i + s] = jnp.where(upper, aj, pltpu.roll(ai, LANE - s * BATCH, axis=1))
        s //= 2
    return a


def _rows_to_scan(x, ref):
    block_rows = SCAN_BLOCK * BATCH
    for r0 in range(0, x.shape[0], block_rows):
        z = x[r0:r0 + block_rows].T
        slabs = [z[h * RWKV_HEAD:(h + 1) * RWKV_HEAD, :] for h in range(RWKV_HEADS)]
        for t, blk in enumerate(_swap_groups(slabs)):
            ref[r0 // BATCH + t] = blk


def _scan_to_rows(ref, t0):
    slabs = _swap_groups([ref[t0 + t] for t in range(SCAN_BLOCK)])
    return jnp.concatenate(slabs, axis=0).T


def _rwkv_front_kernel(h_ref, hp_ref, g_ref, mu_ref, wr_ref, wk_ref, wv_ref, w0_ref, ww1_ref, ww2_ref,
                       a0_ref, wa1_ref, wa2_ref, wg1_ref, wg2_ref,
                       r_ref, k_ref, v_ref, w_ref, a_ref, go_ref):
    g = g_ref[...]
    x = _rms(h_ref[...], g)
    xp8 = _rms(hp_ref[...], g)
    xp8 = jnp.where(pl.program_id(0) == 0, 0.0, xp8)
    rows = x.shape[0]
    xprev = jnp.concatenate([xp8, x[:rows - BATCH]], axis=0)
    xx = xprev - x
    mix = lambda j: x + xx * mu_ref[j:j + 1, :]
    _rows_to_scan(_dot(mix(0), wr_ref[...]), r_ref)
    _rows_to_scan(w0_ref[...] + _dot(jnp.tanh(_dot(mix(1), ww1_ref[...])), ww2_ref[...]), w_ref)
    _rows_to_scan(_dot(mix(2), wk_ref[...]), k_ref)
    _rows_to_scan(_dot(mix(3), wv_ref[...]), v_ref)
    _rows_to_scan(_sigmoid(a0_ref[...] + _dot(_dot(mix(4), wa1_ref[...]), wa2_ref[...])), a_ref)
    go_ref[...] = _dot(_sigmoid(_dot(mix(5), wg1_ref[...])), wg2_ref[...])


def _rwkv_front(h, g, mu, wr, wk, wv, w0, ww1, ww2, a0, wa1, wa2, wg1, wg2):
    assert RWKV_HEADS == SCAN_BLOCK
    T = h.shape[0]
    tm = 2 * SCAN_BLOCK * BATCH
    row = pl.BlockSpec((tm, D_MODEL), lambda i: (i, 0))
    prev = pl.BlockSpec((BATCH, D_MODEL), lambda i: (jnp.maximum(i * (tm // BATCH) - 1, 0), 0))
    scan = pl.BlockSpec((tm // BATCH, RWKV_HEAD, LANE), lambda i: (i, 0, 0))
    consts = [g, mu, wr, wk, wv, w0, ww1, ww2, a0, wa1, wa2, wg1, wg2]
    scan_out = jax.ShapeDtypeStruct((T // BATCH, RWKV_HEAD, LANE), F32)
    return pl.pallas_call(
        _rwkv_front_kernel,
        grid=(T // tm,),
        in_specs=[row, prev] + [_const_spec(c.shape) for c in consts],
        out_specs=[scan] * 5 + [row],
        out_shape=[scan_out] * 5 + [jax.ShapeDtypeStruct((T, D_MODEL), F32)],
        compiler_params=_params(("parallel",)),
        name="rwkv_front",
    )(h, h, *consts)


def _rwkv_scan_kernel(r_ref, k_ref, v_ref, w_ref, a_ref, kk_ref, ka_ref, rk_ref, lg_ref, lb_ref,
                      y_ref, s_ref, pend_s, k2_s, av_s, bv_s, wr_s, br_s, kr_s):
    tl = r_ref.shape[0]
    n = RWKV_HEAD
    ng = n // SUBLANE

    @pl.when(pl.program_id(0) == 0)
    def _():
        s_ref[...] = jnp.zeros_like(s_ref)

    r = r_ref[...]
    k = k_ref[...]
    a = a_ref[...]
    kk = k * kk_ref[...]
    kk = kk / jnp.maximum(jnp.sqrt(jnp.sum(kk * kk, axis=1, keepdims=True)), 1e-12)
    k2 = k * (1.0 + (a - 1.0) * ka_ref[...])
    dec = jnp.exp(-jnp.exp(-_softplus(-w_ref[...]) - 0.5))
    bv = kk * a
    p = dec
    sh = 1
    while sh < tl:
        p = p * jnp.concatenate([jnp.ones((sh, n, LANE), F32), p[:tl - sh]], axis=0)
        sh *= 2
    p_prev = jnp.concatenate([jnp.ones((1, n, LANE), F32), p[:tl - 1]], axis=0)
    inv_p = 1.0 / p
    pend_s[...] = p[tl - 1]
    k2_s[...] = k2 * inv_p
    av_s[...] = -kk * p_prev
    bv_s[...] = bv * inv_p
    wr_s[...] = r * p
    br_s[...] = jnp.sum(bv * r, axis=1, keepdims=True)
    kr_s[...] = jnp.sum(k2 * r, axis=1, keepdims=True)

    def row(ref, t, j):
        return jnp.broadcast_to(ref[t, pl.ds(j, 1), :], (SUBLANE, LANE))

    def step(t, carry):
        def reduce_body(j, accs):
            sa, sy = accs
            a_b = row(av_s, t, j)
            wr_b = row(wr_s, t, j)
            sa_n, sy_n = [], []
            for vg in range(ng):
                sv = s_ref[j, vg * SUBLANE:(vg + 1) * SUBLANE, :]
                sa_n.append(sa[vg] + sv * a_b)
                sy_n.append(sy[vg] + sv * wr_b)
            return tuple(sa_n), tuple(sy_n)

        zeros = tuple(jnp.zeros((SUBLANE, LANE), F32) for _ in range(ng))
        sa, sy = lax.fori_loop(0, n, reduce_body, (zeros, zeros), unroll=8)
        br = jnp.broadcast_to(br_s[t], (SUBLANE, LANE))
        kr = jnp.broadcast_to(kr_s[t], (SUBLANE, LANE))
        vv = [v_ref[t, vg * SUBLANE:(vg + 1) * SUBLANE, :] for vg in range(ng)]
        for vg in range(ng):
            y_ref[t, vg * SUBLANE:(vg + 1) * SUBLANE, :] = sy[vg] + sa[vg] * br + vv[vg] * kr

        def update_body(j, c):
            b_b = row(bv_s, t, j)
            k_b = row(k2_s, t, j)
            for vg in range(ng):
                sl = slice(vg * SUBLANE, (vg + 1) * SUBLANE)
                s_ref[j, sl, :] = s_ref[j, sl, :] + sa[vg] * b_b + vv[vg] * k_b
            return c

        lax.fori_loop(0, n, update_body, 0, unroll=8)
        return carry

    lax.fori_loop(0, tl, step, 0)

    def rescale(j, c):
        s_ref[j] = s_ref[j] * pend_s[pl.ds(j, 1), :]
        return c

    lax.fori_loop(0, n, rescale, 0, unroll=8)

    y = y_ref[...]
    mean = jnp.mean(y, axis=1, keepdims=True)
    var = jnp.mean((y - mean) ** 2, axis=1, keepdims=True)
    y = (y - mean) * lax.rsqrt(var + RWKV_GN_EPS) * lg_ref[...] + lb_ref[...]
    bonus = jnp.sum(r * k2 * rk_ref[...], axis=1, keepdims=True) * v_ref[...]
    y_ref[...] = y + bonus


def _rwkv_scan(r, k, v, w, a, kk, ka, rk, lg, lb):
    L = r.shape[0]
    tl = SCAN_STEPS
    n = RWKV_HEAD
    blk = pl.BlockSpec((tl, n, LANE), lambda i: (i, 0, 0))
    par = _const_spec((n, LANE))
    big = pltpu.VMEM((tl, n, LANE), F32)
    small = pltpu.VMEM((tl, 1, LANE), F32)
    return pl.pallas_call(
        _rwkv_scan_kernel,
        grid=(L // tl,),
        in_specs=[blk] * 5 + [par] * 5,
        out_specs=blk,
        out_shape=jax.ShapeDtypeStruct((L, n, LANE), F32),
        scratch_shapes=[pltpu.VMEM((n, n, LANE), F32), pltpu.VMEM((n, LANE), F32), big, big, big, big,
                        small, small],
        compiler_params=_params(("arbitrary",)),
        name="rwkv_scan",
    )(r, k, v, w, a, kk, ka, rk, lg, lb)


def _head_param(p):
    return jnp.repeat(p.reshape(RWKV_HEADS, RWKV_HEAD).T, BATCH, axis=1)


def _prep_even(w_in, a_re, a_im, log_dt, b_re, b_im, c_re, c_im, w_g2, b_g2):
    assert S5_WIDTH == GLA_WIDTH == GLA_QK
    qk = GLA_HEADS * GLA_DK
    widths = (S5_WIDTH, qk, qk, GLA_WIDTH, GLA_RANK, GLA_WIDTH)
    edges = [sum(widths[:j]) for j in range(len(widths) + 1)]
    w_u, w_q, w_k, w_v, w_gl, w_r = (w_in[:, edges[j]:edges[j + 1]] for j in range(len(widths)))

    def pad_heads(w):
        w = w.reshape(-1, GLA_HEADS, GLA_DK)
        w = jnp.pad(w, ((0, 0), (0, 0), (0, GLA_DKP - GLA_DK)))
        return w.reshape(-1, GLA_HEADS * GLA_DKP)

    w_proj = jnp.concatenate([w_u, w_v, w_r, pad_heads(w_q), pad_heads(w_k),
                              jnp.pad(w_gl, ((0, 0), (0, LANE - GLA_RANK)))], axis=1)
    wg = jnp.pad(pad_heads(w_g2), ((0, LANE - GLA_RANK), (0, 0)))
    bg = pad_heads(b_g2[None, :])
    lm = pad_heads(jnp.ones((1, GLA_HEADS * GLA_DK), F32))

    dt = jnp.exp(log_dt)
    mag = jnp.exp(dt * a_re)
    abar_re = mag * jnp.cos(dt * a_im)
    abar_im = mag * jnp.sin(dt * a_im)
    nr = abar_re - 1.0
    den = a_re * a_re + a_im * a_im
    f_re = (nr * a_re + abar_im * a_im) / den
    f_im = (abar_im * a_re - nr * a_im) / den
    bbar_re = f_re[..., None] * b_re - f_im[..., None] * b_im
    bbar_im = f_re[..., None] * b_im + f_im[..., None] * b_re
    eye = jnp.eye(8, dtype=F32)
    gs = S5_GROUPS // S5_SLABS

    def bd(bb):
        bb = bb.reshape(S5_SLABS, gs, S5_STATE, S5_GROUP)
        return jnp.einsum('sgpc,gh->sgchp', bb, eye).reshape(S5_SLABS, gs * S5_GROUP, gs * S5_STATE)

    def cdm(cc):
        cc = cc.reshape(S5_SLABS, gs, S5_GROUP, S5_STATE)
        return jnp.einsum('sgcp,gh->sgphc', cc, eye).reshape(S5_SLABS, gs * S5_STATE, gs * S5_GROUP)

    bd_all = jnp.concatenate([bd(bbar_re), bd(bbar_im)], axis=2)
    cd_all = jnp.concatenate([cdm(c_re), -cdm(c_im)], axis=1)
    are = jnp.broadcast_to(abar_re.reshape(S5_SLABS, 1, gs * S5_STATE), (S5_SLABS, BATCH, gs * S5_STATE))
    aim = jnp.broadcast_to(abar_im.reshape(S5_SLABS, 1, gs * S5_STATE), (S5_SLABS, BATCH, gs * S5_STATE))
    return w_proj, wg, bg, lm, bd_all, cd_all, are, aim


def kernel(x, norm_mix_g, norm_ffn_g, final_g, e_w_in, e_w_out, s5_a_re, s5_a_im, s5_log_dt, s5_b_re, s5_b_im, s5_c_re, s5_c_im, s5_d, s5_w_glu, gla_w_g2, gla_b_g2, gla_norm_g, o_mu, o_w_r, o_w_k, o_w_v, o_w0, o_w_w1, o_w_w2, o_a0, o_w_a1, o_w_a2, o_w_g1, o_w_g2, o_k_k, o_k_a, o_r_k, o_lnx_g, o_lnx_b, o_w_o, peer_w_q, peer_sub_keys, peer_u, peer_v):
    bsz, L, D = x.shape
    assert bsz == BATCH and D == D_MODEL
    T = bsz * L
    row = lambda p: p.reshape(1, -1)
    h = x.transpose(1, 0, 2).reshape(T, D)

    w_proj, wg, bg, lm, bd_all, cd_all, are, aim = _prep_even(
        e_w_in[0], s5_a_re[0], s5_a_im[0], s5_log_dt[0], s5_b_re[0], s5_b_im[0],
        s5_c_re[0], s5_c_im[0], gla_w_g2[0], gla_b_g2[0])
    bf = lambda p: p.astype(BF16)
    proj = _even_front(h, row(norm_mix_g[0]), bf(w_proj))
    y_s5 = _s5(proj, bf(bd_all), bf(cd_all), are, aim, row(s5_d[0]), bf(s5_w_glu[0]))
    y_gla = _gla(proj, wg, bg, lm, row(gla_norm_g[0]))
    w_out = bf(e_w_out[0])
    h, hbt = _mix_out(h, [y_s5, y_gla], [w_out[:S5_WIDTH], w_out[S5_WIDTH:]], None, row(norm_ffn_g[0]))
    h = _peer_layer(h, hbt, peer_w_q[0], peer_sub_keys[0], peer_u[0], peer_v[0], row(final_g), False)

    r, k, v, w, a, g = _rwkv_front(
        h, row(norm_mix_g[1]), o_mu[0], bf(o_w_r[0]), bf(o_w_k[0]), bf(o_w_v[0]), row(o_w0[0]),
        bf(o_w_w1[0]), bf(o_w_w2[0]), row(o_a0[0]), bf(o_w_a1[0]), bf(o_w_a2[0]), bf(o_w_g1[0]), bf(o_w_g2[0]))
    y = _rwkv_scan(r, k, v, w, a, _head_param(o_k_k[0]), _head_param(o_k_a[0]),
                   _head_param(o_r_k[0].reshape(-1)), _head_param(o_lnx_g[0]), _head_param(o_lnx_b[0]))
    h, hbt = _mix_out(h, [y], [bf(o_w_o[0])], g, row(norm_ffn_g[1]))
    out = _peer_layer(h, hbt, peer_w_q[1], peer_sub_keys[1], peer_u[1], peer_v[1], row(final_g), True)
    return out.reshape(L, bsz, D).transpose(1, 0, 2)
```

```python
import functools
import math

import jax
import jax.numpy as jnp
from jax import lax
from jax.experimental import pallas as pl
from jax.experimental.pallas import tpu as pltpu

F32 = jnp.float32
BF16 = jnp.bfloat16
HIGHEST = lax.Precision.HIGHEST

D_MODEL = 1024
BATCH = 8
NORM_EPS = 1e-6
S5_WIDTH = 512
S5_GROUP = 16
S5_GROUPS = 32
S5_STATE = 64
S5_SLABS = 4
S5_SLAB_STATES = S5_GROUPS // S5_SLABS * S5_STATE
GLA_HEADS = 4
GLA_DV = 128
GLA_DK = 64
GLA_DKP = 128
GLA_RANK = 16
GLA_TAU = 16.0
GLA_CHUNK = 64
RWKV_HEAD = 64
RWKV_HEADS = 16
RWKV_GN_EPS = 64e-5
PEER_HEADS = 8
PEER_NKEYS = 128
PEER_TOPK = 16
PEER_TILE = 1024
PEER_TOKENS = 512
GATE_STEP = 2.0
LANE = 128
SUBLANE = 8
VMEM_LIMIT = 56 * 1024 * 1024
CHUNK_PITCH = PEER_NKEYS + SUBLANE
E2_ROWS = PEER_NKEYS + 4 * SUBLANE
E2_ROW0 = 2 * SUBLANE
E1_ROWS = PEER_NKEYS + 2 * SUBLANE
E1_ROW0 = SUBLANE
ROW_TILE = 512
SCAN_STEPS = 64

GLA_WIDTH = GLA_HEADS * GLA_DV
GLA_QK = GLA_HEADS * GLA_DKP
PROJ_GATE_COL = S5_WIDTH + 2 * GLA_WIDTH + 2 * GLA_QK
PROJ_W = PROJ_GATE_COL + LANE


def _dot(a, b, hi=False):
    if hi:
        return jnp.dot(a, b, preferred_element_type=F32, precision=HIGHEST)
    return jnp.dot(a.astype(BF16), b.astype(BF16), preferred_element_type=F32)


def _dot_nt(a, b):
    return lax.dot_general(a.astype(BF16), b.astype(BF16), (((1,), (1,)), ((), ())),
                           preferred_element_type=F32)


def _rms(x, g):
    return x * lax.rsqrt(jnp.mean(x * x, axis=-1, keepdims=True) + NORM_EPS) * g


def _sigmoid(x):
    return 1.0 / (1.0 + jnp.exp(-x))


def _softplus(x):
    return jnp.maximum(x, 0.0) + jnp.log(1.0 + jnp.exp(-jnp.abs(x)))


def _gelu(x):
    return 0.5 * x * (1.0 + lax.erf(x * (1.0 / math.sqrt(2.0))))


def _const_spec(shape):
    nd = len(shape)
    return pl.BlockSpec(shape, lambda *_: (0,) * nd)


def _params(sem, flags=None):
    return pltpu.CompilerParams(dimension_semantics=sem, vmem_limit_bytes=VMEM_LIMIT, flags=flags)


def _even_front_kernel(h_ref, g_ref, w_ref, o_ref):
    xn = _rms(h_ref[...], g_ref[...])
    o_ref[...] = _dot(xn, w_ref[...])


def _even_front(h, g, w):
    T = h.shape[0]
    tm = ROW_TILE
    return pl.pallas_call(
        _even_front_kernel,
        grid=(T // tm,),
        in_specs=[pl.BlockSpec((tm, D_MODEL), lambda i: (i, 0)),
                  _const_spec((1, D_MODEL)),
                  _const_spec((D_MODEL, PROJ_W))],
        out_specs=pl.BlockSpec((tm, PROJ_W), lambda i: (i, 0)),
        out_shape=jax.ShapeDtypeStruct((T, PROJ_W), F32),
        compiler_params=_params(("parallel",)),
        name="even_front",
    )(h, g, w)


def _s5_kernel(u_ref, bd_ref, cd_ref, are_ref, aim_ref, d_ref, wglu_ref, o_ref, bu_ref, st_ref):
    rows = u_ref.shape[0]
    steps = rows // BATCH
    half = S5_SLAB_STATES

    @pl.when(pl.program_id(0) == 0)
    def _():
        st_ref[...] = jnp.zeros_like(st_ref)

    u = u_ref[...]
    for s in range(S5_SLABS):
        bu_ref[s] = _dot(u[:, s * LANE:(s + 1) * LANE], bd_ref[s])

    def step(t, carry):
        r0 = pl.multiple_of(t * BATCH, BATCH)
        out = []
        for s in range(S5_SLABS):
            sre, sim = carry[2 * s], carry[2 * s + 1]
            ar, ai = are_ref[s], aim_ref[s]
            bre = bu_ref[s, pl.ds(r0, BATCH), 0:half]
            bim = bu_ref[s, pl.ds(r0, BATCH), half:2 * half]
            nre = ar * sre - ai * sim + bre
            nim = ar * sim + ai * sre + bim
            bu_ref[s, pl.ds(r0, BATCH), 0:half] = nre
            bu_ref[s, pl.ds(r0, BATCH), half:2 * half] = nim
            out += [nre, nim]
        return tuple(out)

    init = tuple(st_ref[j] for j in range(2 * S5_SLABS))
    fin = lax.fori_loop(0, steps, step, init)
    for j in range(2 * S5_SLABS):
        st_ref[j] = fin[j]

    y = jnp.concatenate([_dot(bu_ref[s], cd_ref[s]) for s in range(S5_SLABS)], axis=1)
    y = y + d_ref[...] * u
    g = _gelu(y)
    o_ref[...] = g * _sigmoid(_dot(g, wglu_ref[...]))


def _s5(proj, bd, cd, are, aim, d, wglu):
    T = proj.shape[0]
    rows = ROW_TILE
    return pl.pallas_call(
        _s5_kernel,
        grid=(T // rows,),
        in_specs=[pl.BlockSpec((rows, S5_WIDTH), lambda i: (i, 0)),
                  _const_spec(bd.shape), _const_spec(cd.shape),
                  _const_spec(are.shape), _const_spec(aim.shape),
                  _const_spec(d.shape), _const_spec(wglu.shape)],
        out_specs=pl.BlockSpec((rows, S5_WIDTH), lambda i: (i, 0)),
        out_shape=jax.ShapeDtypeStruct((T, S5_WIDTH), F32),
        scratch_shapes=[pltpu.VMEM((S5_SLABS, rows, 2 * S5_SLAB_STATES), F32),
                        pltpu.VMEM((2 * S5_SLABS, BATCH, S5_SLAB_STATES), F32)],
        compiler_params=_params(("arbitrary",)),
        name="s5",
    )(proj, bd, cd, are, aim, d, wglu)


def _gla_kernel(v_ref, r_ref, q_ref, k_ref, gl_ref, wg_ref, bg_ref, lm_ref, ng_ref, o_ref, s_ref):
    rows = q_ref.shape[0]
    width = GLA_HEADS * GLA_DKP

    @pl.when(pl.program_id(0) == 0)
    def _():
        s_ref[...] = jnp.zeros_like(s_ref)

    z = _dot(gl_ref[...], wg_ref[...], True) + bg_ref[...]
    la = -_softplus(-z) * (1.0 / GLA_TAU) * lm_ref[...]
    b = la
    sh = BATCH
    while sh < rows:
        b = b + jnp.concatenate([jnp.zeros((sh, width), F32), b[:rows - sh]], axis=0)
        sh *= 2
    blast = b[rows - BATCH:rows]
    blast_t = jnp.concatenate([blast] * GLA_CHUNK, axis=0)
    q = q_ref[...] * (GLA_DK ** -0.5)
    k = k_ref[...]
    qd = q * jnp.exp(b)
    kd = k * jnp.exp(-b)
    kend = k * jnp.exp(blast_t - b)

    ri = lax.broadcasted_iota(jnp.int32, (rows, rows), 0)
    ci = lax.broadcasted_iota(jnp.int32, (rows, rows), 1)
    causal = jnp.logical_and(((ri - ci) & (BATCH - 1)) == 0, ri >= ci)
    rowb = lax.broadcasted_iota(jnp.int32, (rows, GLA_DKP), 0) & (BATCH - 1)
    laneb = lax.broadcasted_iota(jnp.int32, (GLA_DKP, rows), 1) & (BATCH - 1)

    for h in range(GLA_HEADS):
        sl = slice(h * GLA_DKP, (h + 1) * GLA_DKP)
        vh = v_ref[:, h * GLA_DV:(h + 1) * GLA_DV]
        qh, kdh, keh, bh = qd[:, sl], kd[:, sl], kend[:, sl], b[:, sl]
        att = jnp.where(causal, _dot_nt(qh, kdh), 0.0)
        o = _dot(att, vh)
        qexp = jnp.concatenate([jnp.where(rowb == bb, qh, 0.0) for bb in range(BATCH)], axis=1)
        o = o + _dot(qexp, s_ref[h])
        ket = keh.T
        kexp_t = jnp.concatenate([jnp.where(laneb == bb, ket, 0.0) for bb in range(BATCH)], axis=0)
        kv = _dot(kexp_t, vh)
        bt = bh[rows - GLA_DKP:rows].T
        dec = jnp.concatenate(
            [jnp.broadcast_to(jnp.exp(bt[:, GLA_DKP - BATCH + bb:GLA_DKP - BATCH + bb + 1]),
                              (GLA_DKP, GLA_DV)) for bb in range(BATCH)], axis=0)
        s_ref[h] = s_ref[h] * dec + kv
        o = o * lax.rsqrt(jnp.mean(o * o, axis=-1, keepdims=True) + NORM_EPS)
        o = o * ng_ref[:, h * GLA_DV:(h + 1) * GLA_DV]
        rr = r_ref[:, h * GLA_DV:(h + 1) * GLA_DV]
        o_ref[:, h * GLA_DV:(h + 1) * GLA_DV] = o * (rr * _sigmoid(rr))


def _gla(proj, wg, bg, lm, ng):
    T = proj.shape[0]
    rows = GLA_CHUNK * BATCH
    width = GLA_HEADS * GLA_DKP
    return pl.pallas_call(
        _gla_kernel,
        grid=(T // rows,),
        in_specs=[pl.BlockSpec((rows, GLA_WIDTH), lambda i: (i, 1)),
                  pl.BlockSpec((rows, GLA_WIDTH), lambda i: (i, 2)),
                  pl.BlockSpec((rows, width), lambda i: (i, 3)),
                  pl.BlockSpec((rows, width), lambda i: (i, 4)),
                  pl.BlockSpec((rows, LANE), lambda i: (i, PROJ_GATE_COL // LANE)),
                  _const_spec(wg.shape), _const_spec(bg.shape), _const_spec(lm.shape),
                  _const_spec(ng.shape)],
        out_specs=pl.BlockSpec((rows, GLA_HEADS * GLA_DV), lambda i: (i, 0)),
        out_shape=jax.ShapeDtypeStruct((T, GLA_HEADS * GLA_DV), F32),
        scratch_shapes=[pltpu.VMEM((GLA_HEADS, BATCH * GLA_DKP, GLA_DV), F32)],
        compiler_params=_params(("arbitrary",)),
        name="gla",
    )(proj, proj, proj, proj, proj, wg, bg, lm, ng)


def _mix_out_kernel(n_in, has_gate, *refs):
    h_ref = refs[0]
    y_refs = refs[1:1 + n_in]
    w_refs = refs[1 + n_in:1 + 2 * n_in]
    rest = refs[1 + 2 * n_in:]
    gate_ref = rest[0] if has_gate else None
    g_ref, ho_ref, hbt_ref = rest[1:] if has_gate else rest
    acc = h_ref[...]
    for y_ref, w_ref in zip(y_refs, w_refs):
        if len(y_ref.shape) == 3:
            y = jnp.concatenate([_scan_to_rows(y_ref, t0) for t0 in range(0, y_ref.shape[0], SCAN_BLOCK)],
                                axis=0)
        else:
            y = y_ref[...]
        if has_gate:
            y = y * gate_ref[...]
        acc = acc + _dot(y, w_ref[...])
    ho_ref[...] = acc
    hbt_ref[...] = _rms(acc, g_ref[...]).T.astype(BF16)


def _mix_out(h, ys, ws, gate, g):
    T = h.shape[0]
    tm = ROW_TILE
    n_in = len(ys)
    row = pl.BlockSpec((tm, D_MODEL), lambda i: (i, 0))
    in_specs = [row]
    in_specs += [pl.BlockSpec((tm, y.shape[1]), lambda i: (i, 0)) if y.ndim == 2 else
                 pl.BlockSpec((tm // BATCH,) + y.shape[1:], lambda i: (i, 0, 0)) for y in ys]
    in_specs += [_const_spec(w.shape) for w in ws]
    args = [h, *ys, *ws]
    if gate is not None:
        in_specs.append(row)
        args.append(gate)
    in_specs.append(_const_spec((1, D_MODEL)))
    args.append(g)
    return pl.pallas_call(
        functools.partial(_mix_out_kernel, n_in, gate is not None),
        grid=(T // tm,),
        in_specs=in_specs,
        out_specs=[row, pl.BlockSpec((D_MODEL, tm), lambda i: (0, i))],
        out_shape=[jax.ShapeDtypeStruct((T, D_MODEL), F32),
                   jax.ShapeDtypeStruct((D_MODEL, T), BF16)],
        compiler_params=_params(("parallel",)),
        name="mix_out",
    )(*args)


def _tree_sum(xs):
    while len(xs) > 1:
        xs = [xs[i] + xs[i + 1] for i in range(0, len(xs) - 1, 2)] + ([xs[-1]] if len(xs) % 2 else [])
    return xs[0]


def _insert(tops, x):
    out = []
    for t in tops:
        out.append(jnp.maximum(t, x))
        x = jnp.minimum(t, x)
    return out


def _compare_exchange(v, i, l, descending):
    hi, lo = jnp.maximum(v[i], v[l]), jnp.minimum(v[i], v[l])
    v[i], v[l] = (hi, lo) if descending else (lo, hi)


def _bitonic_merge_desc(v):
    v = list(v)
    j = len(v) // 2
    while j >= 1:
        for i in range(len(v)):
            if i ^ j > i:
                _compare_exchange(v, i, i ^ j, True)
        j //= 2
    return v


def _bitonic_sort_desc(v):
    v = list(v)
    k = 2
    while k <= len(v):
        j = k // 2
        while j >= 1:
            for i in range(len(v)):
                if i ^ j > i:
                    _compare_exchange(v, i, i ^ j, (i & k) == 0)
            j //= 2
        k *= 2
    return v


def _top_values(scr_ref):
    tops = None
    for g in range(PEER_NKEYS // PEER_TOPK):
        grp = _bitonic_sort_desc([scr_ref[pl.ds(g * PEER_TOPK + i, SUBLANE, stride=CHUNK_PITCH), :]
                                  for i in range(PEER_TOPK)])
        if tops is None:
            tops = grp
        else:
            tops = _bitonic_merge_desc([jnp.maximum(tops[i], grp[PEER_TOPK - 1 - i])
                                        for i in range(PEER_TOPK)])
    return tops


def _pick(vals, bits):
    if not bits:
        return vals[0]
    half = len(vals) // 2
    return jnp.where(bits[0], _pick(vals[half:], bits[1:]), _pick(vals[:half], bits[1:]))


def _count_greater(x, s):
    bits = []
    step = PEER_TOPK // 2
    while step >= 1:
        vals = [s[base + step - 1] for base in range(0, PEER_TOPK, 2 * step)]
        bits.append(_pick(vals, bits) > x)
        step //= 2
    return bits, s[PEER_TOPK - 1] > x


def _peer_topk_kernel(hn_ref, wq_ref, keys_ref, n1_ref, e1_ref, r2_ref, e2_ref,
                      scr1, scr2, stat_ref):
    nch = hn_ref.shape[1] // LANE
    qt = _dot(wq_ref[...], hn_ref[...])
    for c, scr in enumerate((scr1, scr2)):
        st = _dot(keys_ref[0, c], qt[c * LANE:(c + 1) * LANE, :])
        for j in range(nch):
            scr[j * CHUNK_PITCH:j * CHUNK_PITCH + PEER_NKEYS, :] = st[:, j * LANE:(j + 1) * LANE]
    a = _top_values(scr1)
    b = _top_values(scr2)
    pairs = [(i, j) for i in range(PEER_TOPK) for j in range(PEER_TOPK) if (i + 1) * (j + 1) <= PEER_TOPK]
    tops = [jnp.full((SUBLANE, LANE), -jnp.inf, F32) for _ in range(PEER_TOPK)]
    for i, j in pairs:
        tops = _insert(tops, a[i] + b[j])
    thr = tops[PEER_TOPK - 1]
    z = jnp.zeros((SUBLANE, LANE), F32)
    for t in tops:
        z = z + jnp.exp(t - tops[0])
    inv_z = 1.0 / z
    cnt = [jnp.zeros((SUBLANE, LANE), F32) for _ in range(PEER_TOPK)]
    for i, j in pairs:
        cnt[i] = cnt[i] + jnp.where(a[i] + b[j] >= thr, GATE_STEP, 0.0)

    for i in range(PEER_TOPK):
        stat_ref[i] = a[i]
        stat_ref[PEER_TOPK + i] = b[i]
        stat_ref[2 * PEER_TOPK + i] = cnt[i]
    stat_ref[3 * PEER_TOPK] = inv_z
    weights = [GATE_STEP * (PEER_TOPK >> (i + 1)) for i in range(4)] + [GATE_STEP]

    def chunk(c, carry):
        def stat(i):
            return jnp.broadcast_to(stat_ref[i, pl.ds(c, 1), :], (SUBLANE, LANE))
        a_c = [stat(i) for i in range(PEER_TOPK)]
        b_c = [stat(PEER_TOPK + i) for i in range(PEER_TOPK)]
        cnt_c = [stat(2 * PEER_TOPK + i) for i in range(PEER_TOPK)]
        inv_z_c = stat(3 * PEER_TOPK)
        e1_ref[0, c, 0:E1_ROW0, :] = jnp.zeros((E1_ROW0, LANE), F32)
        e1_ref[0, c, E1_ROW0 + PEER_NKEYS:E1_ROWS, :] = jnp.zeros((E1_ROWS - E1_ROW0 - PEER_NKEYS, LANE), F32)
        for g in range(PEER_NKEYS // SUBLANE):
            rows = pl.ds(pl.multiple_of(c * CHUNK_PITCH, SUBLANE) + g * SUBLANE, SUBLANE)
            x = scr1[rows, :]
            bits, below = _count_greater(x, a_c)
            n1_ref[0, c, g * SUBLANE:(g + 1) * SUBLANE, :] = jnp.where(below, 0.0, _pick(cnt_c, bits))
            e1_ref[0, c, E1_ROW0 + g * SUBLANE:E1_ROW0 + (g + 1) * SUBLANE, :] = jnp.exp(x - a_c[0]) * inv_z_c
            x = scr2[rows, :]
            bits, below = _count_greater(x, b_c)
            rank = _tree_sum([jnp.where(m, w, 0.0) for m, w in zip(bits + [below], weights)])
            scr1[rows, :] = rank
            scr2[rows, :] = jnp.exp(x - b_c[0])
        return carry

    lax.fori_loop(0, nch, chunk, 0)
    for j in range(nch):
        rows = slice(j * CHUNK_PITCH, j * CHUNK_PITCH + PEER_NKEYS)
        r2_ref[0, j] = scr1[rows, :].astype(BF16)
        e2_ref[0, j, 0:E2_ROW0, :] = jnp.zeros((E2_ROW0, LANE), BF16)
        e2_ref[0, j, E2_ROW0:E2_ROW0 + PEER_NKEYS, :] = scr2[rows, :].astype(BF16)
        e2_ref[0, j, E2_ROW0 + PEER_NKEYS:E2_ROWS, :] = jnp.zeros((E2_ROWS - E2_ROW0 - PEER_NKEYS, LANE), BF16)


def _peer_topk(hn, wq, keys):
    T = hn.shape[1]
    tt = SUBLANE * LANE
    nch = T // LANE
    shape = (PEER_HEADS, nch, PEER_NKEYS, LANE)
    spec = pl.BlockSpec((1, SUBLANE, PEER_NKEYS, LANE), lambda i, h: (h, i, 0, 0))
    return pl.pallas_call(
        _peer_topk_kernel,
        grid=(T // tt, PEER_HEADS),
        in_specs=[pl.BlockSpec((D_MODEL, tt), lambda i, h: (0, i)),
                  pl.BlockSpec((2 * LANE, D_MODEL), lambda i, h: (h, 0)),
                  pl.BlockSpec((1, 2, PEER_NKEYS, LANE), lambda i, h: (h, 0, 0, 0))],
        out_specs=[spec, pl.BlockSpec((1, SUBLANE, E1_ROWS, LANE), lambda i, h: (h, i, 0, 0)), spec,
                   pl.BlockSpec((1, SUBLANE, E2_ROWS, LANE), lambda i, h: (h, i, 0, 0))],
        out_shape=[jax.ShapeDtypeStruct(shape, F32),
                   jax.ShapeDtypeStruct((PEER_HEADS, nch, E1_ROWS, LANE), F32),
                   jax.ShapeDtypeStruct(shape, BF16),
                   jax.ShapeDtypeStruct((PEER_HEADS, nch, E2_ROWS, LANE), BF16)],
        scratch_shapes=[pltpu.VMEM((SUBLANE * CHUNK_PITCH, LANE), F32),
                        pltpu.VMEM((SUBLANE * CHUNK_PITCH, LANE), F32),
                        pltpu.VMEM((3 * PEER_TOPK + 1, SUBLANE, LANE), F32)],
        compiler_params=_params(("parallel", "arbitrary")),
        name="peer_topk",
    )(hn, wq, keys)


def _row_bf16(ref, h, c, i1, dep=None):
    packed_rows = 2 * SUBLANE
    row = ref[h, c, pl.ds(i1, 1), :]
    if dep is not None:
        row = row + dep
    x = jnp.broadcast_to(row, (packed_rows, LANE)).astype(BF16)
    return jnp.concatenate([x] * (PEER_NKEYS // packed_rows), axis=0)


def _zero_after(x):
    bits = pltpu.bitcast(x[0:2 * SUBLANE, :], jnp.uint32)
    sixteen = jnp.uint32(16)
    zero_bits = lax.shift_right_logical(lax.shift_right_logical(bits, sixteen), sixteen)
    return pltpu.bitcast(zero_bits, F32)[0:1, :]


def _peer_main_kernel(final, hbt_ref, hres_ref, u0_ref, ub_ref, ua_ref, vta_ref, vtb_ref,
                      n1_ref, e1_ref, r2_ref, e2_ref, fg_ref, o_ref,
                      acc_ref, hga_ref, hgb_ref, hta_ref, htb_ref):
    e = pl.program_id(1)
    n_pairs = pl.num_programs(1) - 1
    tt = hbt_ref.shape[1]
    n_i1 = PEER_TILE // PEER_NKEYS
    n_ch = tt // LANE
    zero = jnp.zeros((PEER_NKEYS, LANE), BF16)

    @pl.when(e == 0)
    def _():
        acc_ref[...] = jnp.zeros_like(acc_ref)
        hgb_ref[...] = jnp.zeros_like(hgb_ref)
        hta_ref[...] = _dot(u0_ref[...], hbt_ref[...])

    def gates(tile, ht_ref, hg_ref, c, dep):
        for il in range(n_i1):
            i1 = tile * n_i1 + il
            gate = None
            for h in range(PEER_HEADS):
                code = _row_bf16(n1_ref, h, c, i1, dep if h == 0 else None) - r2_ref[h, c]
                w = _row_bf16(e1_ref, h, c, i1 + E1_ROW0) * e2_ref[h, c, E2_ROW0:E2_ROW0 + PEER_NKEYS, :]
                term = jnp.maximum(jnp.minimum(w, code), zero)
                gate = term if gate is None else gate + term
            rows = slice(il * PEER_NKEYS, (il + 1) * PEER_NKEYS)
            cols = slice(c * LANE, (c + 1) * LANE)
            hg = _gelu(ht_ref[rows, cols]).astype(BF16) * gate
            hg_ref[rows, cols] = hg
            dep = _zero_after(hg)
        return dep

    def phase(tile, ht_ref, hg_ref, vt_prev_ref, hg_prev_ref, u_next_ref, ht_next_ref, dep):
        half = tt // 2
        for piece in range(4):
            cols = slice((piece % 2) * half, (piece % 2 + 1) * half)
            if piece < 2:
                acc_ref[:, cols] += _dot(vt_prev_ref[...], hg_prev_ref[:, cols])
            else:
                ht_next_ref[:, cols] = _dot(u_next_ref[...], hbt_ref[:, cols])
            for c in range(piece * n_ch // 4, (piece + 1) * n_ch // 4):
                dep = gates(tile, ht_ref, hg_ref, c, dep)
        return dep

    @pl.when(e < n_pairs)
    def _():
        dep = phase(2 * e, hta_ref, hga_ref, vtb_ref, hgb_ref, ub_ref, htb_ref, None)
        phase(2 * e + 1, htb_ref, hgb_ref, vta_ref, hga_ref, ua_ref, hta_ref, dep)

    @pl.when(e == n_pairs)
    def _():
        acc = acc_ref[...] + _dot(vtb_ref[...], hgb_ref[...])
        out = hres_ref[...] + acc.T
        if final:
            out = _rms(out, fg_ref[...])
        o_ref[...] = out


def _peer_main(hbt, hres, u, vt, n1, e1, r2, e2, fg, final):
    T = hres.shape[0]
    E = u.shape[0]
    tt = PEER_TOKENS
    n_tiles = E // PEER_TILE
    n_pairs = n_tiles // 2
    big_spec = pl.BlockSpec((PEER_HEADS, tt // LANE, PEER_NKEYS, LANE), lambda i, e: (0, i, 0, 0))
    tile_of = lambda f: (lambda i, e: (jnp.clip(f(e), 0, n_tiles - 1), 0, 0))
    vt_spec = lambda f: pl.BlockSpec((D_MODEL, PEER_TILE), lambda i, e: tile_of(f)(i, e)[:2])
    u_spec = lambda f: pl.BlockSpec((PEER_TILE, D_MODEL), lambda i, e: tile_of(f)(i, e)[:2])
    return pl.pallas_call(
        functools.partial(_peer_main_kernel, final),
        grid=(T // tt, n_pairs + 1),
        in_specs=[pl.BlockSpec((D_MODEL, tt), lambda i, e: (0, i)),
                  pl.BlockSpec((tt, D_MODEL), lambda i, e: (i, 0)),
                  u_spec(lambda e: 0), u_spec(lambda e: 2 * e + 1), u_spec(lambda e: 2 * e + 2),
                  vt_spec(lambda e: 2 * e), vt_spec(lambda e: 2 * e - 1),
                  big_spec,
                  pl.BlockSpec((PEER_HEADS, tt // LANE, E1_ROWS, LANE), lambda i, e: (0, i, 0, 0)),
                  big_spec,
                  pl.BlockSpec((PEER_HEADS, tt // LANE, E2_ROWS, LANE), lambda i, e: (0, i, 0, 0)),
                  _const_spec((1, D_MODEL))],
        out_specs=pl.BlockSpec((tt, D_MODEL), lambda i, e: (i, 0)),
        out_shape=jax.ShapeDtypeStruct((T, D_MODEL), F32),
        scratch_shapes=[pltpu.VMEM((D_MODEL, tt), F32),
                        pltpu.VMEM((PEER_TILE, tt), BF16), pltpu.VMEM((PEER_TILE, tt), BF16),
                        pltpu.VMEM((PEER_TILE, tt), F32), pltpu.VMEM((PEER_TILE, tt), F32)],
        compiler_params=_params(("parallel", "arbitrary")),
        name="peer_main",
    )(hbt, hres, u, u, u, vt, vt, n1, e1, r2, e2, fg)


def _peer_layer(h, hbt, wq, keys, u_tab, v_tab, fg, final):
    n1, e1, r2, e2 = _peer_topk(hbt, wq.T.astype(BF16), keys.astype(BF16))
    vt = v_tab.astype(BF16).reshape(-1, PEER_TILE, D_MODEL).transpose(0, 2, 1).reshape(-1, PEER_TILE)
    return _peer_main(hbt, h, u_tab.astype(BF16), vt, n1, e1, r2, e2, fg, final)


SCAN_BLOCK = LANE // BATCH


def _swap_groups(slabs):
    a = list(slabs)
    group = lax.broadcasted_iota(jnp.int32, a[0].shape, 1) // BATCH
    s = SCAN_BLOCK // 2
    while s >= 1:
        upper = (group & s) != 0
        for i in range(SCAN_BLOCK):
            if i & s == 0:
                ai, aj = a[i], a[i + s]
                a[i] = jnp.where(upper, pltpu.roll(aj, s * BATCH, axis=1), ai)
                a[i + s] = jnp.where(upper, aj, pltpu.roll(ai, LANE - s * BATCH, axis=1))
        s //= 2
    return a


def _rows_to_scan(x, ref):
    block_rows = SCAN_BLOCK * BATCH
    for r0 in range(0, x.shape[0], block_rows):
        z = x[r0:r0 + block_rows].T
        slabs = [z[h * RWKV_HEAD:(h + 1) * RWKV_HEAD, :] for h in range(RWKV_HEADS)]
        for t, blk in enumerate(_swap_groups(slabs)):
            ref[r0 // BATCH + t] = blk


def _scan_to_rows(ref, t0):
    slabs = _swap_groups([ref[t0 + t] for t in range(SCAN_BLOCK)])
    return jnp.concatenate(slabs, axis=0).T


def _rwkv_front_kernel(h_ref, hp_ref, g_ref, mu_ref, wr_ref, wk_ref, wv_ref, w0_ref, ww1_ref, ww2_ref,
                       a0_ref, wa1_ref, wa2_ref, wg1_ref, wg2_ref,
                       r_ref, k_ref, v_ref, w_ref, a_ref, go_ref):
    g = g_ref[...]
    x = _rms(h_ref[...], g)
    xp8 = _rms(hp_ref[...], g)
    xp8 = jnp.where(pl.program_id(0) == 0, 0.0, xp8)
    rows = x.shape[0]
    xprev = jnp.concatenate([xp8, x[:rows - BATCH]], axis=0)
    xx = xprev - x
    mix = lambda j: x + xx * mu_ref[j:j + 1, :]
    _rows_to_scan(_dot(mix(0), wr_ref[...]), r_ref)
    _rows_to_scan(w0_ref[...] + _dot(jnp.tanh(_dot(mix(1), ww1_ref[...])), ww2_ref[...]), w_ref)
    _rows_to_scan(_dot(mix(2), wk_ref[...]), k_ref)
    _rows_to_scan(_dot(mix(3), wv_ref[...]), v_ref)
    _rows_to_scan(_sigmoid(a0_ref[...] + _dot(_dot(mix(4), wa1_ref[...]), wa2_ref[...])), a_ref)
    go_ref[...] = _dot(_sigmoid(_dot(mix(5), wg1_ref[...])), wg2_ref[...])


def _rwkv_front(h, g, mu, wr, wk, wv, w0, ww1, ww2, a0, wa1, wa2, wg1, wg2):
    assert RWKV_HEADS == SCAN_BLOCK
    T = h.shape[0]
    tm = 2 * SCAN_BLOCK * BATCH
    row = pl.BlockSpec((tm, D_MODEL), lambda i: (i, 0))
    prev = pl.BlockSpec((BATCH, D_MODEL), lambda i: (jnp.maximum(i * (tm // BATCH) - 1, 0), 0))
    scan = pl.BlockSpec((tm // BATCH, RWKV_HEAD, LANE), lambda i: (i, 0, 0))
    consts = [g, mu, wr, wk, wv, w0, ww1, ww2, a0, wa1, wa2, wg1, wg2]
    scan_out = jax.ShapeDtypeStruct((T // BATCH, RWKV_HEAD, LANE), F32)
    return pl.pallas_call(
        _rwkv_front_kernel,
        grid=(T // tm,),
        in_specs=[row, prev] + [_const_spec(c.shape) for c in consts],
        out_specs=[scan] * 5 + [row],
        out_shape=[scan_out] * 5 + [jax.ShapeDtypeStruct((T, D_MODEL), F32)],
        compiler_params=_params(("parallel",)),
        name="rwkv_front",
    )(h, h, *consts)


def _rwkv_scan_kernel(r_ref, k_ref, v_ref, w_ref, a_ref, kk_ref, ka_ref, rk_ref, lg_ref, lb_ref,
                      y_ref, s_ref, pend_s, k2_s, av_s, bv_s, wr_s, br_s, kr_s):
    tl = r_ref.shape[0]
    n = RWKV_HEAD
    ng = n // SUBLANE

    @pl.when(pl.program_id(0) == 0)
    def _():
        s_ref[...] = jnp.zeros_like(s_ref)

    r = r_ref[...]
    k = k_ref[...]
    a = a_ref[...]
    kk = k * kk_ref[...]
    kk = kk / jnp.maximum(jnp.sqrt(jnp.sum(kk * kk, axis=1, keepdims=True)), 1e-12)
    k2 = k * (1.0 + (a - 1.0) * ka_ref[...])
    dec = jnp.exp(-jnp.exp(-_softplus(-w_ref[...]) - 0.5))
    bv = kk * a
    p = dec
    sh = 1
    while sh < tl:
        p = p * jnp.concatenate([jnp.ones((sh, n, LANE), F32), p[:tl - sh]], axis=0)
        sh *= 2
    p_prev = jnp.concatenate([jnp.ones((1, n, LANE), F32), p[:tl - 1]], axis=0)
    inv_p = 1.0 / p
    pend_s[...] = p[tl - 1]
    k2_s[...] = k2 * inv_p
    av_s[...] = -kk * p_prev
    bv_s[...] = bv * inv_p
    wr_s[...] = r * p
    br_s[...] = jnp.sum(bv * r, axis=1, keepdims=True)
    kr_s[...] = jnp.sum(k2 * r, axis=1, keepdims=True)

    def row(ref, t, j):
        return jnp.broadcast_to(ref[t, pl.ds(j, 1), :], (SUBLANE, LANE))

    def step(t, carry):
        def reduce_body(j, accs):
            sa, sy = accs
            a_b = row(av_s, t, j)
            wr_b = row(wr_s, t, j)
            sa_n, sy_n = [], []
            for vg in range(ng):
                sv = s_ref[j, vg * SUBLANE:(vg + 1) * SUBLANE, :]
                sa_n.append(sa[vg] + sv * a_b)
                sy_n.append(sy[vg] + sv * wr_b)
            return tuple(sa_n), tuple(sy_n)

        zeros = tuple(jnp.zeros((SUBLANE, LANE), F32) for _ in range(ng))
        sa, sy = lax.fori_loop(0, n, reduce_body, (zeros, zeros), unroll=8)
        br = jnp.broadcast_to(br_s[t], (SUBLANE, LANE))
        kr = jnp.broadcast_to(kr_s[t], (SUBLANE, LANE))
        vv = [v_ref[t, vg * SUBLANE:(vg + 1) * SUBLANE, :] for vg in range(ng)]
        for vg in range(ng):
            y_ref[t, vg * SUBLANE:(vg + 1) * SUBLANE, :] = sy[vg] + sa[vg] * br + vv[vg] * kr

        def update_body(j, c):
            b_b = row(bv_s, t, j)
            k_b = row(k2_s, t, j)
            for vg in range(ng):
                sl = slice(vg * SUBLANE, (vg + 1) * SUBLANE)
                s_ref[j, sl, :] = s_ref[j, sl, :] + sa[vg] * b_b + vv[vg] * k_b
            return c

        lax.fori_loop(0, n, update_body, 0, unroll=8)
        return carry

    lax.fori_loop(0, tl, step, 0)

    def rescale(j, c):
        s_ref[j] = s_ref[j] * pend_s[pl.ds(j, 1), :]
        return c

    lax.fori_loop(0, n, rescale, 0, unroll=8)

    y = y_ref[...]
    mean = jnp.mean(y, axis=1, keepdims=True)
    var = jnp.mean((y - mean) ** 2, axis=1, keepdims=True)
    y = (y - mean) * lax.rsqrt(var + RWKV_GN_EPS) * lg_ref[...] + lb_ref[...]
    bonus = jnp.sum(r * k2 * rk_ref[...], axis=1, keepdims=True) * v_ref[...]
    y_ref[...] = y + bonus


def _rwkv_scan(r, k, v, w, a, kk, ka, rk, lg, lb):
    L = r.shape[0]
    tl = SCAN_STEPS
    n = RWKV_HEAD
    blk = pl.BlockSpec((tl, n, LANE), lambda i: (i, 0, 0))
    par = _const_spec((n, LANE))
    big = pltpu.VMEM((tl, n, LANE), F32)
    small = pltpu.VMEM((tl, 1, LANE), F32)
    return pl.pallas_call(
        _rwkv_scan_kernel,
        grid=(L // tl,),
        in_specs=[blk] * 5 + [par] * 5,
        out_specs=blk,
        out_shape=jax.ShapeDtypeStruct((L, n, LANE), F32),
        scratch_shapes=[pltpu.VMEM((n, n, LANE), F32), pltpu.VMEM((n, LANE), F32), big, big, big, big,
                        small, small],
        compiler_params=_params(("arbitrary",)),
        name="rwkv_scan",
    )(r, k, v, w, a, kk, ka, rk, lg, lb)


def _head_param(p):
    return jnp.repeat(p.reshape(RWKV_HEADS, RWKV_HEAD).T, BATCH, axis=1)


def _prep_even(w_in, a_re, a_im, log_dt, b_re, b_im, c_re, c_im, w_g2, b_g2):
    assert S5_WIDTH == GLA_WIDTH == GLA_QK
    qk = GLA_HEADS * GLA_DK
    widths = (S5_WIDTH, qk, qk, GLA_WIDTH, GLA_RANK, GLA_WIDTH)
    edges = [sum(widths[:j]) for j in range(len(widths) + 1)]
    w_u, w_q, w_k, w_v, w_gl, w_r = (w_in[:, edges[j]:edges[j + 1]] for j in range(len(widths)))

    def pad_heads(w):
        w = w.reshape(-1, GLA_HEADS, GLA_DK)
        w = jnp.pad(w, ((0, 0), (0, 0), (0, GLA_DKP - GLA_DK)))
        return w.reshape(-1, GLA_HEADS * GLA_DKP)

    w_proj = jnp.concatenate([w_u, w_v, w_r, pad_heads(w_q), pad_heads(w_k),
                              jnp.pad(w_gl, ((0, 0), (0, LANE - GLA_RANK)))], axis=1)
    wg = jnp.pad(pad_heads(w_g2), ((0, LANE - GLA_RANK), (0, 0)))
    bg = pad_heads(b_g2[None, :])
    lm = pad_heads(jnp.ones((1, GLA_HEADS * GLA_DK), F32))

    dt = jnp.exp(log_dt)
    mag = jnp.exp(dt * a_re)
    abar_re = mag * jnp.cos(dt * a_im)
    abar_im = mag * jnp.sin(dt * a_im)
    nr = abar_re - 1.0
    den = a_re * a_re + a_im * a_im
    f_re = (nr * a_re + abar_im * a_im) / den
    f_im = (abar_im * a_re - nr * a_im) / den
    bbar_re = f_re[..., None] * b_re - f_im[..., None] * b_im
    bbar_im = f_re[..., None] * b_im + f_im[..., None] * b_re
    eye = jnp.eye(8, dtype=F32)
    gs = S5_GROUPS // S5_SLABS

    def bd(bb):
        bb = bb.reshape(S5_SLABS, gs, S5_STATE, S5_GROUP)
        return jnp.einsum('sgpc,gh->sgchp', bb, eye).reshape(S5_SLABS, gs * S5_GROUP, gs * S5_STATE)

    def cdm(cc):
        cc = cc.reshape(S5_SLABS, gs, S5_GROUP, S5_STATE)
        return jnp.einsum('sgcp,gh->sgphc', cc, eye).reshape(S5_SLABS, gs * S5_STATE, gs * S5_GROUP)

    bd_all = jnp.concatenate([bd(bbar_re), bd(bbar_im)], axis=2)
    cd_all = jnp.concatenate([cdm(c_re), -cdm(c_im)], axis=1)
    are = jnp.broadcast_to(abar_re.reshape(S5_SLABS, 1, gs * S5_STATE), (S5_SLABS, BATCH, gs * S5_STATE))
    aim = jnp.broadcast_to(abar_im.reshape(S5_SLABS, 1, gs * S5_STATE), (S5_SLABS, BATCH, gs * S5_STATE))
    return w_proj, wg, bg, lm, bd_all, cd_all, are, aim


def kernel(x, norm_mix_g, norm_ffn_g, final_g, e_w_in, e_w_out, s5_a_re, s5_a_im, s5_log_dt, s5_b_re, s5_b_im, s5_c_re, s5_c_im, s5_d, s5_w_glu, gla_w_g2, gla_b_g2, gla_norm_g, o_mu, o_w_r, o_w_k, o_w_v, o_w0, o_w_w1, o_w_w2, o_a0, o_w_a1, o_w_a2, o_w_g1, o_w_g2, o_k_k, o_k_a, o_r_k, o_lnx_g, o_lnx_b, o_w_o, peer_w_q, peer_sub_keys, peer_u, peer_v):
    bsz, L, D = x.shape
    assert bsz == BATCH and D == D_MODEL
    T = bsz * L
    row = lambda p: p.reshape(1, -1)
    h = x.transpose(1, 0, 2).reshape(T, D)

    w_proj, wg, bg, lm, bd_all, cd_all, are, aim = _prep_even(
        e_w_in[0], s5_a_re[0], s5_a_im[0], s5_log_dt[0], s5_b_re[0], s5_b_im[0],
        s5_c_re[0], s5_c_im[0], gla_w_g2[0], gla_b_g2[0])
    bf = lambda p: p.astype(BF16)
    proj = _even_front(h, row(norm_mix_g[0]), bf(w_proj))
    y_s5 = _s5(proj, bf(bd_all), bf(cd_all), are, aim, row(s5_d[0]), bf(s5_w_glu[0]))
    y_gla = _gla(proj, wg, bg, lm, row(gla_norm_g[0]))
    w_out = bf(e_w_out[0])
    h, hbt = _mix_out(h, [y_s5, y_gla], [w_out[:S5_WIDTH], w_out[S5_WIDTH:]], None, row(norm_ffn_g[0]))
    h = _peer_layer(h, hbt, peer_w_q[0], peer_sub_keys[0], peer_u[0], peer_v[0], row(final_g), False)

    r, k, v, w, a, g = _rwkv_front(
        h, row(norm_mix_g[1]), o_mu[0], bf(o_w_r[0]), bf(o_w_k[0]), bf(o_w_v[0]), row(o_w0[0]),
        bf(o_w_w1[0]), bf(o_w_w2[0]), row(o_a0[0]), bf(o_w_a1[0]), bf(o_w_a2[0]), bf(o_w_g1[0]), bf(o_w_g2[0]))
    y = _rwkv_scan(r, k, v, w, a, _head_param(o_k_k[0]), _head_param(o_k_a[0]),
                   _head_param(o_r_k[0].reshape(-1)), _head_param(o_lnx_g[0]), _head_param(o_lnx_b[0]))
    h, hbt = _mix_out(h, [y], [bf(o_w_o[0])], g, row(norm_ffn_g[1]))
    out = _peer_layer(h, hbt, peer_w_q[1], peer_sub_keys[1], peer_u[1], peer_v[1], row(final_g), True)
    return out.reshape(L, bsz, D).transpose(1, 0, 2)
```
